```python
import jax, jax.numpy as jnp
from jax import lax
import numpy as np

D_MODEL = 1024
BATCH = 32
SEQ = 256
DEPTH = 1
DEC_BATCH = 2
DEC_SEQ = 4096
PAST_LEN = 512

GRID_W = 64
D_MIX = D_MODEL
GLA_HEADS = 4
GLA_WIDTH = D_MIX // 2
GLA_DV = GLA_WIDTH // GLA_HEADS
GLA_DK = GLA_DV // 2
GLA_LOWRANK = 16
GLA_GATE_NORM = 16.0
GLA_CHUNK = 64
SGU_WIDTH = D_MIX - GLA_WIDTH
SGU_GROUPS = 4
SGU_CH = SGU_WIDTH // SGU_GROUPS
SGU_CHUNK = 128
N_EXPERTS = 16
EC_CAPACITY_FACTOR = 2
D_EXPERT = 2048
EPS = 1e-6

QK_W = GLA_HEADS * GLA_DK
OFF_Q = 0
OFF_K = OFF_Q + QK_W
OFF_V = OFF_K + QK_W
OFF_G = OFF_V + GLA_WIDTH
OFF_AF = OFF_G + GLA_WIDTH
OFF_AB = OFF_AF + GLA_LOWRANK
OFF_U = OFF_AB + GLA_LOWRANK
OFF_SV = OFF_U + SGU_WIDTH
D_IN = OFF_SV + SGU_WIDTH

kernel_name = "hybrid_gla_sgu_ecmoe_diffusion_step"


def rmsnorm(x, w):
    xf = x.astype(jnp.float32)
    y = xf * lax.rsqrt(jnp.mean(xf * xf, axis=-1, keepdims=True) + EPS)
    return (y * w.astype(jnp.float32)).astype(x.dtype)


def grid_sincos(n_tokens, dtype):
    rows = n_tokens // GRID_W
    r = jnp.broadcast_to(jnp.arange(rows, dtype=jnp.float32)[:, None], (rows, GRID_W)).reshape(-1)
    col = jnp.broadcast_to(jnp.arange(GRID_W, dtype=jnp.float32)[None, :], (rows, GRID_W)).reshape(-1)
    nf = D_MODEL // 4
    omega = 1.0 / (10000.0 ** (jnp.arange(nf, dtype=jnp.float32) / nf))
    def emb(p):
        a = p[:, None] * omega[None, :]
        return jnp.concatenate([jnp.sin(a), jnp.cos(a)], axis=-1)
    return jnp.concatenate([emb(r), emb(col)], axis=-1).astype(dtype)


def gla_chunked(q, k, v, log_a, s0):
    b, T, H, DK = q.shape
    DV = v.shape[-1]
    n = T // GLA_CHUNK
    f32 = jnp.float32
    q = q.astype(f32).reshape(b, n, GLA_CHUNK, H, DK)
    k = k.astype(f32).reshape(b, n, GLA_CHUNK, H, DK)
    v32 = v.astype(f32).reshape(b, n, GLA_CHUNK, H, DV)
    cum = jnp.cumsum(log_a.astype(f32).reshape(b, n, GLA_CHUNK, H, DK), axis=2)
    qe = q * jnp.exp(cum)
    ke = k * jnp.exp(-cum)
    mask = jnp.tril(jnp.ones((GLA_CHUNK, GLA_CHUNK), dtype=bool))
    att = jnp.einsum('bnihd,bnjhd->bnhij', qe, ke)
    att = jnp.where(mask[None, None, None], att, 0.0)
    o_intra = jnp.einsum('bnhij,bnjhe->bnihe', att, v32)
    last = cum[:, :, -1]
    kd = k * jnp.exp(last[:, :, None] - cum)
    ds = jnp.einsum('bnjhd,bnjhe->bnhde', kd, v32)
    def step(S, inp):
        dec, d = inp
        return dec[..., None] * S + d, S
    s_fin, s_starts = lax.scan(step, s0.astype(f32),
                               (jnp.moveaxis(jnp.exp(last), 1, 0), jnp.moveaxis(ds, 1, 0)))
    o_inter = jnp.einsum('bnihd,nbhde->bnihe', qe, s_starts)
    out = (o_intra + o_inter).reshape(b, T, H, DV)
    return out.astype(v.dtype), s_fin.astype(v.dtype)


def sgu_chunked(u_in, v_in, norm_w, ws, bs):
    b, T, _ = u_in.shape
    u = jax.nn.gelu(u_in)
    v = rmsnorm(jax.nn.gelu(v_in).reshape(b, T, SGU_GROUPS, SGU_CH), norm_w.reshape(SGU_GROUPS, SGU_CH))
    v = v.reshape(b, T // SGU_CHUNK, SGU_CHUNK, SGU_GROUPS, SGU_CH)
    vm = jnp.einsum('gpq,bnqgc->bnpgc', ws, v) + bs.T[None, None, :, :, None]
    return u * vm.reshape(b, T, SGU_WIDTH)


def ec_moe(x, router_w, w1, w3, w2):
    b, T, D = x.shape
    N = b * T
    xf = x.reshape(N, D)
    probs = jax.nn.softmax(jnp.matmul(xf.astype(jnp.float32), router_w.astype(jnp.float32)), axis=-1)
    cap = EC_CAPACITY_FACTOR * N // N_EXPERTS
    gates, idx = lax.top_k(probs.T, cap)
    xe = xf[idx]
    h = jax.nn.silu(jnp.einsum('ecd,edf->ecf', xe, w1)) * jnp.einsum('ecd,edf->ecf', xe, w3)
    ye = jnp.einsum('ecf,efd->ecd', h, w2) * gates[..., None].astype(x.dtype)
    out = jnp.zeros_like(xf).at[idx.reshape(-1)].add(ye.reshape(-1, D))
    return out.reshape(b, T, D)


def trunk_layer(x, mod, s0_f, s0_b, norm1_w, w_in, wa2_f, ba_f, wa2_b, ba_b, gla_norm_w,
                sgu_norm_w, sgu_ws, sgu_bs, w_out, norm2_w, router_w, w1, w3, w2):
    b, T, _ = x.shape
    shift1, scale1, gate1, shift2, scale2, gate2 = jnp.split(mod[:, None, :].astype(x.dtype), 6, axis=-1)
    h = rmsnorm(x, norm1_w) * (1.0 + scale1) + shift1
    p = jnp.matmul(h, w_in)
    q = p[..., OFF_Q:OFF_Q + QK_W].reshape(b, T, GLA_HEADS, GLA_DK) * (GLA_DK ** -0.5)
    k = p[..., OFF_K:OFF_K + QK_W].reshape(b, T, GLA_HEADS, GLA_DK)
    v = p[..., OFF_V:OFF_V + GLA_WIDTH].reshape(b, T, GLA_HEADS, GLA_DV)
    g = p[..., OFF_G:OFF_G + GLA_WIDTH]
    la_f = jax.nn.log_sigmoid((jnp.matmul(p[..., OFF_AF:OFF_AF + GLA_LOWRANK], wa2_f) + ba_f).astype(jnp.float32)) / GLA_GATE_NORM
    la_b = jax.nn.log_sigmoid((jnp.matmul(p[..., OFF_AB:OFF_AB + GLA_LOWRANK], wa2_b) + ba_b).astype(jnp.float32)) / GLA_GATE_NORM
    la_f = la_f.reshape(b, T, GLA_HEADS, GLA_DK)
    la_b = la_b.reshape(b, T, GLA_HEADS, GLA_DK)
    o_f, s_f = gla_chunked(q, k, v, la_f, s0_f)
    o_b, s_b = gla_chunked(jnp.flip(q, 1), jnp.flip(k, 1), jnp.flip(v, 1), jnp.flip(la_b, 1), s0_b)
    o = rmsnorm(o_f + jnp.flip(o_b, 1), gla_norm_w).reshape(b, T, GLA_WIDTH) * jax.nn.silu(g)
    s = sgu_chunked(p[..., OFF_U:OFF_U + SGU_WIDTH], p[..., OFF_SV:OFF_SV + SGU_WIDTH], sgu_norm_w, sgu_ws, sgu_bs)
    x = x + gate1 * jnp.matmul(jnp.concatenate([o, s], axis=-1), w_out)
    h2 = rmsnorm(x, norm2_w) * (1.0 + scale2) + shift2
    x = x + gate2 * ec_moe(h2, router_w, w1, w3, w2)
    return x, s_f, s_b


def setup_inputs(seed: int = 0) -> dict:
    key = jax.random.key(seed)
    ks = jax.random.split(key, 32)
    f32 = jnp.float32
    nrm = lambda i, shape, s: jax.random.normal(ks[i], shape, f32) * s
    st_shape = (DEC_BATCH, DEPTH, GLA_HEADS, GLA_DK, GLA_DV)
    return {
        "x_prompt": nrm(0, (BATCH, SEQ, D_MODEL), 1.0),
        "x_sample": nrm(1, (DEC_BATCH, DEC_SEQ, D_MODEL), 1.0),
        "state_gla_fwd": nrm(2, st_shape, 0.5),
        "state_gla_bwd": nrm(3, st_shape, 0.5),
        "c": nrm(4, (DEC_BATCH, D_MODEL), 1.0),
        "c_ctx": nrm(5, (D_MODEL,), 1.0),
        "ada_w": nrm(6, (DEPTH, D_MODEL, 6 * D_MODEL), 0.5 * D_MODEL ** -0.5),
        "ada_b": nrm(7, (DEPTH, 6 * D_MODEL), 0.02),
        "norm1_w": 1.0 + nrm(8, (DEPTH, D_MODEL), 0.02),
        "w_in": nrm(9, (DEPTH, D_MODEL, D_IN), D_MODEL ** -0.5),
        "gla_wa2_f": nrm(10, (DEPTH, GLA_LOWRANK, QK_W), GLA_LOWRANK ** -0.5),
        "gla_ba_f": nrm(11, (DEPTH, QK_W), 0.1),
        "gla_wa2_b": nrm(12, (DEPTH, GLA_LOWRANK, QK_W), GLA_LOWRANK ** -0.5),
        "gla_ba_b": nrm(13, (DEPTH, QK_W), 0.1),
        "gla_norm_w": 1.0 + nrm(14, (DEPTH, GLA_DV), 0.02),
        "sgu_norm_w": 1.0 + nrm(15, (DEPTH, SGU_WIDTH), 0.02),
        "sgu_ws": nrm(16, (DEPTH, SGU_GROUPS, SGU_CHUNK, SGU_CHUNK), SGU_CHUNK ** -0.5),
        "sgu_bs": 1.0 + nrm(17, (DEPTH, SGU_GROUPS, SGU_CHUNK), 0.02),
        "w_out": nrm(18, (DEPTH, D_MIX, D_MODEL), D_MIX ** -0.5),
        "norm2_w": 1.0 + nrm(19, (DEPTH, D_MODEL), 0.02),
        "router_w": nrm(20, (DEPTH, D_MODEL, N_EXPERTS), D_MODEL ** -0.5),
        "exp_w1": nrm(21, (DEPTH, N_EXPERTS, D_MODEL, D_EXPERT), D_MODEL ** -0.5),
        "exp_w3": nrm(22, (DEPTH, N_EXPERTS, D_MODEL, D_EXPERT), D_MODEL ** -0.5),
        "exp_w2": nrm(23, (DEPTH, N_EXPERTS, D_EXPERT, D_MODEL), D_EXPERT ** -0.5),
        "final_norm_w": 1.0 + nrm(24, (D_MODEL,), 0.02),
    }


def reference(x_prompt, x_sample, state_gla_fwd, state_gla_bwd, c, c_ctx, ada_w, ada_b, norm1_w, w_in,
              gla_wa2_f, gla_ba_f, gla_wa2_b, gla_ba_b, gla_norm_w, sgu_norm_w, sgu_ws, sgu_bs, w_out,
              norm2_w, router_w, exp_w1, exp_w3, exp_w2, final_norm_w):
    def layer_params(l):
        return (norm1_w[l], w_in[l], gla_wa2_f[l], gla_ba_f[l], gla_wa2_b[l], gla_ba_b[l], gla_norm_w[l],
                sgu_norm_w[l], sgu_ws[l], sgu_bs[l], w_out[l], norm2_w[l], router_w[l],
                exp_w1[l], exp_w3[l], exp_w2[l])

    xp = x_prompt
    zero_state = jnp.zeros((xp.shape[0], GLA_HEADS, GLA_DK, GLA_DV), xp.dtype)
    ctx_f, ctx_b = [], []
    for l in range(DEPTH):
        mod_ctx = (jnp.matmul(jax.nn.silu(c_ctx), ada_w[l]) + ada_b[l])[None, :]
        xp, s_f, s_b = trunk_layer(xp, mod_ctx, zero_state, zero_state, *layer_params(l))
        ctx_f.append(s_f)
        ctx_b.append(s_b)
    y_prompt = rmsnorm(xp, final_norm_w)
    new_state_gla_fwd = jnp.stack(ctx_f, axis=1)
    new_state_gla_bwd = jnp.stack(ctx_b, axis=1)

    xs = x_sample + grid_sincos(x_sample.shape[1], x_sample.dtype)[None]
    for l in range(DEPTH):
        mod = jnp.matmul(jax.nn.silu(c), ada_w[l]) + ada_b[l]
        xs, _, _ = trunk_layer(xs, mod, state_gla_fwd[:, l], state_gla_bwd[:, l], *layer_params(l))
    y_sample = rmsnorm(xs, final_norm_w)
    return (y_prompt, y_sample, new_state_gla_fwd, new_state_gla_bwd)
```

```python
import functools
import math

import jax
import jax.numpy as jnp
from jax import lax
from jax.experimental import pallas as pl
from jax.experimental.pallas import tpu as pltpu

F32 = jnp.float32
BF16 = jnp.bfloat16
I32 = jnp.int32

D_MODEL = 1024
GRID_W = 64
GLA_HEADS = 4
GLA_DK = 64
GLA_DV = 128
GLA_WIDTH = GLA_HEADS * GLA_DV
QK_W = GLA_HEADS * GLA_DK
GLA_LOWRANK = 16
GLA_GATE_NORM = 16.0
GLA_CHUNK = 64
SGU_WIDTH = 512
SGU_GROUPS = 4
SGU_CH = 128
SGU_CHUNK = 128
N_EXPERTS = 16
EC_CAPACITY_FACTOR = 2
D_EXPERT = 2048
EPS = 1e-6

SUBLANES = 8
LANES = 128
TILES_PER_TOKEN = D_MODEL // LANES

TOKEN_BLOCK = 256
P_Q, P_K, P_V, P_G, P_U, P_SV, P_A = 0, 256, 512, 1024, 1536, 2048, 2560
P_WIDTH = 2688
EXPERT_F_BLOCK = 512
VMEM_LIMIT = 56 * 1024 * 1024


def _dot(a, b):
    return jnp.dot(a.astype(BF16), b.astype(BF16), preferred_element_type=F32)


def _dot_nt(a, b):
    return lax.dot_general(a.astype(BF16), b.astype(BF16), (((1,), (1,)), ((), ())),
                           preferred_element_type=F32)


def _dot_tn(a, b):
    return lax.dot_general(a.astype(BF16), b.astype(BF16), (((0,), (0,)), ((), ())),
                           preferred_element_type=F32)


def _dot_f32(a, b, dims=(((1,), (0,)), ((), ()))):
    return lax.dot_general(a, b, dims, precision=lax.Precision.HIGHEST, preferred_element_type=F32)


def _rms(x, w):
    return x * lax.rsqrt(jnp.mean(x * x, axis=-1, keepdims=True) + EPS) * w


def _compiler_params(sem):
    return pltpu.CompilerParams(dimension_semantics=sem, vmem_limit_bytes=VMEM_LIMIT)


def _mod_kernel(c_ref, w_ref, b_ref, o_ref):
    o_ref[...] = _dot(jax.nn.silu(c_ref[...]), w_ref[...]) + b_ref[...]


def _modulation(cvec, ada_w, ada_b):
    n = ada_w.shape[1]
    bn = 1536
    return pl.pallas_call(
        _mod_kernel,
        grid=(n // bn,),
        in_specs=[pl.BlockSpec((SUBLANES, D_MODEL), lambda j: (0, 0)),
                  pl.BlockSpec((D_MODEL, bn), lambda j: (0, j)),
                  pl.BlockSpec((1, bn), lambda j: (0, j))],
        out_specs=pl.BlockSpec((SUBLANES, bn), lambda j: (0, j)),
        out_shape=jax.ShapeDtypeStruct((SUBLANES, n), F32),
        compiler_params=_compiler_params(("arbitrary",)),
        name="adaln_mod",
    )(cvec, ada_w, ada_b)


def _pos_kernel(o_ref):
    nf = D_MODEL // 4
    p = lax.broadcasted_iota(I32, (GRID_W, nf), 0).astype(F32)
    i = lax.broadcasted_iota(I32, (GRID_W, nf), 1).astype(F32)
    omega = jnp.exp(i * (-math.log(10000.0) / nf))
    a = p * omega
    o_ref[:, 0:nf] = jnp.sin(a)
    o_ref[:, nf:2 * nf] = jnp.cos(a)


def _pos_table():
    return pl.pallas_call(
        _pos_kernel,
        out_shape=jax.ShapeDtypeStruct((GRID_W, D_MODEL // 2), F32),
        name="sincos_table",
    )()


def _load_xin(x_ref, e_ref, blk, add_pos):
    x = x_ref[...]
    if not add_pos:
        return x
    half = D_MODEL // 2
    e_all = e_ref[...]
    rows = []
    for j in range(TOKEN_BLOCK // GRID_W):
        xj = x[j * GRID_W:(j + 1) * GRID_W]
        e_row = e_ref[pl.ds(blk * (TOKEN_BLOCK // GRID_W) + j, 1), :]
        rows.append(jnp.concatenate([xj[:, 0:half] + e_row, xj[:, half:] + e_all], axis=1))
    return jnp.concatenate(rows, axis=0)


def _chunk_masks():
    r = lax.broadcasted_iota(I32, (TOKEN_BLOCK, TOKEN_BLOCK), 0)
    c = lax.broadcasted_iota(I32, (TOKEN_BLOCK, TOKEN_BLOCK), 1)
    same = (r // GLA_CHUNK) == (c // GLA_CHUNK)
    return same & (c <= r), same & (c >= r)


def _gla_direction(q, k, v, cum, fwd, att_mask, st_ref):
    qe = q * jnp.exp(cum)
    ke = k * jnp.exp(-cum)
    lane = lax.broadcasted_iota(I32, (1, LANES), 1)
    o_intra = []
    for pair in range(2):
        qp = qe[:, pair * LANES:(pair + 1) * LANES]
        kp = ke[:, pair * LANES:(pair + 1) * LANES]
        for hh in range(2):
            qm = jnp.where((lane // GLA_DK) == hh, qp, 0.0)
            att = jnp.where(att_mask, _dot_nt(qm, kp), 0.0)
            head = 2 * pair + hh
            o_intra.append(_dot(att, v[:, head * GLA_DV:(head + 1) * GLA_DV]))
    o_intra = jnp.concatenate(o_intra, axis=1)

    er = lax.broadcasted_iota(I32, (2 * GLA_DV, 2 * GLA_DK), 0)
    dc = lax.broadcasted_iota(I32, (2 * GLA_DV, 2 * GLA_DK), 1)
    same_head = (er // GLA_DV) == (dc // GLA_DK)
    n_chunks = TOKEN_BLOCK // GLA_CHUNK
    o_inter = [None] * n_chunks
    for c in (range(n_chunks) if fwd else reversed(range(n_chunks))):
        r0 = c * GLA_CHUNK
        rows = slice(r0, r0 + GLA_CHUNK)
        last = cum[r0 + GLA_CHUNK - 1:r0 + GLA_CHUNK] if fwd else cum[r0:r0 + 1]
        kd = k[rows] * jnp.exp(last - cum[rows])
        dec = jnp.exp(last)
        parts = []
        for pair in range(2):
            dl = slice(pair * LANES, (pair + 1) * LANES)
            st = st_ref[pair]
            parts.append(_dot_nt(qe[rows, dl], st))
            ds_t = _dot_tn(v[rows, pair * 2 * GLA_DV:(pair + 1) * 2 * GLA_DV], kd[:, dl])
            st_ref[pair] = dec[:, dl] * st + jnp.where(same_head, ds_t, 0.0)
        o_inter[c] = jnp.concatenate(parts, axis=1)
    return o_intra + jnp.concatenate(o_inter, axis=0)


def _load_state(s0_ref, st_ref):
    zero = jnp.zeros((GLA_DV, GLA_DK), F32)
    for pair in range(2):
        a = s0_ref[0, 2 * pair].T
        b = s0_ref[0, 2 * pair + 1].T
        st_ref[pair] = jnp.concatenate(
            [jnp.concatenate([a, zero], axis=1), jnp.concatenate([zero, b], axis=1)], axis=0)


def _store_state(st_ref, sfin_ref):
    for pair in range(2):
        st = st_ref[pair]
        sfin_ref[0, 2 * pair] = st[0:GLA_DV, 0:GLA_DK].T
        sfin_ref[0, 2 * pair + 1] = st[GLA_DV:2 * GLA_DV, GLA_DK:2 * GLA_DK].T


def _mod_row(mod_ref, row, part):
    return mod_ref[pl.ds(row, 1), part * D_MODEL:(part + 1) * D_MODEL]


def _mixer_fwd_kernel(add_pos, mod_base, mod_per_seq, nb,
                      x_ref, e_ref, mod_ref, n1_ref, win_ref, wa_ref, ba_ref,
                      snw_ref, sws_ref, sbs_ref, s0_ref,
                      qkv_ref, g_ref, lab_ref, s_ref, of_ref, sfin_ref,
                      st_ref):
    seq = pl.program_id(0)
    blk = pl.program_id(1)
    row = mod_base + (seq if mod_per_seq else 0)

    @pl.when(blk == 0)
    def _():
        _load_state(s0_ref, st_ref)

    xin = _load_xin(x_ref, e_ref, blk, add_pos)
    h = _rms(xin, n1_ref[...]) * (1.0 + _mod_row(mod_ref, row, 1)) + _mod_row(mod_ref, row, 0)
    p = _dot(h, win_ref[...])
    q = p[:, P_Q:P_K] * (GLA_DK ** -0.5)
    k = p[:, P_K:P_V]
    v = p[:, P_V:P_G]
    z = _dot(p[:, P_A:P_WIDTH], wa_ref[...]) + ba_ref[...]
    la = (jnp.minimum(z, 0.0) - jnp.log1p(jnp.exp(-jnp.abs(z)))) * (1.0 / GLA_GATE_NORM)

    ug = jax.nn.gelu(p[:, P_U:P_SV])
    vg = jax.nn.gelu(p[:, P_SV:P_A])
    s_cols = []
    for gi in range(SGU_GROUPS):
        cols = slice(gi * SGU_CH, (gi + 1) * SGU_CH)
        vn = _rms(vg[:, cols], snw_ref[:, cols])
        rhs = jnp.concatenate([vn[0:SGU_CHUNK], vn[SGU_CHUNK:2 * SGU_CHUNK]], axis=1)
        vm = _dot(sws_ref[gi], rhs) + jnp.concatenate([sbs_ref[gi], sbs_ref[gi]], axis=1)
        vm = jnp.concatenate([vm[:, 0:SGU_CH], vm[:, SGU_CH:2 * SGU_CH]], axis=0)
        s_cols.append(ug[:, cols] * vm)
    s_ref[...] = jnp.concatenate(s_cols, axis=1)

    lo_mask, _ = _chunk_masks()
    cum = _dot_f32(lo_mask.astype(F32), la[:, 0:QK_W])
    of_ref[...] = _gla_direction(q, k, v, cum, True, lo_mask, st_ref)
    qkv_ref[...] = jnp.concatenate([q, k, v], axis=1)
    g_ref[...] = p[:, P_G:P_U]
    lab_ref[...] = la[:, QK_W:2 * QK_W]

    @pl.when(blk == nb - 1)
    def _():
        _store_state(st_ref, sfin_ref)


def _mixer_fwd(x, e_tab, mod, n1, win, wa, ba, snw, sws, sbs, s0, *, nseq, nb, add_pos,
               mod_base, mod_per_seq):
    n = nseq * nb * TOKEN_BLOCK
    tok = lambda w: pl.BlockSpec((TOKEN_BLOCK, w), lambda s, i: (s * nb + i, 0))
    full = lambda a: pl.BlockSpec(a.shape, lambda s, i: (0,) * a.ndim)
    st_spec = pl.BlockSpec((1, GLA_HEADS, GLA_DK, GLA_DV), lambda s, i: (s, 0, 0, 0))
    kern = functools.partial(_mixer_fwd_kernel, add_pos, mod_base, mod_per_seq, nb)
    return pl.pallas_call(
        kern,
        grid=(nseq, nb),
        in_specs=[tok(D_MODEL), full(e_tab), full(mod), full(n1), full(win), full(wa), full(ba),
                  full(snw), full(sws), full(sbs), st_spec],
        out_specs=[tok(1024), tok(GLA_WIDTH), tok(QK_W), tok(SGU_WIDTH), tok(GLA_WIDTH), st_spec],
        out_shape=[jax.ShapeDtypeStruct((n, 1024), F32),
                   jax.ShapeDtypeStruct((n, GLA_WIDTH), F32),
                   jax.ShapeDtypeStruct((n, QK_W), F32),
                   jax.ShapeDtypeStruct((n, SGU_WIDTH), F32),
                   jax.ShapeDtypeStruct((n, GLA_WIDTH), F32),
                   jax.ShapeDtypeStruct((nseq, GLA_HEADS, GLA_DK, GLA_DV), F32)],
        scratch_shapes=[pltpu.VMEM((2, 2 * GLA_DV, 2 * GLA_DK), F32)],
        compiler_params=_compiler_params(("arbitrary", "arbitrary")),
        name="mixer_fwd",
    )(x, e_tab, mod, n1, win, wa, ba, snw, sws, sbs, s0)


def _mixer_bwd_kernel(add_pos, mod_base, mod_per_seq, nb,
                      x_ref, e_ref, mod_ref, qkv_ref, g_ref, lab_ref, s_ref, of_ref,
                      gnw_ref, wout_ref, n2_ref, rw_ref, s0_ref,
                      x1_ref, h2t_ref, probs_ref, sfin_ref,
                      st_ref):
    seq = pl.program_id(0)
    step = pl.program_id(1)
    blk = nb - 1 - step
    row = mod_base + (seq if mod_per_seq else 0)

    @pl.when(step == 0)
    def _():
        _load_state(s0_ref, st_ref)

    qkv = qkv_ref[...]
    q, k, v = qkv[:, 0:QK_W], qkv[:, QK_W:2 * QK_W], qkv[:, 2 * QK_W:]
    _, hi_mask = _chunk_masks()
    cum = _dot_f32(hi_mask.astype(F32), lab_ref[...])
    o = of_ref[...] + _gla_direction(q, k, v, cum, False, hi_mask, st_ref)
    g = g_ref[...]
    cols = []
    for head in range(GLA_HEADS):
        hs = slice(head * GLA_DV, (head + 1) * GLA_DV)
        cols.append(_rms(o[:, hs], gnw_ref[...]) * jax.nn.silu(g[:, hs]))
    cols.append(s_ref[...])
    y = _dot(jnp.concatenate(cols, axis=1), wout_ref[...])

    xin = _load_xin(x_ref, e_ref, blk, add_pos)
    x1 = xin + _mod_row(mod_ref, row, 2) * y
    x1_ref[...] = x1
    h2 = _rms(x1, n2_ref[...]) * (1.0 + _mod_row(mod_ref, row, 4)) + _mod_row(mod_ref, row, 3)
    for s in range(TILES_PER_TOKEN):
        h2t_ref[pl.ds(s, TOKEN_BLOCK, stride=TILES_PER_TOKEN), :] = h2[:, s * LANES:(s + 1) * LANES]

    logits = _dot_f32(rw_ref[...], h2, (((1,), (1,)), ((), ())))
    m = jnp.max(logits, axis=0, keepdims=True)
    ex = jnp.exp(logits - m)
    probs_ref[0] = ex / jnp.sum(ex, axis=0, keepdims=True)

    @pl.when(step == nb - 1)
    def _():
        _store_state(st_ref, sfin_ref)


def _mixer_bwd(x, e_tab, mod, qkv, g, lab, s, of, gnw, wout, n2, rw_t, s0, *, nseq, nb, add_pos,
               mod_base, mod_per_seq):
    n = nseq * nb * TOKEN_BLOCK
    rev = lambda s_, i: (s_ * nb + nb - 1 - i, 0)
    tok = lambda w: pl.BlockSpec((TOKEN_BLOCK, w), rev)
    full = lambda a: pl.BlockSpec(a.shape, lambda s_, i: (0,) * a.ndim)
    st_spec = pl.BlockSpec((1, GLA_HEADS, GLA_DK, GLA_DV), lambda s_, i: (s_, 0, 0, 0))
    kern = functools.partial(_mixer_bwd_kernel, add_pos, mod_base, mod_per_seq, nb)
    return pl.pallas_call(
        kern,
        grid=(nseq, nb),
        in_specs=[tok(D_MODEL), full(e_tab), full(mod), tok(1024), tok(GLA_WIDTH), tok(QK_W),
                  tok(SGU_WIDTH), tok(GLA_WIDTH), full(gnw), full(wout), full(n2), full(rw_t),
                  st_spec],
        out_specs=[tok(D_MODEL),
                   pl.BlockSpec((TOKEN_BLOCK * TILES_PER_TOKEN, LANES), rev),
                   pl.BlockSpec((1, N_EXPERTS, TOKEN_BLOCK), lambda s_, i: (s_ * nb + nb - 1 - i, 0, 0)),
                   st_spec],
        out_shape=[jax.ShapeDtypeStruct((n, D_MODEL), F32),
                   jax.ShapeDtypeStruct((n * TILES_PER_TOKEN, LANES), F32),
                   jax.ShapeDtypeStruct((n // TOKEN_BLOCK, N_EXPERTS, TOKEN_BLOCK), F32),
                   jax.ShapeDtypeStruct((nseq, GLA_HEADS, GLA_DK, GLA_DV), F32)],
        scratch_shapes=[pltpu.VMEM((2, 2 * GLA_DV, 2 * GLA_DK), F32)],
        compiler_params=_compiler_params(("arbitrary", "arbitrary")),
        name="mixer_bwd",
    )(x, e_tab, mod, qkv, g, lab, s, of, gnw, wout, n2, rw_t, s0)


def _route_kernel(n_tok, cap, probs_ref, idx_ref, gate_ref, xs_ref, ps_ref):
    n_blk = n_tok // TOKEN_BLOCK
    n_chunk = n_tok // LANES
    probs = jnp.concatenate([probs_ref[b] for b in range(n_blk)], axis=1)
    bits = lax.bitcast_convert_type(probs, I32)
    capf = jnp.float32(cap)

    def count(mask):
        return jnp.sum(mask.astype(F32), axis=1, keepdims=True)

    def thr_step(_, lohi):
        lo, hi = lohi
        mid = lo + ((hi - lo + 1) >> 1)
        ok = count(bits >= mid) >= capf
        return jnp.where(ok, mid, lo), jnp.where(ok, hi, mid - 1)

    lo0 = jnp.zeros((N_EXPERTS, 1), I32)
    hi0 = jnp.full((N_EXPERTS, 1), 0x3F800000, I32)
    thr, _ = lax.fori_loop(0, 31, thr_step, (lo0, hi0))
    gt = bits > thr
    eq = bits == thr
    need = capf - count(gt)
    tok = lax.broadcasted_iota(I32, (N_EXPERTS, n_tok), 1)

    def tie_step(_, lohi):
        lo, hi = lohi
        mid = (lo + hi) >> 1
        ok = count(eq & (tok <= mid)) >= need
        return jnp.where(ok, lo, mid + 1), jnp.where(ok, mid, hi)

    n_bits = max(1, (n_tok - 1).bit_length())
    cut, _ = lax.fori_loop(0, n_bits, tie_step,
                           (jnp.zeros((N_EXPERTS, 1), I32), jnp.full((N_EXPERTS, 1), n_tok - 1, I32)))
    sel = (gt | (eq & (tok <= cut))).astype(F32)

    xs_ref[...] = jnp.concatenate([sel[:, c * LANES:(c + 1) * LANES] for c in range(n_chunk)], axis=0)
    ps_ref[...] = jnp.concatenate([probs[:, c * LANES:(c + 1) * LANES] for c in range(n_chunk)], axis=0)

    li = lax.broadcasted_iota(I32, (LANES, LANES), 0)
    lj = lax.broadcasted_iota(I32, (LANES, LANES), 1)
    upper = (li <= lj).astype(F32)
    ci = lax.broadcasted_iota(I32, (n_chunk, n_chunk), 0)
    cj = lax.broadcasted_iota(I32, (n_chunk, n_chunk), 1)
    lower = (cj <= ci).astype(F32)
    slot = lax.broadcasted_iota(I32, (1, cap), 1).astype(F32)
    chunk_id = lax.broadcasted_iota(I32, (n_chunk, cap), 0).astype(F32)
    lane_id = lax.broadcasted_iota(I32, (LANES, cap), 0).astype(F32)
    reps = cap // LANES

    def per_expert(e, _):
        x = xs_ref[pl.ds(e, n_chunk, stride=N_EXPERTS), :]
        pe = ps_ref[pl.ds(e, n_chunk, stride=N_EXPERTS), :]
        ploc = _dot(x, upper)
        tot = jnp.broadcast_to(ploc[:, LANES - 1:LANES], (n_chunk, LANES))
        cum = _dot(lower, tot)
        cum_w = jnp.concatenate([cum] * reps, axis=1)
        base_w = jnp.concatenate([cum - tot] * reps, axis=1)
        chunk_of = jnp.sum((cum_w <= slot).astype(F32), axis=0, keepdims=True)
        onehot = chunk_id == chunk_of
        local = slot - jnp.sum(jnp.where(onehot, base_w, 0.0), axis=0, keepdims=True)
        pref = _dot_tn(ploc, onehot.astype(F32))
        lane_of = jnp.sum((pref <= local).astype(F32), axis=0, keepdims=True)
        idx_ref[pl.ds(e, 1), :] = (chunk_of * LANES + lane_of).astype(I32)
        pg = _dot_f32(pe, onehot.astype(F32), (((0,), (0,)), ((), ())))
        gate_ref[pl.ds(e, 1), :] = jnp.sum(jnp.where(lane_id == lane_of, pg, 0.0), axis=0, keepdims=True)
        return 0

    lax.fori_loop(0, N_EXPERTS, per_expert, 0)


def _route(probs, n_tok, cap):
    return pl.pallas_call(
        functools.partial(_route_kernel, n_tok, cap),
        out_shape=[jax.ShapeDtypeStruct((N_EXPERTS, cap), I32),
                   jax.ShapeDtypeStruct((N_EXPERTS, cap), F32)],
        scratch_shapes=[pltpu.VMEM((n_tok // LANES * N_EXPERTS, LANES), F32),
                        pltpu.VMEM((n_tok // LANES * N_EXPERTS, LANES), F32)],
        compiler_params=pltpu.CompilerParams(vmem_limit_bytes=VMEM_LIMIT),
        name="route_topk",
    )(probs)


def _expert_kernel(cap, n_f, idx_ref, h2t_ref, w1_ref, w3_ref, w2_ref, ye_ref,
                   xe_ref, x2_ref, acc_ref, sem):
    e = pl.program_id(0)
    f = pl.program_id(1)

    @pl.when(f == 0)
    def _():
        def issue(j, _):
            src = pl.multiple_of(idx_ref[e * cap + j] * TILES_PER_TOKEN, TILES_PER_TOKEN)
            dst = pl.multiple_of(j * TILES_PER_TOKEN, TILES_PER_TOKEN)
            pltpu.make_async_copy(h2t_ref.at[pl.ds(src, TILES_PER_TOKEN), :],
                                  xe_ref.at[pl.ds(dst, TILES_PER_TOKEN), :], sem).start()
            return 0

        lax.fori_loop(0, cap, issue, 0, unroll=8)
        pltpu.make_async_copy(h2t_ref.at[pl.ds(0, cap * TILES_PER_TOKEN), :], xe_ref, sem).wait()
        for s in range(TILES_PER_TOKEN):
            x2_ref[:, s * LANES:(s + 1) * LANES] = (
                xe_ref[pl.ds(s, cap, stride=TILES_PER_TOKEN), :].astype(BF16))

    x2 = x2_ref[...]
    a = jnp.dot(x2, w1_ref[0].astype(BF16), preferred_element_type=F32)
    b = jnp.dot(x2, w3_ref[0].astype(BF16), preferred_element_type=F32)
    hid = (jax.nn.silu(a) * b).astype(BF16)
    part = jnp.dot(hid, w2_ref[0].astype(BF16), preferred_element_type=F32)

    @pl.when(f == 0)
    def _():
        acc_ref[...] = part

    @pl.when(f > 0)
    def _():
        acc_ref[...] += part

    @pl.when(f == n_f - 1)
    def _():
        for s in range(TILES_PER_TOKEN):
            ye_ref[pl.ds(s, cap, stride=TILES_PER_TOKEN), :] = acc_ref[:, s * LANES:(s + 1) * LANES]


def _experts(idx_flat, h2t, w1, w3, w2, cap):
    n_f = D_EXPERT // EXPERT_F_BLOCK
    grid_spec = pltpu.PrefetchScalarGridSpec(
        num_scalar_prefetch=1,
        grid=(N_EXPERTS, n_f),
        in_specs=[pl.BlockSpec(memory_space=pl.ANY),
                  pl.BlockSpec((1, D_MODEL, EXPERT_F_BLOCK), lambda e, f, idx: (e, 0, f)),
                  pl.BlockSpec((1, D_MODEL, EXPERT_F_BLOCK), lambda e, f, idx: (e, 0, f)),
                  pl.BlockSpec((1, EXPERT_F_BLOCK, D_MODEL), lambda e, f, idx: (e, f, 0))],
        out_specs=pl.BlockSpec((cap * TILES_PER_TOKEN, LANES), lambda e, f, idx: (e, 0)),
        scratch_shapes=[pltpu.VMEM((cap * TILES_PER_TOKEN, LANES), F32),
                        pltpu.VMEM((cap, D_MODEL), BF16),
                        pltpu.VMEM((cap, D_MODEL), F32),
                        pltpu.SemaphoreType.DMA],
    )
    return pl.pallas_call(
        functools.partial(_expert_kernel, cap, n_f),
        grid_spec=grid_spec,
        out_shape=jax.ShapeDtypeStruct((N_EXPERTS * cap * TILES_PER_TOKEN, LANES), F32),
        compiler_params=_compiler_params(("arbitrary", "arbitrary")),
        name="expert_swiglu",
    )(idx_flat, h2t, w1, w3, w2)


COMBINE_BATCH = 8
ZERO_ROWS = 512


def _combine_kernel(cap, n_tok, idx_ref, gate_ref, ye_ref, out_ref, acc_ref, sem):
    e = pl.program_id(0)

    @pl.when(e == 0)
    def _():
        def zero(i, _):
            r = pl.multiple_of(i * ZERO_ROWS, ZERO_ROWS)
            acc_ref[pl.ds(r, ZERO_ROWS), :] = jnp.zeros((ZERO_ROWS, LANES), F32)
            return 0

        lax.fori_loop(0, n_tok * TILES_PER_TOKEN // ZERO_ROWS, zero, 0)

    def batch(jb, _):
        vals = []
        for u in range(COMBINE_BATCH):
            j = jb * COMBINE_BATCH + u
            t = pl.multiple_of(idx_ref[e * cap + j] * TILES_PER_TOKEN, TILES_PER_TOKEN)
            gate = gate_ref[e * cap + j]
            src = pl.multiple_of(j * TILES_PER_TOKEN, TILES_PER_TOKEN)
            vals.append((t, acc_ref[pl.ds(t, TILES_PER_TOKEN), :]
                         + ye_ref[pl.ds(src, TILES_PER_TOKEN), :] * gate))
        for t, val in vals:
            acc_ref[pl.ds(t, TILES_PER_TOKEN), :] = val
        return 0

    lax.fori_loop(0, cap // COMBINE_BATCH, batch, 0)

    @pl.when(e == N_EXPERTS - 1)
    def _():
        cp = pltpu.make_async_copy(acc_ref, out_ref, sem)
        cp.start()
        cp.wait()


def _combine(idx_flat, gate_flat, ye, cap, n_tok):
    grid_spec = pltpu.PrefetchScalarGridSpec(
        num_scalar_prefetch=2,
        grid=(N_EXPERTS,),
        in_specs=[pl.BlockSpec((cap * TILES_PER_TOKEN, LANES), lambda e, idx, gate: (e, 0))],
        out_specs=pl.BlockSpec(memory_space=pl.ANY),
        scratch_shapes=[pltpu.VMEM((n_tok * TILES_PER_TOKEN, LANES), F32),
                        pltpu.SemaphoreType.DMA],
    )
    return pl.pallas_call(
        functools.partial(_combine_kernel, cap, n_tok),
        grid_spec=grid_spec,
        out_shape=jax.ShapeDtypeStruct((n_tok * TILES_PER_TOKEN, LANES), F32),
        compiler_params=_compiler_params(("arbitrary",)),
        name="moe_combine",
    )(idx_flat, gate_flat, ye)


def _final_kernel(mod_base, mod_per_seq, nb, x1_ref, moe_ref, mod_ref, fw_ref, y_ref):
    row = mod_base + ((pl.program_id(0) // nb) if mod_per_seq else 0)
    moe = jnp.concatenate(
        [moe_ref[pl.ds(s, TOKEN_BLOCK, stride=TILES_PER_TOKEN), :] for s in range(TILES_PER_TOKEN)],
        axis=1)
    y_ref[...] = _rms(x1_ref[...] + _mod_row(mod_ref, row, 5) * moe, fw_ref[...])


def _final(x1, moe_t, mod, fw, *, nb, mod_base, mod_per_seq):
    n = x1.shape[0]
    return pl.pallas_call(
        functools.partial(_final_kernel, mod_base, mod_per_seq, nb),
        grid=(n // TOKEN_BLOCK,),
        in_specs=[pl.BlockSpec((TOKEN_BLOCK, D_MODEL), lambda i: (i, 0)),
                  pl.BlockSpec((TOKEN_BLOCK * TILES_PER_TOKEN, LANES), lambda i: (i, 0)),
                  pl.BlockSpec(mod.shape, lambda i: (0, 0)),
                  pl.BlockSpec(fw.shape, lambda i: (0, 0))],
        out_specs=pl.BlockSpec((TOKEN_BLOCK, D_MODEL), lambda i: (i, 0)),
        out_shape=jax.ShapeDtypeStruct((n, D_MODEL), F32),
        compiler_params=_compiler_params(("arbitrary",)),
        name="final_norm",
    )(x1, moe_t, mod, fw)


def _trunk_and_norm(x, e_tab, mod, s0_f, s0_b, prm, *, nseq, seq_len, add_pos, mod_base, mod_per_seq):
    nb = seq_len // TOKEN_BLOCK
    n_tok = nseq * seq_len
    cap = EC_CAPACITY_FACTOR * n_tok // N_EXPERTS
    kw = dict(nseq=nseq, nb=nb, add_pos=add_pos, mod_base=mod_base, mod_per_seq=mod_per_seq)
    x2d = x.reshape(n_tok, D_MODEL)
    qkv, g, lab, s, of, sfin_f = _mixer_fwd(
        x2d, e_tab, mod, prm["n1"], prm["win"], prm["wa"], prm["ba"], prm["snw"], prm["sws"],
        prm["sbs"], s0_f, **kw)
    x1, h2t, probs, sfin_b = _mixer_bwd(
        x2d, e_tab, mod, qkv, g, lab, s, of, prm["gnw"], prm["wout"], prm["n2"], prm["rw_t"],
        s0_b, **kw)
    idx, gates = _route(probs, n_tok, cap)
    idx_flat = idx.reshape(-1)
    ye = _experts(idx_flat, h2t, prm["w1"], prm["w3"], prm["w2"], cap)
    moe_t = _combine(idx_flat, gates.reshape(-1), ye, cap, n_tok)
    y = _final(x1, moe_t, mod, prm["fw"], nb=nb, mod_base=mod_base, mod_per_seq=mod_per_seq)
    return y.reshape(nseq, seq_len, D_MODEL), sfin_f, sfin_b


def kernel(x_prompt, x_sample, state_gla_fwd, state_gla_bwd, c, c_ctx, ada_w, ada_b, norm1_w, w_in, gla_wa2_f, gla_ba_f, gla_wa2_b, gla_ba_b, gla_norm_w, sgu_norm_w, sgu_ws, sgu_bs, w_out, norm2_w, router_w, exp_w1, exp_w3, exp_w2, final_norm_w):
    assert ada_w.shape[0] == 1, "single trunk layer"
    batch, seq, _ = x_prompt.shape
    dec_batch, dec_seq, _ = x_sample.shape

    w = w_in[0]
    off_af = 2 * QK_W + 2 * GLA_WIDTH
    off_u = off_af + 2 * GLA_LOWRANK
    win = jnp.concatenate(
        [w[:, :off_af], w[:, off_u:], w[:, off_af:off_u],
         jnp.zeros((D_MODEL, P_WIDTH - w.shape[1]), w.dtype)], axis=1).astype(BF16)
    wa = jnp.zeros((P_WIDTH - P_A, 2 * QK_W), F32)
    wa = wa.at[0:GLA_LOWRANK, 0:QK_W].set(gla_wa2_f[0])
    wa = wa.at[GLA_LOWRANK:2 * GLA_LOWRANK, QK_W:].set(gla_wa2_b[0]).astype(BF16)
    prm = dict(
        n1=norm1_w, win=win, wa=wa,
        ba=jnp.concatenate([gla_ba_f[0], gla_ba_b[0]])[None, :],
        snw=sgu_norm_w, sws=sgu_ws[0].astype(BF16),
        sbs=jnp.broadcast_to(sgu_bs[0][:, :, None], (SGU_GROUPS, SGU_CHUNK, SGU_CH)),
        gnw=gla_norm_w, wout=w_out[0].astype(BF16), n2=norm2_w, rw_t=router_w[0].T,
        w1=exp_w1[0], w3=exp_w3[0], w2=exp_w2[0], fw=final_norm_w[None, :])

    cvec = jnp.concatenate([c_ctx[None, :], c, jnp.zeros((SUBLANES - 1 - dec_batch, D_MODEL), F32)])
    mod = _modulation(cvec, ada_w[0], ada_b)
    e_tab = _pos_table()

    zero_state = jnp.zeros((batch, GLA_HEADS, GLA_DK, GLA_DV), F32)
    y_prompt, sf, sb = _trunk_and_norm(
        x_prompt, e_tab, mod, zero_state, zero_state, prm,
        nseq=batch, seq_len=seq, add_pos=False, mod_base=0, mod_per_seq=False)
    y_sample, _, _ = _trunk_and_norm(
        x_sample, e_tab, mod, state_gla_fwd[:, 0], state_gla_bwd[:, 0], prm,
        nseq=dec_batch, seq_len=dec_seq, add_pos=True, mod_base=1, mod_per_seq=True)
    return (y_prompt, y_sample, sf[:, None], sb[:, None])
```

```python
import functools
import math

import jax
import jax.numpy as jnp
from jax import lax
from jax.experimental import pallas as pl
from jax.experimental.pallas import tpu as pltpu

F32 = jnp.float32
BF16 = jnp.bfloat16
I32 = jnp.int32

D_MODEL = 1024
GRID_W = 64
GLA_HEADS = 4
GLA_DK = 64
GLA_DV = 128
GLA_WIDTH = GLA_HEADS * GLA_DV
QK_W = GLA_HEADS * GLA_DK
GLA_LOWRANK = 16
GLA_GATE_NORM = 16.0
GLA_CHUNK = 64
SGU_WIDTH = 512
SGU_GROUPS = 4
SGU_CH = 128
SGU_CHUNK = 128
N_EXPERTS = 16
EC_CAPACITY_FACTOR = 2
D_EXPERT = 2048
EPS = 1e-6

SUBLANES = 8
LANES = 128
TILES_PER_TOKEN = D_MODEL // LANES

TOKEN_BLOCK = 256
P_Q, P_K, P_V, P_G, P_U, P_SV, P_A = 0, 256, 512, 1024, 1536, 2048, 2560
P_WIDTH = 2688
EXPERT_F_BLOCK = 512
EXPERT_N_BLOCK = 256
N_HID_STEPS = D_EXPERT // EXPERT_F_BLOCK
EXPERT_STEPS = N_HID_STEPS + D_MODEL // EXPERT_N_BLOCK
VMEM_LIMIT = 56 * 1024 * 1024


def _dot(a, b):
    return jnp.dot(a.astype(BF16), b.astype(BF16), preferred_element_type=F32)


def _dot_nt(a, b):
    return lax.dot_general(a.astype(BF16), b.astype(BF16), (((1,), (1,)), ((), ())),
                           preferred_element_type=F32)


def _dot_tn(a, b):
    return lax.dot_general(a.astype(BF16), b.astype(BF16), (((0,), (0,)), ((), ())),
                           preferred_element_type=F32)


def _dot_f32(a, b, dims=(((1,), (0,)), ((), ()))):
    return lax.dot_general(a, b, dims, precision=lax.Precision.HIGHEST, preferred_element_type=F32)


def _rms(x, w):
    return x * lax.rsqrt(jnp.mean(x * x, axis=-1, keepdims=True) + EPS) * w


def _compiler_params(sem):
    return pltpu.CompilerParams(dimension_semantics=sem, vmem_limit_bytes=VMEM_LIMIT)


def _mod_kernel(c_ref, w_ref, b_ref, o_ref):
    o_ref[...] = _dot(jax.nn.silu(c_ref[...]), w_ref[...]) + b_ref[...]


def _modulation(cvec, ada_w, ada_b):
    n = ada_w.shape[1]
    bn = 1536
    return pl.pallas_call(
        _mod_kernel,
        grid=(n // bn,),
        in_specs=[pl.BlockSpec((SUBLANES, D_MODEL), lambda j: (0, 0)),
                  pl.BlockSpec((D_MODEL, bn), lambda j: (0, j)),
                  pl.BlockSpec((1, bn), lambda j: (0, j))],
        out_specs=pl.BlockSpec((SUBLANES, bn), lambda j: (0, j)),
        out_shape=jax.ShapeDtypeStruct((SUBLANES, n), F32),
        compiler_params=_compiler_params(("arbitrary",)),
        name="adaln_mod",
    )(cvec, ada_w, ada_b)


def _pos_kernel(o_ref):
    nf = D_MODEL // 4
    p = lax.broadcasted_iota(I32, (GRID_W, nf), 0).astype(F32)
    i = lax.broadcasted_iota(I32, (GRID_W, nf), 1).astype(F32)
    omega = jnp.exp(i * (-math.log(10000.0) / nf))
    a = p * omega
    o_ref[:, 0:nf] = jnp.sin(a)
    o_ref[:, nf:2 * nf] = jnp.cos(a)


def _pos_table():
    return pl.pallas_call(
        _pos_kernel,
        out_shape=jax.ShapeDtypeStruct((GRID_W, D_MODEL // 2), F32),
        name="sincos_table",
    )()


def _load_xin(x_ref, e_ref, blk, add_pos):
    x = x_ref[...]
    if not add_pos:
        return x
    half = D_MODEL // 2
    e_all = e_ref[...]
    rows = []
    for j in range(TOKEN_BLOCK // GRID_W):
        xj = x[j * GRID_W:(j + 1) * GRID_W]
        e_row = e_ref[pl.ds(blk * (TOKEN_BLOCK // GRID_W) + j, 1), :]
        rows.append(jnp.concatenate([xj[:, 0:half] + e_row, xj[:, half:] + e_all], axis=1))
    return jnp.concatenate(rows, axis=0)


def _chunk_masks():
    r = lax.broadcasted_iota(I32, (TOKEN_BLOCK, TOKEN_BLOCK), 0)
    c = lax.broadcasted_iota(I32, (TOKEN_BLOCK, TOKEN_BLOCK), 1)
    same = (r // GLA_CHUNK) == (c // GLA_CHUNK)
    return same & (c <= r), same & (c >= r)


def _gla_direction(q, k, v, cum, fwd, att_mask, st_ref):
    qe = q * jnp.exp(cum)
    ke = k * jnp.exp(-cum)
    lane = lax.broadcasted_iota(I32, (1, LANES), 1)
    o_intra = []
    for pair in range(2):
        qp = qe[:, pair * LANES:(pair + 1) * LANES]
        kp = ke[:, pair * LANES:(pair + 1) * LANES]
        for hh in range(2):
            qm = jnp.where((lane // GLA_DK) == hh, qp, 0.0)
            att = jnp.where(att_mask, _dot_nt(qm, kp), 0.0)
            head = 2 * pair + hh
            o_intra.append(_dot(att, v[:, head * GLA_DV:(head + 1) * GLA_DV]))
    o_intra = jnp.concatenate(o_intra, axis=1)

    er = lax.broadcasted_iota(I32, (2 * GLA_DV, 2 * GLA_DK), 0)
    dc = lax.broadcasted_iota(I32, (2 * GLA_DV, 2 * GLA_DK), 1)
    same_head = (er // GLA_DV) == (dc // GLA_DK)
    n_chunks = TOKEN_BLOCK // GLA_CHUNK
    o_inter = [None] * n_chunks
    for c in (range(n_chunks) if fwd else reversed(range(n_chunks))):
        r0 = c * GLA_CHUNK
        rows = slice(r0, r0 + GLA_CHUNK)
        last = cum[r0 + GLA_CHUNK - 1:r0 + GLA_CHUNK] if fwd else cum[r0:r0 + 1]
        kd = k[rows] * jnp.exp(last - cum[rows])
        dec = jnp.exp(last)
        parts = []
        for pair in range(2):
            dl = slice(pair * LANES, (pair + 1) * LANES)
            st = st_ref[pair]
            parts.append(_dot_nt(qe[rows, dl], st))
            ds_t = _dot_tn(v[rows, pair * 2 * GLA_DV:(pair + 1) * 2 * GLA_DV], kd[:, dl])
            st_ref[pair] = dec[:, dl] * st + jnp.where(same_head, ds_t, 0.0)
        o_inter[c] = jnp.concatenate(parts, axis=1)
    return o_intra + jnp.concatenate(o_inter, axis=0)


def _load_state(s0_ref, st_ref):
    zero = jnp.zeros((GLA_DV, GLA_DK), F32)
    for pair in range(2):
        a = s0_ref[0, 2 * pair].T
        b = s0_ref[0, 2 * pair + 1].T
        st_ref[pair] = jnp.concatenate(
            [jnp.concatenate([a, zero], axis=1), jnp.concatenate([zero, b], axis=1)], axis=0)


def _store_state(st_ref, sfin_ref):
    for pair in range(2):
        st = st_ref[pair]
        sfin_ref[0, 2 * pair] = st[0:GLA_DV, 0:GLA_DK].T
        sfin_ref[0, 2 * pair + 1] = st[GLA_DV:2 * GLA_DV, GLA_DK:2 * GLA_DK].T


def _mod_row(mod_ref, row, part):
    return mod_ref[pl.ds(row, 1), part * D_MODEL:(part + 1) * D_MODEL]


def _mixer_fwd_kernel(add_pos, mod_base, mod_per_seq, nb,
                      x_ref, e_ref, mod_ref, n1_ref, win_ref, wa_ref, ba_ref,
                      snw_ref, sws_ref, sbs_ref, s0_ref,
                      qkv_ref, g_ref, lab_ref, s_ref, of_ref, sfin_ref,
                      st_ref):
    seq = pl.program_id(0)
    blk = pl.program_id(1)
    row = mod_base + (seq if mod_per_seq else 0)

    @pl.when(blk == 0)
    def _():
        _load_state(s0_ref, st_ref)

    xin = _load_xin(x_ref, e_ref, blk, add_pos)
    h = _rms(xin, n1_ref[...]) * (1.0 + _mod_row(mod_ref, row, 1)) + _mod_row(mod_ref, row, 0)
    p = _dot(h, win_ref[...])
    q = p[:, P_Q:P_K] * (GLA_DK ** -0.5)
    k = p[:, P_K:P_V]
    v = p[:, P_V:P_G]
    z = _dot(p[:, P_A:P_WIDTH], wa_ref[...]) + ba_ref[...]
    la = (jnp.minimum(z, 0.0) - jnp.log1p(jnp.exp(-jnp.abs(z)))) * (1.0 / GLA_GATE_NORM)

    ug = jax.nn.gelu(p[:, P_U:P_SV])
    vg = jax.nn.gelu(p[:, P_SV:P_A])
    s_cols = []
    for gi in range(SGU_GROUPS):
        cols = slice(gi * SGU_CH, (gi + 1) * SGU_CH)
        vn = _rms(vg[:, cols], snw_ref[:, cols])
        rhs = jnp.concatenate([vn[0:SGU_CHUNK], vn[SGU_CHUNK:2 * SGU_CHUNK]], axis=1)
        vm = _dot(sws_ref[gi], rhs) + jnp.concatenate([sbs_ref[gi], sbs_ref[gi]], axis=1)
        vm = jnp.concatenate([vm[:, 0:SGU_CH], vm[:, SGU_CH:2 * SGU_CH]], axis=0)
        s_cols.append(ug[:, cols] * vm)
    s_ref[...] = jnp.concatenate(s_cols, axis=1)

    lo_mask, _ = _chunk_masks()
    cum = _dot_f32(lo_mask.astype(F32), la[:, 0:QK_W])
    of_ref[...] = _gla_direction(q, k, v, cum, True, lo_mask, st_ref)
    qkv_ref[...] = jnp.concatenate([q, k, v], axis=1)
    g_ref[...] = p[:, P_G:P_U]
    lab_ref[...] = la[:, QK_W:2 * QK_W]

    @pl.when(blk == nb - 1)
    def _():
        _store_state(st_ref, sfin_ref)


def _mixer_fwd(x, e_tab, mod, n1, win, wa, ba, snw, sws, sbs, s0, *, nseq, nb, add_pos,
               mod_base, mod_per_seq):
    n = nseq * nb * TOKEN_BLOCK
    tok = lambda w: pl.BlockSpec((TOKEN_BLOCK, w), lambda s, i: (s * nb + i, 0))
    full = lambda a: pl.BlockSpec(a.shape, lambda s, i: (0,) * a.ndim)
    st_spec = pl.BlockSpec((1, GLA_HEADS, GLA_DK, GLA_DV), lambda s, i: (s, 0, 0, 0))
    kern = functools.partial(_mixer_fwd_kernel, add_pos, mod_base, mod_per_seq, nb)
    return pl.pallas_call(
        kern,
        grid=(nseq, nb),
        in_specs=[tok(D_MODEL), full(e_tab), full(mod), full(n1), full(win), full(wa), full(ba),
                  full(snw), full(sws), full(sbs), st_spec],
        out_specs=[tok(1024), tok(GLA_WIDTH), tok(QK_W), tok(SGU_WIDTH), tok(GLA_WIDTH), st_spec],
        out_shape=[jax.ShapeDtypeStruct((n, 1024), F32),
                   jax.ShapeDtypeStruct((n, GLA_WIDTH), F32),
                   jax.ShapeDtypeStruct((n, QK_W), F32),
                   jax.ShapeDtypeStruct((n, SGU_WIDTH), F32),
                   jax.ShapeDtypeStruct((n, GLA_WIDTH), F32),
                   jax.ShapeDtypeStruct((nseq, GLA_HEADS, GLA_DK, GLA_DV), F32)],
        scratch_shapes=[pltpu.VMEM((2, 2 * GLA_DV, 2 * GLA_DK), F32)],
        compiler_params=_compiler_params(("arbitrary", "arbitrary")),
        name="mixer_fwd",
    )(x, e_tab, mod, n1, win, wa, ba, snw, sws, sbs, s0)


def _mixer_bwd_kernel(add_pos, mod_base, mod_per_seq, nb,
                      x_ref, e_ref, mod_ref, qkv_ref, g_ref, lab_ref, s_ref, of_ref,
                      gnw_ref, wout_ref, n2_ref, rw_ref, s0_ref,
                      x1_ref, h2t_ref, probs_ref, sfin_ref,
                      st_ref):
    seq = pl.program_id(0)
    step = pl.program_id(1)
    blk = nb - 1 - step
    row = mod_base + (seq if mod_per_seq else 0)

    @pl.when(step == 0)
    def _():
        _load_state(s0_ref, st_ref)

    qkv = qkv_ref[...]
    q, k, v = qkv[:, 0:QK_W], qkv[:, QK_W:2 * QK_W], qkv[:, 2 * QK_W:]
    _, hi_mask = _chunk_masks()
    cum = _dot_f32(hi_mask.astype(F32), lab_ref[...])
    o = of_ref[...] + _gla_direction(q, k, v, cum, False, hi_mask, st_ref)
    g = g_ref[...]
    cols = []
    for head in range(GLA_HEADS):
        hs = slice(head * GLA_DV, (head + 1) * GLA_DV)
        cols.append(_rms(o[:, hs], gnw_ref[...]) * jax.nn.silu(g[:, hs]))
    cols.append(s_ref[...])
    y = _dot(jnp.concatenate(cols, axis=1), wout_ref[...])

    xin = _load_xin(x_ref, e_ref, blk, add_pos)
    x1 = xin + _mod_row(mod_ref, row, 2) * y
    x1_ref[...] = x1
    h2 = _rms(x1, n2_ref[...]) * (1.0 + _mod_row(mod_ref, row, 4)) + _mod_row(mod_ref, row, 3)
    for s in range(TILES_PER_TOKEN):
        h2t_ref[pl.ds(s, TOKEN_BLOCK, stride=TILES_PER_TOKEN), :] = h2[:, s * LANES:(s + 1) * LANES]

    logits = _dot_f32(rw_ref[...], h2, (((1,), (1,)), ((), ())))
    m = jnp.max(logits, axis=0, keepdims=True)
    ex = jnp.exp(logits - m)
    probs_ref[0] = ex / jnp.sum(ex, axis=0, keepdims=True)

    @pl.when(step == nb - 1)
    def _():
        _store_state(st_ref, sfin_ref)


def _mixer_bwd(x, e_tab, mod, qkv, g, lab, s, of, gnw, wout, n2, rw_t, s0, *, nseq, nb, add_pos,
               mod_base, mod_per_seq):
    n = nseq * nb * TOKEN_BLOCK
    rev = lambda s_, i: (s_ * nb + nb - 1 - i, 0)
    tok = lambda w: pl.BlockSpec((TOKEN_BLOCK, w), rev)
    full = lambda a: pl.BlockSpec(a.shape, lambda s_, i: (0,) * a.ndim)
    st_spec = pl.BlockSpec((1, GLA_HEADS, GLA_DK, GLA_DV), lambda s_, i: (s_, 0, 0, 0))
    kern = functools.partial(_mixer_bwd_kernel, add_pos, mod_base, mod_per_seq, nb)
    return pl.pallas_call(
        kern,
        grid=(nseq, nb),
        in_specs=[tok(D_MODEL), full(e_tab), full(mod), tok(1024), tok(GLA_WIDTH), tok(QK_W),
                  tok(SGU_WIDTH), tok(GLA_WIDTH), full(gnw), full(wout), full(n2), full(rw_t),
                  st_spec],
        out_specs=[tok(D_MODEL),
                   pl.BlockSpec((TOKEN_BLOCK * TILES_PER_TOKEN, LANES), rev),
                   pl.BlockSpec((1, N_EXPERTS, TOKEN_BLOCK), lambda s_, i: (s_ * nb + nb - 1 - i, 0, 0)),
                   st_spec],
        out_shape=[jax.ShapeDtypeStruct((n, D_MODEL), F32),
                   jax.ShapeDtypeStruct((n * TILES_PER_TOKEN, LANES), F32),
                   jax.ShapeDtypeStruct((n // TOKEN_BLOCK, N_EXPERTS, TOKEN_BLOCK), F32),
                   jax.ShapeDtypeStruct((nseq, GLA_HEADS, GLA_DK, GLA_DV), F32)],
        scratch_shapes=[pltpu.VMEM((2, 2 * GLA_DV, 2 * GLA_DK), F32)],
        compiler_params=_compiler_params(("arbitrary", "arbitrary")),
        name="mixer_bwd",
    )(x, e_tab, mod, qkv, g, lab, s, of, gnw, wout, n2, rw_t, s0)


def _route_kernel(n_tok, cap, probs_ref, idx_ref, gate_ref, xs_ref, ps_ref):
    n_blk = n_tok // TOKEN_BLOCK
    n_chunk = n_tok // LANES
    probs = jnp.concatenate([probs_ref[b] for b in range(n_blk)], axis=1)
    capf = jnp.float32(cap)

    def count(mask):
        return jnp.sum(mask.astype(F32), axis=1, keepdims=True)

    def as_f32(bits):
        return lax.bitcast_convert_type(bits, F32)

    def thr_step(_, lohi):
        lo, hi = lohi
        mid = lo + ((hi - lo + 1) >> 1)
        ok = count(probs >= as_f32(mid)) >= capf
        return jnp.where(ok, mid, lo), jnp.where(ok, hi, mid - 1)

    lo0 = jnp.zeros((N_EXPERTS, 1), I32)
    hi0 = jnp.full((N_EXPERTS, 1), 0x3F800000, I32)
    thr, _ = lax.fori_loop(0, 31, thr_step, (lo0, hi0))
    gt = probs >= as_f32(thr + 1)
    eq = (probs >= as_f32(thr)) & jnp.logical_not(gt)
    need = capf - count(gt)
    tok = lax.broadcasted_iota(I32, (N_EXPERTS, n_tok), 1)

    def tie_step(_, lohi):
        lo, hi = lohi
        mid = (lo + hi) >> 1
        ok = count(eq & (tok <= mid)) >= need
        return jnp.where(ok, lo, mid + 1), jnp.where(ok, mid, hi)

    n_bits = max(1, (n_tok - 1).bit_length())
    cut, _ = lax.fori_loop(0, n_bits, tie_step,
                           (jnp.zeros((N_EXPERTS, 1), I32), jnp.full((N_EXPERTS, 1), n_tok - 1, I32)))
    sel = (gt | (eq & (tok <= cut))).astype(F32)

    xs_ref[...] = jnp.concatenate([sel[:, c * LANES:(c + 1) * LANES] for c in range(n_chunk)], axis=0)
    ps_ref[...] = jnp.concatenate([probs[:, c * LANES:(c + 1) * LANES] for c in range(n_chunk)], axis=0)

    li = lax.broadcasted_iota(I32, (LANES, LANES), 0)
    lj = lax.broadcasted_iota(I32, (LANES, LANES), 1)
    upper = (li <= lj).astype(F32)
    ci = lax.broadcasted_iota(I32, (n_chunk, n_chunk), 0)
    cj = lax.broadcasted_iota(I32, (n_chunk, n_chunk), 1)
    lower = (cj <= ci).astype(F32)
    slot = lax.broadcasted_iota(I32, (1, cap), 1).astype(F32)
    chunk_id = lax.broadcasted_iota(I32, (n_chunk, cap), 0).astype(F32)
    lane_id = lax.broadcasted_iota(I32, (LANES, cap), 0).astype(F32)
    reps = cap // LANES

    def per_expert(e, _):
        x = xs_ref[pl.ds(e, n_chunk, stride=N_EXPERTS), :]
        pe = ps_ref[pl.ds(e, n_chunk, stride=N_EXPERTS), :]
        ploc = _dot(x, upper)
        tot = jnp.broadcast_to(ploc[:, LANES - 1:LANES], (n_chunk, LANES))
        cum = _dot(lower, tot)
        cum_w = jnp.concatenate([cum] * reps, axis=1)
        base_w = jnp.concatenate([cum - tot] * reps, axis=1)
        chunk_of = jnp.sum((cum_w <= slot).astype(F32), axis=0, keepdims=True)
        onehot = chunk_id == chunk_of
        local = slot - jnp.sum(jnp.where(onehot, base_w, 0.0), axis=0, keepdims=True)
        pref = _dot_tn(ploc, onehot.astype(F32))
        lane_of = jnp.sum((pref <= local).astype(F32), axis=0, keepdims=True)
        idx_ref[pl.ds(e, 1), :] = (chunk_of * LANES + lane_of).astype(I32)
        pg = _dot_f32(pe, onehot.astype(F32), (((0,), (0,)), ((), ())))
        gate_ref[pl.ds(e, 1), :] = jnp.sum(jnp.where(lane_id == lane_of, pg, 0.0), axis=0, keepdims=True)
        return 0

    lax.fori_loop(0, N_EXPERTS, per_expert, 0)


def _route(probs, n_tok, cap):
    return pl.pallas_call(
        functools.partial(_route_kernel, n_tok, cap),
        out_shape=[jax.ShapeDtypeStruct((N_EXPERTS, cap), I32),
                   jax.ShapeDtypeStruct((N_EXPERTS, cap), F32)],
        scratch_shapes=[pltpu.VMEM((n_tok // LANES * N_EXPERTS, LANES), F32),
                        pltpu.VMEM((n_tok // LANES * N_EXPERTS, LANES), F32)],
        compiler_params=pltpu.CompilerParams(vmem_limit_bytes=VMEM_LIMIT),
        name="route_topk",
    )(probs)


def _expert_kernel(cap, idx_ref, h2t_ref, w1_ref, w3_ref, w2_ref, ye_ref,
                   xe_ref, x2_ref, hid_ref, sem):
    e = pl.program_id(0)
    f = pl.program_id(1)
    slot = e % 2
    rows_per_step = cap // EXPERT_STEPS

    def gather(expert, buf, first, count):
        def issue(j, _):
            src = pl.multiple_of(idx_ref[expert * cap + first + j] * TILES_PER_TOKEN, TILES_PER_TOKEN)
            dst = pl.multiple_of((first + j) * TILES_PER_TOKEN, TILES_PER_TOKEN)
            pltpu.make_async_copy(h2t_ref.at[pl.ds(src, TILES_PER_TOKEN), :],
                                  xe_ref.at[buf, pl.ds(dst, TILES_PER_TOKEN), :], sem.at[buf]).start()
            return 0

        lax.fori_loop(0, count, issue, 0, unroll=8)

    @pl.when((e == 0) & (f == 0))
    def _():
        gather(0, 0, 0, cap)

    @pl.when(f == 0)
    def _():
        pltpu.make_async_copy(h2t_ref.at[pl.ds(0, cap * TILES_PER_TOKEN), :], xe_ref.at[slot],
                              sem.at[slot]).wait()
        for s in range(TILES_PER_TOKEN):
            x2_ref[:, s * LANES:(s + 1) * LANES] = (
                xe_ref[slot, pl.ds(s, cap, stride=TILES_PER_TOKEN), :].astype(BF16))

    @pl.when(e < N_EXPERTS - 1)
    def _():
        gather(e + 1, 1 - slot, pl.multiple_of(f * rows_per_step, rows_per_step), rows_per_step)

    @pl.when(f < N_HID_STEPS)
    def _():
        x2 = x2_ref[...]
        a = jnp.dot(x2, w1_ref[0].astype(BF16), preferred_element_type=F32)
        b = jnp.dot(x2, w3_ref[0].astype(BF16), preferred_element_type=F32)
        hid_ref[f] = (jax.nn.silu(a) * b).astype(BF16)

    @pl.when(f >= N_HID_STEPS)
    def _():
        w2 = w2_ref[0].astype(BF16)
        out = jnp.dot(hid_ref[0], w2[0:EXPERT_F_BLOCK], preferred_element_type=F32)
        for kb in range(1, N_HID_STEPS):
            out += jnp.dot(hid_ref[kb], w2[kb * EXPERT_F_BLOCK:(kb + 1) * EXPERT_F_BLOCK],
                           preferred_element_type=F32)
        tile0 = (f - N_HID_STEPS) * (EXPERT_N_BLOCK // LANES)
        for i in range(EXPERT_N_BLOCK // LANES):
            ye_ref[pl.ds(tile0 + i, cap, stride=TILES_PER_TOKEN), :] = out[:, i * LANES:(i + 1) * LANES]


def _experts(idx_flat, h2t, w1, w3, w2, cap):
    hid_blk = lambda e, f, idx: (e, 0, jnp.minimum(f, N_HID_STEPS - 1))
    out_blk = lambda e, f, idx: (e, 0, jnp.maximum(f - N_HID_STEPS, 0))
    grid_spec = pltpu.PrefetchScalarGridSpec(
        num_scalar_prefetch=1,
        grid=(N_EXPERTS, EXPERT_STEPS),
        in_specs=[pl.BlockSpec(memory_space=pl.ANY),
                  pl.BlockSpec((1, D_MODEL, EXPERT_F_BLOCK), hid_blk),
                  pl.BlockSpec((1, D_MODEL, EXPERT_F_BLOCK), hid_blk),
                  pl.BlockSpec((1, D_EXPERT, EXPERT_N_BLOCK), out_blk)],
        out_specs=pl.BlockSpec((cap * TILES_PER_TOKEN, LANES), lambda e, f, idx: (e, 0)),
        scratch_shapes=[pltpu.VMEM((2, cap * TILES_PER_TOKEN, LANES), F32),
                        pltpu.VMEM((cap, D_MODEL), BF16),
                        pltpu.VMEM((N_HID_STEPS, cap, EXPERT_F_BLOCK), BF16),
                        pltpu.SemaphoreType.DMA((2,))],
    )
    return pl.pallas_call(
        functools.partial(_expert_kernel, cap),
        grid_spec=grid_spec,
        out_shape=jax.ShapeDtypeStruct((N_EXPERTS * cap * TILES_PER_TOKEN, LANES), F32),
        compiler_params=_compiler_params(("arbitrary", "arbitrary")),
        name="expert_swiglu",
    )(idx_flat, h2t, w1, w3, w2)


COMBINE_BATCH = 8
ZERO_ROWS = 512


def _combine_kernel(cap, n_tok, idx_ref, gate_ref, ye_ref, out_ref, acc_ref, sem):
    e = pl.program_id(0)

    @pl.when(e == 0)
    def _():
        def zero(i, _):
            r = pl.multiple_of(i * ZERO_ROWS, ZERO_ROWS)
            acc_ref[pl.ds(r, ZERO_ROWS), :] = jnp.zeros((ZERO_ROWS, LANES), F32)
            return 0

        lax.fori_loop(0, n_tok * TILES_PER_TOKEN // ZERO_ROWS, zero, 0)

    def batch(jb, _):
        vals = []
        for u in range(COMBINE_BATCH):
            j = jb * COMBINE_BATCH + u
            t = pl.multiple_of(idx_ref[e * cap + j] * TILES_PER_TOKEN, TILES_PER_TOKEN)
            gate = gate_ref[e * cap + j]
            src = pl.multiple_of(j * TILES_PER_TOKEN, TILES_PER_TOKEN)
            vals.append((t, acc_ref[pl.ds(t, TILES_PER_TOKEN), :]
                         + ye_ref[pl.ds(src, TILES_PER_TOKEN), :] * gate))
        for t, val in vals:
            acc_ref[pl.ds(t, TILES_PER_TOKEN), :] = val
        return 0

    lax.fori_loop(0, cap // COMBINE_BATCH, batch, 0)

    @pl.when(e == N_EXPERTS - 1)
    def _():
        cp = pltpu.make_async_copy(acc_ref, out_ref, sem)
        cp.start()
        cp.wait()


def _combine(idx_flat, gate_flat, ye, cap, n_tok):
    grid_spec = pltpu.PrefetchScalarGridSpec(
        num_scalar_prefetch=2,
        grid=(N_EXPERTS,),
        in_specs=[pl.BlockSpec((cap * TILES_PER_TOKEN, LANES), lambda e, idx, gate: (e, 0))],
        out_specs=pl.BlockSpec(memory_space=pl.ANY),
        scratch_shapes=[pltpu.VMEM((n_tok * TILES_PER_TOKEN, LANES), F32),
                        pltpu.SemaphoreType.DMA],
    )
    return pl.pallas_call(
        functools.partial(_combine_kernel, cap, n_tok),
        grid_spec=grid_spec,
        out_shape=jax.ShapeDtypeStruct((n_tok * TILES_PER_TOKEN, LANES), F32),
        compiler_params=_compiler_params(("arbitrary",)),
        name="moe_combine",
    )(idx_flat, gate_flat, ye)


def _final_kernel(mod_base, mod_per_seq, nb, x1_ref, moe_ref, mod_ref, fw_ref, y_ref):
    row = mod_base + ((pl.program_id(0) // nb) if mod_per_seq else 0)
    moe = jnp.concatenate(
        [moe_ref[pl.ds(s, TOKEN_BLOCK, stride=TILES_PER_TOKEN), :] for s in range(TILES_PER_TOKEN)],
        axis=1)
    y_ref[...] = _rms(x1_ref[...] + _mod_row(mod_ref, row, 5) * moe, fw_ref[...])


def _final(x1, moe_t, mod, fw, *, nb, mod_base, mod_per_seq):
    n = x1.shape[0]
    return pl.pallas_call(
        functools.partial(_final_kernel, mod_base, mod_per_seq, nb),
        grid=(n // TOKEN_BLOCK,),
        in_specs=[pl.BlockSpec((TOKEN_BLOCK, D_MODEL), lambda i: (i, 0)),
                  pl.BlockSpec((TOKEN_BLOCK * TILES_PER_TOKEN, LANES), lambda i: (i, 0)),
                  pl.BlockSpec(mod.shape, lambda i: (0, 0)),
                  pl.BlockSpec(fw.shape, lambda i: (0, 0))],
        out_specs=pl.BlockSpec((TOKEN_BLOCK, D_MODEL), lambda i: (i, 0)),
        out_shape=jax.ShapeDtypeStruct((n, D_MODEL), F32),
        compiler_params=_compiler_params(("arbitrary",)),
        name="final_norm",
    )(x1, moe_t, mod, fw)


def _trunk_and_norm(x, e_tab, mod, s0_f, s0_b, prm, *, nseq, seq_len, add_pos, mod_base, mod_per_seq):
    nb = seq_len // TOKEN_BLOCK
    n_tok = nseq * seq_len
    cap = EC_CAPACITY_FACTOR * n_tok // N_EXPERTS
    kw = dict(nseq=nseq, nb=nb, add_pos=add_pos, mod_base=mod_base, mod_per_seq=mod_per_seq)
    x2d = x.reshape(n_tok, D_MODEL)
    qkv, g, lab, s, of, sfin_f = _mixer_fwd(
        x2d, e_tab, mod, prm["n1"], prm["win"], prm["wa"], prm["ba"], prm["snw"], prm["sws"],
        prm["sbs"], s0_f, **kw)
    x1, h2t, probs, sfin_b = _mixer_bwd(
        x2d, e_tab, mod, qkv, g, lab, s, of, prm["gnw"], prm["wout"], prm["n2"], prm["rw_t"],
        s0_b, **kw)
    idx, gates = _route(probs, n_tok, cap)
    idx_flat = idx.reshape(-1)
    ye = _experts(idx_flat, h2t, prm["w1"], prm["w3"], prm["w2"], cap)
    moe_t = _combine(idx_flat, gates.reshape(-1), ye, cap, n_tok)
    y = _final(x1, moe_t, mod, prm["fw"], nb=nb, mod_base=mod_base, mod_per_seq=mod_per_seq)
    return y.reshape(nseq, seq_len, D_MODEL), sfin_f, sfin_b


def kernel(x_prompt, x_sample, state_gla_fwd, state_gla_bwd, c, c_ctx, ada_w, ada_b, norm1_w, w_in, gla_wa2_f, gla_ba_f, gla_wa2_b, gla_ba_b, gla_norm_w, sgu_norm_w, sgu_ws, sgu_bs, w_out, norm2_w, router_w, exp_w1, exp_w3, exp_w2, final_norm_w):
    assert ada_w.shape[0] == 1, "single trunk layer"
    batch, seq, _ = x_prompt.shape
    dec_batch, dec_seq, _ = x_sample.shape

    w = w_in[0]
    off_af = 2 * QK_W + 2 * GLA_WIDTH
    off_u = off_af + 2 * GLA_LOWRANK
    win = jnp.concatenate(
        [w[:, :off_af], w[:, off_u:], w[:, off_af:off_u],
         jnp.zeros((D_MODEL, P_WIDTH - w.shape[1]), w.dtype)], axis=1).astype(BF16)
    wa = jnp.zeros((P_WIDTH - P_A, 2 * QK_W), F32)
    wa = wa.at[0:GLA_LOWRANK, 0:QK_W].set(gla_wa2_f[0])
    wa = wa.at[GLA_LOWRANK:2 * GLA_LOWRANK, QK_W:].set(gla_wa2_b[0]).astype(BF16)
    prm = dict(
        n1=norm1_w, win=win, wa=wa,
        ba=jnp.concatenate([gla_ba_f[0], gla_ba_b[0]])[None, :],
        snw=sgu_norm_w, sws=sgu_ws[0].astype(BF16),
        sbs=jnp.broadcast_to(sgu_bs[0][:, :, None], (SGU_GROUPS, SGU_CHUNK, SGU_CH)),
        gnw=gla_norm_w, wout=w_out[0].astype(BF16), n2=norm2_w, rw_t=router_w[0].T,
        w1=exp_w1[0], w3=exp_w3[0], w2=exp_w2[0], fw=final_norm_w[None, :])

    cvec = jnp.concatenate([c_ctx[None, :], c, jnp.zeros((SUBLANES - 1 - dec_batch, D_MODEL), F32)])
    mod = _modulation(cvec, ada_w[0], ada_b)
    e_tab = _pos_table()

    zero_state = jnp.zeros((batch, GLA_HEADS, GLA_DK, GLA_DV), F32)
    y_prompt, sf, sb = _trunk_and_norm(
        x_prompt, e_tab, mod, zero_state, zero_state, prm,
        nseq=batch, seq_len=seq, add_pos=False, mod_base=0, mod_per_seq=False)
    y_sample, _, _ = _trunk_and_norm(
        x_sample, e_tab, mod, state_gla_fwd[:, 0], state_gla_bwd[:, 0], prm,
        nseq=dec_batch, seq_len=dec_seq, add_pos=True, mod_base=1, mod_per_seq=True)
    return (y_prompt, y_sample, sf[:, None], sb[:, None])
```

```python
import functools
import math

import jax
import jax.numpy as jnp
from jax import lax
from jax.experimental import pallas as pl
from jax.experimental.pallas import tpu as pltpu

F32 = jnp.float32
BF16 = jnp.bfloat16
I32 = jnp.int32

D_MODEL = 1024
GRID_W = 64
GLA_HEADS = 4
GLA_DK = 64
GLA_DV = 128
GLA_WIDTH = GLA_HEADS * GLA_DV
QK_W = GLA_HEADS * GLA_DK
GLA_LOWRANK = 16
GLA_GATE_NORM = 16.0
GLA_CHUNK = 64
SGU_WIDTH = 512
SGU_GROUPS = 4
SGU_CH = 128
SGU_CHUNK = 128
N_EXPERTS = 16
EC_CAPACITY_FACTOR = 2
D_EXPERT = 2048
EPS = 1e-6

SUBLANES = 8
LANES = 128
TILES_PER_TOKEN = D_MODEL // LANES

TOKEN_BLOCK = 256
P_Q, P_K, P_V, P_G, P_U, P_SV, P_A = 0, 256, 512, 1024, 1536, 2048, 2560
P_WIDTH = 2688
EXPERT_F_BLOCK = 512
EXPERT_N_BLOCK = 256
N_HID_STEPS = D_EXPERT // EXPERT_F_BLOCK
EXPERT_STEPS = N_HID_STEPS + D_MODEL // EXPERT_N_BLOCK
VMEM_LIMIT = 56 * 1024 * 1024


def _dot(a, b):
    return jnp.dot(a.astype(BF16), b.astype(BF16), preferred_element_type=F32)


def _dot_nt(a, b):
    return lax.dot_general(a.astype(BF16), b.astype(BF16), (((1,), (1,)), ((), ())),
                           preferred_element_type=F32)


def _dot_tn(a, b):
    return lax.dot_general(a.astype(BF16), b.astype(BF16), (((0,), (0,)), ((), ())),
                           preferred_element_type=F32)


def _dot_f32(a, b, dims=(((1,), (0,)), ((), ()))):
    return lax.dot_general(a, b, dims, precision=lax.Precision.HIGHEST, preferred_element_type=F32)


def _rms(x, w):
    return x * lax.rsqrt(jnp.mean(x * x, axis=-1, keepdims=True) + EPS) * w


def _compiler_params(sem):
    return pltpu.CompilerParams(dimension_semantics=sem, vmem_limit_bytes=VMEM_LIMIT)


def _mod_kernel(c_ref, w_ref, b_ref, o_ref):
    o_ref[...] = _dot(jax.nn.silu(c_ref[...]), w_ref[...]) + b_ref[...]


def _modulation(cvec, ada_w, ada_b):
    n = ada_w.shape[1]
    bn = 1536
    return pl.pallas_call(
        _mod_kernel,
        grid=(n // bn,),
        in_specs=[pl.BlockSpec((SUBLANES, D_MODEL), lambda j: (0, 0)),
                  pl.BlockSpec((D_MODEL, bn), lambda j: (0, j)),
                  pl.BlockSpec((1, bn), lambda j: (0, j))],
        out_specs=pl.BlockSpec((SUBLANES, bn), lambda j: (0, j)),
        out_shape=jax.ShapeDtypeStruct((SUBLANES, n), F32),
        compiler_params=_compiler_params(("arbitrary",)),
        name="adaln_mod",
    )(cvec, ada_w, ada_b)


def _pos_kernel(o_ref):
    nf = D_MODEL // 4
    p = lax.broadcasted_iota(I32, (GRID_W, nf), 0).astype(F32)
    i = lax.broadcasted_iota(I32, (GRID_W, nf), 1).astype(F32)
    omega = jnp.exp(i * (-math.log(10000.0) / nf))
    a = p * omega
    o_ref[:, 0:nf] = jnp.sin(a)
    o_ref[:, nf:2 * nf] = jnp.cos(a)


def _pos_table():
    return pl.pallas_call(
        _pos_kernel,
        out_shape=jax.ShapeDtypeStruct((GRID_W, D_MODEL // 2), F32),
        name="sincos_table",
    )()


def _load_xin(x_ref, e_ref, blk, add_pos):
    x = x_ref[...]
    if not add_pos:
        return x
    half = D_MODEL // 2
    e_all = e_ref[...]
    rows = []
    for j in range(TOKEN_BLOCK // GRID_W):
        xj = x[j * GRID_W:(j + 1) * GRID_W]
        e_row = e_ref[pl.ds(blk * (TOKEN_BLOCK // GRID_W) + j, 1), :]
        rows.append(jnp.concatenate([xj[:, 0:half] + e_row, xj[:, half:] + e_all], axis=1))
    return jnp.concatenate(rows, axis=0)


def _chunk_masks():
    r = lax.broadcasted_iota(I32, (TOKEN_BLOCK, TOKEN_BLOCK), 0)
    c = lax.broadcasted_iota(I32, (TOKEN_BLOCK, TOKEN_BLOCK), 1)
    same = (r // GLA_CHUNK) == (c // GLA_CHUNK)
    return same & (c <= r), same & (c >= r)


def _gla_direction(q, k, v, cum, fwd, att_mask, st_ref):
    qe = q * jnp.exp(cum)
    ke = k * jnp.exp(-cum)
    lane = lax.broadcasted_iota(I32, (1, LANES), 1)
    o_intra = []
    for pair in range(2):
        qp = qe[:, pair * LANES:(pair + 1) * LANES]
        kp = ke[:, pair * LANES:(pair + 1) * LANES]
        for hh in range(2):
            qm = jnp.where((lane // GLA_DK) == hh, qp, 0.0)
            att = jnp.where(att_mask, _dot_nt(qm, kp), 0.0)
            head = 2 * pair + hh
            o_intra.append(_dot(att, v[:, head * GLA_DV:(head + 1) * GLA_DV]))
    o_intra = jnp.concatenate(o_intra, axis=1)

    er = lax.broadcasted_iota(I32, (2 * GLA_DV, 2 * GLA_DK), 0)
    dc = lax.broadcasted_iota(I32, (2 * GLA_DV, 2 * GLA_DK), 1)
    same_head = (er // GLA_DV) == (dc // GLA_DK)
    n_chunks = TOKEN_BLOCK // GLA_CHUNK
    o_inter = [None] * n_chunks
    for c in (range(n_chunks) if fwd else reversed(range(n_chunks))):
        r0 = c * GLA_CHUNK
        rows = slice(r0, r0 + GLA_CHUNK)
        last = cum[r0 + GLA_CHUNK - 1:r0 + GLA_CHUNK] if fwd else cum[r0:r0 + 1]
        kd = k[rows] * jnp.exp(last - cum[rows])
        dec = jnp.exp(last)
        parts = []
        for pair in range(2):
            dl = slice(pair * LANES, (pair + 1) * LANES)
            st = st_ref[pair]
            parts.append(_dot_nt(qe[rows, dl], st))
            ds_t = _dot_tn(v[rows, pair * 2 * GLA_DV:(pair + 1) * 2 * GLA_DV], kd[:, dl])
            st_ref[pair] = dec[:, dl] * st + jnp.where(same_head, ds_t, 0.0)
        o_inter[c] = jnp.concatenate(parts, axis=1)
    return o_intra + jnp.concatenate(o_inter, axis=0)


def _load_state(s0_ref, st_ref):
    zero = jnp.zeros((GLA_DV, GLA_DK), F32)
    for pair in range(2):
        a = s0_ref[0, 2 * pair].T
        b = s0_ref[0, 2 * pair + 1].T
        st_ref[pair] = jnp.concatenate(
            [jnp.concatenate([a, zero], axis=1), jnp.concatenate([zero, b], axis=1)], axis=0)


def _store_state(st_ref, sfin_ref):
    for pair in range(2):
        st = st_ref[pair]
        sfin_ref[0, 2 * pair] = st[0:GLA_DV, 0:GLA_DK].T
        sfin_ref[0, 2 * pair + 1] = st[GLA_DV:2 * GLA_DV, GLA_DK:2 * GLA_DK].T


def _mod_row(mod_ref, row, part):
    return mod_ref[pl.ds(row, 1), part * D_MODEL:(part + 1) * D_MODEL]


def _mixer_fwd_kernel(add_pos, mod_base, mod_per_seq, nb,
                      x_ref, e_ref, mod_ref, n1_ref, win_ref, wa_ref, ba_ref,
                      snw_ref, sws_ref, sbs_ref, s0_ref,
                      qkv_ref, g_ref, lab_ref, s_ref, of_ref, sfin_ref,
                      st_ref):
    seq = pl.program_id(0)
    blk = pl.program_id(1)
    row = mod_base + (seq if mod_per_seq else 0)

    @pl.when(blk == 0)
    def _():
        _load_state(s0_ref, st_ref)

    xin = _load_xin(x_ref, e_ref, blk, add_pos)
    h = _rms(xin, n1_ref[...]) * (1.0 + _mod_row(mod_ref, row, 1)) + _mod_row(mod_ref, row, 0)
    p = _dot(h, win_ref[...])
    q = p[:, P_Q:P_K] * (GLA_DK ** -0.5)
    k = p[:, P_K:P_V]
    v = p[:, P_V:P_G]
    z = _dot(p[:, P_A:P_WIDTH], wa_ref[...]) + ba_ref[...]
    la = (jnp.minimum(z, 0.0) - jnp.log1p(jnp.exp(-jnp.abs(z)))) * (1.0 / GLA_GATE_NORM)

    ug = jax.nn.gelu(p[:, P_U:P_SV])
    vg = jax.nn.gelu(p[:, P_SV:P_A])
    s_cols = []
    for gi in range(SGU_GROUPS):
        cols = slice(gi * SGU_CH, (gi + 1) * SGU_CH)
        vn = _rms(vg[:, cols], snw_ref[:, cols])
        rhs = jnp.concatenate([vn[0:SGU_CHUNK], vn[SGU_CHUNK:2 * SGU_CHUNK]], axis=1)
        vm = _dot(sws_ref[gi], rhs) + jnp.concatenate([sbs_ref[gi], sbs_ref[gi]], axis=1)
        vm = jnp.concatenate([vm[:, 0:SGU_CH], vm[:, SGU_CH:2 * SGU_CH]], axis=0)
        s_cols.append(ug[:, cols] * vm)
    s_ref[...] = jnp.concatenate(s_cols, axis=1)

    lo_mask, _ = _chunk_masks()
    cum = _dot_f32(lo_mask.astype(F32), la[:, 0:QK_W])
    of_ref[...] = _gla_direction(q, k, v, cum, True, lo_mask, st_ref)
    qkv_ref[...] = jnp.concatenate([q, k, v], axis=1)
    g_ref[...] = p[:, P_G:P_U]
    lab_ref[...] = la[:, QK_W:2 * QK_W]

    @pl.when(blk == nb - 1)
    def _():
        _store_state(st_ref, sfin_ref)


def _mixer_fwd(x, e_tab, mod, n1, win, wa, ba, snw, sws, sbs, s0, *, nseq, nb, add_pos,
               mod_base, mod_per_seq):
    n = nseq * nb * TOKEN_BLOCK
    tok = lambda w: pl.BlockSpec((TOKEN_BLOCK, w), lambda s, i: (s * nb + i, 0))
    full = lambda a: pl.BlockSpec(a.shape, lambda s, i: (0,) * a.ndim)
    st_spec = pl.BlockSpec((1, GLA_HEADS, GLA_DK, GLA_DV), lambda s, i: (s, 0, 0, 0))
    kern = functools.partial(_mixer_fwd_kernel, add_pos, mod_base, mod_per_seq, nb)
    return pl.pallas_call(
        kern,
        grid=(nseq, nb),
        in_specs=[tok(D_MODEL), full(e_tab), full(mod), full(n1), full(win), full(wa), full(ba),
                  full(snw), full(sws), full(sbs), st_spec],
        out_specs=[tok(1024), tok(GLA_WIDTH), tok(QK_W), tok(SGU_WIDTH), tok(GLA_WIDTH), st_spec],
        out_shape=[jax.ShapeDtypeStruct((n, 1024), F32),
                   jax.ShapeDtypeStruct((n, GLA_WIDTH), F32),
                   jax.ShapeDtypeStruct((n, QK_W), F32),
                   jax.ShapeDtypeStruct((n, SGU_WIDTH), F32),
                   jax.ShapeDtypeStruct((n, GLA_WIDTH), F32),
                   jax.ShapeDtypeStruct((nseq, GLA_HEADS, GLA_DK, GLA_DV), F32)],
        scratch_shapes=[pltpu.VMEM((2, 2 * GLA_DV, 2 * GLA_DK), F32)],
        compiler_params=_compiler_params(("arbitrary", "arbitrary")),
        name="mixer_fwd",
    )(x, e_tab, mod, n1, win, wa, ba, snw, sws, sbs, s0)


def _mixer_bwd_kernel(add_pos, mod_base, mod_per_seq, nb,
                      x_ref, e_ref, mod_ref, qkv_ref, g_ref, lab_ref, s_ref, of_ref,
                      gnw_ref, wout_ref, n2_ref, rw_ref, s0_ref,
                      x1_ref, h2t_ref, probs_ref, sfin_ref,
                      st_ref):
    seq = pl.program_id(0)
    step = pl.program_id(1)
    blk = nb - 1 - step
    row = mod_base + (seq if mod_per_seq else 0)

    @pl.when(step == 0)
    def _():
        _load_state(s0_ref, st_ref)

    qkv = qkv_ref[...]
    q, k, v = qkv[:, 0:QK_W], qkv[:, QK_W:2 * QK_W], qkv[:, 2 * QK_W:]
    _, hi_mask = _chunk_masks()
    cum = _dot_f32(hi_mask.astype(F32), lab_ref[...])
    o = of_ref[...] + _gla_direction(q, k, v, cum, False, hi_mask, st_ref)
    g = g_ref[...]
    cols = []
    for head in range(GLA_HEADS):
        hs = slice(head * GLA_DV, (head + 1) * GLA_DV)
        cols.append(_rms(o[:, hs], gnw_ref[...]) * jax.nn.silu(g[:, hs]))
    cols.append(s_ref[...])
    y = _dot(jnp.concatenate(cols, axis=1), wout_ref[...])

    xin = _load_xin(x_ref, e_ref, blk, add_pos)
    x1 = xin + _mod_row(mod_ref, row, 2) * y
    x1_ref[...] = x1
    h2 = _rms(x1, n2_ref[...]) * (1.0 + _mod_row(mod_ref, row, 4)) + _mod_row(mod_ref, row, 3)
    for s in range(TILES_PER_TOKEN):
        h2t_ref[pl.ds(s, TOKEN_BLOCK, stride=TILES_PER_TOKEN), :] = h2[:, s * LANES:(s + 1) * LANES]

    logits = _dot_f32(rw_ref[...], h2, (((1,), (1,)), ((), ())))
    m = jnp.max(logits, axis=0, keepdims=True)
    ex = jnp.exp(logits - m)
    probs_ref[0] = ex / jnp.sum(ex, axis=0, keepdims=True)

    @pl.when(step == nb - 1)
    def _():
        _store_state(st_ref, sfin_ref)


def _mixer_bwd(x, e_tab, mod, qkv, g, lab, s, of, gnw, wout, n2, rw_t, s0, *, nseq, nb, add_pos,
               mod_base, mod_per_seq):
    n = nseq * nb * TOKEN_BLOCK
    rev = lambda s_, i: (s_ * nb + nb - 1 - i, 0)
    tok = lambda w: pl.BlockSpec((TOKEN_BLOCK, w), rev)
    full = lambda a: pl.BlockSpec(a.shape, lambda s_, i: (0,) * a.ndim)
    st_spec = pl.BlockSpec((1, GLA_HEADS, GLA_DK, GLA_DV), lambda s_, i: (s_, 0, 0, 0))
    kern = functools.partial(_mixer_bwd_kernel, add_pos, mod_base, mod_per_seq, nb)
    return pl.pallas_call(
        kern,
        grid=(nseq, nb),
        in_specs=[tok(D_MODEL), full(e_tab), full(mod), tok(1024), tok(GLA_WIDTH), tok(QK_W),
                  tok(SGU_WIDTH), tok(GLA_WIDTH), full(gnw), full(wout), full(n2), full(rw_t),
                  st_spec],
        out_specs=[tok(D_MODEL),
                   pl.BlockSpec((TOKEN_BLOCK * TILES_PER_TOKEN, LANES), rev),
                   pl.BlockSpec((1, N_EXPERTS, TOKEN_BLOCK), lambda s_, i: (s_ * nb + nb - 1 - i, 0, 0)),
                   st_spec],
        out_shape=[jax.ShapeDtypeStruct((n, D_MODEL), F32),
                   jax.ShapeDtypeStruct((n * TILES_PER_TOKEN, LANES), F32),
                   jax.ShapeDtypeStruct((n // TOKEN_BLOCK, N_EXPERTS, TOKEN_BLOCK), F32),
                   jax.ShapeDtypeStruct((nseq, GLA_HEADS, GLA_DK, GLA_DV), F32)],
        scratch_shapes=[pltpu.VMEM((2, 2 * GLA_DV, 2 * GLA_DK), F32)],
        compiler_params=_compiler_params(("arbitrary", "arbitrary")),
        name="mixer_bwd",
    )(x, e_tab, mod, qkv, g, lab, s, of, gnw, wout, n2, rw_t, s0)


def _route_kernel(n_tok, cap, probs_ref, idx_ref, gate_ref, xs_ref, ps_ref):
    n_blk = n_tok // TOKEN_BLOCK
    n_chunk = n_tok // LANES
    probs = jnp.concatenate([probs_ref[b] for b in range(n_blk)], axis=1)
    capf = jnp.float32(cap)

    def count(mask):
        return jnp.sum(mask.astype(F32), axis=1, keepdims=True)

    def as_f32(bits):
        return lax.bitcast_convert_type(bits, F32)

    def thr_step(_, lohi):
        lo, hi = lohi
        mid = lo + ((hi - lo + 1) >> 1)
        ok = count(probs >= as_f32(mid)) >= capf
        return jnp.where(ok, mid, lo), jnp.where(ok, hi, mid - 1)

    lo0 = jnp.zeros((N_EXPERTS, 1), I32)
    hi0 = jnp.full((N_EXPERTS, 1), 0x3F800000, I32)
    thr, _ = lax.fori_loop(0, 31, thr_step, (lo0, hi0))
    gt = probs >= as_f32(thr + 1)
    eq = (probs >= as_f32(thr)) & jnp.logical_not(gt)
    need = capf - count(gt)
    tok = lax.broadcasted_iota(I32, (N_EXPERTS, n_tok), 1)

    def tie_step(_, lohi):
        lo, hi = lohi
        mid = (lo + hi) >> 1
        ok = count(eq & (tok <= mid)) >= need
        return jnp.where(ok, lo, mid + 1), jnp.where(ok, mid, hi)

    n_bits = max(1, (n_tok - 1).bit_length())
    cut, _ = lax.fori_loop(0, n_bits, tie_step,
                           (jnp.zeros((N_EXPERTS, 1), I32), jnp.full((N_EXPERTS, 1), n_tok - 1, I32)))
    sel = (gt | (eq & (tok <= cut))).astype(F32)

    xs_ref[...] = jnp.concatenate([sel[:, c * LANES:(c + 1) * LANES] for c in range(n_chunk)], axis=0)
    ps_ref[...] = jnp.concatenate([probs[:, c * LANES:(c + 1) * LANES] for c in range(n_chunk)], axis=0)

    li = lax.broadcasted_iota(I32, (LANES, LANES), 0)
    lj = lax.broadcasted_iota(I32, (LANES, LANES), 1)
    upper = (li <= lj).astype(F32)
    ci = lax.broadcasted_iota(I32, (n_chunk, n_chunk), 0)
    cj = lax.broadcasted_iota(I32, (n_chunk, n_chunk), 1)
    lower = (cj <= ci).astype(F32)
    slot = lax.broadcasted_iota(I32, (1, cap), 1).astype(F32)
    chunk_id = lax.broadcasted_iota(I32, (n_chunk, cap), 0).astype(F32)
    lane_id = lax.broadcasted_iota(I32, (LANES, cap), 0).astype(F32)
    reps = cap // LANES

    def per_expert(e, _):
        x = xs_ref[pl.ds(e, n_chunk, stride=N_EXPERTS), :]
        pe = ps_ref[pl.ds(e, n_chunk, stride=N_EXPERTS), :]
        ploc = _dot(x, upper)
        tot = jnp.broadcast_to(ploc[:, LANES - 1:LANES], (n_chunk, LANES))
        cum = _dot(lower, tot)
        cum_w = jnp.concatenate([cum] * reps, axis=1)
        base_w = jnp.concatenate([cum - tot] * reps, axis=1)
        chunk_of = jnp.sum((cum_w <= slot).astype(F32), axis=0, keepdims=True)
        onehot = chunk_id == chunk_of
        local = slot - jnp.sum(jnp.where(onehot, base_w, 0.0), axis=0, keepdims=True)
        pref = _dot_tn(ploc, onehot.astype(F32))
        lane_of = jnp.sum((pref <= local).astype(F32), axis=0, keepdims=True)
        idx_ref[pl.ds(e, 1), :] = (chunk_of * LANES + lane_of).astype(I32)
        pg = _dot_f32(pe, onehot.astype(F32), (((0,), (0,)), ((), ())))
        gate_ref[pl.ds(e, 1), :] = jnp.sum(jnp.where(lane_id == lane_of, pg, 0.0), axis=0, keepdims=True)
        return 0

    lax.fori_loop(0, N_EXPERTS, per_expert, 0)


def _route(probs, n_tok, cap):
    return pl.pallas_call(
        functools.partial(_route_kernel, n_tok, cap),
        out_shape=[jax.ShapeDtypeStruct((N_EXPERTS, cap), I32),
                   jax.ShapeDtypeStruct((N_EXPERTS, cap), F32)],
        scratch_shapes=[pltpu.VMEM((n_tok // LANES * N_EXPERTS, LANES), F32),
                        pltpu.VMEM((n_tok // LANES * N_EXPERTS, LANES), F32)],
        compiler_params=pltpu.CompilerParams(vmem_limit_bytes=VMEM_LIMIT),
        name="route_topk",
    )(probs)


def _expert_kernel(cap, idx_ref, h2t_ref, w1_ref, w3_ref, w2_ref, ye_ref,
                   xe_ref, x2_ref, hid_ref, sem):
    e = pl.program_id(0)
    f = pl.program_id(1)
    slot = e % 2
    rows_per_step = cap // EXPERT_STEPS

    def start_row(expert, buf, j):
        src = pl.multiple_of(idx_ref[expert * cap + j] * TILES_PER_TOKEN, TILES_PER_TOKEN)
        dst = pl.multiple_of(j * TILES_PER_TOKEN, TILES_PER_TOKEN)
        pltpu.make_async_copy(h2t_ref.at[pl.ds(src, TILES_PER_TOKEN), :],
                              xe_ref.at[buf, pl.ds(dst, TILES_PER_TOKEN), :], sem.at[buf]).start()

    def wait_rows(buf):
        pltpu.make_async_copy(h2t_ref.at[pl.ds(0, cap * TILES_PER_TOKEN), :], xe_ref.at[buf],
                              sem.at[buf]).wait()

    def prefetch_next():
        nxt = jnp.minimum(e + 1, N_EXPERTS - 1)
        first = f * rows_per_step
        for j in range(rows_per_step):
            start_row(nxt, 1 - slot, first + j)

    @pl.when((e == 0) & (f == 0))
    def _():
        def issue(j, _):
            start_row(0, 0, j)
            return 0

        lax.fori_loop(0, cap, issue, 0, unroll=8)

    @pl.when(f == 0)
    def _():
        wait_rows(slot)
        for s in range(TILES_PER_TOKEN):
            x2_ref[:, s * LANES:(s + 1) * LANES] = (
                xe_ref[slot, pl.ds(s, cap, stride=TILES_PER_TOKEN), :].astype(BF16))

    @pl.when(f < N_HID_STEPS)
    def _():
        prefetch_next()
        x2 = x2_ref[...]
        a = jnp.dot(x2, w1_ref[0].astype(BF16), preferred_element_type=F32)
        b = jnp.dot(x2, w3_ref[0].astype(BF16), preferred_element_type=F32)
        hid_ref[f] = (jax.nn.silu(a) * b).astype(BF16)

    @pl.when(f >= N_HID_STEPS)
    def _():
        prefetch_next()
        w2 = w2_ref[0].astype(BF16)
        out = jnp.dot(hid_ref[0], w2[0:EXPERT_F_BLOCK], preferred_element_type=F32)
        for kb in range(1, N_HID_STEPS):
            out += jnp.dot(hid_ref[kb], w2[kb * EXPERT_F_BLOCK:(kb + 1) * EXPERT_F_BLOCK],
                           preferred_element_type=F32)
        tile0 = (f - N_HID_STEPS) * (EXPERT_N_BLOCK // LANES)
        for i in range(EXPERT_N_BLOCK // LANES):
            ye_ref[pl.ds(tile0 + i, cap, stride=TILES_PER_TOKEN), :] = out[:, i * LANES:(i + 1) * LANES]

    @pl.when((e == N_EXPERTS - 1) & (f == EXPERT_STEPS - 1))
    def _():
        wait_rows(1 - slot)


def _experts(idx_flat, h2t, w1, w3, w2, cap):
    hid_blk = lambda e, f, idx: (e, 0, jnp.minimum(f, N_HID_STEPS - 1))
    out_blk = lambda e, f, idx: (e, 0, jnp.maximum(f - N_HID_STEPS, 0))
    grid_spec = pltpu.PrefetchScalarGridSpec(
        num_scalar_prefetch=1,
        grid=(N_EXPERTS, EXPERT_STEPS),
        in_specs=[pl.BlockSpec(memory_space=pl.ANY),
                  pl.BlockSpec((1, D_MODEL, EXPERT_F_BLOCK), hid_blk),
                  pl.BlockSpec((1, D_MODEL, EXPERT_F_BLOCK), hid_blk),
                  pl.BlockSpec((1, D_EXPERT, EXPERT_N_BLOCK), out_blk)],
        out_specs=pl.BlockSpec((cap * TILES_PER_TOKEN, LANES), lambda e, f, idx: (e, 0)),
        scratch_shapes=[pltpu.VMEM((2, cap * TILES_PER_TOKEN, LANES), F32),
                        pltpu.VMEM((cap, D_MODEL), BF16),
                        pltpu.VMEM((N_HID_STEPS, cap, EXPERT_F_BLOCK), BF16),
                        pltpu.SemaphoreType.DMA((2,))],
    )
    return pl.pallas_call(
        functools.partial(_expert_kernel, cap),
        grid_spec=grid_spec,
        out_shape=jax.ShapeDtypeStruct((N_EXPERTS * cap * TILES_PER_TOKEN, LANES), F32),
        compiler_params=_compiler_params(("arbitrary", "arbitrary")),
        name="expert_swiglu",
    )(idx_flat, h2t, w1, w3, w2)


COMBINE_BATCH = 8
ZERO_ROWS = 512


def _combine_kernel(cap, n_tok, idx_ref, gate_ref, ye_ref, out_ref, acc_ref, sem):
    e = pl.program_id(0)

    @pl.when(e == 0)
    def _():
        def zero(i, _):
            r = pl.multiple_of(i * ZERO_ROWS, ZERO_ROWS)
            acc_ref[pl.ds(r, ZERO_ROWS), :] = jnp.zeros((ZERO_ROWS, LANES), F32)
            return 0

        lax.fori_loop(0, n_tok * TILES_PER_TOKEN // ZERO_ROWS, zero, 0)

    def batch(jb, _):
        vals = []
        for u in range(COMBINE_BATCH):
            j = jb * COMBINE_BATCH + u
            t = pl.multiple_of(idx_ref[e * cap + j] * TILES_PER_TOKEN, TILES_PER_TOKEN)
            gate = gate_ref[e * cap + j]
            src = pl.multiple_of(j * TILES_PER_TOKEN, TILES_PER_TOKEN)
            vals.append((t, acc_ref[pl.ds(t, TILES_PER_TOKEN), :]
                         + ye_ref[pl.ds(src, TILES_PER_TOKEN), :] * gate))
        for t, val in vals:
            acc_ref[pl.ds(t, TILES_PER_TOKEN), :] = val
        return 0

    lax.fori_loop(0, cap // COMBINE_BATCH, batch, 0)

    @pl.when(e == N_EXPERTS - 1)
    def _():
        cp = pltpu.make_async_copy(acc_ref, out_ref, sem)
        cp.start()
        cp.wait()


def _combine(idx_flat, gate_flat, ye, cap, n_tok):
    grid_spec = pltpu.PrefetchScalarGridSpec(
        num_scalar_prefetch=2,
        grid=(N_EXPERTS,),
        in_specs=[pl.BlockSpec((cap * TILES_PER_TOKEN, LANES), lambda e, idx, gate: (e, 0))],
        out_specs=pl.BlockSpec(memory_space=pl.ANY),
        scratch_shapes=[pltpu.VMEM((n_tok * TILES_PER_TOKEN, LANES), F32),
                        pltpu.SemaphoreType.DMA],
    )
    return pl.pallas_call(
        functools.partial(_combine_kernel, cap, n_tok),
        grid_spec=grid_spec,
        out_shape=jax.ShapeDtypeStruct((n_tok * TILES_PER_TOKEN, LANES), F32),
        compiler_params=_compiler_params(("arbitrary",)),
        name="moe_combine",
    )(idx_flat, gate_flat, ye)


def _final_kernel(mod_base, mod_per_seq, nb, x1_ref, moe_ref, mod_ref, fw_ref, y_ref):
    row = mod_base + ((pl.program_id(0) // nb) if mod_per_seq else 0)
    moe = jnp.concatenate(
        [moe_ref[pl.ds(s, TOKEN_BLOCK, stride=TILES_PER_TOKEN), :] for s in range(TILES_PER_TOKEN)],
        axis=1)
    y_ref[...] = _rms(x1_ref[...] + _mod_row(mod_ref, row, 5) * moe, fw_ref[...])


def _final(x1, moe_t, mod, fw, *, nb, mod_base, mod_per_seq):
    n = x1.shape[0]
    return pl.pallas_call(
        functools.partial(_final_kernel, mod_base, mod_per_seq, nb),
        grid=(n // TOKEN_BLOCK,),
        in_specs=[pl.BlockSpec((TOKEN_BLOCK, D_MODEL), lambda i: (i, 0)),
                  pl.BlockSpec((TOKEN_BLOCK * TILES_PER_TOKEN, LANES), lambda i: (i, 0)),
                  pl.BlockSpec(mod.shape, lambda i: (0, 0)),
                  pl.BlockSpec(fw.shape, lambda i: (0, 0))],
        out_specs=pl.BlockSpec((TOKEN_BLOCK, D_MODEL), lambda i: (i, 0)),
        out_shape=jax.ShapeDtypeStruct((n, D_MODEL), F32),
        compiler_params=_compiler_params(("arbitrary",)),
        name="final_norm",
    )(x1, moe_t, mod, fw)


def _trunk_and_norm(x, e_tab, mod, s0_f, s0_b, prm, *, nseq, seq_len, add_pos, mod_base, mod_per_seq):
    nb = seq_len // TOKEN_BLOCK
    n_tok = nseq * seq_len
    cap = EC_CAPACITY_FACTOR * n_tok // N_EXPERTS
    kw = dict(nseq=nseq, nb=nb, add_pos=add_pos, mod_base=mod_base, mod_per_seq=mod_per_seq)
    x2d = x.reshape(n_tok, D_MODEL)
    qkv, g, lab, s, of, sfin_f = _mixer_fwd(
        x2d, e_tab, mod, prm["n1"], prm["win"], prm["wa"], prm["ba"], prm["snw"], prm["sws"],
        prm["sbs"], s0_f, **kw)
    x1, h2t, probs, sfin_b = _mixer_bwd(
        x2d, e_tab, mod, qkv, g, lab, s, of, prm["gnw"], prm["wout"], prm["n2"], prm["rw_t"],
        s0_b, **kw)
    idx, gates = _route(probs, n_tok, cap)
    idx_flat = idx.reshape(-1)
    ye = _experts(idx_flat, h2t, prm["w1"], prm["w3"], prm["w2"], cap)
    moe_t = _combine(idx_flat, gates.reshape(-1), ye, cap, n_tok)
    y = _final(x1, moe_t, mod, prm["fw"], nb=nb, mod_base=mod_base, mod_per_seq=mod_per_seq)
    return y.reshape(nseq, seq_len, D_MODEL), sfin_f, sfin_b


def kernel(x_prompt, x_sample, state_gla_fwd, state_gla_bwd, c, c_ctx, ada_w, ada_b, norm1_w, w_in, gla_wa2_f, gla_ba_f, gla_wa2_b, gla_ba_b, gla_norm_w, sgu_norm_w, sgu_ws, sgu_bs, w_out, norm2_w, router_w, exp_w1, exp_w3, exp_w2, final_norm_w):
    assert ada_w.shape[0] == 1, "single trunk layer"
    batch, seq, _ = x_prompt.shape
    dec_batch, dec_seq, _ = x_sample.shape

    w = w_in[0]
    off_af = 2 * QK_W + 2 * GLA_WIDTH
    off_u = off_af + 2 * GLA_LOWRANK
    win = jnp.concatenate(
        [w[:, :off_af], w[:, off_u:], w[:, off_af:off_u],
         jnp.zeros((D_MODEL, P_WIDTH - w.shape[1]), w.dtype)], axis=1).astype(BF16)
    wa = jnp.zeros((P_WIDTH - P_A, 2 * QK_W), F32)
    wa = wa.at[0:GLA_LOWRANK, 0:QK_W].set(gla_wa2_f[0])
    wa = wa.at[GLA_LOWRANK:2 * GLA_LOWRANK, QK_W:].set(gla_wa2_b[0]).astype(BF16)
    prm = dict(
        n1=norm1_w, win=win, wa=wa,
        ba=jnp.concatenate([gla_ba_f[0], gla_ba_b[0]])[None, :],
        snw=sgu_norm_w, sws=sgu_ws[0].astype(BF16),
        sbs=jnp.broadcast_to(sgu_bs[0][:, :, None], (SGU_GROUPS, SGU_CHUNK, SGU_CH)),
        gnw=gla_norm_w, wout=w_out[0].astype(BF16), n2=norm2_w, rw_t=router_w[0].T,
        w1=exp_w1[0], w3=exp_w3[0], w2=exp_w2[0], fw=final_norm_w[None, :])

    cvec = jnp.concatenate([c_ctx[None, :], c, jnp.zeros((SUBLANES - 1 - dec_batch, D_MODEL), F32)])
    mod = _modulation(cvec, ada_w[0], ada_b)
    e_tab = _pos_table()

    zero_state = jnp.zeros((batch, GLA_HEADS, GLA_DK, GLA_DV), F32)
    y_prompt, sf, sb = _trunk_and_norm(
        x_prompt, e_tab, mod, zero_state, zero_state, prm,
        nseq=batch, seq_len=seq, add_pos=False, mod_base=0, mod_per_seq=False)
    y_sample, _, _ = _trunk_and_norm(
        x_sample, e_tab, mod, state_gla_fwd[:, 0], state_gla_bwd[:, 0], prm,
        nseq=dec_batch, seq_len=dec_seq, add_pos=True, mod_base=1, mod_per_seq=True)
    return (y_prompt, y_sample, sf[:, None], sb[:, None])
```

```python
import functools
import math

import jax
import jax.numpy as jnp
from jax import lax
from jax.experimental import pallas as pl
from jax.experimental.pallas import tpu as pltpu

F32 = jnp.float32
BF16 = jnp.bfloat16
I32 = jnp.int32

D_MODEL = 1024
GRID_W = 64
GLA_HEADS = 4
GLA_DK = 64
GLA_DV = 128
GLA_WIDTH = GLA_HEADS * GLA_DV
QK_W = GLA_HEADS * GLA_DK
GLA_LOWRANK = 16
GLA_GATE_NORM = 16.0
GLA_CHUNK = 64
SGU_WIDTH = 512
SGU_GROUPS = 4
SGU_CH = 128
SGU_CHUNK = 128
N_EXPERTS = 16
EC_CAPACITY_FACTOR = 2
D_EXPERT = 2048
EPS = 1e-6

SUBLANES = 8
LANES = 128
TILES_PER_TOKEN = D_MODEL // LANES

TOKEN_BLOCK = 256
SEQ_GROUP = 2
P_Q, P_K, P_V, P_G, P_U, P_SV, P_A = 0, 256, 512, 1024, 1536, 2048, 2560
P_WIDTH = 2688
EXPERT_F_BLOCK = 512
EXPERT_N_BLOCK = 256
N_HID_STEPS = D_EXPERT // EXPERT_F_BLOCK
EXPERT_STEPS = N_HID_STEPS + D_MODEL // EXPERT_N_BLOCK
VMEM_LIMIT = 56 * 1024 * 1024


def _dot(a, b):
    return jnp.dot(a.astype(BF16), b.astype(BF16), preferred_element_type=F32)


def _dot_nt(a, b):
    return lax.dot_general(a.astype(BF16), b.astype(BF16), (((1,), (1,)), ((), ())),
                           preferred_element_type=F32)


def _dot_tn(a, b):
    return lax.dot_general(a.astype(BF16), b.astype(BF16), (((0,), (0,)), ((), ())),
                           preferred_element_type=F32)


def _dot_f32(a, b, dims=(((1,), (0,)), ((), ()))):
    return lax.dot_general(a, b, dims, precision=lax.Precision.HIGHEST, preferred_element_type=F32)


def _rms(x, w):
    return x * lax.rsqrt(jnp.mean(x * x, axis=-1, keepdims=True) + EPS) * w


def _compiler_params(sem):
    return pltpu.CompilerParams(dimension_semantics=sem, vmem_limit_bytes=VMEM_LIMIT)


def _mod_kernel(c_ref, w_ref, b_ref, o_ref):
    o_ref[...] = _dot(jax.nn.silu(c_ref[...]), w_ref[...]) + b_ref[...]


def _modulation(cvec, ada_w, ada_b):
    n = ada_w.shape[1]
    bn = 1536
    return pl.pallas_call(
        _mod_kernel,
        grid=(n // bn,),
        in_specs=[pl.BlockSpec((SUBLANES, D_MODEL), lambda j: (0, 0)),
                  pl.BlockSpec((D_MODEL, bn), lambda j: (0, j)),
                  pl.BlockSpec((1, bn), lambda j: (0, j))],
        out_specs=pl.BlockSpec((SUBLANES, bn), lambda j: (0, j)),
        out_shape=jax.ShapeDtypeStruct((SUBLANES, n), F32),
        compiler_params=_compiler_params(("arbitrary",)),
        name="adaln_mod",
    )(cvec, ada_w, ada_b)


def _pos_kernel(o_ref):
    nf = D_MODEL // 4
    p = lax.broadcasted_iota(I32, (GRID_W, nf), 0).astype(F32)
    i = lax.broadcasted_iota(I32, (GRID_W, nf), 1).astype(F32)
    omega = jnp.exp(i * (-math.log(10000.0) / nf))
    a = p * omega
    o_ref[:, 0:nf] = jnp.sin(a)
    o_ref[:, nf:2 * nf] = jnp.cos(a)


def _pos_table():
    return pl.pallas_call(
        _pos_kernel,
        out_shape=jax.ShapeDtypeStruct((GRID_W, D_MODEL // 2), F32),
        name="sincos_table",
    )()


def _add_pos(x, e_ref, blk, add_pos):
    if not add_pos:
        return x
    half = D_MODEL // 2
    e_all = e_ref[...]
    rows = []
    for j in range(TOKEN_BLOCK // GRID_W):
        xj = x[j * GRID_W:(j + 1) * GRID_W]
        e_row = e_ref[pl.ds(blk * (TOKEN_BLOCK // GRID_W) + j, 1), :]
        rows.append(jnp.concatenate([xj[:, 0:half] + e_row, xj[:, half:] + e_all], axis=1))
    return jnp.concatenate(rows, axis=0)


def _chunk_masks():
    r = lax.broadcasted_iota(I32, (TOKEN_BLOCK, TOKEN_BLOCK), 0)
    c = lax.broadcasted_iota(I32, (TOKEN_BLOCK, TOKEN_BLOCK), 1)
    same = (r // GLA_CHUNK) == (c // GLA_CHUNK)
    return same & (c <= r), same & (c >= r)


def _gla_direction(q, k, v, cum, fwd, att_mask, st_ref):
    qe = q * jnp.exp(cum)
    ke = k * jnp.exp(-cum)
    lane = lax.broadcasted_iota(I32, (1, LANES), 1)
    o_intra = []
    for pair in range(2):
        qp = qe[:, pair * LANES:(pair + 1) * LANES]
        kp = ke[:, pair * LANES:(pair + 1) * LANES]
        for hh in range(2):
            qm = jnp.where((lane // GLA_DK) == hh, qp, 0.0)
            att = jnp.where(att_mask, _dot_nt(qm, kp), 0.0)
            head = 2 * pair + hh
            o_intra.append(_dot(att, v[:, head * GLA_DV:(head + 1) * GLA_DV]))
    o_intra = jnp.concatenate(o_intra, axis=1)

    er = lax.broadcasted_iota(I32, (2 * GLA_DV, 2 * GLA_DK), 0)
    dc = lax.broadcasted_iota(I32, (2 * GLA_DV, 2 * GLA_DK), 1)
    same_head = (er // GLA_DV) == (dc // GLA_DK)
    n_chunks = TOKEN_BLOCK // GLA_CHUNK
    o_inter = [None] * n_chunks
    for c in (range(n_chunks) if fwd else reversed(range(n_chunks))):
        r0 = c * GLA_CHUNK
        rows = slice(r0, r0 + GLA_CHUNK)
        last = cum[r0 + GLA_CHUNK - 1:r0 + GLA_CHUNK] if fwd else cum[r0:r0 + 1]
        kd = k[rows] * jnp.exp(last - cum[rows])
        dec = jnp.exp(last)
        parts = []
        for pair in range(2):
            dl = slice(pair * LANES, (pair + 1) * LANES)
            st = st_ref[pair]
            parts.append(_dot_nt(qe[rows, dl], st))
            ds_t = _dot_tn(v[rows, pair * 2 * GLA_DV:(pair + 1) * 2 * GLA_DV], kd[:, dl])
            st_ref[pair] = dec[:, dl] * st + jnp.where(same_head, ds_t, 0.0)
        o_inter[c] = jnp.concatenate(parts, axis=1)
    return o_intra + jnp.concatenate(o_inter, axis=0)


def _load_state(s0_ref, u, st_ref):
    zero = jnp.zeros((GLA_DV, GLA_DK), F32)
    for pair in range(2):
        a = s0_ref[u, 2 * pair].T
        b = s0_ref[u, 2 * pair + 1].T
        st_ref[pair] = jnp.concatenate(
            [jnp.concatenate([a, zero], axis=1), jnp.concatenate([zero, b], axis=1)], axis=0)


def _store_state(st_ref, sfin_ref, u):
    for pair in range(2):
        st = st_ref[pair]
        sfin_ref[u, 2 * pair] = st[0:GLA_DV, 0:GLA_DK].T
        sfin_ref[u, 2 * pair + 1] = st[GLA_DV:2 * GLA_DV, GLA_DK:2 * GLA_DK].T


def _mod_row(mod_ref, row, part):
    return mod_ref[pl.ds(row, 1), part * D_MODEL:(part + 1) * D_MODEL]


def _mixer_fwd_kernel(add_pos, mod_base, mod_per_seq, nb,
                      x_ref, e_ref, mod_ref, n1_ref, win_ref, wa_ref, ba_ref,
                      snw_ref, sws_ref, sbs_ref, s0_ref,
                      qkv_ref, g_ref, lab_ref, s_ref, of_ref, sfin_ref,
                      st_ref):
    grp = pl.program_id(0)
    blk = pl.program_id(1)
    lo_mask, _ = _chunk_masks()

    @pl.when(blk == 0)
    def _():
        for u in range(SEQ_GROUP):
            _load_state(s0_ref, u, st_ref.at[u])

    for u in range(SEQ_GROUP):
        row = mod_base + ((grp * SEQ_GROUP + u) if mod_per_seq else 0)
        st_u = st_ref.at[u]
        xin = _add_pos(x_ref[u], e_ref, blk, add_pos)
        h = _rms(xin, n1_ref[...]) * (1.0 + _mod_row(mod_ref, row, 1)) + _mod_row(mod_ref, row, 0)
        p = _dot(h, win_ref[...])
        q = p[:, P_Q:P_K] * (GLA_DK ** -0.5)
        k = p[:, P_K:P_V]
        v = p[:, P_V:P_G]
        z = _dot(p[:, P_A:P_WIDTH], wa_ref[...]) + ba_ref[...]
        la = (jnp.minimum(z, 0.0) - jnp.log1p(jnp.exp(-jnp.abs(z)))) * (1.0 / GLA_GATE_NORM)

        ug = jax.nn.gelu(p[:, P_U:P_SV])
        vg = jax.nn.gelu(p[:, P_SV:P_A])
        s_cols = []
        for gi in range(SGU_GROUPS):
            cols = slice(gi * SGU_CH, (gi + 1) * SGU_CH)
            vn = _rms(vg[:, cols], snw_ref[:, cols])
            rhs = jnp.concatenate([vn[0:SGU_CHUNK], vn[SGU_CHUNK:2 * SGU_CHUNK]], axis=1)
            vm = _dot(sws_ref[gi], rhs) + jnp.concatenate([sbs_ref[gi], sbs_ref[gi]], axis=1)
            vm = jnp.concatenate([vm[:, 0:SGU_CH], vm[:, SGU_CH:2 * SGU_CH]], axis=0)
            s_cols.append(ug[:, cols] * vm)
        s_ref[u] = jnp.concatenate(s_cols, axis=1)

        cum = _dot_f32(lo_mask.astype(F32), la[:, 0:QK_W])
        of_ref[u] = _gla_direction(q, k, v, cum, True, lo_mask, st_u)
        qkv_ref[u] = jnp.concatenate([q, k, v], axis=1)
        g_ref[u] = p[:, P_G:P_U]
        lab_ref[u] = la[:, QK_W:2 * QK_W]

    @pl.when(blk == nb - 1)
    def _():
        for u in range(SEQ_GROUP):
            _store_state(st_ref.at[u], sfin_ref, u)


def _mixer_fwd(x, e_tab, mod, n1, win, wa, ba, snw, sws, sbs, s0, *, nseq, nb, add_pos,
               mod_base, mod_per_seq):
    seq_len = nb * TOKEN_BLOCK
    tok = lambda w: pl.BlockSpec((SEQ_GROUP, TOKEN_BLOCK, w), lambda s, i: (s, i, 0))
    full = lambda a: pl.BlockSpec(a.shape, lambda s, i: (0,) * a.ndim)
    st_spec = pl.BlockSpec((SEQ_GROUP, GLA_HEADS, GLA_DK, GLA_DV), lambda s, i: (s, 0, 0, 0))
    act = lambda w: jax.ShapeDtypeStruct((nseq, seq_len, w), F32)
    kern = functools.partial(_mixer_fwd_kernel, add_pos, mod_base, mod_per_seq, nb)
    return pl.pallas_call(
        kern,
        grid=(nseq // SEQ_GROUP, nb),
        in_specs=[tok(D_MODEL), full(e_tab), full(mod), full(n1), full(win), full(wa), full(ba),
                  full(snw), full(sws), full(sbs), st_spec],
        out_specs=[tok(1024), tok(GLA_WIDTH), tok(QK_W), tok(SGU_WIDTH), tok(GLA_WIDTH), st_spec],
        out_shape=[act(1024), act(GLA_WIDTH), act(QK_W), act(SGU_WIDTH), act(GLA_WIDTH),
                   jax.ShapeDtypeStruct((nseq, GLA_HEADS, GLA_DK, GLA_DV), F32)],
        scratch_shapes=[pltpu.VMEM((SEQ_GROUP, 2, 2 * GLA_DV, 2 * GLA_DK), F32)],
        compiler_params=_compiler_params(("arbitrary", "arbitrary")),
        name="mixer_fwd",
    )(x, e_tab, mod, n1, win, wa, ba, snw, sws, sbs, s0)


def _mixer_bwd_kernel(add_pos, mod_base, mod_per_seq, nb,
                      x_ref, e_ref, mod_ref, qkv_ref, g_ref, lab_ref, s_ref, of_ref,
                      gnw_ref, wout_ref, n2_ref, rw_ref, s0_ref,
                      x1_ref, h2t_ref, probs_ref, sfin_ref,
                      st_ref):
    grp = pl.program_id(0)
    step = pl.program_id(1)
    blk = nb - 1 - step
    _, hi_mask = _chunk_masks()

    @pl.when(step == 0)
    def _():
        for u in range(SEQ_GROUP):
            _load_state(s0_ref, u, st_ref.at[u])

    for u in range(SEQ_GROUP):
        row = mod_base + ((grp * SEQ_GROUP + u) if mod_per_seq else 0)
        st_u = st_ref.at[u]
        qkv = qkv_ref[u]
        q, k, v = qkv[:, 0:QK_W], qkv[:, QK_W:2 * QK_W], qkv[:, 2 * QK_W:]
        cum = _dot_f32(hi_mask.astype(F32), lab_ref[u])
        o = of_ref[u] + _gla_direction(q, k, v, cum, False, hi_mask, st_u)
        g = g_ref[u]
        cols = []
        for head in range(GLA_HEADS):
            hs = slice(head * GLA_DV, (head + 1) * GLA_DV)
            cols.append(_rms(o[:, hs], gnw_ref[...]) * jax.nn.silu(g[:, hs]))
        cols.append(s_ref[u])
        y = _dot(jnp.concatenate(cols, axis=1), wout_ref[...])

        xin = _add_pos(x_ref[u], e_ref, blk, add_pos)
        x1 = xin + _mod_row(mod_ref, row, 2) * y
        x1_ref[u] = x1
        h2 = _rms(x1, n2_ref[...]) * (1.0 + _mod_row(mod_ref, row, 4)) + _mod_row(mod_ref, row, 3)
        for s in range(TILES_PER_TOKEN):
            h2t_ref[u, pl.ds(s, TOKEN_BLOCK, stride=TILES_PER_TOKEN), :] = h2[:, s * LANES:(s + 1) * LANES]

        logits = _dot_f32(rw_ref[...], h2, (((1,), (1,)), ((), ())))
        m = jnp.max(logits, axis=0, keepdims=True)
        ex = jnp.exp(logits - m)
        probs_ref[u, 0] = ex / jnp.sum(ex, axis=0, keepdims=True)

    @pl.when(step == nb - 1)
    def _():
        for u in range(SEQ_GROUP):
            _store_state(st_ref.at[u], sfin_ref, u)


def _mixer_bwd(x, e_tab, mod, qkv, g, lab, s, of, gnw, wout, n2, rw_t, s0, *, nseq, nb, add_pos,
               mod_base, mod_per_seq):
    seq_len = nb * TOKEN_BLOCK
    tok = lambda w: pl.BlockSpec((SEQ_GROUP, TOKEN_BLOCK, w), lambda s_, i: (s_, nb - 1 - i, 0))
    full = lambda a: pl.BlockSpec(a.shape, lambda s_, i: (0,) * a.ndim)
    st_spec = pl.BlockSpec((SEQ_GROUP, GLA_HEADS, GLA_DK, GLA_DV), lambda s_, i: (s_, 0, 0, 0))
    kern = functools.partial(_mixer_bwd_kernel, add_pos, mod_base, mod_per_seq, nb)
    return pl.pallas_call(
        kern,
        grid=(nseq // SEQ_GROUP, nb),
        in_specs=[tok(D_MODEL), full(e_tab), full(mod), tok(1024), tok(GLA_WIDTH), tok(QK_W),
                  tok(SGU_WIDTH), tok(GLA_WIDTH), full(gnw), full(wout), full(n2), full(rw_t),
                  st_spec],
        out_specs=[tok(D_MODEL),
                   pl.BlockSpec((SEQ_GROUP, TOKEN_BLOCK * TILES_PER_TOKEN, LANES),
                                lambda s_, i: (s_, nb - 1 - i, 0)),
                   pl.BlockSpec((SEQ_GROUP, 1, N_EXPERTS, TOKEN_BLOCK),
                                lambda s_, i: (s_, nb - 1 - i, 0, 0)),
                   st_spec],
        out_shape=[jax.ShapeDtypeStruct((nseq, seq_len, D_MODEL), F32),
                   jax.ShapeDtypeStruct((nseq, seq_len * TILES_PER_TOKEN, LANES), F32),
                   jax.ShapeDtypeStruct((nseq, nb, N_EXPERTS, TOKEN_BLOCK), F32),
                   jax.ShapeDtypeStruct((nseq, GLA_HEADS, GLA_DK, GLA_DV), F32)],
        scratch_shapes=[pltpu.VMEM((SEQ_GROUP, 2, 2 * GLA_DV, 2 * GLA_DK), F32)],
        compiler_params=_compiler_params(("arbitrary", "arbitrary")),
        name="mixer_bwd",
    )(x, e_tab, mod, qkv, g, lab, s, of, gnw, wout, n2, rw_t, s0)


def _route_kernel(n_tok, cap, probs_ref, idx_ref, gate_ref, xs_ref, ps_ref):
    n_blk = n_tok // TOKEN_BLOCK
    n_chunk = n_tok // LANES
    probs = jnp.concatenate([probs_ref[b] for b in range(n_blk)], axis=1)
    capf = jnp.float32(cap)

    def count(mask):
        return jnp.sum(mask.astype(F32), axis=1, keepdims=True)

    def as_f32(bits):
        return lax.bitcast_convert_type(bits, F32)

    def thr_step(_, lohi):
        lo, hi = lohi
        mid = lo + ((hi - lo + 1) >> 1)
        ok = count(probs >= as_f32(mid)) >= capf
        return jnp.where(ok, mid, lo), jnp.where(ok, hi, mid - 1)

    lo0 = jnp.zeros((N_EXPERTS, 1), I32)
    hi0 = jnp.full((N_EXPERTS, 1), 0x3F800000, I32)
    thr, _ = lax.fori_loop(0, 31, thr_step, (lo0, hi0))
    gt = probs >= as_f32(thr + 1)
    eq = (probs >= as_f32(thr)) & jnp.logical_not(gt)
    need = capf - count(gt)
    tok = lax.broadcasted_iota(I32, (N_EXPERTS, n_tok), 1)

    def tie_step(_, lohi):
        lo, hi = lohi
        mid = (lo + hi) >> 1
        ok = count(eq & (tok <= mid)) >= need
        return jnp.where(ok, lo, mid + 1), jnp.where(ok, mid, hi)

    n_bits = max(1, (n_tok - 1).bit_length())
    cut, _ = lax.fori_loop(0, n_bits, tie_step,
                           (jnp.zeros((N_EXPERTS, 1), I32), jnp.full((N_EXPERTS, 1), n_tok - 1, I32)))
    sel = (gt | (eq & (tok <= cut))).astype(F32)

    xs_ref[...] = jnp.concatenate([sel[:, c * LANES:(c + 1) * LANES] for c in range(n_chunk)], axis=0)
    ps_ref[...] = jnp.concatenate([probs[:, c * LANES:(c + 1) * LANES] for c in range(n_chunk)], axis=0)

    li = lax.broadcasted_iota(I32, (LANES, LANES), 0)
    lj = lax.broadcasted_iota(I32, (LANES, LANES), 1)
    upper = (li <= lj).astype(F32)
    ci = lax.broadcasted_iota(I32, (n_chunk, n_chunk), 0)
    cj = lax.broadcasted_iota(I32, (n_chunk, n_chunk), 1)
    lower = (cj <= ci).astype(F32)
    slot = lax.broadcasted_iota(I32, (1, cap), 1).astype(F32)
    chunk_id = lax.broadcasted_iota(I32, (n_chunk, cap), 0).astype(F32)
    lane_id = lax.broadcasted_iota(I32, (LANES, cap), 0).astype(F32)
    reps = cap // LANES

    def per_expert(e, _):
        x = xs_ref[pl.ds(e, n_chunk, stride=N_EXPERTS), :]
        pe = ps_ref[pl.ds(e, n_chunk, stride=N_EXPERTS), :]
        ploc = _dot(x, upper)
        tot = jnp.broadcast_to(ploc[:, LANES - 1:LANES], (n_chunk, LANES))
        cum = _dot(lower, tot)
        cum_w = jnp.concatenate([cum] * reps, axis=1)
        base_w = jnp.concatenate([cum - tot] * reps, axis=1)
        chunk_of = jnp.sum((cum_w <= slot).astype(F32), axis=0, keepdims=True)
        onehot = chunk_id == chunk_of
        local = slot - jnp.sum(jnp.where(onehot, base_w, 0.0), axis=0, keepdims=True)
        pref = _dot_tn(ploc, onehot.astype(F32))
        lane_of = jnp.sum((pref <= local).astype(F32), axis=0, keepdims=True)
        idx_ref[pl.ds(e, 1), :] = (chunk_of * LANES + lane_of).astype(I32)
        pg = _dot_f32(pe, onehot.astype(F32), (((0,), (0,)), ((), ())))
        gate_ref[pl.ds(e, 1), :] = jnp.sum(jnp.where(lane_id == lane_of, pg, 0.0), axis=0, keepdims=True)
        return 0

    lax.fori_loop(0, N_EXPERTS, per_expert, 0)


def _route(probs, n_tok, cap):
    return pl.pallas_call(
        functools.partial(_route_kernel, n_tok, cap),
        out_shape=[jax.ShapeDtypeStruct((N_EXPERTS, cap), I32),
                   jax.ShapeDtypeStruct((N_EXPERTS, cap), F32)],
        scratch_shapes=[pltpu.VMEM((n_tok // LANES * N_EXPERTS, LANES), F32),
                        pltpu.VMEM((n_tok // LANES * N_EXPERTS, LANES), F32)],
        compiler_params=pltpu.CompilerParams(vmem_limit_bytes=VMEM_LIMIT),
        name="route_topk",
    )(probs)


def _expert_kernel(cap, idx_ref, h2t_ref, w1_ref, w3_ref, w2_ref, ye_ref,
                   xe_ref, x2_ref, hid_ref, sem):
    e = pl.program_id(0)
    f = pl.program_id(1)
    slot = e % 2
    rows_per_step = cap // EXPERT_STEPS

    def start_row(expert, buf, j):
        src = pl.multiple_of(idx_ref[expert * cap + j] * TILES_PER_TOKEN, TILES_PER_TOKEN)
        dst = pl.multiple_of(j * TILES_PER_TOKEN, TILES_PER_TOKEN)
        pltpu.make_async_copy(h2t_ref.at[pl.ds(src, TILES_PER_TOKEN), :],
                              xe_ref.at[buf, pl.ds(dst, TILES_PER_TOKEN), :],
                              sem.at[buf]).start(priority=1)

    def wait_rows(buf):
        pltpu.make_async_copy(h2t_ref.at[pl.ds(0, cap * TILES_PER_TOKEN), :], xe_ref.at[buf],
                              sem.at[buf]).wait()

    def prefetch_next():
        nxt = jnp.minimum(e + 1, N_EXPERTS - 1)
        first = f * rows_per_step
        for j in range(rows_per_step):
            start_row(nxt, 1 - slot, first + j)

    @pl.when((e == 0) & (f == 0))
    def _():
        def issue(j, _):
            start_row(0, 0, j)
            return 0

        lax.fori_loop(0, cap, issue, 0, unroll=8)

    @pl.when(f == 0)
    def _():
        wait_rows(slot)
        for s in range(TILES_PER_TOKEN):
            x2_ref[:, s * LANES:(s + 1) * LANES] = (
                xe_ref[slot, pl.ds(s, cap, stride=TILES_PER_TOKEN), :].astype(BF16))

    @pl.when(f < N_HID_STEPS)
    def _():
        prefetch_next()
        x2 = x2_ref[...]
        a = jnp.dot(x2, w1_ref[0].astype(BF16), preferred_element_type=F32)
        b = jnp.dot(x2, w3_ref[0].astype(BF16), preferred_element_type=F32)
        hid_ref[f] = (jax.nn.silu(a) * b).astype(BF16)

    @pl.when(f >= N_HID_STEPS)
    def _():
        prefetch_next()
        w2 = w2_ref[0].astype(BF16)
        out = jnp.dot(hid_ref[0], w2[0:EXPERT_F_BLOCK], preferred_element_type=F32)
        for kb in range(1, N_HID_STEPS):
            out += jnp.dot(hid_ref[kb], w2[kb * EXPERT_F_BLOCK:(kb + 1) * EXPERT_F_BLOCK],
                           preferred_element_type=F32)
        tile0 = (f - N_HID_STEPS) * (EXPERT_N_BLOCK // LANES)
        for i in range(EXPERT_N_BLOCK // LANES):
            ye_ref[pl.ds(tile0 + i, cap, stride=TILES_PER_TOKEN), :] = out[:, i * LANES:(i + 1) * LANES]

    @pl.when((e == N_EXPERTS - 1) & (f == EXPERT_STEPS - 1))
    def _():
        wait_rows(1 - slot)


def _experts(idx_flat, h2t, w1, w3, w2, cap):
    hid_blk = lambda e, f, idx: (e, 0, jnp.minimum(f, N_HID_STEPS - 1))
    out_blk = lambda e, f, idx: (e, 0, jnp.maximum(f - N_HID_STEPS, 0))
    grid_spec = pltpu.PrefetchScalarGridSpec(
        num_scalar_prefetch=1,
        grid=(N_EXPERTS, EXPERT_STEPS),
        in_specs=[pl.BlockSpec(memory_space=pl.ANY),
                  pl.BlockSpec((1, D_MODEL, EXPERT_F_BLOCK), hid_blk),
                  pl.BlockSpec((1, D_MODEL, EXPERT_F_BLOCK), hid_blk),
                  pl.BlockSpec((1, D_EXPERT, EXPERT_N_BLOCK), out_blk)],
        out_specs=pl.BlockSpec((cap * TILES_PER_TOKEN, LANES), lambda e, f, idx: (e, 0)),
        scratch_shapes=[pltpu.VMEM((2, cap * TILES_PER_TOKEN, LANES), F32),
                        pltpu.VMEM((cap, D_MODEL), BF16),
                        pltpu.VMEM((N_HID_STEPS, cap, EXPERT_F_BLOCK), BF16),
                        pltpu.SemaphoreType.DMA((2,))],
    )
    return pl.pallas_call(
        functools.partial(_expert_kernel, cap),
        grid_spec=grid_spec,
        out_shape=jax.ShapeDtypeStruct((N_EXPERTS * cap * TILES_PER_TOKEN, LANES), F32),
        compiler_params=_compiler_params(("arbitrary", "arbitrary")),
        name="expert_swiglu",
    )(idx_flat, h2t, w1, w3, w2)


COMBINE_BATCH = 8
ZERO_ROWS = 512


def _combine_kernel(cap, n_tok, idx_ref, gate_ref, ye_ref, out_ref, acc_ref, sem):
    e = pl.program_id(0)

    @pl.when(e == 0)
    def _():
        def zero(i, _):
            r = pl.multiple_of(i * ZERO_ROWS, ZERO_ROWS)
            acc_ref[pl.ds(r, ZERO_ROWS), :] = jnp.zeros((ZERO_ROWS, LANES), F32)
            return 0

        lax.fori_loop(0, n_tok * TILES_PER_TOKEN // ZERO_ROWS, zero, 0)

    def batch(jb, _):
        vals = []
        for u in range(COMBINE_BATCH):
            j = jb * COMBINE_BATCH + u
            t = pl.multiple_of(idx_ref[e * cap + j] * TILES_PER_TOKEN, TILES_PER_TOKEN)
            gate = gate_ref[e * cap + j]
            src = pl.multiple_of(j * TILES_PER_TOKEN, TILES_PER_TOKEN)
            vals.append((t, acc_ref[pl.ds(t, TILES_PER_TOKEN), :]
                         + ye_ref[pl.ds(src, TILES_PER_TOKEN), :] * gate))
        for t, val in vals:
            acc_ref[pl.ds(t, TILES_PER_TOKEN), :] = val
        return 0

    lax.fori_loop(0, cap // COMBINE_BATCH, batch, 0)

    @pl.when(e == N_EXPERTS - 1)
    def _():
        cp = pltpu.make_async_copy(acc_ref, out_ref, sem)
        cp.start()
        cp.wait()


def _combine(idx_flat, gate_flat, ye, cap, n_tok):
    grid_spec = pltpu.PrefetchScalarGridSpec(
        num_scalar_prefetch=2,
        grid=(N_EXPERTS,),
        in_specs=[pl.BlockSpec((cap * TILES_PER_TOKEN, LANES), lambda e, idx, gate: (e, 0))],
        out_specs=pl.BlockSpec(memory_space=pl.ANY),
        scratch_shapes=[pltpu.VMEM((n_tok * TILES_PER_TOKEN, LANES), F32),
                        pltpu.SemaphoreType.DMA],
    )
    return pl.pallas_call(
        functools.partial(_combine_kernel, cap, n_tok),
        grid_spec=grid_spec,
        out_shape=jax.ShapeDtypeStruct((n_tok * TILES_PER_TOKEN, LANES), F32),
        compiler_params=_compiler_params(("arbitrary",)),
        name="moe_combine",
    )(idx_flat, gate_flat, ye)


def _final_kernel(mod_base, mod_per_seq, nb, x1_ref, moe_ref, mod_ref, fw_ref, y_ref):
    row = mod_base + ((pl.program_id(0) // nb) if mod_per_seq else 0)
    moe = jnp.concatenate(
        [moe_ref[pl.ds(s, TOKEN_BLOCK, stride=TILES_PER_TOKEN), :] for s in range(TILES_PER_TOKEN)],
        axis=1)
    y_ref[...] = _rms(x1_ref[...] + _mod_row(mod_ref, row, 5) * moe, fw_ref[...])


def _final(x1, moe_t, mod, fw, *, nb, mod_base, mod_per_seq):
    n = x1.shape[0]
    return pl.pallas_call(
        functools.partial(_final_kernel, mod_base, mod_per_seq, nb),
        grid=(n // TOKEN_BLOCK,),
        in_specs=[pl.BlockSpec((TOKEN_BLOCK, D_MODEL), lambda i: (i, 0)),
                  pl.BlockSpec((TOKEN_BLOCK * TILES_PER_TOKEN, LANES), lambda i: (i, 0)),
                  pl.BlockSpec(mod.shape, lambda i: (0, 0)),
                  pl.BlockSpec(fw.shape, lambda i: (0, 0))],
        out_specs=pl.BlockSpec((TOKEN_BLOCK, D_MODEL), lambda i: (i, 0)),
        out_shape=jax.ShapeDtypeStruct((n, D_MODEL), F32),
        compiler_params=_compiler_params(("arbitrary",)),
        name="final_norm",
    )(x1, moe_t, mod, fw)


def _trunk_and_norm(x, e_tab, mod, s0_f, s0_b, prm, *, nseq, seq_len, add_pos, mod_base, mod_per_seq):
    nb = seq_len // TOKEN_BLOCK
    n_tok = nseq * seq_len
    cap = EC_CAPACITY_FACTOR * n_tok // N_EXPERTS
    kw = dict(nseq=nseq, nb=nb, add_pos=add_pos, mod_base=mod_base, mod_per_seq=mod_per_seq)
    qkv, g, lab, s, of, sfin_f = _mixer_fwd(
        x, e_tab, mod, prm["n1"], prm["win"], prm["wa"], prm["ba"], prm["snw"], prm["sws"],
        prm["sbs"], s0_f, **kw)
    x1, h2t, probs, sfin_b = _mixer_bwd(
        x, e_tab, mod, qkv, g, lab, s, of, prm["gnw"], prm["wout"], prm["n2"], prm["rw_t"],
        s0_b, **kw)
    x1 = x1.reshape(n_tok, D_MODEL)
    h2t = h2t.reshape(n_tok * TILES_PER_TOKEN, LANES)
    probs = probs.reshape(n_tok // TOKEN_BLOCK, N_EXPERTS, TOKEN_BLOCK)
    idx, gates = _route(probs, n_tok, cap)
    idx_flat = idx.reshape(-1)
    ye = _experts(idx_flat, h2t, prm["w1"], prm["w3"], prm["w2"], cap)
    moe_t = _combine(idx_flat, gates.reshape(-1), ye, cap, n_tok)
    y = _final(x1, moe_t, mod, prm["fw"], nb=nb, mod_base=mod_base, mod_per_seq=mod_per_seq)
    return y.reshape(nseq, seq_len, D_MODEL), sfin_f, sfin_b


def kernel(x_prompt, x_sample, state_gla_fwd, state_gla_bwd, c, c_ctx, ada_w, ada_b, norm1_w, w_in, gla_wa2_f, gla_ba_f, gla_wa2_b, gla_ba_b, gla_norm_w, sgu_norm_w, sgu_ws, sgu_bs, w_out, norm2_w, router_w, exp_w1, exp_w3, exp_w2, final_norm_w):
    assert ada_w.shape[0] == 1, "single trunk layer"
    batch, seq, _ = x_prompt.shape
    dec_batch, dec_seq, _ = x_sample.shape

    w = w_in[0]
    off_af = 2 * QK_W + 2 * GLA_WIDTH
    off_u = off_af + 2 * GLA_LOWRANK
    win = jnp.concatenate(
        [w[:, :off_af], w[:, off_u:], w[:, off_af:off_u],
         jnp.zeros((D_MODEL, P_WIDTH - w.shape[1]), w.dtype)], axis=1).astype(BF16)
    wa = jnp.zeros((P_WIDTH - P_A, 2 * QK_W), F32)
    wa = wa.at[0:GLA_LOWRANK, 0:QK_W].set(gla_wa2_f[0])
    wa = wa.at[GLA_LOWRANK:2 * GLA_LOWRANK, QK_W:].set(gla_wa2_b[0]).astype(BF16)
    prm = dict(
        n1=norm1_w, win=win, wa=wa,
        ba=jnp.concatenate([gla_ba_f[0], gla_ba_b[0]])[None, :],
        snw=sgu_norm_w, sws=sgu_ws[0].astype(BF16),
        sbs=jnp.broadcast_to(sgu_bs[0][:, :, None], (SGU_GROUPS, SGU_CHUNK, SGU_CH)),
        gnw=gla_norm_w, wout=w_out[0].astype(BF16), n2=norm2_w, rw_t=router_w[0].T,
        w1=exp_w1[0], w3=exp_w3[0], w2=exp_w2[0], fw=final_norm_w[None, :])

    cvec = jnp.concatenate([c_ctx[None, :], c, jnp.zeros((SUBLANES - 1 - dec_batch, D_MODEL), F32)])
    mod = _modulation(cvec, ada_w[0], ada_b)
    e_tab = _pos_table()

    zero_state = jnp.zeros((batch, GLA_HEADS, GLA_DK, GLA_DV), F32)
    y_prompt, sf, sb = _trunk_and_norm(
        x_prompt, e_tab, mod, zero_state, zero_state, prm,
        nseq=batch, seq_len=seq, add_pos=False, mod_base=0, mod_per_seq=False)
    y_sample, _, _ = _trunk_and_norm(
        x_sample, e_tab, mod, state_gla_fwd[:, 0], state_gla_bwd[:, 0], prm,
        nseq=dec_batch, seq_len=dec_seq, add_pos=True, mod_base=1, mod_per_seq=True)
    return (y_prompt, y_sample, sf[:, None], sb[:, None])
```

```python
import functools
import math

import jax
import jax.numpy as jnp
from jax import lax
from jax.experimental import pallas as pl
from jax.experimental.pallas import tpu as pltpu

F32 = jnp.float32
BF16 = jnp.bfloat16
I32 = jnp.int32

D_MODEL = 1024
GRID_W = 64
GLA_HEADS = 4
GLA_DK = 64
GLA_DV = 128
GLA_WIDTH = GLA_HEADS * GLA_DV
QK_W = GLA_HEADS * GLA_DK
GLA_LOWRANK = 16
GLA_GATE_NORM = 16.0
GLA_CHUNK = 64
SGU_WIDTH = 512
SGU_GROUPS = 4
SGU_CH = 128
SGU_CHUNK = 128
N_EXPERTS = 16
EC_CAPACITY_FACTOR = 2
D_EXPERT = 2048
EPS = 1e-6

SUBLANES = 8
LANES = 128
TILES_PER_TOKEN = D_MODEL // LANES

TOKEN_BLOCK = 256
SEQ_GROUP = 4
P_Q, P_K, P_V, P_G, P_U, P_SV, P_A = 0, 256, 512, 1024, 1536, 2048, 2560
P_WIDTH = 2688
EXPERT_F_BLOCK = 512
EXPERT_N_BLOCK = 256
N_HID_STEPS = D_EXPERT // EXPERT_F_BLOCK
EXPERT_STEPS = N_HID_STEPS + D_MODEL // EXPERT_N_BLOCK
VMEM_LIMIT = 56 * 1024 * 1024


def _dot(a, b):
    return jnp.dot(a.astype(BF16), b.astype(BF16), preferred_element_type=F32)


def _dot_nt(a, b):
    return lax.dot_general(a.astype(BF16), b.astype(BF16), (((1,), (1,)), ((), ())),
                           preferred_element_type=F32)


def _dot_tn(a, b):
    return lax.dot_general(a.astype(BF16), b.astype(BF16), (((0,), (0,)), ((), ())),
                           preferred_element_type=F32)


def _dot_f32(a, b, dims=(((1,), (0,)), ((), ()))):
    return lax.dot_general(a, b, dims, precision=lax.Precision.HIGHEST, preferred_element_type=F32)


def _split_bf16(x, terms):
    parts = []
    for _ in range(terms - 1):
        part = x.astype(BF16)
        parts.append(part)
        x = x - part.astype(F32)
    parts.append(x.astype(BF16))
    return parts


def _select_dot(sel, x):
    s = sel.astype(BF16)
    hi, mid, lo = _split_bf16(x, 3)
    return (jnp.dot(s, lo, preferred_element_type=F32) + jnp.dot(s, mid, preferred_element_type=F32)
            + jnp.dot(s, hi, preferred_element_type=F32))


def _dot_nt_3pass(a, b):
    a_hi, a_lo = _split_bf16(a, 2)
    b_hi, b_lo = _split_bf16(b, 2)
    nt = lambda x, y: lax.dot_general(x, y, (((1,), (1,)), ((), ())), preferred_element_type=F32)
    return (nt(a_hi, b_lo) + nt(a_lo, b_hi)) + nt(a_hi, b_hi)


def _rms(x, w):
    return x * lax.rsqrt(jnp.mean(x * x, axis=-1, keepdims=True) + EPS) * w


def _compiler_params(sem):
    return pltpu.CompilerParams(dimension_semantics=sem, vmem_limit_bytes=VMEM_LIMIT)


def _mod_kernel(c_ref, w_ref, b_ref, o_ref):
    o_ref[...] = _dot(jax.nn.silu(c_ref[...]), w_ref[...]) + b_ref[...]


def _modulation(cvec, ada_w, ada_b):
    n = ada_w.shape[1]
    bn = 1536
    return pl.pallas_call(
        _mod_kernel,
        grid=(n // bn,),
        in_specs=[pl.BlockSpec((SUBLANES, D_MODEL), lambda j: (0, 0)),
                  pl.BlockSpec((D_MODEL, bn), lambda j: (0, j)),
                  pl.BlockSpec((1, bn), lambda j: (0, j))],
        out_specs=pl.BlockSpec((SUBLANES, bn), lambda j: (0, j)),
        out_shape=jax.ShapeDtypeStruct((SUBLANES, n), F32),
        compiler_params=_compiler_params(("arbitrary",)),
        name="adaln_mod",
    )(cvec, ada_w, ada_b)


def _pos_kernel(o_ref):
    nf = D_MODEL // 4
    p = lax.broadcasted_iota(I32, (GRID_W, nf), 0).astype(F32)
    i = lax.broadcasted_iota(I32, (GRID_W, nf), 1).astype(F32)
    omega = jnp.exp(i * (-math.log(10000.0) / nf))
    a = p * omega
    o_ref[:, 0:nf] = jnp.sin(a)
    o_ref[:, nf:2 * nf] = jnp.cos(a)


def _pos_table():
    return pl.pallas_call(
        _pos_kernel,
        out_shape=jax.ShapeDtypeStruct((GRID_W, D_MODEL // 2), F32),
        name="sincos_table",
    )()


def _add_pos(x, e_ref, blk, add_pos):
    if not add_pos:
        return x
    half = D_MODEL // 2
    e_all = e_ref[...]
    rows = []
    for j in range(TOKEN_BLOCK // GRID_W):
        xj = x[j * GRID_W:(j + 1) * GRID_W]
        e_row = e_ref[pl.ds(blk * (TOKEN_BLOCK // GRID_W) + j, 1), :]
        rows.append(jnp.concatenate([xj[:, 0:half] + e_row, xj[:, half:] + e_all], axis=1))
    return jnp.concatenate(rows, axis=0)


def _chunk_masks():
    r = lax.broadcasted_iota(I32, (TOKEN_BLOCK, TOKEN_BLOCK), 0)
    c = lax.broadcasted_iota(I32, (TOKEN_BLOCK, TOKEN_BLOCK), 1)
    same = (r // GLA_CHUNK) == (c // GLA_CHUNK)
    return same & (c <= r), same & (c >= r)


def _gla_direction(q, k, v, cum, fwd, att_mask, st_ref):
    qe = q * jnp.exp(cum)
    ke = k * jnp.exp(-cum)
    yield
    lane = lax.broadcasted_iota(I32, (1, LANES), 1)
    o_intra = []
    for pair in range(2):
        qp = qe[:, pair * LANES:(pair + 1) * LANES]
        kp = ke[:, pair * LANES:(pair + 1) * LANES]
        for hh in range(2):
            qm = jnp.where((lane // GLA_DK) == hh, qp, 0.0)
            att = jnp.where(att_mask, _dot_nt(qm, kp), 0.0)
            head = 2 * pair + hh
            o_intra.append(_dot(att, v[:, head * GLA_DV:(head + 1) * GLA_DV]))
            yield
    o_intra = jnp.concatenate(o_intra, axis=1)

    er = lax.broadcasted_iota(I32, (2 * GLA_DV, 2 * GLA_DK), 0)
    dc = lax.broadcasted_iota(I32, (2 * GLA_DV, 2 * GLA_DK), 1)
    same_head = (er // GLA_DV) == (dc // GLA_DK)
    n_chunks = TOKEN_BLOCK // GLA_CHUNK
    o_inter = [None] * n_chunks
    for c in (range(n_chunks) if fwd else reversed(range(n_chunks))):
        r0 = c * GLA_CHUNK
        rows = slice(r0, r0 + GLA_CHUNK)
        last = cum[r0 + GLA_CHUNK - 1:r0 + GLA_CHUNK] if fwd else cum[r0:r0 + 1]
        kd = k[rows] * jnp.exp(last - cum[rows])
        dec = jnp.exp(last)
        parts = []
        for pair in range(2):
            dl = slice(pair * LANES, (pair + 1) * LANES)
            st = st_ref[pair]
            parts.append(_dot_nt(qe[rows, dl], st))
            ds_t = _dot_tn(v[rows, pair * 2 * GLA_DV:(pair + 1) * 2 * GLA_DV], kd[:, dl])
            st_ref[pair] = dec[:, dl] * st + jnp.where(same_head, ds_t, 0.0)
        o_inter[c] = jnp.concatenate(parts, axis=1)
        yield
    return o_intra + jnp.concatenate(o_inter, axis=0)


def _interleave(chains):
    chains = list(chains)
    done = [False] * len(chains)
    tick = 0
    while not all(done):
        for i, ch in enumerate(chains):
            if tick >= i and not done[i]:
                try:
                    next(ch)
                except StopIteration:
                    done[i] = True
        tick += 1


def _load_state(s0_ref, u, st_ref):
    zero = jnp.zeros((GLA_DV, GLA_DK), F32)
    for pair in range(2):
        a = s0_ref[u, 2 * pair].T
        b = s0_ref[u, 2 * pair + 1].T
        st_ref[pair] = jnp.concatenate(
            [jnp.concatenate([a, zero], axis=1), jnp.concatenate([zero, b], axis=1)], axis=0)


def _store_state(st_ref, sfin_ref, u):
    for pair in range(2):
        st = st_ref[pair]
        sfin_ref[u, 2 * pair] = st[0:GLA_DV, 0:GLA_DK].T
        sfin_ref[u, 2 * pair + 1] = st[GLA_DV:2 * GLA_DV, GLA_DK:2 * GLA_DK].T


def _mod_row(mod_ref, row, part):
    return mod_ref[pl.ds(row, 1), part * D_MODEL:(part + 1) * D_MODEL]


def _mixer_fwd_kernel(group, add_pos, mod_base, mod_per_seq, nb,
                      x_ref, e_ref, mod_ref, n1_ref, win_ref, wa_ref, ba_ref,
                      snw_ref, sws_ref, sbs_ref, s0_ref,
                      qkv_ref, g_ref, lab_ref, s_ref, of_ref, sfin_ref,
                      st_ref):
    grp = pl.program_id(0)
    blk = pl.program_id(1)
    lo_mask, _ = _chunk_masks()

    @pl.when(blk == 0)
    def _():
        for u in range(group):
            _load_state(s0_ref, u, st_ref.at[u])

    def chain(u):
        row = mod_base + ((grp * group + u) if mod_per_seq else 0)
        st_u = st_ref.at[u]
        xin = _add_pos(x_ref[u], e_ref, blk, add_pos)
        h = _rms(xin, n1_ref[...]) * (1.0 + _mod_row(mod_ref, row, 1)) + _mod_row(mod_ref, row, 0)
        yield
        hb = h.astype(BF16)
        p_parts = []
        for c0, c1 in ((P_Q, P_G), (P_G, P_SV), (P_SV, P_WIDTH)):
            p_parts.append(jnp.dot(hb, win_ref[:, c0:c1], preferred_element_type=F32))
            yield
        p = jnp.concatenate(p_parts, axis=1)
        q = p[:, P_Q:P_K] * (GLA_DK ** -0.5)
        k = p[:, P_K:P_V]
        v = p[:, P_V:P_G]
        z = _dot(p[:, P_A:P_WIDTH], wa_ref[...]) + ba_ref[...]
        la = (jnp.minimum(z, 0.0) - jnp.log1p(jnp.exp(-jnp.abs(z)))) * (1.0 / GLA_GATE_NORM)
        qkv_ref[u] = jnp.concatenate([q, k, v], axis=1)
        g_ref[u] = p[:, P_G:P_U]
        lab_ref[u] = la[:, QK_W:2 * QK_W]
        yield

        ug = jax.nn.gelu(p[:, P_U:P_SV])
        yield
        vg = jax.nn.gelu(p[:, P_SV:P_A])
        yield
        s_cols = []
        for gi in range(SGU_GROUPS):
            cols = slice(gi * SGU_CH, (gi + 1) * SGU_CH)
            vn = _rms(vg[:, cols], snw_ref[:, cols])
            rhs = jnp.concatenate([vn[0:SGU_CHUNK], vn[SGU_CHUNK:2 * SGU_CHUNK]], axis=1)
            vm = _dot(sws_ref[gi], rhs) + jnp.concatenate([sbs_ref[gi], sbs_ref[gi]], axis=1)
            vm = jnp.concatenate([vm[:, 0:SGU_CH], vm[:, SGU_CH:2 * SGU_CH]], axis=0)
            s_cols.append(ug[:, cols] * vm)
        s_ref[u] = jnp.concatenate(s_cols, axis=1)
        yield

        cum = _select_dot(lo_mask, la[:, 0:QK_W])
        yield
        of_ref[u] = yield from _gla_direction(q, k, v, cum, True, lo_mask, st_u)

    _interleave(chain(u) for u in range(group))

    @pl.when(blk == nb - 1)
    def _():
        for u in range(group):
            _store_state(st_ref.at[u], sfin_ref, u)


def _mixer_fwd(x, e_tab, mod, n1, win, wa, ba, snw, sws, sbs, s0, *, nseq, nb, add_pos,
               mod_base, mod_per_seq):
    seq_len = nb * TOKEN_BLOCK
    group = min(SEQ_GROUP, nseq)
    tok = lambda w: pl.BlockSpec((group, TOKEN_BLOCK, w), lambda s, i: (s, i, 0))
    full = lambda a: pl.BlockSpec(a.shape, lambda s, i: (0,) * a.ndim)
    st_spec = pl.BlockSpec((group, GLA_HEADS, GLA_DK, GLA_DV), lambda s, i: (s, 0, 0, 0))
    act = lambda w: jax.ShapeDtypeStruct((nseq, seq_len, w), F32)
    kern = functools.partial(_mixer_fwd_kernel, group, add_pos, mod_base, mod_per_seq, nb)
    return pl.pallas_call(
        kern,
        grid=(nseq // group, nb),
        in_specs=[tok(D_MODEL), full(e_tab), full(mod), full(n1), full(win), full(wa), full(ba),
                  full(snw), full(sws), full(sbs), st_spec],
        out_specs=[tok(1024), tok(GLA_WIDTH), tok(QK_W), tok(SGU_WIDTH), tok(GLA_WIDTH), st_spec],
        out_shape=[act(1024), act(GLA_WIDTH), act(QK_W), act(SGU_WIDTH), act(GLA_WIDTH),
                   jax.ShapeDtypeStruct((nseq, GLA_HEADS, GLA_DK, GLA_DV), F32)],
        scratch_shapes=[pltpu.VMEM((group, 2, 2 * GLA_DV, 2 * GLA_DK), F32)],
        compiler_params=_compiler_params(("arbitrary", "arbitrary")),
        name="mixer_fwd",
    )(x, e_tab, mod, n1, win, wa, ba, snw, sws, sbs, s0)


def _mixer_bwd_kernel(group, add_pos, mod_base, mod_per_seq, nb,
                      x_ref, e_ref, mod_ref, qkv_ref, g_ref, lab_ref, s_ref, of_ref,
                      gnw_ref, wout_ref, n2_ref, rw_ref, s0_ref,
                      x1_ref, h2t_ref, probs_ref, sfin_ref,
                      st_ref):
    grp = pl.program_id(0)
    step = pl.program_id(1)
    blk = nb - 1 - step
    _, hi_mask = _chunk_masks()

    @pl.when(step == 0)
    def _():
        for u in range(group):
            _load_state(s0_ref, u, st_ref.at[u])

    def chain(u):
        row = mod_base + ((grp * group + u) if mod_per_seq else 0)
        st_u = st_ref.at[u]
        qkv = qkv_ref[u]
        q, k, v = qkv[:, 0:QK_W], qkv[:, QK_W:2 * QK_W], qkv[:, 2 * QK_W:]
        cum = _select_dot(hi_mask, lab_ref[u])
        yield
        o_b = yield from _gla_direction(q, k, v, cum, False, hi_mask, st_u)
        o = of_ref[u] + o_b
        g = g_ref[u]
        cols = []
        for head in range(GLA_HEADS):
            hs = slice(head * GLA_DV, (head + 1) * GLA_DV)
            cols.append(_rms(o[:, hs], gnw_ref[...]) * jax.nn.silu(g[:, hs]))
        cols.append(s_ref[u])
        yield
        y = _dot(jnp.concatenate(cols, axis=1), wout_ref[...])
        yield

        xin = _add_pos(x_ref[u], e_ref, blk, add_pos)
        x1 = xin + _mod_row(mod_ref, row, 2) * y
        x1_ref[u] = x1
        h2 = _rms(x1, n2_ref[...]) * (1.0 + _mod_row(mod_ref, row, 4)) + _mod_row(mod_ref, row, 3)
        for s in range(TILES_PER_TOKEN):
            h2t_ref[u, pl.ds(s, TOKEN_BLOCK, stride=TILES_PER_TOKEN), :] = h2[:, s * LANES:(s + 1) * LANES]
        yield

        logits = _dot_nt_3pass(rw_ref[...], h2)
        m = jnp.max(logits, axis=0, keepdims=True)
        ex = jnp.exp(logits - m)
        probs_ref[u, 0] = ex / jnp.sum(ex, axis=0, keepdims=True)

    _interleave(chain(u) for u in range(group))

    @pl.when(step == nb - 1)
    def _():
        for u in range(group):
            _store_state(st_ref.at[u], sfin_ref, u)


def _mixer_bwd(x, e_tab, mod, qkv, g, lab, s, of, gnw, wout, n2, rw_t, s0, *, nseq, nb, add_pos,
               mod_base, mod_per_seq):
    seq_len = nb * TOKEN_BLOCK
    group = min(SEQ_GROUP, nseq)
    tok = lambda w: pl.BlockSpec((group, TOKEN_BLOCK, w), lambda s_, i: (s_, nb - 1 - i, 0))
    full = lambda a: pl.BlockSpec(a.shape, lambda s_, i: (0,) * a.ndim)
    st_spec = pl.BlockSpec((group, GLA_HEADS, GLA_DK, GLA_DV), lambda s_, i: (s_, 0, 0, 0))
    kern = functools.partial(_mixer_bwd_kernel, group, add_pos, mod_base, mod_per_seq, nb)
    return pl.pallas_call(
        kern,
        grid=(nseq // group, nb),
        in_specs=[tok(D_MODEL), full(e_tab), full(mod), tok(1024), tok(GLA_WIDTH), tok(QK_W),
                  tok(SGU_WIDTH), tok(GLA_WIDTH), full(gnw), full(wout), full(n2), full(rw_t),
                  st_spec],
        out_specs=[tok(D_MODEL),
                   pl.BlockSpec((group, TOKEN_BLOCK * TILES_PER_TOKEN, LANES),
                                lambda s_, i: (s_, nb - 1 - i, 0)),
                   pl.BlockSpec((group, 1, N_EXPERTS, TOKEN_BLOCK),
                                lambda s_, i: (s_, nb - 1 - i, 0, 0)),
                   st_spec],
        out_shape=[jax.ShapeDtypeStruct((nseq, seq_len, D_MODEL), F32),
                   jax.ShapeDtypeStruct((nseq, seq_len * TILES_PER_TOKEN, LANES), F32),
                   jax.ShapeDtypeStruct((nseq, nb, N_EXPERTS, TOKEN_BLOCK), F32),
                   jax.ShapeDtypeStruct((nseq, GLA_HEADS, GLA_DK, GLA_DV), F32)],
        scratch_shapes=[pltpu.VMEM((group, 2, 2 * GLA_DV, 2 * GLA_DK), F32)],
        compiler_params=_compiler_params(("arbitrary", "arbitrary")),
        name="mixer_bwd",
    )(x, e_tab, mod, qkv, g, lab, s, of, gnw, wout, n2, rw_t, s0)


def _route_kernel(n_tok, cap, probs_ref, idx_ref, gate_ref, xs_ref, ps_ref):
    n_blk = n_tok // TOKEN_BLOCK
    n_chunk = n_tok // LANES
    probs = jnp.concatenate([probs_ref[b] for b in range(n_blk)], axis=1)
    capf = jnp.float32(cap)

    def count(mask):
        return jnp.sum(mask.astype(F32), axis=1, keepdims=True)

    def as_f32(bits):
        return lax.bitcast_convert_type(bits, F32)

    def thr_step(_, lohi):
        lo, hi = lohi
        mid = lo + ((hi - lo + 1) >> 1)
        ok = count(probs >= as_f32(mid)) >= capf
        return jnp.where(ok, mid, lo), jnp.where(ok, hi, mid - 1)

    lo0 = jnp.zeros((N_EXPERTS, 1), I32)
    hi0 = jnp.full((N_EXPERTS, 1), 0x3F800000, I32)
    thr, _ = lax.fori_loop(0, 31, thr_step, (lo0, hi0))
    gt = probs >= as_f32(thr + 1)
    eq = (probs >= as_f32(thr)) & jnp.logical_not(gt)
    need = capf - count(gt)
    tok = lax.broadcasted_iota(I32, (N_EXPERTS, n_tok), 1)

    def tie_step(_, lohi):
        lo, hi = lohi
        mid = (lo + hi) >> 1
        ok = count(eq & (tok <= mid)) >= need
        return jnp.where(ok, lo, mid + 1), jnp.where(ok, mid, hi)

    n_bits = max(1, (n_tok - 1).bit_length())
    cut, _ = lax.fori_loop(0, n_bits, tie_step,
                           (jnp.zeros((N_EXPERTS, 1), I32), jnp.full((N_EXPERTS, 1), n_tok - 1, I32)))
    sel = (gt | (eq & (tok <= cut))).astype(F32)

    xs_ref[...] = jnp.concatenate([sel[:, c * LANES:(c + 1) * LANES] for c in range(n_chunk)], axis=0)
    ps_ref[...] = jnp.concatenate([probs[:, c * LANES:(c + 1) * LANES] for c in range(n_chunk)], axis=0)

    li = lax.broadcasted_iota(I32, (LANES, LANES), 0)
    lj = lax.broadcasted_iota(I32, (LANES, LANES), 1)
    upper = (li <= lj).astype(F32)
    ci = lax.broadcasted_iota(I32, (n_chunk, n_chunk), 0)
    cj = lax.broadcasted_iota(I32, (n_chunk, n_chunk), 1)
    lower = (cj <= ci).astype(F32)
    slot = lax.broadcasted_iota(I32, (1, cap), 1).astype(F32)
    chunk_id = lax.broadcasted_iota(I32, (n_chunk, cap), 0).astype(F32)
    lane_id = lax.broadcasted_iota(I32, (LANES, cap), 0).astype(F32)
    reps = cap // LANES

    def per_expert(e, _):
        x = xs_ref[pl.ds(e, n_chunk, stride=N_EXPERTS), :]
        pe = ps_ref[pl.ds(e, n_chunk, stride=N_EXPERTS), :]
        ploc = _dot(x, upper)
        tot = jnp.broadcast_to(ploc[:, LANES - 1:LANES], (n_chunk, LANES))
        cum = _dot(lower, tot)
        cum_w = jnp.concatenate([cum] * reps, axis=1)
        base_w = jnp.concatenate([cum - tot] * reps, axis=1)
        chunk_of = jnp.sum((cum_w <= slot).astype(F32), axis=0, keepdims=True)
        onehot = chunk_id == chunk_of
        local = slot - jnp.sum(jnp.where(onehot, base_w, 0.0), axis=0, keepdims=True)
        pref = _dot_tn(ploc, onehot.astype(F32))
        lane_of = jnp.sum((pref <= local).astype(F32), axis=0, keepdims=True)
        idx_ref[pl.ds(e, 1), :] = (chunk_of * LANES + lane_of).astype(I32)
        pg = _dot_f32(pe, onehot.astype(F32), (((0,), (0,)), ((), ())))
        gate_ref[pl.ds(e, 1), :] = jnp.sum(jnp.where(lane_id == lane_of, pg, 0.0), axis=0, keepdims=True)
        return 0

    lax.fori_loop(0, N_EXPERTS, per_expert, 0)


def _route(probs, n_tok, cap):
    return pl.pallas_call(
        functools.partial(_route_kernel, n_tok, cap),
        out_shape=[jax.ShapeDtypeStruct((N_EXPERTS, cap), I32),
                   jax.ShapeDtypeStruct((N_EXPERTS, cap), F32)],
        scratch_shapes=[pltpu.VMEM((n_tok // LANES * N_EXPERTS, LANES), F32),
                        pltpu.VMEM((n_tok // LANES * N_EXPERTS, LANES), F32)],
        compiler_params=pltpu.CompilerParams(vmem_limit_bytes=VMEM_LIMIT),
        name="route_topk",
    )(probs)


def _expert_kernel(cap, idx_ref, h2t_ref, w1_ref, w3_ref, w2_ref, ye_ref,
                   xe_ref, x2_ref, hid_ref, sem):
    e = pl.program_id(0)
    f = pl.program_id(1)
    slot = e % 2
    rows_per_step = cap // EXPERT_STEPS

    def start_row(expert, buf, j):
        src = pl.multiple_of(idx_ref[expert * cap + j] * TILES_PER_TOKEN, TILES_PER_TOKEN)
        dst = pl.multiple_of(j * TILES_PER_TOKEN, TILES_PER_TOKEN)
        pltpu.make_async_copy(h2t_ref.at[pl.ds(src, TILES_PER_TOKEN), :],
                              xe_ref.at[buf, pl.ds(dst, TILES_PER_TOKEN), :], sem.at[buf]).start()

    def wait_rows(buf):
        pltpu.make_async_copy(h2t_ref.at[pl.ds(0, cap * TILES_PER_TOKEN), :], xe_ref.at[buf],
                              sem.at[buf]).wait()

    def prefetch_next():
        nxt = jnp.minimum(e + 1, N_EXPERTS - 1)
        first = f * rows_per_step
        for j in range(rows_per_step):
            start_row(nxt, 1 - slot, first + j)

    @pl.when((e == 0) & (f == 0))
    def _():
        def issue(j, _):
            start_row(0, 0, j)
            return 0

        lax.fori_loop(0, cap, issue, 0, unroll=8)

    @pl.when(f == 0)
    def _():
        wait_rows(slot)
        for s in range(TILES_PER_TOKEN):
            x2_ref[:, s * LANES:(s + 1) * LANES] = (
                xe_ref[slot, pl.ds(s, cap, stride=TILES_PER_TOKEN), :].astype(BF16))

    @pl.when(f < N_HID_STEPS)
    def _():
        prefetch_next()
        x2 = x2_ref[...]
        a = jnp.dot(x2, w1_ref[0].astype(BF16), preferred_element_type=F32)
        b = jnp.dot(x2, w3_ref[0].astype(BF16), preferred_element_type=F32)
        hid_ref[f] = (jax.nn.silu(a) * b).astype(BF16)

    @pl.when(f >= N_HID_STEPS)
    def _():
        prefetch_next()
        w2 = w2_ref[0].astype(BF16)
        out = jnp.dot(hid_ref[0], w2[0:EXPERT_F_BLOCK], preferred_element_type=F32)
        for kb in range(1, N_HID_STEPS):
            out += jnp.dot(hid_ref[kb], w2[kb * EXPERT_F_BLOCK:(kb + 1) * EXPERT_F_BLOCK],
                           preferred_element_type=F32)
        tile0 = (f - N_HID_STEPS) * (EXPERT_N_BLOCK // LANES)
        for i in range(EXPERT_N_BLOCK // LANES):
            ye_ref[pl.ds(tile0 + i, cap, stride=TILES_PER_TOKEN), :] = out[:, i * LANES:(i + 1) * LANES]

    @pl.when((e == N_EXPERTS - 1) & (f == EXPERT_STEPS - 1))
    def _():
        wait_rows(1 - slot)


def _experts(idx_flat, h2t, w1, w3, w2, cap):
    hid_blk = lambda e, f, idx: (e, 0, jnp.minimum(f, N_HID_STEPS - 1))
    out_blk = lambda e, f, idx: (e, 0, jnp.maximum(f - N_HID_STEPS, 0))
    grid_spec = pltpu.PrefetchScalarGridSpec(
        num_scalar_prefetch=1,
        grid=(N_EXPERTS, EXPERT_STEPS),
        in_specs=[pl.BlockSpec(memory_space=pl.ANY),
                  pl.BlockSpec((1, D_MODEL, EXPERT_F_BLOCK), hid_blk),
                  pl.BlockSpec((1, D_MODEL, EXPERT_F_BLOCK), hid_blk),
                  pl.BlockSpec((1, D_EXPERT, EXPERT_N_BLOCK), out_blk)],
        out_specs=pl.BlockSpec((cap * TILES_PER_TOKEN, LANES), lambda e, f, idx: (e, 0)),
        scratch_shapes=[pltpu.VMEM((2, cap * TILES_PER_TOKEN, LANES), F32),
                        pltpu.VMEM((cap, D_MODEL), BF16),
                        pltpu.VMEM((N_HID_STEPS, cap, EXPERT_F_BLOCK), BF16),
                        pltpu.SemaphoreType.DMA((2,))],
    )
    return pl.pallas_call(
        functools.partial(_expert_kernel, cap),
        grid_spec=grid_spec,
        out_shape=jax.ShapeDtypeStruct((N_EXPERTS * cap * TILES_PER_TOKEN, LANES), F32),
        compiler_params=_compiler_params(("arbitrary", "arbitrary")),
        name="expert_swiglu",
    )(idx_flat, h2t, w1, w3, w2)


COMBINE_BATCH = 8
ZERO_ROWS = 512


def _combine_kernel(cap, n_tok, idx_ref, gate_ref, ye_ref, out_ref, acc_ref, sem):
    e = pl.program_id(0)

    @pl.when(e == 0)
    def _():
        def zero(i, _):
            r = pl.multiple_of(i * ZERO_ROWS, ZERO_ROWS)
            acc_ref[pl.ds(r, ZERO_ROWS), :] = jnp.zeros((ZERO_ROWS, LANES), F32)
            return 0

        lax.fori_loop(0, n_tok * TILES_PER_TOKEN // ZERO_ROWS, zero, 0)

    def batch(jb, _):
        vals = []
        for u in range(COMBINE_BATCH):
            j = jb * COMBINE_BATCH + u
            t = pl.multiple_of(idx_ref[e * cap + j] * TILES_PER_TOKEN, TILES_PER_TOKEN)
            gate = gate_ref[e * cap + j]
            src = pl.multiple_of(j * TILES_PER_TOKEN, TILES_PER_TOKEN)
            vals.append((t, acc_ref[pl.ds(t, TILES_PER_TOKEN), :]
                         + ye_ref[pl.ds(src, TILES_PER_TOKEN), :] * gate))
        for t, val in vals:
            acc_ref[pl.ds(t, TILES_PER_TOKEN), :] = val
        return 0

    lax.fori_loop(0, cap // COMBINE_BATCH, batch, 0)

    @pl.when(e == N_EXPERTS - 1)
    def _():
        cp = pltpu.make_async_copy(acc_ref, out_ref, sem)
        cp.start()
        cp.wait()


def _combine(idx_flat, gate_flat, ye, cap, n_tok):
    grid_spec = pltpu.PrefetchScalarGridSpec(
        num_scalar_prefetch=2,
        grid=(N_EXPERTS,),
        in_specs=[pl.BlockSpec((cap * TILES_PER_TOKEN, LANES), lambda e, idx, gate: (e, 0))],
        out_specs=pl.BlockSpec(memory_space=pl.ANY),
        scratch_shapes=[pltpu.VMEM((n_tok * TILES_PER_TOKEN, LANES), F32),
                        pltpu.SemaphoreType.DMA],
    )
    return pl.pallas_call(
        functools.partial(_combine_kernel, cap, n_tok),
        grid_spec=grid_spec,
        out_shape=jax.ShapeDtypeStruct((n_tok * TILES_PER_TOKEN, LANES), F32),
        compiler_params=_compiler_params(("arbitrary",)),
        name="moe_combine",
    )(idx_flat, gate_flat, ye)


def _final_kernel(mod_base, mod_per_seq, nb, x1_ref, moe_ref, mod_ref, fw_ref, y_ref):
    row = mod_base + ((pl.program_id(0) // nb) if mod_per_seq else 0)
    moe = jnp.concatenate(
        [moe_ref[pl.ds(s, TOKEN_BLOCK, stride=TILES_PER_TOKEN), :] for s in range(TILES_PER_TOKEN)],
        axis=1)
    y_ref[...] = _rms(x1_ref[...] + _mod_row(mod_ref, row, 5) * moe, fw_ref[...])


def _final(x1, moe_t, mod, fw, *, nb, mod_base, mod_per_seq):
    n = x1.shape[0]
    return pl.pallas_call(
        functools.partial(_final_kernel, mod_base, mod_per_seq, nb),
        grid=(n // TOKEN_BLOCK,),
        in_specs=[pl.BlockSpec((TOKEN_BLOCK, D_MODEL), lambda i: (i, 0)),
                  pl.BlockSpec((TOKEN_BLOCK * TILES_PER_TOKEN, LANES), lambda i: (i, 0)),
                  pl.BlockSpec(mod.shape, lambda i: (0, 0)),
                  pl.BlockSpec(fw.shape, lambda i: (0, 0))],
        out_specs=pl.BlockSpec((TOKEN_BLOCK, D_MODEL), lambda i: (i, 0)),
        out_shape=jax.ShapeDtypeStruct((n, D_MODEL), F32),
        compiler_params=_compiler_params(("arbitrary",)),
        name="final_norm",
    )(x1, moe_t, mod, fw)


def _trunk_and_norm(x, e_tab, mod, s0_f, s0_b, prm, *, nseq, seq_len, add_pos, mod_base, mod_per_seq):
    nb = seq_len // TOKEN_BLOCK
    n_tok = nseq * seq_len
    cap = EC_CAPACITY_FACTOR * n_tok // N_EXPERTS
    kw = dict(nseq=nseq, nb=nb, add_pos=add_pos, mod_base=mod_base, mod_per_seq=mod_per_seq)
    qkv, g, lab, s, of, sfin_f = _mixer_fwd(
        x, e_tab, mod, prm["n1"], prm["win"], prm["wa"], prm["ba"], prm["snw"], prm["sws"],
        prm["sbs"], s0_f, **kw)
    x1, h2t, probs, sfin_b = _mixer_bwd(
        x, e_tab, mod, qkv, g, lab, s, of, prm["gnw"], prm["wout"], prm["n2"], prm["rw_t"],
        s0_b, **kw)
    x1 = x1.reshape(n_tok, D_MODEL)
    h2t = h2t.reshape(n_tok * TILES_PER_TOKEN, LANES)
    probs = probs.reshape(n_tok // TOKEN_BLOCK, N_EXPERTS, TOKEN_BLOCK)
    idx, gates = _route(probs, n_tok, cap)
    idx_flat = idx.reshape(-1)
    ye = _experts(idx_flat, h2t, prm["w1"], prm["w3"], prm["w2"], cap)
    moe_t = _combine(idx_flat, gates.reshape(-1), ye, cap, n_tok)
    y = _final(x1, moe_t, mod, prm["fw"], nb=nb, mod_base=mod_base, mod_per_seq=mod_per_seq)
    return y.reshape(nseq, seq_len, D_MODEL), sfin_f, sfin_b


def kernel(x_prompt, x_sample, state_gla_fwd, state_gla_bwd, c, c_ctx, ada_w, ada_b, norm1_w, w_in, gla_wa2_f, gla_ba_f, gla_wa2_b, gla_ba_b, gla_norm_w, sgu_norm_w, sgu_ws, sgu_bs, w_out, norm2_w, router_w, exp_w1, exp_w3, exp_w2, final_norm_w):
    assert ada_w.shape[0] == 1, "single trunk layer"
    batch, seq, _ = x_prompt.shape
    dec_batch, dec_seq, _ = x_sample.shape

    w = w_in[0]
    off_af = 2 * QK_W + 2 * GLA_WIDTH
    off_u = off_af + 2 * GLA_LOWRANK
    win = jnp.concatenate(
        [w[:, :off_af], w[:, off_u:], w[:, off_af:off_u],
         jnp.zeros((D_MODEL, P_WIDTH - w.shape[1]), w.dtype)], axis=1).astype(BF16)
    wa = jnp.zeros((P_WIDTH - P_A, 2 * QK_W), F32)
    wa = wa.at[0:GLA_LOWRANK, 0:QK_W].set(gla_wa2_f[0])
    wa = wa.at[GLA_LOWRANK:2 * GLA_LOWRANK, QK_W:].set(gla_wa2_b[0]).astype(BF16)
    prm = dict(
        n1=norm1_w, win=win, wa=wa,
        ba=jnp.concatenate([gla_ba_f[0], gla_ba_b[0]])[None, :],
        snw=sgu_norm_w, sws=sgu_ws[0].astype(BF16),
        sbs=jnp.broadcast_to(sgu_bs[0][:, :, None], (SGU_GROUPS, SGU_CHUNK, SGU_CH)),
        gnw=gla_norm_w, wout=w_out[0].astype(BF16), n2=norm2_w, rw_t=router_w[0].T,
        w1=exp_w1[0], w3=exp_w3[0], w2=exp_w2[0], fw=final_norm_w[None, :])

    cvec = jnp.concatenate([c_ctx[None, :], c, jnp.zeros((SUBLANES - 1 - dec_batch, D_MODEL), F32)])
    mod = _modulation(cvec, ada_w[0], ada_b)
    e_tab = _pos_table()

    zero_state = jnp.zeros((batch, GLA_HEADS, GLA_DK, GLA_DV), F32)
    y_prompt, sf, sb = _trunk_and_norm(
        x_prompt, e_tab, mod, zero_state, zero_state, prm,
        nseq=batch, seq_len=seq, add_pos=False, mod_base=0, mod_per_seq=False)
    y_sample, _, _ = _trunk_and_norm(
        x_sample, e_tab, mod, state_gla_fwd[:, 0], state_gla_bwd[:, 0], prm,
        nseq=dec_batch, seq_len=dec_seq, add_pos=True, mod_base=1, mod_per_seq=True)
    return (y_prompt, y_sample, sf[:, None], sb[:, None])
```

```python
import functools
import math

import jax
import jax.numpy as jnp
from jax import lax
from jax.experimental import pallas as pl
from jax.experimental.pallas import tpu as pltpu

F32 = jnp.float32
BF16 = jnp.bfloat16
I32 = jnp.int32

D_MODEL = 1024
GRID_W = 64
GLA_HEADS = 4
GLA_DK = 64
GLA_DV = 128
GLA_WIDTH = GLA_HEADS * GLA_DV
QK_W = GLA_HEADS * GLA_DK
GLA_LOWRANK = 16
GLA_GATE_NORM = 16.0
GLA_CHUNK = 64
SGU_WIDTH = 512
SGU_GROUPS = 4
SGU_CH = 128
SGU_CHUNK = 128
N_EXPERTS = 16
EC_CAPACITY_FACTOR = 2
D_EXPERT = 2048
EPS = 1e-6

SUBLANES = 8
LANES = 128
TILES_PER_TOKEN = D_MODEL // LANES

TOKEN_BLOCK = 256
SEQ_GROUP = 4
P_Q, P_K, P_V, P_G, P_U, P_SV, P_A = 0, 256, 512, 1024, 1536, 2048, 2560
P_WIDTH = 2688
EXPERT_F_BLOCK = 1024
EXPERT_N_BLOCK = 512
N_HID_STEPS = D_EXPERT // EXPERT_F_BLOCK
EXPERT_STEPS = N_HID_STEPS + D_MODEL // EXPERT_N_BLOCK
VMEM_LIMIT = 56 * 1024 * 1024


def _dot(a, b):
    return jnp.dot(a.astype(BF16), b.astype(BF16), preferred_element_type=F32)


def _dot_nt(a, b):
    return lax.dot_general(a.astype(BF16), b.astype(BF16), (((1,), (1,)), ((), ())),
                           preferred_element_type=F32)


def _dot_tn(a, b):
    return lax.dot_general(a.astype(BF16), b.astype(BF16), (((0,), (0,)), ((), ())),
                           preferred_element_type=F32)


def _dot_f32(a, b, dims=(((1,), (0,)), ((), ()))):
    return lax.dot_general(a, b, dims, precision=lax.Precision.HIGHEST, preferred_element_type=F32)


def _split_bf16(x, terms):
    parts = []
    for _ in range(terms - 1):
        part = x.astype(BF16)
        parts.append(part)
        x = x - part.astype(F32)
    parts.append(x.astype(BF16))
    return parts


def _select_dot(sel, x):
    s = sel.astype(BF16)
    hi, mid, lo = _split_bf16(x, 3)
    return (jnp.dot(s, lo, preferred_element_type=F32) + jnp.dot(s, mid, preferred_element_type=F32)
            + jnp.dot(s, hi, preferred_element_type=F32))


def _dot_nt_3pass(a, b):
    a_hi, a_lo = _split_bf16(a, 2)
    b_hi, b_lo = _split_bf16(b, 2)
    nt = lambda x, y: lax.dot_general(x, y, (((1,), (1,)), ((), ())), preferred_element_type=F32)
    return (nt(a_hi, b_lo) + nt(a_lo, b_hi)) + nt(a_hi, b_hi)


def _rms(x, w):
    return x * lax.rsqrt(jnp.mean(x * x, axis=-1, keepdims=True) + EPS) * w


def _compiler_params(sem):
    return pltpu.CompilerParams(dimension_semantics=sem, vmem_limit_bytes=VMEM_LIMIT)


def _mod_kernel(c_ref, w_ref, b_ref, o_ref):
    o_ref[...] = _dot(jax.nn.silu(c_ref[...]), w_ref[...]) + b_ref[...]


def _modulation(cvec, ada_w, ada_b):
    n = ada_w.shape[1]
    bn = 1536
    return pl.pallas_call(
        _mod_kernel,
        grid=(n // bn,),
        in_specs=[pl.BlockSpec((SUBLANES, D_MODEL), lambda j: (0, 0)),
                  pl.BlockSpec((D_MODEL, bn), lambda j: (0, j)),
                  pl.BlockSpec((1, bn), lambda j: (0, j))],
        out_specs=pl.BlockSpec((SUBLANES, bn), lambda j: (0, j)),
        out_shape=jax.ShapeDtypeStruct((SUBLANES, n), F32),
        compiler_params=_compiler_params(("arbitrary",)),
        name="adaln_mod",
    )(cvec, ada_w, ada_b)


def _pos_kernel(o_ref):
    nf = D_MODEL // 4
    p = lax.broadcasted_iota(I32, (GRID_W, nf), 0).astype(F32)
    i = lax.broadcasted_iota(I32, (GRID_W, nf), 1).astype(F32)
    omega = jnp.exp(i * (-math.log(10000.0) / nf))
    a = p * omega
    o_ref[:, 0:nf] = jnp.sin(a)
    o_ref[:, nf:2 * nf] = jnp.cos(a)


def _pos_table():
    return pl.pallas_call(
        _pos_kernel,
        out_shape=jax.ShapeDtypeStruct((GRID_W, D_MODEL // 2), F32),
        name="sincos_table",
    )()


def _add_pos(x, e_ref, blk, add_pos):
    if not add_pos:
        return x
    half = D_MODEL // 2
    e_all = e_ref[...]
    rows = []
    for j in range(TOKEN_BLOCK // GRID_W):
        xj = x[j * GRID_W:(j + 1) * GRID_W]
        e_row = e_ref[pl.ds(blk * (TOKEN_BLOCK // GRID_W) + j, 1), :]
        rows.append(jnp.concatenate([xj[:, 0:half] + e_row, xj[:, half:] + e_all], axis=1))
    return jnp.concatenate(rows, axis=0)


def _chunk_masks():
    r = lax.broadcasted_iota(I32, (TOKEN_BLOCK, TOKEN_BLOCK), 0)
    c = lax.broadcasted_iota(I32, (TOKEN_BLOCK, TOKEN_BLOCK), 1)
    same = (r // GLA_CHUNK) == (c // GLA_CHUNK)
    return same & (c <= r), same & (c >= r)


def _gla_direction(q, k, v, cum, fwd, att_mask, st_ref):
    qe = q * jnp.exp(cum)
    ke = k * jnp.exp(-cum)
    yield
    lane = lax.broadcasted_iota(I32, (1, LANES), 1)
    o_intra = []
    for pair in range(2):
        qp = qe[:, pair * LANES:(pair + 1) * LANES]
        kp = ke[:, pair * LANES:(pair + 1) * LANES]
        for hh in range(2):
            qm = jnp.where((lane // GLA_DK) == hh, qp, 0.0)
            att = jnp.where(att_mask, _dot_nt(qm, kp), 0.0)
            head = 2 * pair + hh
            o_intra.append(_dot(att, v[:, head * GLA_DV:(head + 1) * GLA_DV]))
            yield
    o_intra = jnp.concatenate(o_intra, axis=1)

    er = lax.broadcasted_iota(I32, (2 * GLA_DV, 2 * GLA_DK), 0)
    dc = lax.broadcasted_iota(I32, (2 * GLA_DV, 2 * GLA_DK), 1)
    same_head = (er // GLA_DV) == (dc // GLA_DK)
    n_chunks = TOKEN_BLOCK // GLA_CHUNK
    o_inter = [None] * n_chunks
    for c in (range(n_chunks) if fwd else reversed(range(n_chunks))):
        r0 = c * GLA_CHUNK
        rows = slice(r0, r0 + GLA_CHUNK)
        last = cum[r0 + GLA_CHUNK - 1:r0 + GLA_CHUNK] if fwd else cum[r0:r0 + 1]
        kd = k[rows] * jnp.exp(last - cum[rows])
        dec = jnp.exp(last)
        parts = []
        for pair in range(2):
            dl = slice(pair * LANES, (pair + 1) * LANES)
            st = st_ref[pair]
            parts.append(_dot_nt(qe[rows, dl], st))
            ds_t = _dot_tn(v[rows, pair * 2 * GLA_DV:(pair + 1) * 2 * GLA_DV], kd[:, dl])
            st_ref[pair] = dec[:, dl] * st + jnp.where(same_head, ds_t, 0.0)
        o_inter[c] = jnp.concatenate(parts, axis=1)
        yield
    return o_intra + jnp.concatenate(o_inter, axis=0)


def _interleave(chains):
    chains = list(chains)
    done = [False] * len(chains)
    tick = 0
    while not all(done):
        for i, ch in enumerate(chains):
            if tick >= i and not done[i]:
                try:
                    next(ch)
                except StopIteration:
                    done[i] = True
        tick += 1


def _load_state(s0_ref, u, st_ref):
    zero = jnp.zeros((GLA_DV, GLA_DK), F32)
    for pair in range(2):
        a = s0_ref[u, 2 * pair].T
        b = s0_ref[u, 2 * pair + 1].T
        st_ref[pair] = jnp.concatenate(
            [jnp.concatenate([a, zero], axis=1), jnp.concatenate([zero, b], axis=1)], axis=0)


def _store_state(st_ref, sfin_ref, u):
    for pair in range(2):
        st = st_ref[pair]
        sfin_ref[u, 2 * pair] = st[0:GLA_DV, 0:GLA_DK].T
        sfin_ref[u, 2 * pair + 1] = st[GLA_DV:2 * GLA_DV, GLA_DK:2 * GLA_DK].T


def _mod_row(mod_ref, row, part):
    return mod_ref[pl.ds(row, 1), part * D_MODEL:(part + 1) * D_MODEL]


def _mixer_fwd_kernel(group, add_pos, mod_base, mod_per_seq, nb,
                      x_ref, e_ref, mod_ref, n1_ref, win_ref, wa_ref, ba_ref,
                      snw_ref, sws_ref, sbs_ref, s0_ref,
                      qkv_ref, g_ref, lab_ref, s_ref, of_ref, sfin_ref,
                      st_ref):
    grp = pl.program_id(0)
    blk = pl.program_id(1)
    lo_mask, _ = _chunk_masks()

    @pl.when(blk == 0)
    def _():
        for u in range(group):
            _load_state(s0_ref, u, st_ref.at[u])

    def chain(u):
        row = mod_base + ((grp * group + u) if mod_per_seq else 0)
        st_u = st_ref.at[u]
        xin = _add_pos(x_ref[u], e_ref, blk, add_pos)
        h = _rms(xin, n1_ref[...]) * (1.0 + _mod_row(mod_ref, row, 1)) + _mod_row(mod_ref, row, 0)
        yield
        hb = h.astype(BF16)
        p_parts = []
        for c0, c1 in ((P_Q, P_G), (P_G, P_SV), (P_SV, P_WIDTH)):
            p_parts.append(jnp.dot(hb, win_ref[:, c0:c1], preferred_element_type=F32))
            yield
        p = jnp.concatenate(p_parts, axis=1)
        q = p[:, P_Q:P_K] * (GLA_DK ** -0.5)
        k = p[:, P_K:P_V]
        v = p[:, P_V:P_G]
        z = _dot(p[:, P_A:P_WIDTH], wa_ref[...]) + ba_ref[...]
        la = (jnp.minimum(z, 0.0) - jnp.log1p(jnp.exp(-jnp.abs(z)))) * (1.0 / GLA_GATE_NORM)
        qkv_ref[u] = jnp.concatenate([q, k, v], axis=1)
        g_ref[u] = p[:, P_G:P_U]
        lab_ref[u] = la[:, QK_W:2 * QK_W]
        yield

        ug = jax.nn.gelu(p[:, P_U:P_SV])
        yield
        vg = jax.nn.gelu(p[:, P_SV:P_A])
        yield
        s_cols = []
        for gi in range(SGU_GROUPS):
            cols = slice(gi * SGU_CH, (gi + 1) * SGU_CH)
            vn = _rms(vg[:, cols], snw_ref[:, cols])
            rhs = jnp.concatenate([vn[0:SGU_CHUNK], vn[SGU_CHUNK:2 * SGU_CHUNK]], axis=1)
            vm = _dot(sws_ref[gi], rhs) + jnp.concatenate([sbs_ref[gi], sbs_ref[gi]], axis=1)
            vm = jnp.concatenate([vm[:, 0:SGU_CH], vm[:, SGU_CH:2 * SGU_CH]], axis=0)
            s_cols.append(ug[:, cols] * vm)
        s_ref[u] = jnp.concatenate(s_cols, axis=1)
        yield

        cum = _select_dot(lo_mask, la[:, 0:QK_W])
        yield
        of_ref[u] = yield from _gla_direction(q, k, v, cum, True, lo_mask, st_u)

    _interleave(chain(u) for u in range(group))

    @pl.when(blk == nb - 1)
    def _():
        for u in range(group):
            _store_state(st_ref.at[u], sfin_ref, u)


def _mixer_fwd(x, e_tab, mod, n1, win, wa, ba, snw, sws, sbs, s0, *, nseq, nb, add_pos,
               mod_base, mod_per_seq):
    seq_len = nb * TOKEN_BLOCK
    group = min(SEQ_GROUP, nseq)
    tok = lambda w: pl.BlockSpec((group, TOKEN_BLOCK, w), lambda s, i: (s, i, 0))
    full = lambda a: pl.BlockSpec(a.shape, lambda s, i: (0,) * a.ndim)
    st_spec = pl.BlockSpec((group, GLA_HEADS, GLA_DK, GLA_DV), lambda s, i: (s, 0, 0, 0))
    act = lambda w: jax.ShapeDtypeStruct((nseq, seq_len, w), F32)
    kern = functools.partial(_mixer_fwd_kernel, group, add_pos, mod_base, mod_per_seq, nb)
    return pl.pallas_call(
        kern,
        grid=(nseq // group, nb),
        in_specs=[tok(D_MODEL), full(e_tab), full(mod), full(n1), full(win), full(wa), full(ba),
                  full(snw), full(sws), full(sbs), st_spec],
        out_specs=[tok(1024), tok(GLA_WIDTH), tok(QK_W), tok(SGU_WIDTH), tok(GLA_WIDTH), st_spec],
        out_shape=[act(1024), act(GLA_WIDTH), act(QK_W), act(SGU_WIDTH), act(GLA_WIDTH),
                   jax.ShapeDtypeStruct((nseq, GLA_HEADS, GLA_DK, GLA_DV), F32)],
        scratch_shapes=[pltpu.VMEM((group, 2, 2 * GLA_DV, 2 * GLA_DK), F32)],
        compiler_params=_compiler_params(("arbitrary", "arbitrary")),
        name="mixer_fwd",
    )(x, e_tab, mod, n1, win, wa, ba, snw, sws, sbs, s0)


def _mixer_bwd_kernel(group, add_pos, mod_base, mod_per_seq, nb,
                      x_ref, e_ref, mod_ref, qkv_ref, g_ref, lab_ref, s_ref, of_ref,
                      gnw_ref, wout_ref, n2_ref, rw_ref, s0_ref,
                      x1_ref, h2t_ref, probs_ref, sfin_ref,
                      st_ref):
    grp = pl.program_id(0)
    step = pl.program_id(1)
    blk = nb - 1 - step
    _, hi_mask = _chunk_masks()

    @pl.when(step == 0)
    def _():
        for u in range(group):
            _load_state(s0_ref, u, st_ref.at[u])

    def chain(u):
        row = mod_base + ((grp * group + u) if mod_per_seq else 0)
        st_u = st_ref.at[u]
        qkv = qkv_ref[u]
        q, k, v = qkv[:, 0:QK_W], qkv[:, QK_W:2 * QK_W], qkv[:, 2 * QK_W:]
        cum = _select_dot(hi_mask, lab_ref[u])
        yield
        o_b = yield from _gla_direction(q, k, v, cum, False, hi_mask, st_u)
        o = of_ref[u] + o_b
        g = g_ref[u]
        cols = []
        for head in range(GLA_HEADS):
            hs = slice(head * GLA_DV, (head + 1) * GLA_DV)
            cols.append(_rms(o[:, hs], gnw_ref[...]) * jax.nn.silu(g[:, hs]))
        cols.append(s_ref[u])
        yield
        y = _dot(jnp.concatenate(cols, axis=1), wout_ref[...])
        yield

        xin = _add_pos(x_ref[u], e_ref, blk, add_pos)
        x1 = xin + _mod_row(mod_ref, row, 2) * y
        x1_ref[u] = x1
        h2 = _rms(x1, n2_ref[...]) * (1.0 + _mod_row(mod_ref, row, 4)) + _mod_row(mod_ref, row, 3)
        for s in range(TILES_PER_TOKEN):
            h2t_ref[u, pl.ds(s, TOKEN_BLOCK, stride=TILES_PER_TOKEN), :] = h2[:, s * LANES:(s + 1) * LANES]
        yield

        logits = _dot_nt_3pass(rw_ref[...], h2)
        m = jnp.max(logits, axis=0, keepdims=True)
        ex = jnp.exp(logits - m)
        probs_ref[u, 0] = ex / jnp.sum(ex, axis=0, keepdims=True)

    _interleave(chain(u) for u in range(group))

    @pl.when(step == nb - 1)
    def _():
        for u in range(group):
            _store_state(st_ref.at[u], sfin_ref, u)


def _mixer_bwd(x, e_tab, mod, qkv, g, lab, s, of, gnw, wout, n2, rw_t, s0, *, nseq, nb, add_pos,
               mod_base, mod_per_seq):
    seq_len = nb * TOKEN_BLOCK
    group = min(SEQ_GROUP, nseq)
    tok = lambda w: pl.BlockSpec((group, TOKEN_BLOCK, w), lambda s_, i: (s_, nb - 1 - i, 0))
    full = lambda a: pl.BlockSpec(a.shape, lambda s_, i: (0,) * a.ndim)
    st_spec = pl.BlockSpec((group, GLA_HEADS, GLA_DK, GLA_DV), lambda s_, i: (s_, 0, 0, 0))
    kern = functools.partial(_mixer_bwd_kernel, group, add_pos, mod_base, mod_per_seq, nb)
    return pl.pallas_call(
        kern,
        grid=(nseq // group, nb),
        in_specs=[tok(D_MODEL), full(e_tab), full(mod), tok(1024), tok(GLA_WIDTH), tok(QK_W),
                  tok(SGU_WIDTH), tok(GLA_WIDTH), full(gnw), full(wout), full(n2), full(rw_t),
                  st_spec],
        out_specs=[tok(D_MODEL),
                   pl.BlockSpec((group, TOKEN_BLOCK * TILES_PER_TOKEN, LANES),
                                lambda s_, i: (s_, nb - 1 - i, 0)),
                   pl.BlockSpec((group, 1, N_EXPERTS, TOKEN_BLOCK),
                                lambda s_, i: (s_, nb - 1 - i, 0, 0)),
                   st_spec],
        out_shape=[jax.ShapeDtypeStruct((nseq, seq_len, D_MODEL), F32),
                   jax.ShapeDtypeStruct((nseq, seq_len * TILES_PER_TOKEN, LANES), F32),
                   jax.ShapeDtypeStruct((nseq, nb, N_EXPERTS, TOKEN_BLOCK), F32),
                   jax.ShapeDtypeStruct((nseq, GLA_HEADS, GLA_DK, GLA_DV), F32)],
        scratch_shapes=[pltpu.VMEM((group, 2, 2 * GLA_DV, 2 * GLA_DK), F32)],
        compiler_params=_compiler_params(("arbitrary", "arbitrary")),
        name="mixer_bwd",
    )(x, e_tab, mod, qkv, g, lab, s, of, gnw, wout, n2, rw_t, s0)


def _route_kernel(n_tok, cap, probs_ref, row_ref, gate_ref, xs_ref, ps_ref):
    n_blk = n_tok // TOKEN_BLOCK
    n_chunk = n_tok // LANES
    probs = jnp.concatenate([probs_ref[b] for b in range(n_blk)], axis=1)
    capf = jnp.float32(cap)

    def count(mask):
        return jnp.sum(mask.astype(F32), axis=1, keepdims=True)

    def as_f32(bits):
        return lax.bitcast_convert_type(bits, F32)

    def thr_step(_, lohi):
        lo, hi = lohi
        mid = lo + ((hi - lo + 1) >> 1)
        ok = count(probs >= as_f32(mid)) >= capf
        return jnp.where(ok, mid, lo), jnp.where(ok, hi, mid - 1)

    lo0 = jnp.zeros((N_EXPERTS, 1), I32)
    hi0 = jnp.full((N_EXPERTS, 1), 0x3F800000, I32)
    thr, _ = lax.fori_loop(0, 31, thr_step, (lo0, hi0))
    gt = probs >= as_f32(thr + 1)
    eq = (probs >= as_f32(thr)) & jnp.logical_not(gt)
    need = capf - count(gt)
    tok = lax.broadcasted_iota(I32, (N_EXPERTS, n_tok), 1)

    def tie_step(_, lohi):
        lo, hi = lohi
        mid = (lo + hi) >> 1
        ok = count(eq & (tok <= mid)) >= need
        return jnp.where(ok, lo, mid + 1), jnp.where(ok, mid, hi)

    n_bits = max(1, (n_tok - 1).bit_length())
    cut, _ = lax.fori_loop(0, n_bits, tie_step,
                           (jnp.zeros((N_EXPERTS, 1), I32), jnp.full((N_EXPERTS, 1), n_tok - 1, I32)))
    sel = (gt | (eq & (tok <= cut))).astype(F32)

    xs_ref[...] = jnp.concatenate([sel[:, c * LANES:(c + 1) * LANES] for c in range(n_chunk)], axis=0)
    ps_ref[...] = jnp.concatenate([probs[:, c * LANES:(c + 1) * LANES] for c in range(n_chunk)], axis=0)

    li = lax.broadcasted_iota(I32, (LANES, LANES), 0)
    lj = lax.broadcasted_iota(I32, (LANES, LANES), 1)
    upper = (li <= lj).astype(F32)
    ci = lax.broadcasted_iota(I32, (n_chunk, n_chunk), 0)
    cj = lax.broadcasted_iota(I32, (n_chunk, n_chunk), 1)
    lower = (cj <= ci).astype(F32)
    slot = lax.broadcasted_iota(I32, (1, cap), 1).astype(F32)
    chunk_id = lax.broadcasted_iota(I32, (n_chunk, cap), 0).astype(F32)
    lane_id = lax.broadcasted_iota(I32, (LANES, cap), 0).astype(F32)
    reps = cap // LANES

    def per_expert(e, _):
        x = xs_ref[pl.ds(e, n_chunk, stride=N_EXPERTS), :]
        pe = ps_ref[pl.ds(e, n_chunk, stride=N_EXPERTS), :]
        ploc = _dot(x, upper)
        tot = jnp.broadcast_to(ploc[:, LANES - 1:LANES], (n_chunk, LANES))
        cum = _dot(lower, tot)
        cum_w = jnp.concatenate([cum] * reps, axis=1)
        base_w = jnp.concatenate([cum - tot] * reps, axis=1)
        chunk_of = jnp.sum((cum_w <= slot).astype(F32), axis=0, keepdims=True)
        onehot = chunk_id == chunk_of
        local = slot - jnp.sum(jnp.where(onehot, base_w, 0.0), axis=0, keepdims=True)
        pref = _dot_tn(ploc, onehot.astype(F32))
        lane_of = jnp.sum((pref <= local).astype(F32), axis=0, keepdims=True)
        token = chunk_of * LANES + lane_of
        row_ref[pl.ds(e, 1), :] = (token * TILES_PER_TOKEN).astype(I32)
        pg = _dot_f32(pe, onehot.astype(F32), (((0,), (0,)), ((), ())))
        gate_ref[pl.ds(e, 1), :] = jnp.sum(jnp.where(lane_id == lane_of, pg, 0.0), axis=0, keepdims=True)
        return 0

    lax.fori_loop(0, N_EXPERTS, per_expert, 0)


def _route(probs, n_tok, cap):
    return pl.pallas_call(
        functools.partial(_route_kernel, n_tok, cap),
        out_shape=[jax.ShapeDtypeStruct((N_EXPERTS, cap), I32),
                   jax.ShapeDtypeStruct((N_EXPERTS, cap), F32)],
        scratch_shapes=[pltpu.VMEM((n_tok // LANES * N_EXPERTS, LANES), F32),
                        pltpu.VMEM((n_tok // LANES * N_EXPERTS, LANES), F32)],
        compiler_params=pltpu.CompilerParams(vmem_limit_bytes=VMEM_LIMIT),
        name="route_topk",
    )(probs)


def _expert_kernel(cap, row_ref, h2t_ref, w1_ref, w3_ref, w2_ref, ye_ref,
                   xe_ref, x2_ref, hid_ref, sem):
    e = pl.program_id(0)
    f = pl.program_id(1)
    slot = e % 2
    rows_per_step = cap // EXPERT_STEPS

    def start_row(expert, buf, j):
        src = pl.multiple_of(row_ref[expert * cap + j], TILES_PER_TOKEN)
        dst = pl.multiple_of(j * TILES_PER_TOKEN, TILES_PER_TOKEN)
        pltpu.make_async_copy(h2t_ref.at[pl.ds(src, TILES_PER_TOKEN), :],
                              xe_ref.at[buf, pl.ds(dst, TILES_PER_TOKEN), :], sem.at[buf]).start()

    def wait_rows(buf):
        pltpu.make_async_copy(h2t_ref.at[pl.ds(0, cap * TILES_PER_TOKEN), :], xe_ref.at[buf],
                              sem.at[buf]).wait()

    def prefetch_next():
        nxt = jnp.minimum(e + 1, N_EXPERTS - 1)
        first = f * rows_per_step
        for j in range(rows_per_step):
            start_row(nxt, 1 - slot, first + j)

    @pl.when((e == 0) & (f == 0))
    def _():
        def issue(j, _):
            start_row(0, 0, j)
            return 0

        lax.fori_loop(0, cap, issue, 0, unroll=8)

    @pl.when(f == 0)
    def _():
        wait_rows(slot)
        for s in range(TILES_PER_TOKEN):
            x2_ref[:, s * LANES:(s + 1) * LANES] = (
                xe_ref[slot, pl.ds(s, cap, stride=TILES_PER_TOKEN), :].astype(BF16))

    @pl.when(f < N_HID_STEPS)
    def _():
        prefetch_next()
        x2 = x2_ref[...]
        a = jnp.dot(x2, w1_ref[0].astype(BF16), preferred_element_type=F32)
        b = jnp.dot(x2, w3_ref[0].astype(BF16), preferred_element_type=F32)
        hid_ref[f] = (jax.nn.silu(a) * b).astype(BF16)

    @pl.when(f >= N_HID_STEPS)
    def _():
        prefetch_next()
        w2 = w2_ref[0].astype(BF16)
        out = jnp.dot(hid_ref[0], w2[0:EXPERT_F_BLOCK], preferred_element_type=F32)
        for kb in range(1, N_HID_STEPS):
            out += jnp.dot(hid_ref[kb], w2[kb * EXPERT_F_BLOCK:(kb + 1) * EXPERT_F_BLOCK],
                           preferred_element_type=F32)
        tile0 = (f - N_HID_STEPS) * (EXPERT_N_BLOCK // LANES)
        for i in range(EXPERT_N_BLOCK // LANES):
            ye_ref[pl.ds(tile0 + i, cap, stride=TILES_PER_TOKEN), :] = out[:, i * LANES:(i + 1) * LANES]

    @pl.when((e == N_EXPERTS - 1) & (f == EXPERT_STEPS - 1))
    def _():
        wait_rows(1 - slot)


def _experts(row_flat, h2t, w1, w3, w2, cap):
    hid_blk = lambda e, f, idx: (e, 0, jnp.minimum(f, N_HID_STEPS - 1))
    out_blk = lambda e, f, idx: (e, 0, jnp.maximum(f - N_HID_STEPS, 0))
    grid_spec = pltpu.PrefetchScalarGridSpec(
        num_scalar_prefetch=1,
        grid=(N_EXPERTS, EXPERT_STEPS),
        in_specs=[pl.BlockSpec(memory_space=pl.ANY),
                  pl.BlockSpec((1, D_MODEL, EXPERT_F_BLOCK), hid_blk),
                  pl.BlockSpec((1, D_MODEL, EXPERT_F_BLOCK), hid_blk),
                  pl.BlockSpec((1, D_EXPERT, EXPERT_N_BLOCK), out_blk)],
        out_specs=pl.BlockSpec((cap * TILES_PER_TOKEN, LANES), lambda e, f, idx: (e, 0)),
        scratch_shapes=[pltpu.VMEM((2, cap * TILES_PER_TOKEN, LANES), F32),
                        pltpu.VMEM((cap, D_MODEL), BF16),
                        pltpu.VMEM((N_HID_STEPS, cap, EXPERT_F_BLOCK), BF16),
                        pltpu.SemaphoreType.DMA((2,))],
    )
    return pl.pallas_call(
        functools.partial(_expert_kernel, cap),
        grid_spec=grid_spec,
        out_shape=jax.ShapeDtypeStruct((N_EXPERTS * cap * TILES_PER_TOKEN, LANES), F32),
        compiler_params=_compiler_params(("arbitrary", "arbitrary")),
        name="expert_swiglu",
    )(row_flat, h2t, w1, w3, w2)


COMBINE_BATCH = 8
ZERO_ROWS = 512


def _combine_kernel(cap, n_tok, mod_base, mod_per_seq, nb,
                    row_ref, gate_ref, ye_ref, x1_ref, mod_ref, fw_ref, y_ref,
                    acc_ref, xbuf_ref, ybuf_ref, sem_x, sem_y):
    e = pl.program_id(0)

    @pl.when(e == 0)
    def _():
        def zero(i, _):
            r = pl.multiple_of(i * ZERO_ROWS, ZERO_ROWS)
            acc_ref[pl.ds(r, ZERO_ROWS), :] = jnp.zeros((ZERO_ROWS, LANES), F32)
            return 0

        lax.fori_loop(0, n_tok * TILES_PER_TOKEN // ZERO_ROWS, zero, 0)

    def batch(jb, _):
        vals = []
        for u in range(COMBINE_BATCH):
            j = jb * COMBINE_BATCH + u
            t = pl.multiple_of(row_ref[e * cap + j], TILES_PER_TOKEN)
            gate = gate_ref[e * cap + j]
            src = pl.multiple_of(j * TILES_PER_TOKEN, TILES_PER_TOKEN)
            vals.append((t, acc_ref[pl.ds(t, TILES_PER_TOKEN), :]
                         + ye_ref[pl.ds(src, TILES_PER_TOKEN), :] * gate))
        for t, val in vals:
            acc_ref[pl.ds(t, TILES_PER_TOKEN), :] = val
        return 0

    lax.fori_loop(0, cap // COMBINE_BATCH, batch, 0)

    @pl.when(e == N_EXPERTS - 1)
    def _():
        n_blk = n_tok // TOKEN_BLOCK

        def x1_copy(b, slot):
            rows = pl.ds(pl.multiple_of(b * TOKEN_BLOCK, TOKEN_BLOCK), TOKEN_BLOCK)
            return pltpu.make_async_copy(x1_ref.at[rows, :], xbuf_ref.at[slot], sem_x.at[slot])

        def y_copy(b, slot):
            rows = pl.ds(pl.multiple_of(b * TOKEN_BLOCK, TOKEN_BLOCK), TOKEN_BLOCK)
            return pltpu.make_async_copy(ybuf_ref.at[slot], y_ref.at[rows, :], sem_y.at[slot])

        x1_copy(0, 0).start()

        def block(b, _):
            slot = b % 2

            @pl.when(b + 1 < n_blk)
            def _():
                x1_copy(b + 1, 1 - slot).start()

            x1_copy(b, slot).wait()

            @pl.when(b >= 2)
            def _():
                y_copy(b - 2, slot).wait()

            base = pl.multiple_of(b * (TOKEN_BLOCK * TILES_PER_TOKEN), TOKEN_BLOCK * TILES_PER_TOKEN)
            moe = jnp.concatenate(
                [acc_ref[pl.ds(base + s, TOKEN_BLOCK, stride=TILES_PER_TOKEN), :]
                 for s in range(TILES_PER_TOKEN)], axis=1)
            row = mod_base + ((b // nb) if mod_per_seq else 0)
            ybuf_ref[slot] = _rms(xbuf_ref[slot] + _mod_row(mod_ref, row, 5) * moe, fw_ref[...])
            y_copy(b, slot).start()
            return 0

        lax.fori_loop(0, n_blk, block, 0)
        y_copy(n_blk - 2, n_blk % 2).wait()
        y_copy(n_blk - 1, (n_blk - 1) % 2).wait()


def _combine_final(row_flat, gate_flat, ye, x1, mod, fw, cap, n_tok, *, nb, mod_base, mod_per_seq):
    full = lambda a: pl.BlockSpec(a.shape, lambda e, rows, gate: (0,) * a.ndim)
    grid_spec = pltpu.PrefetchScalarGridSpec(
        num_scalar_prefetch=2,
        grid=(N_EXPERTS,),
        in_specs=[pl.BlockSpec((cap * TILES_PER_TOKEN, LANES), lambda e, rows, gate: (e, 0)),
                  pl.BlockSpec(memory_space=pl.ANY), full(mod), full(fw)],
        out_specs=pl.BlockSpec(memory_space=pl.ANY),
        scratch_shapes=[pltpu.VMEM((n_tok * TILES_PER_TOKEN, LANES), F32),
                        pltpu.VMEM((2, TOKEN_BLOCK, D_MODEL), F32),
                        pltpu.VMEM((2, TOKEN_BLOCK, D_MODEL), F32),
                        pltpu.SemaphoreType.DMA((2,)),
                        pltpu.SemaphoreType.DMA((2,))],
    )
    return pl.pallas_call(
        functools.partial(_combine_kernel, cap, n_tok, mod_base, mod_per_seq, nb),
        grid_spec=grid_spec,
        out_shape=jax.ShapeDtypeStruct((n_tok, D_MODEL), F32),
        compiler_params=_compiler_params(("arbitrary",)),
        name="moe_combine_norm",
    )(row_flat, gate_flat, ye, x1, mod, fw)


def _trunk_and_norm(x, e_tab, mod, s0_f, s0_b, prm, *, nseq, seq_len, add_pos, mod_base, mod_per_seq):
    nb = seq_len // TOKEN_BLOCK
    n_tok = nseq * seq_len
    cap = EC_CAPACITY_FACTOR * n_tok // N_EXPERTS
    kw = dict(nseq=nseq, nb=nb, add_pos=add_pos, mod_base=mod_base, mod_per_seq=mod_per_seq)
    qkv, g, lab, s, of, sfin_f = _mixer_fwd(
        x, e_tab, mod, prm["n1"], prm["win"], prm["wa"], prm["ba"], prm["snw"], prm["sws"],
        prm["sbs"], s0_f, **kw)
    x1, h2t, probs, sfin_b = _mixer_bwd(
        x, e_tab, mod, qkv, g, lab, s, of, prm["gnw"], prm["wout"], prm["n2"], prm["rw_t"],
        s0_b, **kw)
    x1 = x1.reshape(n_tok, D_MODEL)
    h2t = h2t.reshape(n_tok * TILES_PER_TOKEN, LANES)
    probs = probs.reshape(n_tok // TOKEN_BLOCK, N_EXPERTS, TOKEN_BLOCK)
    rows, gates = _route(probs, n_tok, cap)
    row_flat = rows.reshape(-1)
    ye = _experts(row_flat, h2t, prm["w1"], prm["w3"], prm["w2"], cap)
    y = _combine_final(row_flat, gates.reshape(-1), ye, x1, mod, prm["fw"], cap, n_tok,
                       nb=nb, mod_base=mod_base, mod_per_seq=mod_per_seq)
    return y.reshape(nseq, seq_len, D_MODEL), sfin_f, sfin_b


def kernel(x_prompt, x_sample, state_gla_fwd, state_gla_bwd, c, c_ctx, ada_w, ada_b, norm1_w, w_in, gla_wa2_f, gla_ba_f, gla_wa2_b, gla_ba_b, gla_norm_w, sgu_norm_w, sgu_ws, sgu_bs, w_out, norm2_w, router_w, exp_w1, exp_w3, exp_w2, final_norm_w):
    assert ada_w.shape[0] == 1, "single trunk layer"
    batch, seq, _ = x_prompt.shape
    dec_batch, dec_seq, _ = x_sample.shape

    w = w_in[0]
    off_af = 2 * QK_W + 2 * GLA_WIDTH
    off_u = off_af + 2 * GLA_LOWRANK
    win = jnp.concatenate(
        [w[:, :off_af], w[:, off_u:], w[:, off_af:off_u],
         jnp.zeros((D_MODEL, P_WIDTH - w.shape[1]), w.dtype)], axis=1).astype(BF16)
    wa = jnp.zeros((P_WIDTH - P_A, 2 * QK_W), F32)
    wa = wa.at[0:GLA_LOWRANK, 0:QK_W].set(gla_wa2_f[0])
    wa = wa.at[GLA_LOWRANK:2 * GLA_LOWRANK, QK_W:].set(gla_wa2_b[0]).astype(BF16)
    prm = dict(
        n1=norm1_w, win=win, wa=wa,
        ba=jnp.concatenate([gla_ba_f[0], gla_ba_b[0]])[None, :],
        snw=sgu_norm_w, sws=sgu_ws[0].astype(BF16),
        sbs=jnp.broadcast_to(sgu_bs[0][:, :, None], (SGU_GROUPS, SGU_CHUNK, SGU_CH)),
        gnw=gla_norm_w, wout=w_out[0].astype(BF16), n2=norm2_w, rw_t=router_w[0].T,
        w1=exp_w1[0], w3=exp_w3[0], w2=exp_w2[0], fw=final_norm_w[None, :])

    cvec = jnp.concatenate([c_ctx[None, :], c, jnp.zeros((SUBLANES - 1 - dec_batch, D_MODEL), F32)])
    mod = _modulation(cvec, ada_w[0], ada_b)
    e_tab = _pos_table()

    zero_state = jnp.zeros((batch, GLA_HEADS, GLA_DK, GLA_DV), F32)
    y_prompt, sf, sb = _trunk_and_norm(
        x_prompt, e_tab, mod, zero_state, zero_state, prm,
        nseq=batch, seq_len=seq, add_pos=False, mod_base=0, mod_per_seq=False)
    y_sample, _, _ = _trunk_and_norm(
        x_sample, e_tab, mod, state_gla_fwd[:, 0], state_gla_bwd[:, 0], prm,
        nseq=dec_batch, seq_len=dec_seq, add_pos=True, mod_base=1, mod_per_seq=True)
    return (y_prompt, y_sample, sf[:, None], sb[:, None])
```

```python
import functools
import math

import jax
import jax.numpy as jnp
from jax import lax
from jax.experimental import pallas as pl
from jax.experimental.pallas import tpu as pltpu

F32 = jnp.float32
BF16 = jnp.bfloat16
I32 = jnp.int32

D_MODEL = 1024
GRID_W = 64
GLA_HEADS = 4
GLA_DK = 64
GLA_DV = 128
GLA_WIDTH = GLA_HEADS * GLA_DV
QK_W = GLA_HEADS * GLA_DK
GLA_LOWRANK = 16
GLA_GATE_NORM = 16.0
GLA_CHUNK = 64
SGU_WIDTH = 512
SGU_GROUPS = 4
SGU_CH = 128
SGU_CHUNK = 128
N_EXPERTS = 16
EC_CAPACITY_FACTOR = 2
D_EXPERT = 2048
EPS = 1e-6

SUBLANES = 8
LANES = 128
TILES_PER_TOKEN = D_MODEL // LANES

TOKEN_BLOCK = 256
SEQ_GROUP = 4
P_Q, P_K, P_V, P_G, P_U, P_SV, P_A = 0, 256, 512, 1024, 1536, 2048, 2560
P_WIDTH = 2688
EXPERT_F_BLOCK = 512
EXPERT_N_BLOCK = 256
N_HID_STEPS = D_EXPERT // EXPERT_F_BLOCK
EXPERT_STEPS = N_HID_STEPS + D_MODEL // EXPERT_N_BLOCK
VMEM_LIMIT = 56 * 1024 * 1024


def _dot(a, b):
    return jnp.dot(a.astype(BF16), b.astype(BF16), preferred_element_type=F32)


def _dot_nt(a, b):
    return lax.dot_general(a.astype(BF16), b.astype(BF16), (((1,), (1,)), ((), ())),
                           preferred_element_type=F32)


def _dot_tn(a, b):
    return lax.dot_general(a.astype(BF16), b.astype(BF16), (((0,), (0,)), ((), ())),
                           preferred_element_type=F32)


def _dot_f32(a, b, dims=(((1,), (0,)), ((), ()))):
    return lax.dot_general(a, b, dims, precision=lax.Precision.HIGHEST, preferred_element_type=F32)


def _split_bf16(x, terms):
    parts = []
    for _ in range(terms - 1):
        part = x.astype(BF16)
        parts.append(part)
        x = x - part.astype(F32)
    parts.append(x.astype(BF16))
    return parts


def _select_dot(sel, x):
    s = sel.astype(BF16)
    hi, mid, lo = _split_bf16(x, 3)
    return (jnp.dot(s, lo, preferred_element_type=F32) + jnp.dot(s, mid, preferred_element_type=F32)
            + jnp.dot(s, hi, preferred_element_type=F32))


def _dot_nt_3pass(a, b):
    a_hi, a_lo = _split_bf16(a, 2)
    b_hi, b_lo = _split_bf16(b, 2)
    nt = lambda x, y: lax.dot_general(x, y, (((1,), (1,)), ((), ())), preferred_element_type=F32)
    return (nt(a_hi, b_lo) + nt(a_lo, b_hi)) + nt(a_hi, b_hi)


def _rms(x, w):
    return x * lax.rsqrt(jnp.mean(x * x, axis=-1, keepdims=True) + EPS) * w


def _compiler_params(sem):
    return pltpu.CompilerParams(dimension_semantics=sem, vmem_limit_bytes=VMEM_LIMIT)


def _mod_kernel(c_ref, w_ref, b_ref, o_ref):
    o_ref[...] = _dot(jax.nn.silu(c_ref[...]), w_ref[...]) + b_ref[...]


def _modulation(cvec, ada_w, ada_b):
    n = ada_w.shape[1]
    bn = 1536
    return pl.pallas_call(
        _mod_kernel,
        grid=(n // bn,),
        in_specs=[pl.BlockSpec((SUBLANES, D_MODEL), lambda j: (0, 0)),
                  pl.BlockSpec((D_MODEL, bn), lambda j: (0, j)),
                  pl.BlockSpec((1, bn), lambda j: (0, j))],
        out_specs=pl.BlockSpec((SUBLANES, bn), lambda j: (0, j)),
        out_shape=jax.ShapeDtypeStruct((SUBLANES, n), F32),
        compiler_params=_compiler_params(("arbitrary",)),
        name="adaln_mod",
    )(cvec, ada_w, ada_b)


def _pos_kernel(o_ref):
    nf = D_MODEL // 4
    p = lax.broadcasted_iota(I32, (GRID_W, nf), 0).astype(F32)
    i = lax.broadcasted_iota(I32, (GRID_W, nf), 1).astype(F32)
    omega = jnp.exp(i * (-math.log(10000.0) / nf))
    a = p * omega
    o_ref[:, 0:nf] = jnp.sin(a)
    o_ref[:, nf:2 * nf] = jnp.cos(a)


def _pos_table():
    return pl.pallas_call(
        _pos_kernel,
        out_shape=jax.ShapeDtypeStruct((GRID_W, D_MODEL // 2), F32),
        name="sincos_table",
    )()


def _add_pos(x, e_ref, blk, add_pos):
    if not add_pos:
        return x
    half = D_MODEL // 2
    e_all = e_ref[...]
    rows = []
    for j in range(TOKEN_BLOCK // GRID_W):
        xj = x[j * GRID_W:(j + 1) * GRID_W]
        e_row = e_ref[pl.ds(blk * (TOKEN_BLOCK // GRID_W) + j, 1), :]
        rows.append(jnp.concatenate([xj[:, 0:half] + e_row, xj[:, half:] + e_all], axis=1))
    return jnp.concatenate(rows, axis=0)


def _chunk_masks():
    r = lax.broadcasted_iota(I32, (TOKEN_BLOCK, TOKEN_BLOCK), 0)
    c = lax.broadcasted_iota(I32, (TOKEN_BLOCK, TOKEN_BLOCK), 1)
    same = (r // GLA_CHUNK) == (c // GLA_CHUNK)
    return same & (c <= r), same & (c >= r)


def _gla_direction(q, k, v, cum, fwd, att_mask, st_ref):
    qe = q * jnp.exp(cum)
    ke = k * jnp.exp(-cum)
    yield
    lane = lax.broadcasted_iota(I32, (1, LANES), 1)
    o_intra = []
    for pair in range(2):
        qp = qe[:, pair * LANES:(pair + 1) * LANES]
        kp = ke[:, pair * LANES:(pair + 1) * LANES]
        for hh in range(2):
            qm = jnp.where((lane // GLA_DK) == hh, qp, 0.0)
            att = jnp.where(att_mask, _dot_nt(qm, kp), 0.0)
            head = 2 * pair + hh
            o_intra.append(_dot(att, v[:, head * GLA_DV:(head + 1) * GLA_DV]))
            yield
    o_intra = jnp.concatenate(o_intra, axis=1)

    er = lax.broadcasted_iota(I32, (2 * GLA_DV, 2 * GLA_DK), 0)
    dc = lax.broadcasted_iota(I32, (2 * GLA_DV, 2 * GLA_DK), 1)
    same_head = (er // GLA_DV) == (dc // GLA_DK)
    n_chunks = TOKEN_BLOCK // GLA_CHUNK
    o_inter = [None] * n_chunks
    for c in (range(n_chunks) if fwd else reversed(range(n_chunks))):
        r0 = c * GLA_CHUNK
        rows = slice(r0, r0 + GLA_CHUNK)
        last = cum[r0 + GLA_CHUNK - 1:r0 + GLA_CHUNK] if fwd else cum[r0:r0 + 1]
        kd = k[rows] * jnp.exp(last - cum[rows])
        dec = jnp.exp(last)
        parts = []
        for pair in range(2):
            dl = slice(pair * LANES, (pair + 1) * LANES)
            st = st_ref[pair]
            parts.append(_dot_nt(qe[rows, dl], st))
            ds_t = _dot_tn(v[rows, pair * 2 * GLA_DV:(pair + 1) * 2 * GLA_DV], kd[:, dl])
            st_ref[pair] = dec[:, dl] * st + jnp.where(same_head, ds_t, 0.0)
        o_inter[c] = jnp.concatenate(parts, axis=1)
        yield
    return o_intra + jnp.concatenate(o_inter, axis=0)


def _interleave(chains):
    chains = list(chains)
    done = [False] * len(chains)
    tick = 0
    while not all(done):
        for i, ch in enumerate(chains):
            if tick >= i and not done[i]:
                try:
                    next(ch)
                except StopIteration:
                    done[i] = True
        tick += 1


def _load_state(s0_ref, u, st_ref):
    zero = jnp.zeros((GLA_DV, GLA_DK), F32)
    for pair in range(2):
        a = s0_ref[u, 2 * pair].T
        b = s0_ref[u, 2 * pair + 1].T
        st_ref[pair] = jnp.concatenate(
            [jnp.concatenate([a, zero], axis=1), jnp.concatenate([zero, b], axis=1)], axis=0)


def _store_state(st_ref, sfin_ref, u):
    for pair in range(2):
        st = st_ref[pair]
        sfin_ref[u, 2 * pair] = st[0:GLA_DV, 0:GLA_DK].T
        sfin_ref[u, 2 * pair + 1] = st[GLA_DV:2 * GLA_DV, GLA_DK:2 * GLA_DK].T


def _mod_row(mod_ref, row, part):
    return mod_ref[pl.ds(row, 1), part * D_MODEL:(part + 1) * D_MODEL]


def _mixer_fwd_kernel(group, add_pos, mod_base, mod_per_seq, nb,
                      x_ref, e_ref, mod_ref, n1_ref, win_ref, wa_ref, ba_ref,
                      snw_ref, sws_ref, sbs_ref, s0_ref,
                      qkv_ref, g_ref, lab_ref, s_ref, of_ref, sfin_ref,
                      st_ref):
    grp = pl.program_id(0)
    blk = pl.program_id(1)
    lo_mask, _ = _chunk_masks()

    @pl.when(blk == 0)
    def _():
        for u in range(group):
            _load_state(s0_ref, u, st_ref.at[u])

    def chain(u):
        row = mod_base + ((grp * group + u) if mod_per_seq else 0)
        st_u = st_ref.at[u]
        xin = _add_pos(x_ref[u], e_ref, blk, add_pos)
        h = _rms(xin, n1_ref[...]) * (1.0 + _mod_row(mod_ref, row, 1)) + _mod_row(mod_ref, row, 0)
        yield
        hb = h.astype(BF16)
        p_parts = []
        for c0, c1 in ((P_Q, P_G), (P_G, P_SV), (P_SV, P_WIDTH)):
            p_parts.append(jnp.dot(hb, win_ref[:, c0:c1], preferred_element_type=F32))
            yield
        p = jnp.concatenate(p_parts, axis=1)
        q = p[:, P_Q:P_K] * (GLA_DK ** -0.5)
        k = p[:, P_K:P_V]
        v = p[:, P_V:P_G]
        z = _dot(p[:, P_A:P_WIDTH], wa_ref[...]) + ba_ref[...]
        la = (jnp.minimum(z, 0.0) - jnp.log1p(jnp.exp(-jnp.abs(z)))) * (1.0 / GLA_GATE_NORM)
        qkv_ref[u] = jnp.concatenate([q, k, v], axis=1)
        g_ref[u] = p[:, P_G:P_U]
        lab_ref[u] = la[:, QK_W:2 * QK_W]
        yield

        ug = jax.nn.gelu(p[:, P_U:P_SV])
        yield
        vg = jax.nn.gelu(p[:, P_SV:P_A])
        yield
        s_cols = []
        for gi in range(SGU_GROUPS):
            cols = slice(gi * SGU_CH, (gi + 1) * SGU_CH)
            vn = _rms(vg[:, cols], snw_ref[:, cols])
            rhs = jnp.concatenate([vn[0:SGU_CHUNK], vn[SGU_CHUNK:2 * SGU_CHUNK]], axis=1)
            vm = _dot(sws_ref[gi], rhs) + jnp.concatenate([sbs_ref[gi], sbs_ref[gi]], axis=1)
            vm = jnp.concatenate([vm[:, 0:SGU_CH], vm[:, SGU_CH:2 * SGU_CH]], axis=0)
            s_cols.append(ug[:, cols] * vm)
        s_ref[u] = jnp.concatenate(s_cols, axis=1)
        yield

        cum = _select_dot(lo_mask, la[:, 0:QK_W])
        yield
        of_ref[u] = yield from _gla_direction(q, k, v, cum, True, lo_mask, st_u)

    _interleave(chain(u) for u in range(group))

    @pl.when(blk == nb - 1)
    def _():
        for u in range(group):
            _store_state(st_ref.at[u], sfin_ref, u)


def _mixer_fwd(x, e_tab, mod, n1, win, wa, ba, snw, sws, sbs, s0, *, nseq, nb, add_pos,
               mod_base, mod_per_seq):
    seq_len = nb * TOKEN_BLOCK
    group = min(SEQ_GROUP, nseq)
    tok = lambda w: pl.BlockSpec((group, TOKEN_BLOCK, w), lambda s, i: (s, i, 0))
    full = lambda a: pl.BlockSpec(a.shape, lambda s, i: (0,) * a.ndim)
    st_spec = pl.BlockSpec((group, GLA_HEADS, GLA_DK, GLA_DV), lambda s, i: (s, 0, 0, 0))
    act = lambda w: jax.ShapeDtypeStruct((nseq, seq_len, w), F32)
    kern = functools.partial(_mixer_fwd_kernel, group, add_pos, mod_base, mod_per_seq, nb)
    return pl.pallas_call(
        kern,
        grid=(nseq // group, nb),
        in_specs=[tok(D_MODEL), full(e_tab), full(mod), full(n1), full(win), full(wa), full(ba),
                  full(snw), full(sws), full(sbs), st_spec],
        out_specs=[tok(1024), tok(GLA_WIDTH), tok(QK_W), tok(SGU_WIDTH), tok(GLA_WIDTH), st_spec],
        out_shape=[act(1024), act(GLA_WIDTH), act(QK_W), act(SGU_WIDTH), act(GLA_WIDTH),
                   jax.ShapeDtypeStruct((nseq, GLA_HEADS, GLA_DK, GLA_DV), F32)],
        scratch_shapes=[pltpu.VMEM((group, 2, 2 * GLA_DV, 2 * GLA_DK), F32)],
        compiler_params=_compiler_params(("arbitrary", "arbitrary")),
        name="mixer_fwd",
    )(x, e_tab, mod, n1, win, wa, ba, snw, sws, sbs, s0)


def _mixer_bwd_kernel(group, add_pos, mod_base, mod_per_seq, nb,
                      x_ref, e_ref, mod_ref, qkv_ref, g_ref, lab_ref, s_ref, of_ref,
                      gnw_ref, wout_ref, n2_ref, rw_ref, s0_ref,
                      x1_ref, h2t_ref, probs_ref, sfin_ref,
                      st_ref):
    grp = pl.program_id(0)
    step = pl.program_id(1)
    blk = nb - 1 - step
    _, hi_mask = _chunk_masks()

    @pl.when(step == 0)
    def _():
        for u in range(group):
            _load_state(s0_ref, u, st_ref.at[u])

    def chain(u):
        row = mod_base + ((grp * group + u) if mod_per_seq else 0)
        st_u = st_ref.at[u]
        qkv = qkv_ref[u]
        q, k, v = qkv[:, 0:QK_W], qkv[:, QK_W:2 * QK_W], qkv[:, 2 * QK_W:]
        cum = _select_dot(hi_mask, lab_ref[u])
        yield
        o_b = yield from _gla_direction(q, k, v, cum, False, hi_mask, st_u)
        o = of_ref[u] + o_b
        g = g_ref[u]
        cols = []
        for head in range(GLA_HEADS):
            hs = slice(head * GLA_DV, (head + 1) * GLA_DV)
            cols.append(_rms(o[:, hs], gnw_ref[...]) * jax.nn.silu(g[:, hs]))
        cols.append(s_ref[u])
        yield
        y = _dot(jnp.concatenate(cols, axis=1), wout_ref[...])
        yield

        xin = _add_pos(x_ref[u], e_ref, blk, add_pos)
        x1 = xin + _mod_row(mod_ref, row, 2) * y
        x1_ref[u] = x1
        h2 = _rms(x1, n2_ref[...]) * (1.0 + _mod_row(mod_ref, row, 4)) + _mod_row(mod_ref, row, 3)
        for s in range(TILES_PER_TOKEN):
            h2t_ref[u, pl.ds(s, TOKEN_BLOCK, stride=TILES_PER_TOKEN), :] = h2[:, s * LANES:(s + 1) * LANES]
        yield

        logits = _dot_nt_3pass(rw_ref[...], h2)
        m = jnp.max(logits, axis=0, keepdims=True)
        ex = jnp.exp(logits - m)
        probs_ref[u, 0] = ex / jnp.sum(ex, axis=0, keepdims=True)

    _interleave(chain(u) for u in range(group))

    @pl.when(step == nb - 1)
    def _():
        for u in range(group):
            _store_state(st_ref.at[u], sfin_ref, u)


def _mixer_bwd(x, e_tab, mod, qkv, g, lab, s, of, gnw, wout, n2, rw_t, s0, *, nseq, nb, add_pos,
               mod_base, mod_per_seq):
    seq_len = nb * TOKEN_BLOCK
    group = min(SEQ_GROUP, nseq)
    tok = lambda w: pl.BlockSpec((group, TOKEN_BLOCK, w), lambda s_, i: (s_, nb - 1 - i, 0))
    full = lambda a: pl.BlockSpec(a.shape, lambda s_, i: (0,) * a.ndim)
    st_spec = pl.BlockSpec((group, GLA_HEADS, GLA_DK, GLA_DV), lambda s_, i: (s_, 0, 0, 0))
    kern = functools.partial(_mixer_bwd_kernel, group, add_pos, mod_base, mod_per_seq, nb)
    return pl.pallas_call(
        kern,
        grid=(nseq // group, nb),
        in_specs=[tok(D_MODEL), full(e_tab), full(mod), tok(1024), tok(GLA_WIDTH), tok(QK_W),
                  tok(SGU_WIDTH), tok(GLA_WIDTH), full(gnw), full(wout), full(n2), full(rw_t),
                  st_spec],
        out_specs=[tok(D_MODEL),
                   pl.BlockSpec((group, TOKEN_BLOCK * TILES_PER_TOKEN, LANES),
                                lambda s_, i: (s_, nb - 1 - i, 0)),
                   pl.BlockSpec((group, 1, N_EXPERTS, TOKEN_BLOCK),
                                lambda s_, i: (s_, nb - 1 - i, 0, 0)),
                   st_spec],
        out_shape=[jax.ShapeDtypeStruct((nseq, seq_len, D_MODEL), F32),
                   jax.ShapeDtypeStruct((nseq, seq_len * TILES_PER_TOKEN, LANES), F32),
                   jax.ShapeDtypeStruct((nseq, nb, N_EXPERTS, TOKEN_BLOCK), F32),
                   jax.ShapeDtypeStruct((nseq, GLA_HEADS, GLA_DK, GLA_DV), F32)],
        scratch_shapes=[pltpu.VMEM((group, 2, 2 * GLA_DV, 2 * GLA_DK), F32)],
        compiler_params=_compiler_params(("arbitrary", "arbitrary")),
        name="mixer_bwd",
    )(x, e_tab, mod, qkv, g, lab, s, of, gnw, wout, n2, rw_t, s0)


def _route_kernel(n_tok, cap, probs_ref, row_ref, gate_ref, xs_ref, ps_ref):
    n_blk = n_tok // TOKEN_BLOCK
    n_chunk = n_tok // LANES
    probs = jnp.concatenate([probs_ref[b] for b in range(n_blk)], axis=1)
    capf = jnp.float32(cap)

    def count(mask):
        return jnp.sum(mask.astype(F32), axis=1, keepdims=True)

    def as_f32(bits):
        return lax.bitcast_convert_type(bits, F32)

    def thr_step(_, lohi):
        lo, hi = lohi
        mid = lo + ((hi - lo + 1) >> 1)
        ok = count(probs >= as_f32(mid)) >= capf
        return jnp.where(ok, mid, lo), jnp.where(ok, hi, mid - 1)

    lo0 = jnp.zeros((N_EXPERTS, 1), I32)
    hi0 = jnp.full((N_EXPERTS, 1), 0x3F800000, I32)
    thr, _ = lax.fori_loop(0, 31, thr_step, (lo0, hi0))
    gt = probs >= as_f32(thr + 1)
    eq = (probs >= as_f32(thr)) & jnp.logical_not(gt)
    need = capf - count(gt)
    tok = lax.broadcasted_iota(I32, (N_EXPERTS, n_tok), 1)

    def tie_step(_, lohi):
        lo, hi = lohi
        mid = (lo + hi) >> 1
        ok = count(eq & (tok <= mid)) >= need
        return jnp.where(ok, lo, mid + 1), jnp.where(ok, mid, hi)

    n_bits = max(1, (n_tok - 1).bit_length())
    cut, _ = lax.fori_loop(0, n_bits, tie_step,
                           (jnp.zeros((N_EXPERTS, 1), I32), jnp.full((N_EXPERTS, 1), n_tok - 1, I32)))
    sel = (gt | (eq & (tok <= cut))).astype(F32)

    xs_ref[...] = jnp.concatenate([sel[:, c * LANES:(c + 1) * LANES] for c in range(n_chunk)], axis=0)
    ps_ref[...] = jnp.concatenate([probs[:, c * LANES:(c + 1) * LANES] for c in range(n_chunk)], axis=0)

    li = lax.broadcasted_iota(I32, (LANES, LANES), 0)
    lj = lax.broadcasted_iota(I32, (LANES, LANES), 1)
    upper = (li <= lj).astype(F32)
    ci = lax.broadcasted_iota(I32, (n_chunk, n_chunk), 0)
    cj = lax.broadcasted_iota(I32, (n_chunk, n_chunk), 1)
    lower = (cj <= ci).astype(F32)
    slot = lax.broadcasted_iota(I32, (1, cap), 1).astype(F32)
    chunk_id = lax.broadcasted_iota(I32, (n_chunk, cap), 0).astype(F32)
    lane_id = lax.broadcasted_iota(I32, (LANES, cap), 0).astype(F32)
    reps = cap // LANES

    def per_expert(e):
        x = xs_ref[pl.ds(e, n_chunk, stride=N_EXPERTS), :]
        pe = ps_ref[pl.ds(e, n_chunk, stride=N_EXPERTS), :]
        ploc = _dot(x, upper)
        tot = jnp.broadcast_to(ploc[:, LANES - 1:LANES], (n_chunk, LANES))
        cum = _dot(lower, tot)
        yield
        cum_w = jnp.concatenate([cum] * reps, axis=1)
        base_w = jnp.concatenate([cum - tot] * reps, axis=1)
        chunk_of = jnp.sum((cum_w <= slot).astype(F32), axis=0, keepdims=True)
        onehot = chunk_id == chunk_of
        local = slot - jnp.sum(jnp.where(onehot, base_w, 0.0), axis=0, keepdims=True)
        yield
        lhs = jnp.concatenate([ploc.astype(BF16)] + _split_bf16(pe, 3), axis=1)
        got = _dot_tn(lhs, onehot.astype(F32))
        yield
        pref = got[0:LANES]
        lane_of = jnp.sum((pref <= local).astype(F32), axis=0, keepdims=True)
        token = chunk_of * LANES + lane_of
        row_ref[pl.ds(e, 1), :] = (token * TILES_PER_TOKEN).astype(I32)
        yield
        pg = (got[3 * LANES:4 * LANES] + got[2 * LANES:3 * LANES]) + got[LANES:2 * LANES]
        gate_ref[pl.ds(e, 1), :] = jnp.sum(jnp.where(lane_id == lane_of, pg, 0.0), axis=0, keepdims=True)

    def expert_pair(i, _):
        _interleave(per_expert(2 * i + u) for u in range(2))
        return 0

    lax.fori_loop(0, N_EXPERTS // 2, expert_pair, 0)


def _route(probs, n_tok, cap):
    return pl.pallas_call(
        functools.partial(_route_kernel, n_tok, cap),
        out_shape=[jax.ShapeDtypeStruct((N_EXPERTS, cap), I32),
                   jax.ShapeDtypeStruct((N_EXPERTS, cap), F32)],
        scratch_shapes=[pltpu.VMEM((n_tok // LANES * N_EXPERTS, LANES), F32),
                        pltpu.VMEM((n_tok // LANES * N_EXPERTS, LANES), F32)],
        compiler_params=pltpu.CompilerParams(vmem_limit_bytes=VMEM_LIMIT),
        name="route_topk",
    )(probs)


def _expert_kernel(cap, row_ref, h2t_ref, w1_ref, w3_ref, w2_ref, ye_ref,
                   xe_ref, x2_ref, hid_ref, sem):
    e = pl.program_id(0)
    f = pl.program_id(1)
    slot = e % 2
    rows_per_step = cap // EXPERT_STEPS

    def start_row(expert, buf, j):
        src = pl.multiple_of(row_ref[expert * cap + j], TILES_PER_TOKEN)
        dst = pl.multiple_of(j * TILES_PER_TOKEN, TILES_PER_TOKEN)
        pltpu.make_async_copy(h2t_ref.at[pl.ds(src, TILES_PER_TOKEN), :],
                              xe_ref.at[buf, pl.ds(dst, TILES_PER_TOKEN), :], sem.at[buf]).start()

    def wait_rows(buf):
        pltpu.make_async_copy(h2t_ref.at[pl.ds(0, cap * TILES_PER_TOKEN), :], xe_ref.at[buf],
                              sem.at[buf]).wait()

    def prefetch_next():
        nxt = jnp.minimum(e + 1, N_EXPERTS - 1)
        first = f * rows_per_step
        for j in range(rows_per_step):
            start_row(nxt, 1 - slot, first + j)

    @pl.when((e == 0) & (f == 0))
    def _():
        def issue(j, _):
            start_row(0, 0, j)
            return 0

        lax.fori_loop(0, cap, issue, 0, unroll=8)

    @pl.when(f == 0)
    def _():
        wait_rows(slot)
        for s in range(TILES_PER_TOKEN):
            x2_ref[:, s * LANES:(s + 1) * LANES] = (
                xe_ref[slot, pl.ds(s, cap, stride=TILES_PER_TOKEN), :].astype(BF16))

    @pl.when(f < N_HID_STEPS)
    def _():
        prefetch_next()
        x2 = x2_ref[...]
        a = jnp.dot(x2, w1_ref[0].astype(BF16), preferred_element_type=F32)
        b = jnp.dot(x2, w3_ref[0].astype(BF16), preferred_element_type=F32)
        hid_ref[f] = (jax.nn.silu(a) * b).astype(BF16)

    @pl.when(f >= N_HID_STEPS)
    def _():
        prefetch_next()
        w2 = w2_ref[0].astype(BF16)
        out = jnp.dot(hid_ref[0], w2[0:EXPERT_F_BLOCK], preferred_element_type=F32)
        for kb in range(1, N_HID_STEPS):
            out += jnp.dot(hid_ref[kb], w2[kb * EXPERT_F_BLOCK:(kb + 1) * EXPERT_F_BLOCK],
                           preferred_element_type=F32)
        tile0 = (f - N_HID_STEPS) * (EXPERT_N_BLOCK // LANES)
        for i in range(EXPERT_N_BLOCK // LANES):
            ye_ref[pl.ds(tile0 + i, cap, stride=TILES_PER_TOKEN), :] = out[:, i * LANES:(i + 1) * LANES]

    @pl.when((e == N_EXPERTS - 1) & (f == EXPERT_STEPS - 1))
    def _():
        wait_rows(1 - slot)


def _experts(row_flat, h2t, w1, w3, w2, cap):
    hid_blk = lambda e, f, idx: (e, 0, jnp.minimum(f, N_HID_STEPS - 1))
    n_out = D_MODEL // EXPERT_N_BLOCK

    def out_blk(e, f, idx):
        hold = (f == 0) & (e > 0)
        return (jnp.where(hold, e - 1, e), 0,
                jnp.where(hold, n_out - 1, jnp.maximum(f - N_HID_STEPS, 0)))

    grid_spec = pltpu.PrefetchScalarGridSpec(
        num_scalar_prefetch=1,
        grid=(N_EXPERTS, EXPERT_STEPS),
        in_specs=[pl.BlockSpec(memory_space=pl.ANY),
                  pl.BlockSpec((1, D_MODEL, EXPERT_F_BLOCK), hid_blk),
                  pl.BlockSpec((1, D_MODEL, EXPERT_F_BLOCK), hid_blk),
                  pl.BlockSpec((1, D_EXPERT, EXPERT_N_BLOCK), out_blk)],
        out_specs=pl.BlockSpec((cap * TILES_PER_TOKEN, LANES), lambda e, f, idx: (e, 0)),
        scratch_shapes=[pltpu.VMEM((2, cap * TILES_PER_TOKEN, LANES), F32),
                        pltpu.VMEM((cap, D_MODEL), BF16),
                        pltpu.VMEM((N_HID_STEPS, cap, EXPERT_F_BLOCK), BF16),
                        pltpu.SemaphoreType.DMA((2,))],
    )
    return pl.pallas_call(
        functools.partial(_expert_kernel, cap),
        grid_spec=grid_spec,
        out_shape=jax.ShapeDtypeStruct((N_EXPERTS * cap * TILES_PER_TOKEN, LANES), F32),
        compiler_params=_compiler_params(("arbitrary", "arbitrary")),
        name="expert_swiglu",
    )(row_flat, h2t, w1, w3, w2)


COMBINE_BATCH = 8
ZERO_ROWS = 512


def _combine_kernel(cap, n_tok, mod_base, mod_per_seq, nb,
                    row_ref, gate_ref, ye_ref, x1_ref, mod_ref, fw_ref, y_ref,
                    acc_ref, xbuf_ref, ybuf_ref, sem_x, sem_y):
    e = pl.program_id(0)

    @pl.when(e == 0)
    def _():
        def zero(i, _):
            r = pl.multiple_of(i * ZERO_ROWS, ZERO_ROWS)
            acc_ref[pl.ds(r, ZERO_ROWS), :] = jnp.zeros((ZERO_ROWS, LANES), F32)
            return 0

        lax.fori_loop(0, n_tok * TILES_PER_TOKEN // ZERO_ROWS, zero, 0)

    def batch(jb, _):
        vals = []
        for u in range(COMBINE_BATCH):
            j = jb * COMBINE_BATCH + u
            t = pl.multiple_of(row_ref[e * cap + j], TILES_PER_TOKEN)
            gate = gate_ref[e * cap + j]
            src = pl.multiple_of(j * TILES_PER_TOKEN, TILES_PER_TOKEN)
            vals.append((t, acc_ref[pl.ds(t, TILES_PER_TOKEN), :]
                         + ye_ref[pl.ds(src, TILES_PER_TOKEN), :] * gate))
        for t, val in vals:
            acc_ref[pl.ds(t, TILES_PER_TOKEN), :] = val
        return 0

    lax.fori_loop(0, cap // COMBINE_BATCH, batch, 0)

    @pl.when(e == N_EXPERTS - 1)
    def _():
        n_blk = n_tok // TOKEN_BLOCK

        def x1_copy(b, slot):
            rows = pl.ds(pl.multiple_of(b * TOKEN_BLOCK, TOKEN_BLOCK), TOKEN_BLOCK)
            return pltpu.make_async_copy(x1_ref.at[rows, :], xbuf_ref.at[slot], sem_x.at[slot])

        def y_copy(b, slot):
            rows = pl.ds(pl.multiple_of(b * TOKEN_BLOCK, TOKEN_BLOCK), TOKEN_BLOCK)
            return pltpu.make_async_copy(ybuf_ref.at[slot], y_ref.at[rows, :], sem_y.at[slot])

        x1_copy(0, 0).start()

        def block(b, _):
            slot = b % 2

            @pl.when(b + 1 < n_blk)
            def _():
                x1_copy(b + 1, 1 - slot).start()

            x1_copy(b, slot).wait()

            @pl.when(b >= 2)
            def _():
                y_copy(b - 2, slot).wait()

            base = pl.multiple_of(b * (TOKEN_BLOCK * TILES_PER_TOKEN), TOKEN_BLOCK * TILES_PER_TOKEN)
            moe = jnp.concatenate(
                [acc_ref[pl.ds(base + s, TOKEN_BLOCK, stride=TILES_PER_TOKEN), :]
                 for s in range(TILES_PER_TOKEN)], axis=1)
            row = mod_base + ((b // nb) if mod_per_seq else 0)
            ybuf_ref[slot] = _rms(xbuf_ref[slot] + _mod_row(mod_ref, row, 5) * moe, fw_ref[...])
            y_copy(b, slot).start()
            return 0

        lax.fori_loop(0, n_blk, block, 0)
        y_copy(n_blk - 2, n_blk % 2).wait()
        y_copy(n_blk - 1, (n_blk - 1) % 2).wait()


def _combine_final(row_flat, gate_flat, ye, x1, mod, fw, cap, n_tok, *, nb, mod_base, mod_per_seq):
    full = lambda a: pl.BlockSpec(a.shape, lambda e, rows, gate: (0,) * a.ndim)
    grid_spec = pltpu.PrefetchScalarGridSpec(
        num_scalar_prefetch=2,
        grid=(N_EXPERTS,),
        in_specs=[pl.BlockSpec((cap * TILES_PER_TOKEN, LANES), lambda e, rows, gate: (e, 0)),
                  pl.BlockSpec(memory_space=pl.ANY), full(mod), full(fw)],
        out_specs=pl.BlockSpec(memory_space=pl.ANY),
        scratch_shapes=[pltpu.VMEM((n_tok * TILES_PER_TOKEN, LANES), F32),
                        pltpu.VMEM((2, TOKEN_BLOCK, D_MODEL), F32),
                        pltpu.VMEM((2, TOKEN_BLOCK, D_MODEL), F32),
                        pltpu.SemaphoreType.DMA((2,)),
                        pltpu.SemaphoreType.DMA((2,))],
    )
    return pl.pallas_call(
        functools.partial(_combine_kernel, cap, n_tok, mod_base, mod_per_seq, nb),
        grid_spec=grid_spec,
        out_shape=jax.ShapeDtypeStruct((n_tok, D_MODEL), F32),
        compiler_params=_compiler_params(("arbitrary",)),
        name="moe_combine_norm",
    )(row_flat, gate_flat, ye, x1, mod, fw)


def _trunk_and_norm(x, e_tab, mod, s0_f, s0_b, prm, *, nseq, seq_len, add_pos, mod_base, mod_per_seq):
    nb = seq_len // TOKEN_BLOCK
    n_tok = nseq * seq_len
    cap = EC_CAPACITY_FACTOR * n_tok // N_EXPERTS
    kw = dict(nseq=nseq, nb=nb, add_pos=add_pos, mod_base=mod_base, mod_per_seq=mod_per_seq)
    qkv, g, lab, s, of, sfin_f = _mixer_fwd(
        x, e_tab, mod, prm["n1"], prm["win"], prm["wa"], prm["ba"], prm["snw"], prm["sws"],
        prm["sbs"], s0_f, **kw)
    x1, h2t, probs, sfin_b = _mixer_bwd(
        x, e_tab, mod, qkv, g, lab, s, of, prm["gnw"], prm["wout"], prm["n2"], prm["rw_t"],
        s0_b, **kw)
    x1 = x1.reshape(n_tok, D_MODEL)
    h2t = h2t.reshape(n_tok * TILES_PER_TOKEN, LANES)
    probs = probs.reshape(n_tok // TOKEN_BLOCK, N_EXPERTS, TOKEN_BLOCK)
    rows, gates = _route(probs, n_tok, cap)
    row_flat = rows.reshape(-1)
    ye = _experts(row_flat, h2t, prm["w1"], prm["w3"], prm["w2"], cap)
    y = _combine_final(row_flat, gates.reshape(-1), ye, x1, mod, prm["fw"], cap, n_tok,
                       nb=nb, mod_base=mod_base, mod_per_seq=mod_per_seq)
    return y.reshape(nseq, seq_len, D_MODEL), sfin_f, sfin_b


def kernel(x_prompt, x_sample, state_gla_fwd, state_gla_bwd, c, c_ctx, ada_w, ada_b, norm1_w, w_in, gla_wa2_f, gla_ba_f, gla_wa2_b, gla_ba_b, gla_norm_w, sgu_norm_w, sgu_ws, sgu_bs, w_out, norm2_w, router_w, exp_w1, exp_w3, exp_w2, final_norm_w):
    assert ada_w.shape[0] == 1, "single trunk layer"
    batch, seq, _ = x_prompt.shape
    dec_batch, dec_seq, _ = x_sample.shape

    w = w_in[0]
    off_af = 2 * QK_W + 2 * GLA_WIDTH
    off_u = off_af + 2 * GLA_LOWRANK
    win = jnp.concatenate(
        [w[:, :off_af], w[:, off_u:], w[:, off_af:off_u],
         jnp.zeros((D_MODEL, P_WIDTH - w.shape[1]), w.dtype)], axis=1).astype(BF16)
    wa = jnp.zeros((P_WIDTH - P_A, 2 * QK_W), F32)
    wa = wa.at[0:GLA_LOWRANK, 0:QK_W].set(gla_wa2_f[0])
    wa = wa.at[GLA_LOWRANK:2 * GLA_LOWRANK, QK_W:].set(gla_wa2_b[0]).astype(BF16)
    prm = dict(
        n1=norm1_w, win=win, wa=wa,
        ba=jnp.concatenate([gla_ba_f[0], gla_ba_b[0]])[None, :],
        snw=sgu_norm_w, sws=sgu_ws[0].astype(BF16),
        sbs=jnp.broadcast_to(sgu_bs[0][:, :, None], (SGU_GROUPS, SGU_CHUNK, SGU_CH)),
        gnw=gla_norm_w, wout=w_out[0].astype(BF16), n2=norm2_w, rw_t=router_w[0].T,
        w1=exp_w1[0], w3=exp_w3[0], w2=exp_w2[0], fw=final_norm_w[None, :])

    cvec = jnp.concatenate([c_ctx[None, :], c, jnp.zeros((SUBLANES - 1 - dec_batch, D_MODEL), F32)])
    mod = _modulation(cvec, ada_w[0], ada_b)
    e_tab = _pos_table()

    zero_state = jnp.zeros((batch, GLA_HEADS, GLA_DK, GLA_DV), F32)
    y_prompt, sf, sb = _trunk_and_norm(
        x_prompt, e_tab, mod, zero_state, zero_state, prm,
        nseq=batch, seq_len=seq, add_pos=False, mod_base=0, mod_per_seq=False)
    y_sample, _, _ = _trunk_and_norm(
        x_sample, e_tab, mod, state_gla_fwd[:, 0], state_gla_bwd[:, 0], prm,
        nseq=dec_batch, seq_len=dec_seq, add_pos=True, mod_base=1, mod_per_seq=True)
    return (y_prompt, y_sample, sf[:, None], sb[:, None])
```

```python
import functools
import math

import jax
import jax.numpy as jnp
from jax import lax
from jax.experimental import pallas as pl
from jax.experimental.pallas import tpu as pltpu

F32 = jnp.float32
BF16 = jnp.bfloat16
I32 = jnp.int32

D_MODEL = 1024
GRID_W = 64
GLA_HEADS = 4
GLA_DK = 64
GLA_DV = 128
GLA_WIDTH = GLA_HEADS * GLA_DV
QK_W = GLA_HEADS * GLA_DK
GLA_LOWRANK = 16
GLA_GATE_NORM = 16.0
GLA_CHUNK = 64
SGU_WIDTH = 512
SGU_GROUPS = 4
SGU_CH = 128
SGU_CHUNK = 128
N_EXPERTS = 16
EC_CAPACITY_FACTOR = 2
D_EXPERT = 2048
EPS = 1e-6

SUBLANES = 8
LANES = 128
TILES_PER_TOKEN = D_MODEL // LANES

TOKEN_BLOCK = 256
SEQ_GROUP = 4
P_Q, P_K, P_V, P_G, P_U, P_SV, P_A = 0, 256, 512, 1024, 1536, 2048, 2560
P_WIDTH = 2688
EXPERT_F_BLOCK = 512
EXPERT_N_BLOCK = 256
N_HID_STEPS = D_EXPERT // EXPERT_F_BLOCK
EXPERT_STEPS = N_HID_STEPS + D_MODEL // EXPERT_N_BLOCK
VMEM_LIMIT = 56 * 1024 * 1024


def _dot(a, b):
    return jnp.dot(a.astype(BF16), b.astype(BF16), preferred_element_type=F32)


def _dot_nt(a, b):
    return lax.dot_general(a.astype(BF16), b.astype(BF16), (((1,), (1,)), ((), ())),
                           preferred_element_type=F32)


def _dot_tn(a, b):
    return lax.dot_general(a.astype(BF16), b.astype(BF16), (((0,), (0,)), ((), ())),
                           preferred_element_type=F32)


def _dot_f32(a, b, dims=(((1,), (0,)), ((), ()))):
    return lax.dot_general(a, b, dims, precision=lax.Precision.HIGHEST, preferred_element_type=F32)


def _split_bf16(x, terms):
    parts = []
    for _ in range(terms - 1):
        part = x.astype(BF16)
        parts.append(part)
        x = x - part.astype(F32)
    parts.append(x.astype(BF16))
    return parts


def _select_dot(sel, x):
    s = sel.astype(BF16)
    hi, mid, lo = _split_bf16(x, 3)
    return (jnp.dot(s, lo, preferred_element_type=F32) + jnp.dot(s, mid, preferred_element_type=F32)
            + jnp.dot(s, hi, preferred_element_type=F32))


def _dot_nt_3pass(a, b):
    a_hi, a_lo = _split_bf16(a, 2)
    b_hi, b_lo = _split_bf16(b, 2)
    nt = lambda x, y: lax.dot_general(x, y, (((1,), (1,)), ((), ())), preferred_element_type=F32)
    return (nt(a_hi, b_lo) + nt(a_lo, b_hi)) + nt(a_hi, b_hi)


def _rms(x, w):
    return x * lax.rsqrt(jnp.mean(x * x, axis=-1, keepdims=True) + EPS) * w


def _compiler_params(sem):
    return pltpu.CompilerParams(dimension_semantics=sem, vmem_limit_bytes=VMEM_LIMIT)


def _mod_kernel(c_ref, w_ref, b_ref, o_ref):
    o_ref[...] = _dot(jax.nn.silu(c_ref[...]), w_ref[...]) + b_ref[...]


def _modulation(cvec, ada_w, ada_b):
    n = ada_w.shape[1]
    bn = 1536
    return pl.pallas_call(
        _mod_kernel,
        grid=(n // bn,),
        in_specs=[pl.BlockSpec((SUBLANES, D_MODEL), lambda j: (0, 0)),
                  pl.BlockSpec((D_MODEL, bn), lambda j: (0, j)),
                  pl.BlockSpec((1, bn), lambda j: (0, j))],
        out_specs=pl.BlockSpec((SUBLANES, bn), lambda j: (0, j)),
        out_shape=jax.ShapeDtypeStruct((SUBLANES, n), F32),
        compiler_params=_compiler_params(("arbitrary",)),
        name="adaln_mod",
    )(cvec, ada_w, ada_b)


def _win_kernel(off_a, off_u, w_ref, o_ref):
    w = w_ref[0]
    n_in = w.shape[1]
    o_ref[:, 0:off_a] = w[:, 0:off_a].astype(BF16)
    o_ref[:, off_a:off_a + n_in - off_u] = w[:, off_u:n_in].astype(BF16)
    tail = jnp.concatenate(
        [w[:, off_a:off_u], jnp.zeros((w.shape[0], P_WIDTH - n_in), F32)], axis=1)
    o_ref[:, P_A:P_WIDTH] = tail.astype(BF16)


def _win_layout(w_in, off_a, off_u):
    rows = 256
    kdim, n_in = w_in.shape[1], w_in.shape[2]
    return pl.pallas_call(
        functools.partial(_win_kernel, off_a, off_u),
        grid=(kdim // rows,),
        in_specs=[pl.BlockSpec((1, rows, n_in), lambda i: (0, i, 0))],
        out_specs=pl.BlockSpec((rows, P_WIDTH), lambda i: (i, 0)),
        out_shape=jax.ShapeDtypeStruct((kdim, P_WIDTH), BF16),
        compiler_params=_compiler_params(("arbitrary",)),
        name="win_layout",
    )(w_in)


def _pos_kernel(o_ref):
    nf = D_MODEL // 4
    p = lax.broadcasted_iota(I32, (GRID_W, nf), 0).astype(F32)
    i = lax.broadcasted_iota(I32, (GRID_W, nf), 1).astype(F32)
    omega = jnp.exp(i * (-math.log(10000.0) / nf))
    a = p * omega
    o_ref[:, 0:nf] = jnp.sin(a)
    o_ref[:, nf:2 * nf] = jnp.cos(a)


def _pos_table():
    return pl.pallas_call(
        _pos_kernel,
        out_shape=jax.ShapeDtypeStruct((GRID_W, D_MODEL // 2), F32),
        name="sincos_table",
    )()


def _add_pos(x, e_ref, blk, add_pos):
    if not add_pos:
        return x
    half = D_MODEL // 2
    e_all = e_ref[...]
    rows = []
    for j in range(TOKEN_BLOCK // GRID_W):
        xj = x[j * GRID_W:(j + 1) * GRID_W]
        e_row = e_ref[pl.ds(blk * (TOKEN_BLOCK // GRID_W) + j, 1), :]
        rows.append(jnp.concatenate([xj[:, 0:half] + e_row, xj[:, half:] + e_all], axis=1))
    return jnp.concatenate(rows, axis=0)


def _chunk_masks():
    r = lax.broadcasted_iota(I32, (TOKEN_BLOCK, TOKEN_BLOCK), 0)
    c = lax.broadcasted_iota(I32, (TOKEN_BLOCK, TOKEN_BLOCK), 1)
    same = (r // GLA_CHUNK) == (c // GLA_CHUNK)
    return same & (c <= r), same & (c >= r)


def _gla_direction(q, k, v, cum, fwd, att_mask, st_ref):
    qe = q * jnp.exp(cum)
    ke = k * jnp.exp(-cum)
    yield
    lane = lax.broadcasted_iota(I32, (1, LANES), 1)
    o_intra = []
    for pair in range(2):
        qp = qe[:, pair * LANES:(pair + 1) * LANES]
        kp = ke[:, pair * LANES:(pair + 1) * LANES]
        for hh in range(2):
            qm = jnp.where((lane // GLA_DK) == hh, qp, 0.0)
            att = jnp.where(att_mask, _dot_nt(qm, kp), 0.0)
            head = 2 * pair + hh
            o_intra.append(_dot(att, v[:, head * GLA_DV:(head + 1) * GLA_DV]))
            yield
    o_intra = jnp.concatenate(o_intra, axis=1)

    er = lax.broadcasted_iota(I32, (2 * GLA_DV, 2 * GLA_DK), 0)
    dc = lax.broadcasted_iota(I32, (2 * GLA_DV, 2 * GLA_DK), 1)
    same_head = (er // GLA_DV) == (dc // GLA_DK)
    n_chunks = TOKEN_BLOCK // GLA_CHUNK
    o_inter = [None] * n_chunks
    for c in (range(n_chunks) if fwd else reversed(range(n_chunks))):
        r0 = c * GLA_CHUNK
        rows = slice(r0, r0 + GLA_CHUNK)
        last = cum[r0 + GLA_CHUNK - 1:r0 + GLA_CHUNK] if fwd else cum[r0:r0 + 1]
        kd = k[rows] * jnp.exp(last - cum[rows])
        dec = jnp.exp(last)
        parts = []
        for pair in range(2):
            dl = slice(pair * LANES, (pair + 1) * LANES)
            st = st_ref[pair]
            parts.append(_dot_nt(qe[rows, dl], st))
            ds_t = _dot_tn(v[rows, pair * 2 * GLA_DV:(pair + 1) * 2 * GLA_DV], kd[:, dl])
            st_ref[pair] = dec[:, dl] * st + jnp.where(same_head, ds_t, 0.0)
        o_inter[c] = jnp.concatenate(parts, axis=1)
        yield
    return o_intra + jnp.concatenate(o_inter, axis=0)


def _interleave(chains):
    chains = list(chains)
    done = [False] * len(chains)
    tick = 0
    while not all(done):
        for i, ch in enumerate(chains):
            if tick >= i and not done[i]:
                try:
                    next(ch)
                except StopIteration:
                    done[i] = True
        tick += 1


def _load_state(s0_ref, u, st_ref):
    zero = jnp.zeros((GLA_DV, GLA_DK), F32)
    for pair in range(2):
        a = s0_ref[u, 2 * pair].T
        b = s0_ref[u, 2 * pair + 1].T
        st_ref[pair] = jnp.concatenate(
            [jnp.concatenate([a, zero], axis=1), jnp.concatenate([zero, b], axis=1)], axis=0)


def _store_state(st_ref, sfin_ref, u):
    for pair in range(2):
        st = st_ref[pair]
        sfin_ref[u, 2 * pair] = st[0:GLA_DV, 0:GLA_DK].T
        sfin_ref[u, 2 * pair + 1] = st[GLA_DV:2 * GLA_DV, GLA_DK:2 * GLA_DK].T


def _mod_row(mod_ref, row, part):
    return mod_ref[pl.ds(row, 1), part * D_MODEL:(part + 1) * D_MODEL]


def _mixer_fwd_kernel(group, add_pos, mod_base, mod_per_seq, nb,
                      x_ref, e_ref, mod_ref, n1_ref, win_ref, wa_ref, ba_ref,
                      snw_ref, sws_ref, sbs_ref, s0_ref,
                      qkv_ref, g_ref, lab_ref, s_ref, of_ref, sfin_ref,
                      st_ref):
    grp = pl.program_id(0)
    blk = pl.program_id(1)
    lo_mask, _ = _chunk_masks()

    @pl.when(blk == 0)
    def _():
        for u in range(group):
            _load_state(s0_ref, u, st_ref.at[u])

    def chain(u):
        row = mod_base + ((grp * group + u) if mod_per_seq else 0)
        st_u = st_ref.at[u]
        xin = _add_pos(x_ref[u], e_ref, blk, add_pos)
        h = _rms(xin, n1_ref[...]) * (1.0 + _mod_row(mod_ref, row, 1)) + _mod_row(mod_ref, row, 0)
        yield
        hb = h.astype(BF16)
        p_parts = []
        for c0, c1 in ((P_Q, P_G), (P_G, P_SV), (P_SV, P_WIDTH)):
            p_parts.append(jnp.dot(hb, win_ref[:, c0:c1], preferred_element_type=F32))
            yield
        p = jnp.concatenate(p_parts, axis=1)
        q = p[:, P_Q:P_K] * (GLA_DK ** -0.5)
        k = p[:, P_K:P_V]
        v = p[:, P_V:P_G]
        z = _dot(p[:, P_A:P_WIDTH], wa_ref[...]) + ba_ref[...]
        la = (jnp.minimum(z, 0.0) - jnp.log1p(jnp.exp(-jnp.abs(z)))) * (1.0 / GLA_GATE_NORM)
        qkv_ref[u] = jnp.concatenate([q, k, v], axis=1)
        g_ref[u] = p[:, P_G:P_U]
        lab_ref[u] = la[:, QK_W:2 * QK_W]
        yield

        ug = jax.nn.gelu(p[:, P_U:P_SV])
        yield
        vg = jax.nn.gelu(p[:, P_SV:P_A])
        yield
        s_cols = []
        for gi in range(SGU_GROUPS):
            cols = slice(gi * SGU_CH, (gi + 1) * SGU_CH)
            vn = _rms(vg[:, cols], snw_ref[:, cols])
            rhs = jnp.concatenate([vn[0:SGU_CHUNK], vn[SGU_CHUNK:2 * SGU_CHUNK]], axis=1)
            vm = _dot(sws_ref[gi], rhs) + jnp.concatenate([sbs_ref[gi], sbs_ref[gi]], axis=1)
            vm = jnp.concatenate([vm[:, 0:SGU_CH], vm[:, SGU_CH:2 * SGU_CH]], axis=0)
            s_cols.append(ug[:, cols] * vm)
        s_ref[u] = jnp.concatenate(s_cols, axis=1)
        yield

        cum = _select_dot(lo_mask, la[:, 0:QK_W])
        yield
        of_ref[u] = yield from _gla_direction(q, k, v, cum, True, lo_mask, st_u)

    _interleave(chain(u) for u in range(group))

    @pl.when(blk == nb - 1)
    def _():
        for u in range(group):
            _store_state(st_ref.at[u], sfin_ref, u)


def _mixer_fwd(x, e_tab, mod, n1, win, wa, ba, snw, sws, sbs, s0, *, nseq, nb, add_pos,
               mod_base, mod_per_seq):
    seq_len = nb * TOKEN_BLOCK
    group = min(SEQ_GROUP, nseq)
    tok = lambda w: pl.BlockSpec((group, TOKEN_BLOCK, w), lambda s, i: (s, i, 0))
    full = lambda a: pl.BlockSpec(a.shape, lambda s, i: (0,) * a.ndim)
    st_spec = pl.BlockSpec((group, GLA_HEADS, GLA_DK, GLA_DV), lambda s, i: (s, 0, 0, 0))
    s0_spec = st_spec if s0.shape[0] == nseq else pl.BlockSpec(s0.shape, lambda s, i: (0, 0, 0, 0))
    act = lambda w: jax.ShapeDtypeStruct((nseq, seq_len, w), F32)
    kern = functools.partial(_mixer_fwd_kernel, group, add_pos, mod_base, mod_per_seq, nb)
    return pl.pallas_call(
        kern,
        grid=(nseq // group, nb),
        in_specs=[tok(D_MODEL), full(e_tab), full(mod), full(n1), full(win), full(wa), full(ba),
                  full(snw), full(sws), full(sbs), s0_spec],
        out_specs=[tok(1024), tok(GLA_WIDTH), tok(QK_W), tok(SGU_WIDTH), tok(GLA_WIDTH), st_spec],
        out_shape=[act(1024), act(GLA_WIDTH), act(QK_W), act(SGU_WIDTH), act(GLA_WIDTH),
                   jax.ShapeDtypeStruct((nseq, GLA_HEADS, GLA_DK, GLA_DV), F32)],
        scratch_shapes=[pltpu.VMEM((group, 2, 2 * GLA_DV, 2 * GLA_DK), F32)],
        compiler_params=_compiler_params(("arbitrary", "arbitrary")),
        name="mixer_fwd",
    )(x, e_tab, mod, n1, win, wa, ba, snw, sws, sbs, s0)


def _mixer_bwd_kernel(group, add_pos, mod_base, mod_per_seq, nb,
                      x_ref, e_ref, mod_ref, qkv_ref, g_ref, lab_ref, s_ref, of_ref,
                      gnw_ref, wout_ref, n2_ref, rw_ref, s0_ref,
                      x1_ref, h2t_ref, probs_ref, sfin_ref,
                      st_ref):
    grp = pl.program_id(0)
    step = pl.program_id(1)
    blk = nb - 1 - step
    _, hi_mask = _chunk_masks()

    @pl.when(step == 0)
    def _():
        for u in range(group):
            _load_state(s0_ref, u, st_ref.at[u])

    def chain(u):
        row = mod_base + ((grp * group + u) if mod_per_seq else 0)
        st_u = st_ref.at[u]
        qkv = qkv_ref[u]
        q, k, v = qkv[:, 0:QK_W], qkv[:, QK_W:2 * QK_W], qkv[:, 2 * QK_W:]
        cum = _select_dot(hi_mask, lab_ref[u])
        yield
        o_b = yield from _gla_direction(q, k, v, cum, False, hi_mask, st_u)
        o = of_ref[u] + o_b
        g = g_ref[u]
        cols = []
        for head in range(GLA_HEADS):
            hs = slice(head * GLA_DV, (head + 1) * GLA_DV)
            cols.append(_rms(o[:, hs], gnw_ref[...]) * jax.nn.silu(g[:, hs]))
        cols.append(s_ref[u])
        yield
        y = _dot(jnp.concatenate(cols, axis=1), wout_ref[...])
        yield

        xin = _add_pos(x_ref[u], e_ref, blk, add_pos)
        x1 = xin + _mod_row(mod_ref, row, 2) * y
        x1_ref[u] = x1
        h2 = _rms(x1, n2_ref[...]) * (1.0 + _mod_row(mod_ref, row, 4)) + _mod_row(mod_ref, row, 3)
        for s in range(TILES_PER_TOKEN):
            h2t_ref[u, pl.ds(s, TOKEN_BLOCK, stride=TILES_PER_TOKEN), :] = h2[:, s * LANES:(s + 1) * LANES]
        yield

        logits = _dot_nt_3pass(rw_ref[...], h2)
        m = jnp.max(logits, axis=0, keepdims=True)
        ex = jnp.exp(logits - m)
        probs_ref[u, 0] = ex / jnp.sum(ex, axis=0, keepdims=True)

    _interleave(chain(u) for u in range(group))

    @pl.when(step == nb - 1)
    def _():
        for u in range(group):
            _store_state(st_ref.at[u], sfin_ref, u)


def _mixer_bwd(x, e_tab, mod, qkv, g, lab, s, of, gnw, wout, n2, rw_t, s0, *, nseq, nb, add_pos,
               mod_base, mod_per_seq):
    seq_len = nb * TOKEN_BLOCK
    group = min(SEQ_GROUP, nseq)
    tok = lambda w: pl.BlockSpec((group, TOKEN_BLOCK, w), lambda s_, i: (s_, nb - 1 - i, 0))
    full = lambda a: pl.BlockSpec(a.shape, lambda s_, i: (0,) * a.ndim)
    st_spec = pl.BlockSpec((group, GLA_HEADS, GLA_DK, GLA_DV), lambda s_, i: (s_, 0, 0, 0))
    s0_spec = st_spec if s0.shape[0] == nseq else pl.BlockSpec(s0.shape, lambda s_, i: (0, 0, 0, 0))
    kern = functools.partial(_mixer_bwd_kernel, group, add_pos, mod_base, mod_per_seq, nb)
    return pl.pallas_call(
        kern,
        grid=(nseq // group, nb),
        in_specs=[tok(D_MODEL), full(e_tab), full(mod), tok(1024), tok(GLA_WIDTH), tok(QK_W),
                  tok(SGU_WIDTH), tok(GLA_WIDTH), full(gnw), full(wout), full(n2), full(rw_t),
                  s0_spec],
        out_specs=[tok(D_MODEL),
                   pl.BlockSpec((group, TOKEN_BLOCK * TILES_PER_TOKEN, LANES),
                                lambda s_, i: (s_, nb - 1 - i, 0)),
                   pl.BlockSpec((group, 1, N_EXPERTS, TOKEN_BLOCK),
                                lambda s_, i: (s_, nb - 1 - i, 0, 0)),
                   st_spec],
        out_shape=[jax.ShapeDtypeStruct((nseq, seq_len, D_MODEL), F32),
                   jax.ShapeDtypeStruct((nseq, seq_len * TILES_PER_TOKEN, LANES), F32),
                   jax.ShapeDtypeStruct((nseq, nb, N_EXPERTS, TOKEN_BLOCK), F32),
                   jax.ShapeDtypeStruct((nseq, GLA_HEADS, GLA_DK, GLA_DV), F32)],
        scratch_shapes=[pltpu.VMEM((group, 2, 2 * GLA_DV, 2 * GLA_DK), F32)],
        compiler_params=_compiler_params(("arbitrary", "arbitrary")),
        name="mixer_bwd",
    )(x, e_tab, mod, qkv, g, lab, s, of, gnw, wout, n2, rw_t, s0)


def _route_kernel(n_tok, cap, probs_ref, row_ref, gate_ref, xs_ref, ps_ref):
    n_blk = n_tok // TOKEN_BLOCK
    n_chunk = n_tok // LANES
    probs = jnp.concatenate([probs_ref[b] for b in range(n_blk)], axis=1)
    capf = jnp.float32(cap)

    def count(mask):
        return jnp.sum(mask.astype(F32), axis=1, keepdims=True)

    def as_f32(bits):
        return lax.bitcast_convert_type(bits, F32)

    def thr_step(_, lohi):
        lo, hi = lohi
        mid = lo + ((hi - lo + 1) >> 1)
        ok = count(probs >= as_f32(mid)) >= capf
        return jnp.where(ok, mid, lo), jnp.where(ok, hi, mid - 1)

    lo0 = jnp.zeros((N_EXPERTS, 1), I32)
    hi0 = jnp.full((N_EXPERTS, 1), 0x3F800000, I32)
    thr, _ = lax.fori_loop(0, 31, thr_step, (lo0, hi0))
    gt = probs >= as_f32(thr + 1)
    eq = (probs >= as_f32(thr)) & jnp.logical_not(gt)
    need = capf - count(gt)
    tok = lax.broadcasted_iota(I32, (N_EXPERTS, n_tok), 1)

    def tie_step(_, lohi):
        lo, hi = lohi
        mid = (lo + hi) >> 1
        ok = count(eq & (tok <= mid)) >= need
        return jnp.where(ok, lo, mid + 1), jnp.where(ok, mid, hi)

    n_bits = max(1, (n_tok - 1).bit_length())
    cut, _ = lax.fori_loop(0, n_bits, tie_step,
                           (jnp.zeros((N_EXPERTS, 1), I32), jnp.full((N_EXPERTS, 1), n_tok - 1, I32)))
    sel = (gt | (eq & (tok <= cut))).astype(F32)

    xs_ref[...] = jnp.concatenate([sel[:, c * LANES:(c + 1) * LANES] for c in range(n_chunk)], axis=0)
    ps_ref[...] = jnp.concatenate([probs[:, c * LANES:(c + 1) * LANES] for c in range(n_chunk)], axis=0)

    li = lax.broadcasted_iota(I32, (LANES, LANES), 0)
    lj = lax.broadcasted_iota(I32, (LANES, LANES), 1)
    upper = (li <= lj).astype(F32)
    ci = lax.broadcasted_iota(I32, (n_chunk, n_chunk), 0)
    cj = lax.broadcasted_iota(I32, (n_chunk, n_chunk), 1)
    lower = (cj <= ci).astype(F32)
    slot = lax.broadcasted_iota(I32, (1, cap), 1).astype(F32)
    chunk_id = lax.broadcasted_iota(I32, (n_chunk, cap), 0).astype(F32)
    lane_id = lax.broadcasted_iota(I32, (LANES, cap), 0).astype(F32)
    reps = cap // LANES

    def per_expert(e):
        x = xs_ref[pl.ds(e, n_chunk, stride=N_EXPERTS), :]
        pe = ps_ref[pl.ds(e, n_chunk, stride=N_EXPERTS), :]
        ploc = _dot(x, upper)
        tot = jnp.broadcast_to(ploc[:, LANES - 1:LANES], (n_chunk, LANES))
        cum = _dot(lower, tot)
        yield
        cum_w = jnp.concatenate([cum] * reps, axis=1)
        base_w = jnp.concatenate([cum - tot] * reps, axis=1)
        chunk_of = jnp.sum((cum_w <= slot).astype(F32), axis=0, keepdims=True)
        onehot = chunk_id == chunk_of
        local = slot - jnp.sum(jnp.where(onehot, base_w, 0.0), axis=0, keepdims=True)
        yield
        lhs = jnp.concatenate([ploc.astype(BF16)] + _split_bf16(pe, 3), axis=1)
        got = _dot_tn(lhs, onehot.astype(F32))
        yield
        pref = got[0:LANES]
        lane_of = jnp.sum((pref <= local).astype(F32), axis=0, keepdims=True)
        token = chunk_of * LANES + lane_of
        row_ref[pl.ds(e, 1), :] = (token * TILES_PER_TOKEN).astype(I32)
        yield
        pg = (got[3 * LANES:4 * LANES] + got[2 * LANES:3 * LANES]) + got[LANES:2 * LANES]
        gate_ref[pl.ds(e, 1), :] = jnp.sum(jnp.where(lane_id == lane_of, pg, 0.0), axis=0, keepdims=True)

    def expert_pair(i, _):
        _interleave(per_expert(2 * i + u) for u in range(2))
        return 0

    lax.fori_loop(0, N_EXPERTS // 2, expert_pair, 0)


def _route(probs, n_tok, cap):
    return pl.pallas_call(
        functools.partial(_route_kernel, n_tok, cap),
        out_shape=[jax.ShapeDtypeStruct((N_EXPERTS, cap), I32),
                   jax.ShapeDtypeStruct((N_EXPERTS, cap), F32)],
        scratch_shapes=[pltpu.VMEM((n_tok // LANES * N_EXPERTS, LANES), F32),
                        pltpu.VMEM((n_tok // LANES * N_EXPERTS, LANES), F32)],
        compiler_params=pltpu.CompilerParams(vmem_limit_bytes=VMEM_LIMIT),
        name="route_topk",
    )(probs)


def _expert_kernel(cap, row_ref, h2t_ref, w1_ref, w3_ref, w2_ref, ye_ref,
                   xe_ref, x2_ref, hid_ref, sem):
    e = pl.program_id(0)
    f = pl.program_id(1)
    slot = e % 2
    rows_per_step = cap // EXPERT_STEPS

    def start_row(expert, buf, j):
        src = pl.multiple_of(row_ref[expert, j], TILES_PER_TOKEN)
        dst = pl.multiple_of(j * TILES_PER_TOKEN, TILES_PER_TOKEN)
        pltpu.make_async_copy(h2t_ref.at[pl.ds(src, TILES_PER_TOKEN), :],
                              xe_ref.at[buf, pl.ds(dst, TILES_PER_TOKEN), :], sem.at[buf]).start()

    def wait_rows(buf):
        pltpu.make_async_copy(h2t_ref.at[pl.ds(0, cap * TILES_PER_TOKEN), :], xe_ref.at[buf],
                              sem.at[buf]).wait()

    def prefetch_next():
        nxt = jnp.minimum(e + 1, N_EXPERTS - 1)
        first = f * rows_per_step
        for j in range(rows_per_step):
            start_row(nxt, 1 - slot, first + j)

    @pl.when((e == 0) & (f == 0))
    def _():
        def issue(j, _):
            start_row(0, 0, j)
            return 0

        lax.fori_loop(0, cap, issue, 0, unroll=8)

    @pl.when(f == 0)
    def _():
        wait_rows(slot)
        for s in range(TILES_PER_TOKEN):
            x2_ref[:, s * LANES:(s + 1) * LANES] = (
                xe_ref[slot, pl.ds(s, cap, stride=TILES_PER_TOKEN), :].astype(BF16))

    @pl.when(f < N_HID_STEPS)
    def _():
        prefetch_next()
        x2 = x2_ref[...]
        a = jnp.dot(x2, w1_ref[0].astype(BF16), preferred_element_type=F32)
        b = jnp.dot(x2, w3_ref[0].astype(BF16), preferred_element_type=F32)
        hid_ref[f] = (jax.nn.silu(a) * b).astype(BF16)

    @pl.when(f >= N_HID_STEPS)
    def _():
        prefetch_next()
        w2 = w2_ref[0].astype(BF16)
        out = jnp.dot(hid_ref[0], w2[0:EXPERT_F_BLOCK], preferred_element_type=F32)
        for kb in range(1, N_HID_STEPS):
            out += jnp.dot(hid_ref[kb], w2[kb * EXPERT_F_BLOCK:(kb + 1) * EXPERT_F_BLOCK],
                           preferred_element_type=F32)
        tile0 = (f - N_HID_STEPS) * (EXPERT_N_BLOCK // LANES)
        for i in range(EXPERT_N_BLOCK // LANES):
            ye_ref[pl.ds(tile0 + i, cap, stride=TILES_PER_TOKEN), :] = out[:, i * LANES:(i + 1) * LANES]

    @pl.when((e == N_EXPERTS - 1) & (f == EXPERT_STEPS - 1))
    def _():
        wait_rows(1 - slot)


def _experts(rows, h2t, w1, w3, w2, cap):
    hid_blk = lambda e, f, idx: (e, 0, jnp.minimum(f, N_HID_STEPS - 1))
    n_out = D_MODEL // EXPERT_N_BLOCK

    def out_blk(e, f, idx):
        hold = (f == 0) & (e > 0)
        return (jnp.where(hold, e - 1, e), 0,
                jnp.where(hold, n_out - 1, jnp.maximum(f - N_HID_STEPS, 0)))

    grid_spec = pltpu.PrefetchScalarGridSpec(
        num_scalar_prefetch=1,
        grid=(N_EXPERTS, EXPERT_STEPS),
        in_specs=[pl.BlockSpec(memory_space=pl.ANY),
                  pl.BlockSpec((1, D_MODEL, EXPERT_F_BLOCK), hid_blk),
                  pl.BlockSpec((1, D_MODEL, EXPERT_F_BLOCK), hid_blk),
                  pl.BlockSpec((1, D_EXPERT, EXPERT_N_BLOCK), out_blk)],
        out_specs=pl.BlockSpec((cap * TILES_PER_TOKEN, LANES), lambda e, f, idx: (e, 0)),
        scratch_shapes=[pltpu.VMEM((2, cap * TILES_PER_TOKEN, LANES), F32),
                        pltpu.VMEM((cap, D_MODEL), BF16),
                        pltpu.VMEM((N_HID_STEPS, cap, EXPERT_F_BLOCK), BF16),
                        pltpu.SemaphoreType.DMA((2,))],
    )
    return pl.pallas_call(
        functools.partial(_expert_kernel, cap),
        grid_spec=grid_spec,
        out_shape=jax.ShapeDtypeStruct((N_EXPERTS * cap * TILES_PER_TOKEN, LANES), F32),
        compiler_params=_compiler_params(("arbitrary", "arbitrary")),
        name="expert_swiglu",
    )(rows, h2t, w1, w3, w2)


COMBINE_BATCH = 8
ZERO_ROWS = 512


def _combine_kernel(cap, n_tok, mod_base, mod_per_seq, nb,
                    row_ref, gate_ref, ye_ref, x1_ref, mod_ref, fw_ref, y_ref,
                    acc_ref, xbuf_ref, ybuf_ref, sem_x, sem_y):
    e = pl.program_id(0)

    @pl.when(e == 0)
    def _():
        def zero(i, _):
            r = pl.multiple_of(i * ZERO_ROWS, ZERO_ROWS)
            acc_ref[pl.ds(r, ZERO_ROWS), :] = jnp.zeros((ZERO_ROWS, LANES), F32)
            return 0

        lax.fori_loop(0, n_tok * TILES_PER_TOKEN // ZERO_ROWS, zero, 0)

    def batch(jb, _):
        vals = []
        for u in range(COMBINE_BATCH):
            j = jb * COMBINE_BATCH + u
            t = pl.multiple_of(row_ref[e, j], TILES_PER_TOKEN)
            gate = gate_ref[e, j]
            src = pl.multiple_of(j * TILES_PER_TOKEN, TILES_PER_TOKEN)
            vals.append((t, acc_ref[pl.ds(t, TILES_PER_TOKEN), :]
                         + ye_ref[pl.ds(src, TILES_PER_TOKEN), :] * gate))
        for t, val in vals:
            acc_ref[pl.ds(t, TILES_PER_TOKEN), :] = val
        return 0

    lax.fori_loop(0, cap // COMBINE_BATCH, batch, 0)

    @pl.when(e == N_EXPERTS - 1)
    def _():
        n_blk = n_tok // TOKEN_BLOCK

        def x1_copy(b, slot):
            rows = pl.ds(pl.multiple_of(b * TOKEN_BLOCK, TOKEN_BLOCK), TOKEN_BLOCK)
            return pltpu.make_async_copy(x1_ref.at[rows, :], xbuf_ref.at[slot], sem_x.at[slot])

        def y_copy(b, slot):
            rows = pl.ds(pl.multiple_of(b * TOKEN_BLOCK, TOKEN_BLOCK), TOKEN_BLOCK)
            return pltpu.make_async_copy(ybuf_ref.at[slot], y_ref.at[rows, :], sem_y.at[slot])

        x1_copy(0, 0).start()

        def block(b, _):
            slot = b % 2

            @pl.when(b + 1 < n_blk)
            def _():
                x1_copy(b + 1, 1 - slot).start()

            x1_copy(b, slot).wait()

            @pl.when(b >= 2)
            def _():
                y_copy(b - 2, slot).wait()

            base = pl.multiple_of(b * (TOKEN_BLOCK * TILES_PER_TOKEN), TOKEN_BLOCK * TILES_PER_TOKEN)
            moe = jnp.concatenate(
                [acc_ref[pl.ds(base + s, TOKEN_BLOCK, stride=TILES_PER_TOKEN), :]
                 for s in range(TILES_PER_TOKEN)], axis=1)
            row = mod_base + ((b // nb) if mod_per_seq else 0)
            ybuf_ref[slot] = _rms(xbuf_ref[slot] + _mod_row(mod_ref, row, 5) * moe, fw_ref[...])
            y_copy(b, slot).start()
            return 0

        lax.fori_loop(0, n_blk, block, 0)
        y_copy(n_blk - 2, n_blk % 2).wait()
        y_copy(n_blk - 1, (n_blk - 1) % 2).wait()


def _combine_final(rows, gates, ye, x1, mod, fw, cap, n_tok, *, nb, mod_base, mod_per_seq):
    full = lambda a: pl.BlockSpec(a.shape, lambda e, rows, gate: (0,) * a.ndim)
    grid_spec = pltpu.PrefetchScalarGridSpec(
        num_scalar_prefetch=2,
        grid=(N_EXPERTS,),
        in_specs=[pl.BlockSpec((cap * TILES_PER_TOKEN, LANES), lambda e, rows, gate: (e, 0)),
                  pl.BlockSpec(memory_space=pl.ANY), full(mod), full(fw)],
        out_specs=pl.BlockSpec(memory_space=pl.ANY),
        scratch_shapes=[pltpu.VMEM((n_tok * TILES_PER_TOKEN, LANES), F32),
                        pltpu.VMEM((2, TOKEN_BLOCK, D_MODEL), F32),
                        pltpu.VMEM((2, TOKEN_BLOCK, D_MODEL), F32),
                        pltpu.SemaphoreType.DMA((2,)),
                        pltpu.SemaphoreType.DMA((2,))],
    )
    return pl.pallas_call(
        functools.partial(_combine_kernel, cap, n_tok, mod_base, mod_per_seq, nb),
        grid_spec=grid_spec,
        out_shape=jax.ShapeDtypeStruct((n_tok, D_MODEL), F32),
        compiler_params=_compiler_params(("arbitrary",)),
        name="moe_combine_norm",
    )(rows, gates, ye, x1, mod, fw)


def _trunk_and_norm(x, e_tab, mod, s0_f, s0_b, prm, *, nseq, seq_len, add_pos, mod_base, mod_per_seq):
    nb = seq_len // TOKEN_BLOCK
    n_tok = nseq * seq_len
    cap = EC_CAPACITY_FACTOR * n_tok // N_EXPERTS
    kw = dict(nseq=nseq, nb=nb, add_pos=add_pos, mod_base=mod_base, mod_per_seq=mod_per_seq)
    qkv, g, lab, s, of, sfin_f = _mixer_fwd(
        x, e_tab, mod, prm["n1"], prm["win"], prm["wa"], prm["ba"], prm["snw"], prm["sws"],
        prm["sbs"], s0_f, **kw)
    x1, h2t, probs, sfin_b = _mixer_bwd(
        x, e_tab, mod, qkv, g, lab, s, of, prm["gnw"], prm["wout"], prm["n2"], prm["rw_t"],
        s0_b, **kw)
    x1 = x1.reshape(n_tok, D_MODEL)
    h2t = h2t.reshape(n_tok * TILES_PER_TOKEN, LANES)
    probs = probs.reshape(n_tok // TOKEN_BLOCK, N_EXPERTS, TOKEN_BLOCK)
    rows, gates = _route(probs, n_tok, cap)
    ye = _experts(rows, h2t, prm["w1"], prm["w3"], prm["w2"], cap)
    y = _combine_final(rows, gates, ye, x1, mod, prm["fw"], cap, n_tok,
                       nb=nb, mod_base=mod_base, mod_per_seq=mod_per_seq)
    return y.reshape(nseq, seq_len, D_MODEL), sfin_f, sfin_b


def kernel(x_prompt, x_sample, state_gla_fwd, state_gla_bwd, c, c_ctx, ada_w, ada_b, norm1_w, w_in, gla_wa2_f, gla_ba_f, gla_wa2_b, gla_ba_b, gla_norm_w, sgu_norm_w, sgu_ws, sgu_bs, w_out, norm2_w, router_w, exp_w1, exp_w3, exp_w2, final_norm_w):
    assert ada_w.shape[0] == 1, "single trunk layer"
    batch, seq, _ = x_prompt.shape
    dec_batch, dec_seq, _ = x_sample.shape

    off_af = 2 * QK_W + 2 * GLA_WIDTH
    off_u = off_af + 2 * GLA_LOWRANK
    win = _win_layout(w_in, off_af, off_u)
    wa = jnp.zeros((P_WIDTH - P_A, 2 * QK_W), F32)
    wa = wa.at[0:GLA_LOWRANK, 0:QK_W].set(gla_wa2_f[0])
    wa = wa.at[GLA_LOWRANK:2 * GLA_LOWRANK, QK_W:].set(gla_wa2_b[0]).astype(BF16)
    prm = dict(
        n1=norm1_w, win=win, wa=wa,
        ba=jnp.concatenate([gla_ba_f[0], gla_ba_b[0]])[None, :],
        snw=sgu_norm_w, sws=sgu_ws[0].astype(BF16),
        sbs=jnp.broadcast_to(sgu_bs[0][:, :, None], (SGU_GROUPS, SGU_CHUNK, SGU_CH)),
        gnw=gla_norm_w, wout=w_out[0].astype(BF16), n2=norm2_w, rw_t=router_w[0].T,
        w1=exp_w1[0], w3=exp_w3[0], w2=exp_w2[0], fw=final_norm_w[None, :])

    cvec = jnp.concatenate([c_ctx[None, :], c, jnp.zeros((SUBLANES - 1 - dec_batch, D_MODEL), F32)])
    mod = _modulation(cvec, ada_w[0], ada_b)
    e_tab = _pos_table()

    zero_state = jnp.zeros((min(SEQ_GROUP, batch), GLA_HEADS, GLA_DK, GLA_DV), F32)
    y_prompt, sf, sb = _trunk_and_norm(
        x_prompt, e_tab, mod, zero_state, zero_state, prm,
        nseq=batch, seq_len=seq, add_pos=False, mod_base=0, mod_per_seq=False)
    y_sample, _, _ = _trunk_and_norm(
        x_sample, e_tab, mod, state_gla_fwd[:, 0], state_gla_bwd[:, 0], prm,
        nseq=dec_batch, seq_len=dec_seq, add_pos=True, mod_base=1, mod_per_seq=True)
    return (y_prompt, y_sample, sf[:, None], sb[:, None])
```

```python
import functools
import math

import jax
import jax.numpy as jnp
from jax import lax
from jax.experimental import pallas as pl
from jax.experimental.pallas import tpu as pltpu

F32 = jnp.float32
BF16 = jnp.bfloat16
I32 = jnp.int32

D_MODEL = 1024
GRID_W = 64
GLA_HEADS = 4
GLA_DK = 64
GLA_DV = 128
GLA_WIDTH = GLA_HEADS * GLA_DV
QK_W = GLA_HEADS * GLA_DK
GLA_LOWRANK = 16
GLA_GATE_NORM = 16.0
GLA_CHUNK = 64
SGU_WIDTH = 512
SGU_GROUPS = 4
SGU_CH = 128
SGU_CHUNK = 128
N_EXPERTS = 16
EC_CAPACITY_FACTOR = 2
D_EXPERT = 2048
EPS = 1e-6

SUBLANES = 8
LANES = 128
TILES_PER_TOKEN = D_MODEL // LANES

TOKEN_BLOCK = 256
SEQ_GROUP = 4
P_Q, P_K, P_V, P_G, P_U, P_SV, P_A = 0, 256, 512, 1024, 1536, 2048, 2560
P_WIDTH = 2688
EXPERT_F_BLOCK = 512
EXPERT_N_BLOCK = 256
N_HID_STEPS = D_EXPERT // EXPERT_F_BLOCK
EXPERT_STEPS = N_HID_STEPS + D_MODEL // EXPERT_N_BLOCK
VMEM_LIMIT = 56 * 1024 * 1024


def _dot(a, b):
    return jnp.dot(a.astype(BF16), b.astype(BF16), preferred_element_type=F32)


def _dot_nt(a, b):
    return lax.dot_general(a.astype(BF16), b.astype(BF16), (((1,), (1,)), ((), ())),
                           preferred_element_type=F32)


def _dot_tn(a, b):
    return lax.dot_general(a.astype(BF16), b.astype(BF16), (((0,), (0,)), ((), ())),
                           preferred_element_type=F32)


def _dot_f32(a, b, dims=(((1,), (0,)), ((), ()))):
    return lax.dot_general(a, b, dims, precision=lax.Precision.HIGHEST, preferred_element_type=F32)


def _split_bf16(x, terms):
    parts = []
    for _ in range(terms - 1):
        part = x.astype(BF16)
        parts.append(part)
        x = x - part.astype(F32)
    parts.append(x.astype(BF16))
    return parts


def _select_dot(sel, x):
    s = sel.astype(BF16)
    hi, mid, lo = _split_bf16(x, 3)
    return (jnp.dot(s, lo, preferred_element_type=F32) + jnp.dot(s, mid, preferred_element_type=F32)
            + jnp.dot(s, hi, preferred_element_type=F32))


def _dot_nt_3pass(a, b):
    a_hi, a_lo = _split_bf16(a, 2)
    b_hi, b_lo = _split_bf16(b, 2)
    nt = lambda x, y: lax.dot_general(x, y, (((1,), (1,)), ((), ())), preferred_element_type=F32)
    return (nt(a_hi, b_lo) + nt(a_lo, b_hi)) + nt(a_hi, b_hi)


def _rms(x, w):
    return x * lax.rsqrt(jnp.mean(x * x, axis=-1, keepdims=True) + EPS) * w


def _compiler_params(sem):
    return pltpu.CompilerParams(dimension_semantics=sem, vmem_limit_bytes=VMEM_LIMIT)


def _mod_kernel(c_ref, w_ref, b_ref, o_ref):
    o_ref[...] = _dot(jax.nn.silu(c_ref[...]), w_ref[...]) + b_ref[...]


def _modulation(cvec, ada_w, ada_b):
    n = ada_w.shape[1]
    bn = 1536
    return pl.pallas_call(
        _mod_kernel,
        grid=(n // bn,),
        in_specs=[pl.BlockSpec((SUBLANES, D_MODEL), lambda j: (0, 0)),
                  pl.BlockSpec((D_MODEL, bn), lambda j: (0, j)),
                  pl.BlockSpec((1, bn), lambda j: (0, j))],
        out_specs=pl.BlockSpec((SUBLANES, bn), lambda j: (0, j)),
        out_shape=jax.ShapeDtypeStruct((SUBLANES, n), F32),
        compiler_params=_compiler_params(("arbitrary",)),
        name="adaln_mod",
    )(cvec, ada_w, ada_b)


def _win_kernel(off_a, off_u, w_ref, o_ref):
    w = w_ref[0]
    n_in = w.shape[1]
    o_ref[:, 0:off_a] = w[:, 0:off_a].astype(BF16)
    o_ref[:, off_a:off_a + n_in - off_u] = w[:, off_u:n_in].astype(BF16)
    tail = jnp.concatenate(
        [w[:, off_a:off_u], jnp.zeros((w.shape[0], P_WIDTH - n_in), F32)], axis=1)
    o_ref[:, P_A:P_WIDTH] = tail.astype(BF16)


def _win_layout(w_in, off_a, off_u):
    rows = 256
    kdim, n_in = w_in.shape[1], w_in.shape[2]
    return pl.pallas_call(
        functools.partial(_win_kernel, off_a, off_u),
        grid=(kdim // rows,),
        in_specs=[pl.BlockSpec((1, rows, n_in), lambda i: (0, i, 0))],
        out_specs=pl.BlockSpec((rows, P_WIDTH), lambda i: (i, 0)),
        out_shape=jax.ShapeDtypeStruct((kdim, P_WIDTH), BF16),
        compiler_params=_compiler_params(("arbitrary",)),
        name="win_layout",
    )(w_in)


def _pos_kernel(o_ref):
    nf = D_MODEL // 4
    p = lax.broadcasted_iota(I32, (GRID_W, nf), 0).astype(F32)
    i = lax.broadcasted_iota(I32, (GRID_W, nf), 1).astype(F32)
    omega = jnp.exp(i * (-math.log(10000.0) / nf))
    a = p * omega
    o_ref[:, 0:nf] = jnp.sin(a)
    o_ref[:, nf:2 * nf] = jnp.cos(a)


def _pos_table():
    return pl.pallas_call(
        _pos_kernel,
        out_shape=jax.ShapeDtypeStruct((GRID_W, D_MODEL // 2), F32),
        name="sincos_table",
    )()


def _add_pos(x, e_ref, blk, add_pos):
    if not add_pos:
        return x
    half = D_MODEL // 2
    e_all = e_ref[...]
    rows = []
    for j in range(TOKEN_BLOCK // GRID_W):
        xj = x[j * GRID_W:(j + 1) * GRID_W]
        e_row = e_ref[pl.ds(blk * (TOKEN_BLOCK // GRID_W) + j, 1), :]
        rows.append(jnp.concatenate([xj[:, 0:half] + e_row, xj[:, half:] + e_all], axis=1))
    return jnp.concatenate(rows, axis=0)


def _chunk_masks():
    r = lax.broadcasted_iota(I32, (TOKEN_BLOCK, TOKEN_BLOCK), 0)
    c = lax.broadcasted_iota(I32, (TOKEN_BLOCK, TOKEN_BLOCK), 1)
    same = (r // GLA_CHUNK) == (c // GLA_CHUNK)
    return same & (c <= r), same & (c >= r)


def _gla_direction(q, k, v, cum, fwd, att_mask, st_ref, ready, done):
    qe = q * jnp.exp(cum)
    ke = k * jnp.exp(-cum)
    yield
    lane = lax.broadcasted_iota(I32, (1, LANES), 1)
    o_intra = []
    for pair in range(2):
        qp = qe[:, pair * LANES:(pair + 1) * LANES]
        kp = ke[:, pair * LANES:(pair + 1) * LANES]
        for hh in range(2):
            qm = jnp.where((lane // GLA_DK) == hh, qp, 0.0)
            att = jnp.where(att_mask, _dot_nt(qm, kp), 0.0)
            head = 2 * pair + hh
            o_intra.append(_dot(att, v[:, head * GLA_DV:(head + 1) * GLA_DV]))
            yield
    o_intra = jnp.concatenate(o_intra, axis=1)
    while not ready():
        yield

    er = lax.broadcasted_iota(I32, (2 * GLA_DV, 2 * GLA_DK), 0)
    dc = lax.broadcasted_iota(I32, (2 * GLA_DV, 2 * GLA_DK), 1)
    same_head = (er // GLA_DV) == (dc // GLA_DK)
    n_chunks = TOKEN_BLOCK // GLA_CHUNK
    o_inter = [None] * n_chunks
    for c in (range(n_chunks) if fwd else reversed(range(n_chunks))):
        r0 = c * GLA_CHUNK
        rows = slice(r0, r0 + GLA_CHUNK)
        last = cum[r0 + GLA_CHUNK - 1:r0 + GLA_CHUNK] if fwd else cum[r0:r0 + 1]
        kd = k[rows] * jnp.exp(last - cum[rows])
        dec = jnp.exp(last)
        parts = []
        for pair in range(2):
            dl = slice(pair * LANES, (pair + 1) * LANES)
            st = st_ref[pair]
            parts.append(_dot_nt(qe[rows, dl], st))
            ds_t = _dot_tn(v[rows, pair * 2 * GLA_DV:(pair + 1) * 2 * GLA_DV], kd[:, dl])
            st_ref[pair] = dec[:, dl] * st + jnp.where(same_head, ds_t, 0.0)
        o_inter[c] = jnp.concatenate(parts, axis=1)
        yield
    done()
    return o_intra + jnp.concatenate(o_inter, axis=0)


def _interleave(chains):
    chains = list(chains)
    done = [False] * len(chains)
    tick = 0
    while not all(done):
        for i, ch in enumerate(chains):
            if tick >= i and not done[i]:
                try:
                    next(ch)
                except StopIteration:
                    done[i] = True
        tick += 1


def _load_state(s0_ref, u, st_ref):
    zero = jnp.zeros((GLA_DV, GLA_DK), F32)
    for pair in range(2):
        a = s0_ref[u, 2 * pair].T
        b = s0_ref[u, 2 * pair + 1].T
        st_ref[pair] = jnp.concatenate(
            [jnp.concatenate([a, zero], axis=1), jnp.concatenate([zero, b], axis=1)], axis=0)


def _store_state(st_ref, sfin_ref, u):
    for pair in range(2):
        st = st_ref[pair]
        sfin_ref[u, 2 * pair] = st[0:GLA_DV, 0:GLA_DK].T
        sfin_ref[u, 2 * pair + 1] = st[GLA_DV:2 * GLA_DV, GLA_DK:2 * GLA_DK].T


def _mod_row(mod_ref, row, part):
    return mod_ref[pl.ds(row, 1), part * D_MODEL:(part + 1) * D_MODEL]


def _mixer_fwd_kernel(group, add_pos, mod_base, mod_per_seq, nb,
                      x_ref, e_ref, mod_ref, n1_ref, win_ref, wa_ref, ba_ref,
                      snw_ref, sws_ref, sbs_ref, s0_ref,
                      qkv_ref, g_ref, lab_ref, s_ref, of_ref, sfin_ref,
                      st_ref):
    grp = pl.program_id(0)
    blk = pl.program_id(1)
    sub = x_ref.shape[1] // TOKEN_BLOCK
    lo_mask, _ = _chunk_masks()
    state_done = set()

    @pl.when(blk == 0)
    def _():
        for u in range(group):
            _load_state(s0_ref, u, st_ref.at[u])

    def chain(u, b):
        row = mod_base + ((grp * group + u) if mod_per_seq else 0)
        st_u = st_ref.at[u]
        rows = slice(b * TOKEN_BLOCK, (b + 1) * TOKEN_BLOCK)
        xin = _add_pos(x_ref[u, rows, :], e_ref, blk * sub + b, add_pos)
        h = _rms(xin, n1_ref[...]) * (1.0 + _mod_row(mod_ref, row, 1)) + _mod_row(mod_ref, row, 0)
        yield
        hb = h.astype(BF16)
        p_parts = []
        for c0, c1 in ((P_Q, P_G), (P_G, P_SV), (P_SV, P_WIDTH)):
            p_parts.append(jnp.dot(hb, win_ref[:, c0:c1], preferred_element_type=F32))
            yield
        p = jnp.concatenate(p_parts, axis=1)
        q = p[:, P_Q:P_K] * (GLA_DK ** -0.5)
        k = p[:, P_K:P_V]
        v = p[:, P_V:P_G]
        z = _dot(p[:, P_A:P_WIDTH], wa_ref[...]) + ba_ref[...]
        la = (jnp.minimum(z, 0.0) - jnp.log1p(jnp.exp(-jnp.abs(z)))) * (1.0 / GLA_GATE_NORM)
        qkv_ref[u, rows, :] = jnp.concatenate([q, k, v], axis=1)
        g_ref[u, rows, :] = p[:, P_G:P_U]
        lab_ref[u, rows, :] = la[:, QK_W:2 * QK_W]
        yield

        ug = jax.nn.gelu(p[:, P_U:P_SV])
        yield
        vg = jax.nn.gelu(p[:, P_SV:P_A])
        yield
        s_cols = []
        for gi in range(SGU_GROUPS):
            cols = slice(gi * SGU_CH, (gi + 1) * SGU_CH)
            vn = _rms(vg[:, cols], snw_ref[:, cols])
            rhs = jnp.concatenate([vn[0:SGU_CHUNK], vn[SGU_CHUNK:2 * SGU_CHUNK]], axis=1)
            vm = _dot(sws_ref[gi], rhs) + jnp.concatenate([sbs_ref[gi], sbs_ref[gi]], axis=1)
            vm = jnp.concatenate([vm[:, 0:SGU_CH], vm[:, SGU_CH:2 * SGU_CH]], axis=0)
            s_cols.append(ug[:, cols] * vm)
        s_ref[u, rows, :] = jnp.concatenate(s_cols, axis=1)
        yield

        cum = _select_dot(lo_mask, la[:, 0:QK_W])
        yield
        of_ref[u, rows, :] = yield from _gla_direction(
            q, k, v, cum, True, lo_mask, st_u,
            ready=lambda: b == 0 or (u, b - 1) in state_done, done=lambda: state_done.add((u, b)))

    _interleave(chain(u, b) for b in range(sub) for u in range(group))

    @pl.when(blk == nb - 1)
    def _():
        for u in range(group):
            _store_state(st_ref.at[u], sfin_ref, u)


def _mixer_fwd(x, e_tab, mod, n1, win, wa, ba, snw, sws, sbs, s0, *, nseq, nb, add_pos,
               mod_base, mod_per_seq):
    seq_len = nb * TOKEN_BLOCK
    group = min(SEQ_GROUP, nseq)
    sub = min(SEQ_GROUP // group, nb)
    nb //= sub
    tok = lambda w: pl.BlockSpec((group, sub * TOKEN_BLOCK, w), lambda s, i: (s, i, 0))
    full = lambda a: pl.BlockSpec(a.shape, lambda s, i: (0,) * a.ndim)
    st_spec = pl.BlockSpec((group, GLA_HEADS, GLA_DK, GLA_DV), lambda s, i: (s, 0, 0, 0))
    s0_spec = st_spec if s0.shape[0] == nseq else pl.BlockSpec(s0.shape, lambda s, i: (0, 0, 0, 0))
    act = lambda w: jax.ShapeDtypeStruct((nseq, seq_len, w), F32)
    kern = functools.partial(_mixer_fwd_kernel, group, add_pos, mod_base, mod_per_seq, nb)
    return pl.pallas_call(
        kern,
        grid=(nseq // group, nb),
        in_specs=[tok(D_MODEL), full(e_tab), full(mod), full(n1), full(win), full(wa), full(ba),
                  full(snw), full(sws), full(sbs), s0_spec],
        out_specs=[tok(1024), tok(GLA_WIDTH), tok(QK_W), tok(SGU_WIDTH), tok(GLA_WIDTH), st_spec],
        out_shape=[act(1024), act(GLA_WIDTH), act(QK_W), act(SGU_WIDTH), act(GLA_WIDTH),
                   jax.ShapeDtypeStruct((nseq, GLA_HEADS, GLA_DK, GLA_DV), F32)],
        scratch_shapes=[pltpu.VMEM((group, 2, 2 * GLA_DV, 2 * GLA_DK), F32)],
        compiler_params=_compiler_params(("arbitrary", "arbitrary")),
        name="mixer_fwd",
    )(x, e_tab, mod, n1, win, wa, ba, snw, sws, sbs, s0)


def _mixer_bwd_kernel(group, add_pos, mod_base, mod_per_seq, nb,
                      x_ref, e_ref, mod_ref, qkv_ref, g_ref, lab_ref, s_ref, of_ref,
                      gnw_ref, wout_ref, n2_ref, rw_ref, s0_ref,
                      x1_ref, h2t_ref, probs_ref, sfin_ref,
                      st_ref):
    grp = pl.program_id(0)
    step = pl.program_id(1)
    sub = x_ref.shape[1] // TOKEN_BLOCK
    blk = nb - 1 - step
    _, hi_mask = _chunk_masks()
    state_done = set()

    @pl.when(step == 0)
    def _():
        for u in range(group):
            _load_state(s0_ref, u, st_ref.at[u])

    def chain(u, b):
        row = mod_base + ((grp * group + u) if mod_per_seq else 0)
        st_u = st_ref.at[u]
        rows = slice(b * TOKEN_BLOCK, (b + 1) * TOKEN_BLOCK)
        qkv = qkv_ref[u, rows, :]
        q, k, v = qkv[:, 0:QK_W], qkv[:, QK_W:2 * QK_W], qkv[:, 2 * QK_W:]
        cum = _select_dot(hi_mask, lab_ref[u, rows, :])
        yield
        o_b = yield from _gla_direction(
            q, k, v, cum, False, hi_mask, st_u,
            ready=lambda: b == sub - 1 or (u, b + 1) in state_done,
            done=lambda: state_done.add((u, b)))
        o = of_ref[u, rows, :] + o_b
        g = g_ref[u, rows, :]
        cols = []
        for head in range(GLA_HEADS):
            hs = slice(head * GLA_DV, (head + 1) * GLA_DV)
            cols.append(_rms(o[:, hs], gnw_ref[...]) * jax.nn.silu(g[:, hs]))
        cols.append(s_ref[u, rows, :])
        yield
        y = _dot(jnp.concatenate(cols, axis=1), wout_ref[...])
        yield

        xin = _add_pos(x_ref[u, rows, :], e_ref, blk * sub + b, add_pos)
        x1 = xin + _mod_row(mod_ref, row, 2) * y
        x1_ref[u, rows, :] = x1
        h2 = _rms(x1, n2_ref[...]) * (1.0 + _mod_row(mod_ref, row, 4)) + _mod_row(mod_ref, row, 3)
        for s in range(TILES_PER_TOKEN):
            h2t_ref[u, pl.ds(b * TOKEN_BLOCK * TILES_PER_TOKEN + s, TOKEN_BLOCK, stride=TILES_PER_TOKEN), :] = (
                h2[:, s * LANES:(s + 1) * LANES])
        yield

        logits = _dot_nt_3pass(rw_ref[...], h2)
        m = jnp.max(logits, axis=0, keepdims=True)
        ex = jnp.exp(logits - m)
        probs_ref[u, b] = ex / jnp.sum(ex, axis=0, keepdims=True)

    _interleave(chain(u, b) for b in reversed(range(sub)) for u in range(group))

    @pl.when(step == nb - 1)
    def _():
        for u in range(group):
            _store_state(st_ref.at[u], sfin_ref, u)


def _mixer_bwd(x, e_tab, mod, qkv, g, lab, s, of, gnw, wout, n2, rw_t, s0, *, nseq, nb, add_pos,
               mod_base, mod_per_seq):
    seq_len = nb * TOKEN_BLOCK
    group = min(SEQ_GROUP, nseq)
    sub = min(SEQ_GROUP // group, nb)
    nb //= sub
    tok = lambda w: pl.BlockSpec((group, sub * TOKEN_BLOCK, w), lambda s_, i: (s_, nb - 1 - i, 0))
    full = lambda a: pl.BlockSpec(a.shape, lambda s_, i: (0,) * a.ndim)
    st_spec = pl.BlockSpec((group, GLA_HEADS, GLA_DK, GLA_DV), lambda s_, i: (s_, 0, 0, 0))
    s0_spec = st_spec if s0.shape[0] == nseq else pl.BlockSpec(s0.shape, lambda s_, i: (0, 0, 0, 0))
    kern = functools.partial(_mixer_bwd_kernel, group, add_pos, mod_base, mod_per_seq, nb)
    return pl.pallas_call(
        kern,
        grid=(nseq // group, nb),
        in_specs=[tok(D_MODEL), full(e_tab), full(mod), tok(1024), tok(GLA_WIDTH), tok(QK_W),
                  tok(SGU_WIDTH), tok(GLA_WIDTH), full(gnw), full(wout), full(n2), full(rw_t),
                  s0_spec],
        out_specs=[tok(D_MODEL),
                   pl.BlockSpec((group, sub * TOKEN_BLOCK * TILES_PER_TOKEN, LANES),
                                lambda s_, i: (s_, nb - 1 - i, 0)),
                   pl.BlockSpec((group, sub, N_EXPERTS, TOKEN_BLOCK),
                                lambda s_, i: (s_, nb - 1 - i, 0, 0)),
                   st_spec],
        out_shape=[jax.ShapeDtypeStruct((nseq, seq_len, D_MODEL), F32),
                   jax.ShapeDtypeStruct((nseq, seq_len * TILES_PER_TOKEN, LANES), F32),
                   jax.ShapeDtypeStruct((nseq, nb * sub, N_EXPERTS, TOKEN_BLOCK), F32),
                   jax.ShapeDtypeStruct((nseq, GLA_HEADS, GLA_DK, GLA_DV), F32)],
        scratch_shapes=[pltpu.VMEM((group, 2, 2 * GLA_DV, 2 * GLA_DK), F32)],
        compiler_params=_compiler_params(("arbitrary", "arbitrary")),
        name="mixer_bwd",
    )(x, e_tab, mod, qkv, g, lab, s, of, gnw, wout, n2, rw_t, s0)


def _route_kernel(n_tok, cap, probs_ref, row_ref, gate_ref, xs_ref, ps_ref):
    n_blk = n_tok // TOKEN_BLOCK
    n_chunk = n_tok // LANES
    probs = jnp.concatenate([probs_ref[b] for b in range(n_blk)], axis=1)
    capf = jnp.float32(cap)

    def count(mask):
        return jnp.sum(mask.astype(F32), axis=1, keepdims=True)

    def as_f32(bits):
        return lax.bitcast_convert_type(bits, F32)

    def thr_step(_, lohi):
        lo, hi = lohi
        mid = lo + ((hi - lo + 1) >> 1)
        ok = count(probs >= as_f32(mid)) >= capf
        return jnp.where(ok, mid, lo), jnp.where(ok, hi, mid - 1)

    lo0 = jnp.zeros((N_EXPERTS, 1), I32)
    hi0 = jnp.full((N_EXPERTS, 1), 0x3F800000, I32)
    thr, _ = lax.fori_loop(0, 31, thr_step, (lo0, hi0))
    gt = probs >= as_f32(thr + 1)
    eq = (probs >= as_f32(thr)) & jnp.logical_not(gt)
    need = capf - count(gt)
    tok = lax.broadcasted_iota(I32, (N_EXPERTS, n_tok), 1)

    def tie_step(_, lohi):
        lo, hi = lohi
        mid = (lo + hi) >> 1
        ok = count(eq & (tok <= mid)) >= need
        return jnp.where(ok, lo, mid + 1), jnp.where(ok, mid, hi)

    n_bits = max(1, (n_tok - 1).bit_length())
    cut, _ = lax.fori_loop(0, n_bits, tie_step,
                           (jnp.zeros((N_EXPERTS, 1), I32), jnp.full((N_EXPERTS, 1), n_tok - 1, I32)))
    sel = (gt | (eq & (tok <= cut))).astype(F32)

    xs_ref[...] = jnp.concatenate([sel[:, c * LANES:(c + 1) * LANES] for c in range(n_chunk)], axis=0)
    ps_ref[...] = jnp.concatenate([probs[:, c * LANES:(c + 1) * LANES] for c in range(n_chunk)], axis=0)

    li = lax.broadcasted_iota(I32, (LANES, LANES), 0)
    lj = lax.broadcasted_iota(I32, (LANES, LANES), 1)
    upper = (li <= lj).astype(F32)
    ci = lax.broadcasted_iota(I32, (n_chunk, n_chunk), 0)
    cj = lax.broadcasted_iota(I32, (n_chunk, n_chunk), 1)
    lower = (cj <= ci).astype(F32)
    slot = lax.broadcasted_iota(I32, (1, cap), 1).astype(F32)
    chunk_id = lax.broadcasted_iota(I32, (n_chunk, cap), 0).astype(F32)
    lane_id = lax.broadcasted_iota(I32, (LANES, cap), 0).astype(F32)
    reps = cap // LANES

    def per_expert(e):
        x = xs_ref[pl.ds(e, n_chunk, stride=N_EXPERTS), :]
        pe = ps_ref[pl.ds(e, n_chunk, stride=N_EXPERTS), :]
        ploc = _dot(x, upper)
        tot = jnp.broadcast_to(ploc[:, LANES - 1:LANES], (n_chunk, LANES))
        cum = _dot(lower, tot)
        yield
        cum_w = jnp.concatenate([cum] * reps, axis=1)
        base_w = jnp.concatenate([cum - tot] * reps, axis=1)
        chunk_of = jnp.sum((cum_w <= slot).astype(F32), axis=0, keepdims=True)
        onehot = chunk_id == chunk_of
        local = slot - jnp.sum(jnp.where(onehot, base_w, 0.0), axis=0, keepdims=True)
        yield
        lhs = jnp.concatenate([ploc.astype(BF16)] + _split_bf16(pe, 3), axis=1)
        got = _dot_tn(lhs, onehot.astype(F32))
        yield
        pref = got[0:LANES]
        lane_of = jnp.sum((pref <= local).astype(F32), axis=0, keepdims=True)
        token = chunk_of * LANES + lane_of
        row_ref[pl.ds(e, 1), :] = (token * TILES_PER_TOKEN).astype(I32)
        yield
        pg = (got[3 * LANES:4 * LANES] + got[2 * LANES:3 * LANES]) + got[LANES:2 * LANES]
        gate_ref[pl.ds(e, 1), :] = jnp.sum(jnp.where(lane_id == lane_of, pg, 0.0), axis=0, keepdims=True)

    def expert_pair(i, _):
        _interleave(per_expert(2 * i + u) for u in range(2))
        return 0

    lax.fori_loop(0, N_EXPERTS // 2, expert_pair, 0)


def _route(probs, n_tok, cap):
    return pl.pallas_call(
        functools.partial(_route_kernel, n_tok, cap),
        out_shape=[jax.ShapeDtypeStruct((N_EXPERTS, cap), I32),
                   jax.ShapeDtypeStruct((N_EXPERTS, cap), F32)],
        scratch_shapes=[pltpu.VMEM((n_tok // LANES * N_EXPERTS, LANES), F32),
                        pltpu.VMEM((n_tok // LANES * N_EXPERTS, LANES), F32)],
        compiler_params=pltpu.CompilerParams(vmem_limit_bytes=VMEM_LIMIT),
        name="route_topk",
    )(probs)


def _expert_kernel(cap, row_ref, h2t_ref, w1_ref, w3_ref, w2_ref, ye_ref,
                   xe_ref, x2_ref, hid_ref, sem):
    e = pl.program_id(0)
    f = pl.program_id(1)
    slot = e % 2
    rows_per_step = cap // EXPERT_STEPS

    def start_row(expert, buf, j):
        src = pl.multiple_of(row_ref[expert * cap + j], TILES_PER_TOKEN)
        dst = pl.multiple_of(j * TILES_PER_TOKEN, TILES_PER_TOKEN)
        pltpu.make_async_copy(h2t_ref.at[pl.ds(src, TILES_PER_TOKEN), :],
                              xe_ref.at[buf, pl.ds(dst, TILES_PER_TOKEN), :], sem.at[buf]).start()

    def wait_rows(buf):
        pltpu.make_async_copy(h2t_ref.at[pl.ds(0, cap * TILES_PER_TOKEN), :], xe_ref.at[buf],
                              sem.at[buf]).wait()

    def prefetch_next():
        nxt = jnp.minimum(e + 1, N_EXPERTS - 1)
        first = f * rows_per_step
        for j in range(rows_per_step):
            start_row(nxt, 1 - slot, first + j)

    @pl.when((e == 0) & (f == 0))
    def _():
        def issue(j, _):
            start_row(0, 0, j)
            return 0

        lax.fori_loop(0, cap, issue, 0, unroll=8)

    @pl.when(f == 0)
    def _():
        wait_rows(slot)
        for s in range(TILES_PER_TOKEN):
            x2_ref[:, s * LANES:(s + 1) * LANES] = (
                xe_ref[slot, pl.ds(s, cap, stride=TILES_PER_TOKEN), :].astype(BF16))

    @pl.when(f < N_HID_STEPS)
    def _():
        prefetch_next()
        x2 = x2_ref[...]
        a = jnp.dot(x2, w1_ref[0].astype(BF16), preferred_element_type=F32)
        b = jnp.dot(x2, w3_ref[0].astype(BF16), preferred_element_type=F32)
        hid_ref[f] = (jax.nn.silu(a) * b).astype(BF16)

    @pl.when(f >= N_HID_STEPS)
    def _():
        prefetch_next()
        w2 = w2_ref[0].astype(BF16)
        out = jnp.dot(hid_ref[0], w2[0:EXPERT_F_BLOCK], preferred_element_type=F32)
        for kb in range(1, N_HID_STEPS):
            out += jnp.dot(hid_ref[kb], w2[kb * EXPERT_F_BLOCK:(kb + 1) * EXPERT_F_BLOCK],
                           preferred_element_type=F32)
        tile0 = (f - N_HID_STEPS) * (EXPERT_N_BLOCK // LANES)
        for i in range(EXPERT_N_BLOCK // LANES):
            ye_ref[pl.ds(tile0 + i, cap, stride=TILES_PER_TOKEN), :] = out[:, i * LANES:(i + 1) * LANES]

    @pl.when((e == N_EXPERTS - 1) & (f == EXPERT_STEPS - 1))
    def _():
        wait_rows(1 - slot)


def _experts(rows, h2t, w1, w3, w2, cap):
    hid_blk = lambda e, f, idx: (e, 0, jnp.minimum(f, N_HID_STEPS - 1))
    n_out = D_MODEL // EXPERT_N_BLOCK

    def out_blk(e, f, idx):
        hold = (f == 0) & (e > 0)
        return (jnp.where(hold, e - 1, e), 0,
                jnp.where(hold, n_out - 1, jnp.maximum(f - N_HID_STEPS, 0)))

    grid_spec = pltpu.PrefetchScalarGridSpec(
        num_scalar_prefetch=1,
        grid=(N_EXPERTS, EXPERT_STEPS),
        in_specs=[pl.BlockSpec(memory_space=pl.ANY),
                  pl.BlockSpec((1, D_MODEL, EXPERT_F_BLOCK), hid_blk),
                  pl.BlockSpec((1, D_MODEL, EXPERT_F_BLOCK), hid_blk),
                  pl.BlockSpec((1, D_EXPERT, EXPERT_N_BLOCK), out_blk)],
        out_specs=pl.BlockSpec((cap * TILES_PER_TOKEN, LANES), lambda e, f, idx: (e, 0)),
        scratch_shapes=[pltpu.VMEM((2, cap * TILES_PER_TOKEN, LANES), F32),
                        pltpu.VMEM((cap, D_MODEL), BF16),
                        pltpu.VMEM((N_HID_STEPS, cap, EXPERT_F_BLOCK), BF16),
                        pltpu.SemaphoreType.DMA((2,))],
    )
    return pl.pallas_call(
        functools.partial(_expert_kernel, cap),
        grid_spec=grid_spec,
        out_shape=jax.ShapeDtypeStruct((N_EXPERTS * cap * TILES_PER_TOKEN, LANES), F32),
        compiler_params=_compiler_params(("arbitrary", "arbitrary")),
        name="expert_swiglu",
    )(rows, h2t, w1, w3, w2)


COMBINE_BATCH = 8
ZERO_ROWS = 512


def _combine_kernel(cap, n_tok, mod_base, mod_per_seq, nb,
                    row_ref, gate_ref, ye_ref, x1_ref, mod_ref, fw_ref, y_ref,
                    acc_ref, xbuf_ref, ybuf_ref, sem_x, sem_y):
    e = pl.program_id(0)

    @pl.when(e == 0)
    def _():
        def zero(i, _):
            r = pl.multiple_of(i * ZERO_ROWS, ZERO_ROWS)
            acc_ref[pl.ds(r, ZERO_ROWS), :] = jnp.zeros((ZERO_ROWS, LANES), F32)
            return 0

        lax.fori_loop(0, n_tok * TILES_PER_TOKEN // ZERO_ROWS, zero, 0)

    def batch(jb, _):
        vals = []
        for u in range(COMBINE_BATCH):
            j = jb * COMBINE_BATCH + u
            t = pl.multiple_of(row_ref[e * cap + j], TILES_PER_TOKEN)
            gate = gate_ref[e * cap + j]
            src = pl.multiple_of(j * TILES_PER_TOKEN, TILES_PER_TOKEN)
            vals.append((t, acc_ref[pl.ds(t, TILES_PER_TOKEN), :]
                         + ye_ref[pl.ds(src, TILES_PER_TOKEN), :] * gate))
        for t, val in vals:
            acc_ref[pl.ds(t, TILES_PER_TOKEN), :] = val
        return 0

    lax.fori_loop(0, cap // COMBINE_BATCH, batch, 0)

    @pl.when(e == N_EXPERTS - 1)
    def _():
        n_blk = n_tok // TOKEN_BLOCK

        def x1_copy(b, slot):
            rows = pl.ds(pl.multiple_of(b * TOKEN_BLOCK, TOKEN_BLOCK), TOKEN_BLOCK)
            return pltpu.make_async_copy(x1_ref.at[rows, :], xbuf_ref.at[slot], sem_x.at[slot])

        def y_copy(b, slot):
            rows = pl.ds(pl.multiple_of(b * TOKEN_BLOCK, TOKEN_BLOCK), TOKEN_BLOCK)
            return pltpu.make_async_copy(ybuf_ref.at[slot], y_ref.at[rows, :], sem_y.at[slot])

        x1_copy(0, 0).start()

        def block(b, _):
            slot = b % 2

            @pl.when(b + 1 < n_blk)
            def _():
                x1_copy(b + 1, 1 - slot).start()

            x1_copy(b, slot).wait()

            @pl.when(b >= 2)
            def _():
                y_copy(b - 2, slot).wait()

            base = pl.multiple_of(b * (TOKEN_BLOCK * TILES_PER_TOKEN), TOKEN_BLOCK * TILES_PER_TOKEN)
            moe = jnp.concatenate(
                [acc_ref[pl.ds(base + s, TOKEN_BLOCK, stride=TILES_PER_TOKEN), :]
                 for s in range(TILES_PER_TOKEN)], axis=1)
            row = mod_base + ((b // nb) if mod_per_seq else 0)
            ybuf_ref[slot] = _rms(xbuf_ref[slot] + _mod_row(mod_ref, row, 5) * moe, fw_ref[...])
            y_copy(b, slot).start()
            return 0

        lax.fori_loop(0, n_blk, block, 0)
        y_copy(n_blk - 2, n_blk % 2).wait()
        y_copy(n_blk - 1, (n_blk - 1) % 2).wait()


def _combine_final(rows, gates, ye, x1, mod, fw, cap, n_tok, *, nb, mod_base, mod_per_seq):
    full = lambda a: pl.BlockSpec(a.shape, lambda e, rows, gate: (0,) * a.ndim)
    grid_spec = pltpu.PrefetchScalarGridSpec(
        num_scalar_prefetch=2,
        grid=(N_EXPERTS,),
        in_specs=[pl.BlockSpec((cap * TILES_PER_TOKEN, LANES), lambda e, rows, gate: (e, 0)),
                  pl.BlockSpec(memory_space=pl.ANY), full(mod), full(fw)],
        out_specs=pl.BlockSpec(memory_space=pl.ANY),
        scratch_shapes=[pltpu.VMEM((n_tok * TILES_PER_TOKEN, LANES), F32),
                        pltpu.VMEM((2, TOKEN_BLOCK, D_MODEL), F32),
                        pltpu.VMEM((2, TOKEN_BLOCK, D_MODEL), F32),
                        pltpu.SemaphoreType.DMA((2,)),
                        pltpu.SemaphoreType.DMA((2,))],
    )
    return pl.pallas_call(
        functools.partial(_combine_kernel, cap, n_tok, mod_base, mod_per_seq, nb),
        grid_spec=grid_spec,
        out_shape=jax.ShapeDtypeStruct((n_tok, D_MODEL), F32),
        compiler_params=_compiler_params(("arbitrary",)),
        name="moe_combine_norm",
    )(rows, gates, ye, x1, mod, fw)


def _trunk_and_norm(x, e_tab, mod, s0_f, s0_b, prm, *, nseq, seq_len, add_pos, mod_base, mod_per_seq):
    nb = seq_len // TOKEN_BLOCK
    n_tok = nseq * seq_len
    cap = EC_CAPACITY_FACTOR * n_tok // N_EXPERTS
    kw = dict(nseq=nseq, nb=nb, add_pos=add_pos, mod_base=mod_base, mod_per_seq=mod_per_seq)
    qkv, g, lab, s, of, sfin_f = _mixer_fwd(
        x, e_tab, mod, prm["n1"], prm["win"], prm["wa"], prm["ba"], prm["snw"], prm["sws"],
        prm["sbs"], s0_f, **kw)
    x1, h2t, probs, sfin_b = _mixer_bwd(
        x, e_tab, mod, qkv, g, lab, s, of, prm["gnw"], prm["wout"], prm["n2"], prm["rw_t"],
        s0_b, **kw)
    x1 = x1.reshape(n_tok, D_MODEL)
    h2t = h2t.reshape(n_tok * TILES_PER_TOKEN, LANES)
    probs = probs.reshape(n_tok // TOKEN_BLOCK, N_EXPERTS, TOKEN_BLOCK)
    rows, gates = _route(probs, n_tok, cap)
    rows, gates = rows.reshape(-1), gates.reshape(-1)
    ye = _experts(rows, h2t, prm["w1"], prm["w3"], prm["w2"], cap)
    y = _combine_final(rows, gates, ye, x1, mod, prm["fw"], cap, n_tok,
                       nb=nb, mod_base=mod_base, mod_per_seq=mod_per_seq)
    return y.reshape(nseq, seq_len, D_MODEL), sfin_f, sfin_b


def kernel(x_prompt, x_sample, state_gla_fwd, state_gla_bwd, c, c_ctx, ada_w, ada_b, norm1_w, w_in, gla_wa2_f, gla_ba_f, gla_wa2_b, gla_ba_b, gla_norm_w, sgu_norm_w, sgu_ws, sgu_bs, w_out, norm2_w, router_w, exp_w1, exp_w3, exp_w2, final_norm_w):
    assert ada_w.shape[0] == 1, "single trunk layer"
    batch, seq, _ = x_prompt.shape
    dec_batch, dec_seq, _ = x_sample.shape

    off_af = 2 * QK_W + 2 * GLA_WIDTH
    off_u = off_af + 2 * GLA_LOWRANK
    win = _win_layout(w_in, off_af, off_u)
    wa = jnp.zeros((P_WIDTH - P_A, 2 * QK_W), F32)
    wa = wa.at[0:GLA_LOWRANK, 0:QK_W].set(gla_wa2_f[0])
    wa = wa.at[GLA_LOWRANK:2 * GLA_LOWRANK, QK_W:].set(gla_wa2_b[0]).astype(BF16)
    prm = dict(
        n1=norm1_w, win=win, wa=wa,
        ba=jnp.concatenate([gla_ba_f[0], gla_ba_b[0]])[None, :],
        snw=sgu_norm_w, sws=sgu_ws[0].astype(BF16),
        sbs=jnp.broadcast_to(sgu_bs[0][:, :, None], (SGU_GROUPS, SGU_CHUNK, SGU_CH)),
        gnw=gla_norm_w, wout=w_out[0].astype(BF16), n2=norm2_w, rw_t=router_w[0].T,
        w1=exp_w1[0], w3=exp_w3[0], w2=exp_w2[0], fw=final_norm_w[None, :])

    cvec = jnp.concatenate([c_ctx[None, :], c, jnp.zeros((SUBLANES - 1 - dec_batch, D_MODEL), F32)])
    mod = _modulation(cvec, ada_w[0], ada_b)
    e_tab = _pos_table()

    zero_state = jnp.zeros((min(SEQ_GROUP, batch), GLA_HEADS, GLA_DK, GLA_DV), F32)
    y_prompt, sf, sb = _trunk_and_norm(
        x_prompt, e_tab, mod, zero_state, zero_state, prm,
        nseq=batch, seq_len=seq, add_pos=False, mod_base=0, mod_per_seq=False)
    y_sample, _, _ = _trunk_and_norm(
        x_sample, e_tab, mod, state_gla_fwd[:, 0], state_gla_bwd[:, 0], prm,
        nseq=dec_batch, seq_len=dec_seq, add_pos=True, mod_base=1, mod_per_seq=True)
    return (y_prompt, y_sample, sf[:, None], sb[:, None])
```

```python
import functools
import math

import jax
import jax.numpy as jnp
from jax import lax
from jax.experimental import pallas as pl
from jax.experimental.pallas import tpu as pltpu

F32 = jnp.float32
BF16 = jnp.bfloat16
I32 = jnp.int32

D_MODEL = 1024
GRID_W = 64
GLA_HEADS = 4
GLA_DK = 64
GLA_DV = 128
GLA_WIDTH = GLA_HEADS * GLA_DV
QK_W = GLA_HEADS * GLA_DK
GLA_LOWRANK = 16
GLA_GATE_NORM = 16.0
GLA_CHUNK = 64
SGU_WIDTH = 512
SGU_GROUPS = 4
SGU_CH = 128
SGU_CHUNK = 128
N_EXPERTS = 16
EC_CAPACITY_FACTOR = 2
D_EXPERT = 2048
EPS = 1e-6

SUBLANES = 8
LANES = 128
TILES_PER_TOKEN = D_MODEL // LANES

TOKEN_BLOCK = 256
SEQ_GROUP = 4
P_Q, P_K, P_V, P_G, P_U, P_SV, P_A = 0, 256, 512, 1024, 1536, 2048, 2560
P_WIDTH = 2688
EXPERT_F_BLOCK = 512
EXPERT_N_BLOCK = 256
N_HID_STEPS = D_EXPERT // EXPERT_F_BLOCK
EXPERT_STEPS = (N_HID_STEPS + 1) + D_MODEL // EXPERT_N_BLOCK
VMEM_LIMIT = 56 * 1024 * 1024


def _dot(a, b):
    return jnp.dot(a.astype(BF16), b.astype(BF16), preferred_element_type=F32)


def _dot_nt(a, b):
    return lax.dot_general(a.astype(BF16), b.astype(BF16), (((1,), (1,)), ((), ())),
                           preferred_element_type=F32)


def _dot_tn(a, b):
    return lax.dot_general(a.astype(BF16), b.astype(BF16), (((0,), (0,)), ((), ())),
                           preferred_element_type=F32)


def _dot_f32(a, b, dims=(((1,), (0,)), ((), ()))):
    return lax.dot_general(a, b, dims, precision=lax.Precision.HIGHEST, preferred_element_type=F32)


def _split_bf16(x, terms):
    parts = []
    for _ in range(terms - 1):
        part = x.astype(BF16)
        parts.append(part)
        x = x - part.astype(F32)
    parts.append(x.astype(BF16))
    return parts


def _select_dot(sel, x):
    s = sel.astype(BF16)
    hi, mid, lo = _split_bf16(x, 3)
    return (jnp.dot(s, lo, preferred_element_type=F32) + jnp.dot(s, mid, preferred_element_type=F32)
            + jnp.dot(s, hi, preferred_element_type=F32))


def _dot_nt_3pass(a, b):
    a_hi, a_lo = _split_bf16(a, 2)
    b_hi, b_lo = _split_bf16(b, 2)
    nt = lambda x, y: lax.dot_general(x, y, (((1,), (1,)), ((), ())), preferred_element_type=F32)
    return (nt(a_hi, b_lo) + nt(a_lo, b_hi)) + nt(a_hi, b_hi)


def _rms(x, w):
    return x * lax.rsqrt(jnp.mean(x * x, axis=-1, keepdims=True) + EPS) * w


def _compiler_params(sem):
    return pltpu.CompilerParams(dimension_semantics=sem, vmem_limit_bytes=VMEM_LIMIT)


def _mod_kernel(c_ref, w_ref, b_ref, o_ref):
    o_ref[...] = _dot(jax.nn.silu(c_ref[...]), w_ref[...]) + b_ref[...]


def _modulation(cvec, ada_w, ada_b):
    n = ada_w.shape[1]
    bn = 1536
    return pl.pallas_call(
        _mod_kernel,
        grid=(n // bn,),
        in_specs=[pl.BlockSpec((SUBLANES, D_MODEL), lambda j: (0, 0)),
                  pl.BlockSpec((D_MODEL, bn), lambda j: (0, j)),
                  pl.BlockSpec((1, bn), lambda j: (0, j))],
        out_specs=pl.BlockSpec((SUBLANES, bn), lambda j: (0, j)),
        out_shape=jax.ShapeDtypeStruct((SUBLANES, n), F32),
        compiler_params=_compiler_params(("arbitrary",)),
        name="adaln_mod",
    )(cvec, ada_w, ada_b)


def _win_kernel(off_a, off_u, w_ref, o_ref):
    w = w_ref[0]
    n_in = w.shape[1]
    o_ref[:, 0:off_a] = w[:, 0:off_a].astype(BF16)
    o_ref[:, off_a:off_a + n_in - off_u] = w[:, off_u:n_in].astype(BF16)
    tail = jnp.concatenate(
        [w[:, off_a:off_u], jnp.zeros((w.shape[0], P_WIDTH - n_in), F32)], axis=1)
    o_ref[:, P_A:P_WIDTH] = tail.astype(BF16)


def _win_layout(w_in, off_a, off_u):
    rows = 256
    kdim, n_in = w_in.shape[1], w_in.shape[2]
    return pl.pallas_call(
        functools.partial(_win_kernel, off_a, off_u),
        grid=(kdim // rows,),
        in_specs=[pl.BlockSpec((1, rows, n_in), lambda i: (0, i, 0))],
        out_specs=pl.BlockSpec((rows, P_WIDTH), lambda i: (i, 0)),
        out_shape=jax.ShapeDtypeStruct((kdim, P_WIDTH), BF16),
        compiler_params=_compiler_params(("arbitrary",)),
        name="win_layout",
    )(w_in)


def _pos_kernel(o_ref):
    nf = D_MODEL // 4
    p = lax.broadcasted_iota(I32, (GRID_W, nf), 0).astype(F32)
    i = lax.broadcasted_iota(I32, (GRID_W, nf), 1).astype(F32)
    omega = jnp.exp(i * (-math.log(10000.0) / nf))
    a = p * omega
    o_ref[:, 0:nf] = jnp.sin(a)
    o_ref[:, nf:2 * nf] = jnp.cos(a)


def _pos_table():
    return pl.pallas_call(
        _pos_kernel,
        out_shape=jax.ShapeDtypeStruct((GRID_W, D_MODEL // 2), F32),
        name="sincos_table",
    )()


def _add_pos(x, e_ref, blk, add_pos):
    if not add_pos:
        return x
    half = D_MODEL // 2
    e_all = e_ref[...]
    rows = []
    for j in range(TOKEN_BLOCK // GRID_W):
        xj = x[j * GRID_W:(j + 1) * GRID_W]
        e_row = e_ref[pl.ds(blk * (TOKEN_BLOCK // GRID_W) + j, 1), :]
        rows.append(jnp.concatenate([xj[:, 0:half] + e_row, xj[:, half:] + e_all], axis=1))
    return jnp.concatenate(rows, axis=0)


def _chunk_masks():
    r = lax.broadcasted_iota(I32, (TOKEN_BLOCK, TOKEN_BLOCK), 0)
    c = lax.broadcasted_iota(I32, (TOKEN_BLOCK, TOKEN_BLOCK), 1)
    same = (r // GLA_CHUNK) == (c // GLA_CHUNK)
    return same & (c <= r), same & (c >= r)


def _gla_direction(q, k, v, cum, fwd, att_mask, st_ref, ready, done):
    qe = q * jnp.exp(cum)
    ke = k * jnp.exp(-cum)
    yield
    lane = lax.broadcasted_iota(I32, (1, LANES), 1)
    o_intra = []
    for pair in range(2):
        qp = qe[:, pair * LANES:(pair + 1) * LANES]
        kp = ke[:, pair * LANES:(pair + 1) * LANES]
        for hh in range(2):
            qm = jnp.where((lane // GLA_DK) == hh, qp, 0.0)
            att = jnp.where(att_mask, _dot_nt(qm, kp), 0.0)
            head = 2 * pair + hh
            o_intra.append(_dot(att, v[:, head * GLA_DV:(head + 1) * GLA_DV]))
            yield
    o_intra = jnp.concatenate(o_intra, axis=1)
    while not ready():
        yield

    er = lax.broadcasted_iota(I32, (2 * GLA_DV, 2 * GLA_DK), 0)
    dc = lax.broadcasted_iota(I32, (2 * GLA_DV, 2 * GLA_DK), 1)
    same_head = (er // GLA_DV) == (dc // GLA_DK)
    n_chunks = TOKEN_BLOCK // GLA_CHUNK
    o_inter = [None] * n_chunks
    for c in (range(n_chunks) if fwd else reversed(range(n_chunks))):
        r0 = c * GLA_CHUNK
        rows = slice(r0, r0 + GLA_CHUNK)
        last = cum[r0 + GLA_CHUNK - 1:r0 + GLA_CHUNK] if fwd else cum[r0:r0 + 1]
        kd = k[rows] * jnp.exp(last - cum[rows])
        dec = jnp.exp(last)
        parts = []
        for pair in range(2):
            dl = slice(pair * LANES, (pair + 1) * LANES)
            st = st_ref[pair]
            parts.append(_dot_nt(qe[rows, dl], st))
            ds_t = _dot_tn(v[rows, pair * 2 * GLA_DV:(pair + 1) * 2 * GLA_DV], kd[:, dl])
            st_ref[pair] = dec[:, dl] * st + jnp.where(same_head, ds_t, 0.0)
        o_inter[c] = jnp.concatenate(parts, axis=1)
        yield
    done()
    return o_intra + jnp.concatenate(o_inter, axis=0)


def _interleave(chains):
    chains = list(chains)
    done = [False] * len(chains)
    tick = 0
    while not all(done):
        for i, ch in enumerate(chains):
            if tick >= i and not done[i]:
                try:
                    next(ch)
                except StopIteration:
                    done[i] = True
        tick += 1


def _load_state(s0_ref, u, st_ref):
    zero = jnp.zeros((GLA_DV, GLA_DK), F32)
    for pair in range(2):
        a = s0_ref[u, 2 * pair].T
        b = s0_ref[u, 2 * pair + 1].T
        st_ref[pair] = jnp.concatenate(
            [jnp.concatenate([a, zero], axis=1), jnp.concatenate([zero, b], axis=1)], axis=0)


def _store_state(st_ref, sfin_ref, u):
    for pair in range(2):
        st = st_ref[pair]
        sfin_ref[u, 2 * pair] = st[0:GLA_DV, 0:GLA_DK].T
        sfin_ref[u, 2 * pair + 1] = st[GLA_DV:2 * GLA_DV, GLA_DK:2 * GLA_DK].T


def _mod_row(mod_ref, row, part):
    return mod_ref[pl.ds(row, 1), part * D_MODEL:(part + 1) * D_MODEL]


def _mixer_fwd_kernel(group, add_pos, mod_base, mod_per_seq, nb,
                      x_ref, e_ref, mod_ref, n1_ref, win_ref, wa_ref, ba_ref,
                      snw_ref, sws_ref, sbs_ref, s0_ref,
                      qkv_ref, g_ref, lab_ref, s_ref, of_ref, sfin_ref,
                      st_ref):
    grp = pl.program_id(0)
    blk = pl.program_id(1)
    sub = x_ref.shape[1] // TOKEN_BLOCK
    lo_mask, _ = _chunk_masks()
    state_done = set()

    @pl.when(blk == 0)
    def _():
        for u in range(group):
            _load_state(s0_ref, u, st_ref.at[u])

    def chain(u, b):
        row = mod_base + ((grp * group + u) if mod_per_seq else 0)
        st_u = st_ref.at[u]
        rows = slice(b * TOKEN_BLOCK, (b + 1) * TOKEN_BLOCK)
        xin = _add_pos(x_ref[u, rows, :], e_ref, blk * sub + b, add_pos)
        h = _rms(xin, n1_ref[...]) * (1.0 + _mod_row(mod_ref, row, 1)) + _mod_row(mod_ref, row, 0)
        yield
        hb = h.astype(BF16)
        p_parts = []
        for c0, c1 in ((P_Q, P_G), (P_G, P_SV), (P_SV, P_WIDTH)):
            p_parts.append(jnp.dot(hb, win_ref[:, c0:c1], preferred_element_type=F32))
            yield
        p = jnp.concatenate(p_parts, axis=1)
        q = p[:, P_Q:P_K] * (GLA_DK ** -0.5)
        k = p[:, P_K:P_V]
        v = p[:, P_V:P_G]
        z = _dot(p[:, P_A:P_WIDTH], wa_ref[...]) + ba_ref[...]
        la = (jnp.minimum(z, 0.0) - jnp.log1p(jnp.exp(-jnp.abs(z)))) * (1.0 / GLA_GATE_NORM)
        qkv_ref[u, rows, :] = jnp.concatenate([q, k, v], axis=1)
        g_ref[u, rows, :] = p[:, P_G:P_U]
        lab_ref[u, rows, :] = la[:, QK_W:2 * QK_W]
        yield

        ug = jax.nn.gelu(p[:, P_U:P_SV])
        yield
        vg = jax.nn.gelu(p[:, P_SV:P_A])
        yield
        s_cols = []
        for gi in range(SGU_GROUPS):
            cols = slice(gi * SGU_CH, (gi + 1) * SGU_CH)
            vn = _rms(vg[:, cols], snw_ref[:, cols])
            rhs = jnp.concatenate([vn[0:SGU_CHUNK], vn[SGU_CHUNK:2 * SGU_CHUNK]], axis=1)
            vm = _dot(sws_ref[gi], rhs) + jnp.concatenate([sbs_ref[gi], sbs_ref[gi]], axis=1)
            vm = jnp.concatenate([vm[:, 0:SGU_CH], vm[:, SGU_CH:2 * SGU_CH]], axis=0)
            s_cols.append(ug[:, cols] * vm)
        s_ref[u, rows, :] = jnp.concatenate(s_cols, axis=1)
        yield

        cum = _select_dot(lo_mask, la[:, 0:QK_W])
        yield
        of_ref[u, rows, :] = yield from _gla_direction(
            q, k, v, cum, True, lo_mask, st_u,
            ready=lambda: b == 0 or (u, b - 1) in state_done, done=lambda: state_done.add((u, b)))

    _interleave(chain(u, b) for b in range(sub) for u in range(group))

    @pl.when(blk == nb - 1)
    def _():
        for u in range(group):
            _store_state(st_ref.at[u], sfin_ref, u)


def _mixer_fwd(x, e_tab, mod, n1, win, wa, ba, snw, sws, sbs, s0, *, nseq, nb, add_pos,
               mod_base, mod_per_seq):
    seq_len = nb * TOKEN_BLOCK
    group = min(SEQ_GROUP, nseq)
    sub = min(SEQ_GROUP // group, nb)
    nb //= sub
    tok = lambda w: pl.BlockSpec((group, sub * TOKEN_BLOCK, w), lambda s, i: (s, i, 0))
    full = lambda a: pl.BlockSpec(a.shape, lambda s, i: (0,) * a.ndim)
    st_spec = pl.BlockSpec((group, GLA_HEADS, GLA_DK, GLA_DV), lambda s, i: (s, 0, 0, 0))
    s0_spec = st_spec if s0.shape[0] == nseq else pl.BlockSpec(s0.shape, lambda s, i: (0, 0, 0, 0))
    act = lambda w: jax.ShapeDtypeStruct((nseq, seq_len, w), F32)
    kern = functools.partial(_mixer_fwd_kernel, group, add_pos, mod_base, mod_per_seq, nb)
    return pl.pallas_call(
        kern,
        grid=(nseq // group, nb),
        in_specs=[tok(D_MODEL), full(e_tab), full(mod), full(n1), full(win), full(wa), full(ba),
                  full(snw), full(sws), full(sbs), s0_spec],
        out_specs=[tok(1024), tok(GLA_WIDTH), tok(QK_W), tok(SGU_WIDTH), tok(GLA_WIDTH), st_spec],
        out_shape=[act(1024), act(GLA_WIDTH), act(QK_W), act(SGU_WIDTH), act(GLA_WIDTH),
                   jax.ShapeDtypeStruct((nseq, GLA_HEADS, GLA_DK, GLA_DV), F32)],
        scratch_shapes=[pltpu.VMEM((group, 2, 2 * GLA_DV, 2 * GLA_DK), F32)],
        compiler_params=_compiler_params(("arbitrary", "arbitrary")),
        name="mixer_fwd",
    )(x, e_tab, mod, n1, win, wa, ba, snw, sws, sbs, s0)


def _mixer_bwd_kernel(group, add_pos, mod_base, mod_per_seq, nb,
                      x_ref, e_ref, mod_ref, qkv_ref, g_ref, lab_ref, s_ref, of_ref,
                      gnw_ref, wout_ref, n2_ref, rw_ref, s0_ref,
                      x1_ref, h2t_ref, probs_ref, sfin_ref,
                      st_ref):
    grp = pl.program_id(0)
    step = pl.program_id(1)
    sub = x_ref.shape[1] // TOKEN_BLOCK
    blk = nb - 1 - step
    _, hi_mask = _chunk_masks()
    state_done = set()

    @pl.when(step == 0)
    def _():
        for u in range(group):
            _load_state(s0_ref, u, st_ref.at[u])

    def chain(u, b):
        row = mod_base + ((grp * group + u) if mod_per_seq else 0)
        st_u = st_ref.at[u]
        rows = slice(b * TOKEN_BLOCK, (b + 1) * TOKEN_BLOCK)
        qkv = qkv_ref[u, rows, :]
        q, k, v = qkv[:, 0:QK_W], qkv[:, QK_W:2 * QK_W], qkv[:, 2 * QK_W:]
        cum = _select_dot(hi_mask, lab_ref[u, rows, :])
        yield
        o_b = yield from _gla_direction(
            q, k, v, cum, False, hi_mask, st_u,
            ready=lambda: b == sub - 1 or (u, b + 1) in state_done,
            done=lambda: state_done.add((u, b)))
        o = of_ref[u, rows, :] + o_b
        g = g_ref[u, rows, :]
        cols = []
        for head in range(GLA_HEADS):
            hs = slice(head * GLA_DV, (head + 1) * GLA_DV)
            cols.append(_rms(o[:, hs], gnw_ref[...]) * jax.nn.silu(g[:, hs]))
        cols.append(s_ref[u, rows, :])
        yield
        y = _dot(jnp.concatenate(cols, axis=1), wout_ref[...])
        yield

        xin = _add_pos(x_ref[u, rows, :], e_ref, blk * sub + b, add_pos)
        x1 = xin + _mod_row(mod_ref, row, 2) * y
        x1_ref[u, rows, :] = x1
        h2 = _rms(x1, n2_ref[...]) * (1.0 + _mod_row(mod_ref, row, 4)) + _mod_row(mod_ref, row, 3)
        for s in range(TILES_PER_TOKEN):
            h2t_ref[u, pl.ds(b * TOKEN_BLOCK * TILES_PER_TOKEN + s, TOKEN_BLOCK, stride=TILES_PER_TOKEN), :] = (
                h2[:, s * LANES:(s + 1) * LANES])
        yield

        logits = _dot_nt_3pass(rw_ref[...], h2)
        m = jnp.max(logits, axis=0, keepdims=True)
        ex = jnp.exp(logits - m)
        probs_ref[u, b] = ex / jnp.sum(ex, axis=0, keepdims=True)

    _interleave(chain(u, b) for b in reversed(range(sub)) for u in range(group))

    @pl.when(step == nb - 1)
    def _():
        for u in range(group):
            _store_state(st_ref.at[u], sfin_ref, u)


def _mixer_bwd(x, e_tab, mod, qkv, g, lab, s, of, gnw, wout, n2, rw_t, s0, *, nseq, nb, add_pos,
               mod_base, mod_per_seq):
    seq_len = nb * TOKEN_BLOCK
    group = min(SEQ_GROUP, nseq)
    sub = min(SEQ_GROUP // group, nb)
    nb //= sub
    tok = lambda w: pl.BlockSpec((group, sub * TOKEN_BLOCK, w), lambda s_, i: (s_, nb - 1 - i, 0))
    full = lambda a: pl.BlockSpec(a.shape, lambda s_, i: (0,) * a.ndim)
    st_spec = pl.BlockSpec((group, GLA_HEADS, GLA_DK, GLA_DV), lambda s_, i: (s_, 0, 0, 0))
    s0_spec = st_spec if s0.shape[0] == nseq else pl.BlockSpec(s0.shape, lambda s_, i: (0, 0, 0, 0))
    kern = functools.partial(_mixer_bwd_kernel, group, add_pos, mod_base, mod_per_seq, nb)
    return pl.pallas_call(
        kern,
        grid=(nseq // group, nb),
        in_specs=[tok(D_MODEL), full(e_tab), full(mod), tok(1024), tok(GLA_WIDTH), tok(QK_W),
                  tok(SGU_WIDTH), tok(GLA_WIDTH), full(gnw), full(wout), full(n2), full(rw_t),
                  s0_spec],
        out_specs=[tok(D_MODEL),
                   pl.BlockSpec((group, sub * TOKEN_BLOCK * TILES_PER_TOKEN, LANES),
                                lambda s_, i: (s_, nb - 1 - i, 0)),
                   pl.BlockSpec((group, sub, N_EXPERTS, TOKEN_BLOCK),
                                lambda s_, i: (s_, nb - 1 - i, 0, 0)),
                   st_spec],
        out_shape=[jax.ShapeDtypeStruct((nseq, seq_len, D_MODEL), F32),
                   jax.ShapeDtypeStruct((nseq, seq_len * TILES_PER_TOKEN, LANES), F32),
                   jax.ShapeDtypeStruct((nseq, nb * sub, N_EXPERTS, TOKEN_BLOCK), F32),
                   jax.ShapeDtypeStruct((nseq, GLA_HEADS, GLA_DK, GLA_DV), F32)],
        scratch_shapes=[pltpu.VMEM((group, 2, 2 * GLA_DV, 2 * GLA_DK), F32)],
        compiler_params=_compiler_params(("arbitrary", "arbitrary")),
        name="mixer_bwd",
    )(x, e_tab, mod, qkv, g, lab, s, of, gnw, wout, n2, rw_t, s0)


def _route_kernel(n_tok, cap, probs_ref, row_ref, gate_ref, xs_ref, ps_ref):
    n_blk = n_tok // TOKEN_BLOCK
    n_chunk = n_tok // LANES
    probs = jnp.concatenate([probs_ref[b] for b in range(n_blk)], axis=1)
    capf = jnp.float32(cap)

    def count(mask):
        return jnp.sum(mask.astype(F32), axis=1, keepdims=True)

    def as_f32(bits):
        return lax.bitcast_convert_type(bits, F32)

    def thr_step(_, lohi):
        lo, hi = lohi
        mid = lo + ((hi - lo + 1) >> 1)
        ok = count(probs >= as_f32(mid)) >= capf
        return jnp.where(ok, mid, lo), jnp.where(ok, hi, mid - 1)

    lo0 = jnp.zeros((N_EXPERTS, 1), I32)
    hi0 = jnp.full((N_EXPERTS, 1), 0x3F800000, I32)
    thr, _ = lax.fori_loop(0, 31, thr_step, (lo0, hi0))
    gt = probs >= as_f32(thr + 1)
    eq = (probs >= as_f32(thr)) & jnp.logical_not(gt)
    need = capf - count(gt)
    tok = lax.broadcasted_iota(I32, (N_EXPERTS, n_tok), 1)

    def tie_step(_, lohi):
        lo, hi = lohi
        mid = (lo + hi) >> 1
        ok = count(eq & (tok <= mid)) >= need
        return jnp.where(ok, lo, mid + 1), jnp.where(ok, mid, hi)

    n_bits = max(1, (n_tok - 1).bit_length())
    cut, _ = lax.fori_loop(0, n_bits, tie_step,
                           (jnp.zeros((N_EXPERTS, 1), I32), jnp.full((N_EXPERTS, 1), n_tok - 1, I32)))
    sel = (gt | (eq & (tok <= cut))).astype(F32)

    xs_ref[...] = jnp.concatenate([sel[:, c * LANES:(c + 1) * LANES] for c in range(n_chunk)], axis=0)
    ps_ref[...] = jnp.concatenate([probs[:, c * LANES:(c + 1) * LANES] for c in range(n_chunk)], axis=0)

    li = lax.broadcasted_iota(I32, (LANES, LANES), 0)
    lj = lax.broadcasted_iota(I32, (LANES, LANES), 1)
    upper = (li <= lj).astype(F32)
    ci = lax.broadcasted_iota(I32, (n_chunk, n_chunk), 0)
    cj = lax.broadcasted_iota(I32, (n_chunk, n_chunk), 1)
    lower = (cj <= ci).astype(F32)
    slot = lax.broadcasted_iota(I32, (1, cap), 1).astype(F32)
    chunk_id = lax.broadcasted_iota(I32, (n_chunk, cap), 0).astype(F32)
    lane_id = lax.broadcasted_iota(I32, (LANES, cap), 0).astype(F32)
    reps = cap // LANES

    def per_expert(e):
        x = xs_ref[pl.ds(e, n_chunk, stride=N_EXPERTS), :]
        pe = ps_ref[pl.ds(e, n_chunk, stride=N_EXPERTS), :]
        ploc = _dot(x, upper)
        tot = jnp.broadcast_to(ploc[:, LANES - 1:LANES], (n_chunk, LANES))
        cum = _dot(lower, tot)
        yield
        cum_w = jnp.concatenate([cum] * reps, axis=1)
        base_w = jnp.concatenate([cum - tot] * reps, axis=1)
        chunk_of = jnp.sum((cum_w <= slot).astype(F32), axis=0, keepdims=True)
        onehot = chunk_id == chunk_of
        local = slot - jnp.sum(jnp.where(onehot, base_w, 0.0), axis=0, keepdims=True)
        yield
        lhs = jnp.concatenate([ploc.astype(BF16)] + _split_bf16(pe, 3), axis=1)
        got = _dot_tn(lhs, onehot.astype(F32))
        yield
        pref = got[0:LANES]
        lane_of = jnp.sum((pref <= local).astype(F32), axis=0, keepdims=True)
        token = chunk_of * LANES + lane_of
        row_ref[pl.ds(e, 1), :] = (token * TILES_PER_TOKEN).astype(I32)
        yield
        pg = (got[3 * LANES:4 * LANES] + got[2 * LANES:3 * LANES]) + got[LANES:2 * LANES]
        gate_ref[pl.ds(e, 1), :] = jnp.sum(jnp.where(lane_id == lane_of, pg, 0.0), axis=0, keepdims=True)

    def expert_pair(i, _):
        _interleave(per_expert(2 * i + u) for u in range(2))
        return 0

    lax.fori_loop(0, N_EXPERTS // 2, expert_pair, 0)


def _route(probs, n_tok, cap):
    return pl.pallas_call(
        functools.partial(_route_kernel, n_tok, cap),
        out_shape=[jax.ShapeDtypeStruct((N_EXPERTS, cap), I32),
                   jax.ShapeDtypeStruct((N_EXPERTS, cap), F32)],
        scratch_shapes=[pltpu.VMEM((n_tok // LANES * N_EXPERTS, LANES), F32),
                        pltpu.VMEM((n_tok // LANES * N_EXPERTS, LANES), F32)],
        compiler_params=pltpu.CompilerParams(vmem_limit_bytes=VMEM_LIMIT),
        name="route_topk",
    )(probs)


def _expert_kernel(cap, row_ref, h2t_ref, w1_ref, w3_ref, w2_ref, ye_ref,
                   xe_ref, x2_ref, a_ref, hid_ref, sem):
    e = pl.program_id(0)
    f = pl.program_id(1)
    rows_per_step = cap // (EXPERT_STEPS - 1)

    def start_row(expert, j):
        src = pl.multiple_of(row_ref[expert * cap + j], TILES_PER_TOKEN)
        dst = pl.multiple_of(j * TILES_PER_TOKEN, TILES_PER_TOKEN)
        pltpu.make_async_copy(h2t_ref.at[pl.ds(src, TILES_PER_TOKEN), :],
                              xe_ref.at[pl.ds(dst, TILES_PER_TOKEN), :], sem).start()

    def wait_rows():
        pltpu.make_async_copy(h2t_ref.at[pl.ds(0, cap * TILES_PER_TOKEN), :], xe_ref, sem).wait()

    def rows_to_x2():
        wait_rows()
        for s in range(TILES_PER_TOKEN):
            x2_ref[:, s * LANES:(s + 1) * LANES] = (
                xe_ref[pl.ds(s, cap, stride=TILES_PER_TOKEN), :].astype(BF16))

    def prefetch_next():
        nxt = jnp.minimum(e + 1, N_EXPERTS - 1)
        first = f * rows_per_step
        for j in range(rows_per_step):
            start_row(nxt, first + j)

    def out_block():
        w2 = w2_ref[0].astype(BF16)
        out = jnp.dot(hid_ref[0], w2[0:EXPERT_F_BLOCK], preferred_element_type=F32)
        for kb in range(1, N_HID_STEPS):
            out += jnp.dot(hid_ref[kb], w2[kb * EXPERT_F_BLOCK:(kb + 1) * EXPERT_F_BLOCK],
                           preferred_element_type=F32)
        tile0 = (f - (N_HID_STEPS + 1)) * (EXPERT_N_BLOCK // LANES)
        for i in range(EXPERT_N_BLOCK // LANES):
            ye_ref[pl.ds(tile0 + i, cap, stride=TILES_PER_TOKEN), :] = out[:, i * LANES:(i + 1) * LANES]

    @pl.when((e == 0) & (f == 0))
    def _():
        def issue(j, _):
            start_row(0, j)
            return 0

        lax.fori_loop(0, cap, issue, 0, unroll=8)
        rows_to_x2()

    @pl.when(f == 0)
    def _():
        prefetch_next()
        a_ref[...] = jnp.dot(x2_ref[...], w1_ref[0].astype(BF16), preferred_element_type=F32)

    @pl.when(f == 1)
    def _():
        prefetch_next()
        b = jnp.dot(x2_ref[...], w3_ref[0].astype(BF16), preferred_element_type=F32)
        hid_ref[0] = (jax.nn.silu(a_ref[...]) * b).astype(BF16)

    @pl.when((f > 1) & (f <= N_HID_STEPS))
    def _():
        prefetch_next()
        x2 = x2_ref[...]
        a = jnp.dot(x2, w1_ref[0].astype(BF16), preferred_element_type=F32)
        b = jnp.dot(x2, w3_ref[0].astype(BF16), preferred_element_type=F32)
        hid_ref[f - 1] = (jax.nn.silu(a) * b).astype(BF16)

    @pl.when((f > N_HID_STEPS) & (f < EXPERT_STEPS - 1))
    def _():
        prefetch_next()
        out_block()

    @pl.when(f == EXPERT_STEPS - 1)
    def _():
        out_block()
        rows_to_x2()


def _experts(rows, h2t, w1, w3, w2, cap):
    n_out = D_MODEL // EXPERT_N_BLOCK
    last_hid = N_HID_STEPS - 1

    def w1_blk(e, f, tbl):
        return (e, 0, jnp.clip(f - 1, 0, last_hid))

    def w3_blk(e, f, tbl):
        hold = (f == 0) & (e > 0)
        return (jnp.where(hold, e - 1, e), 0, jnp.where(hold, last_hid, jnp.clip(f - 1, 0, last_hid)))

    def w2_blk(e, f, tbl):
        hold = (f <= N_HID_STEPS) & (e > 0)
        return (jnp.where(hold, e - 1, e), 0,
                jnp.where(hold, n_out - 1, jnp.maximum(f - (N_HID_STEPS + 1), 0)))

    grid_spec = pltpu.PrefetchScalarGridSpec(
        num_scalar_prefetch=1,
        grid=(N_EXPERTS, EXPERT_STEPS),
        in_specs=[pl.BlockSpec(memory_space=pl.ANY),
                  pl.BlockSpec((1, D_MODEL, EXPERT_F_BLOCK), w1_blk),
                  pl.BlockSpec((1, D_MODEL, EXPERT_F_BLOCK), w3_blk),
                  pl.BlockSpec((1, D_EXPERT, EXPERT_N_BLOCK), w2_blk)],
        out_specs=pl.BlockSpec((cap * TILES_PER_TOKEN, LANES), lambda e, f, tbl: (e, 0)),
        scratch_shapes=[pltpu.VMEM((cap * TILES_PER_TOKEN, LANES), F32),
                        pltpu.VMEM((cap, D_MODEL), BF16),
                        pltpu.VMEM((cap, EXPERT_F_BLOCK), F32),
                        pltpu.VMEM((N_HID_STEPS, cap, EXPERT_F_BLOCK), BF16),
                        pltpu.SemaphoreType.DMA],
    )
    return pl.pallas_call(
        functools.partial(_expert_kernel, cap),
        grid_spec=grid_spec,
        out_shape=jax.ShapeDtypeStruct((N_EXPERTS * cap * TILES_PER_TOKEN, LANES), F32),
        compiler_params=_compiler_params(("arbitrary", "arbitrary")),
        name="expert_swiglu",
    )(rows, h2t, w1, w3, w2)


COMBINE_BATCH = 8
ZERO_ROWS = 512


def _combine_kernel(cap, n_tok, mod_base, mod_per_seq, nb,
                    row_ref, gate_ref, ye_ref, x1_ref, mod_ref, fw_ref, y_ref,
                    acc_ref, xbuf_ref, ybuf_ref, sem_x, sem_y):
    e = pl.program_id(0)

    @pl.when(e == 0)
    def _():
        def zero(i, _):
            r = pl.multiple_of(i * ZERO_ROWS, ZERO_ROWS)
            acc_ref[pl.ds(r, ZERO_ROWS), :] = jnp.zeros((ZERO_ROWS, LANES), F32)
            return 0

        lax.fori_loop(0, n_tok * TILES_PER_TOKEN // ZERO_ROWS, zero, 0)

    def batch(jb, _):
        vals = []
        for u in range(COMBINE_BATCH):
            j = jb * COMBINE_BATCH + u
            t = pl.multiple_of(row_ref[e * cap + j], TILES_PER_TOKEN)
            gate = gate_ref[e * cap + j]
            src = pl.multiple_of(j * TILES_PER_TOKEN, TILES_PER_TOKEN)
            vals.append((t, acc_ref[pl.ds(t, TILES_PER_TOKEN), :]
                         + ye_ref[pl.ds(src, TILES_PER_TOKEN), :] * gate))
        for t, val in vals:
            acc_ref[pl.ds(t, TILES_PER_TOKEN), :] = val
        return 0

    lax.fori_loop(0, cap // COMBINE_BATCH, batch, 0)

    @pl.when(e == N_EXPERTS - 1)
    def _():
        n_blk = n_tok // TOKEN_BLOCK

        def x1_copy(b, slot):
            rows = pl.ds(pl.multiple_of(b * TOKEN_BLOCK, TOKEN_BLOCK), TOKEN_BLOCK)
            return pltpu.make_async_copy(x1_ref.at[rows, :], xbuf_ref.at[slot], sem_x.at[slot])

        def y_copy(b, slot):
            rows = pl.ds(pl.multiple_of(b * TOKEN_BLOCK, TOKEN_BLOCK), TOKEN_BLOCK)
            return pltpu.make_async_copy(ybuf_ref.at[slot], y_ref.at[rows, :], sem_y.at[slot])

        x1_copy(0, 0).start()

        def block(b, _):
            slot = b % 2

            @pl.when(b + 1 < n_blk)
            def _():
                x1_copy(b + 1, 1 - slot).start()

            x1_copy(b, slot).wait()

            @pl.when(b >= 2)
            def _():
                y_copy(b - 2, slot).wait()

            base = pl.multiple_of(b * (TOKEN_BLOCK * TILES_PER_TOKEN), TOKEN_BLOCK * TILES_PER_TOKEN)
            moe = jnp.concatenate(
                [acc_ref[pl.ds(base + s, TOKEN_BLOCK, stride=TILES_PER_TOKEN), :]
                 for s in range(TILES_PER_TOKEN)], axis=1)
            row = mod_base + ((b // nb) if mod_per_seq else 0)
            ybuf_ref[slot] = _rms(xbuf_ref[slot] + _mod_row(mod_ref, row, 5) * moe, fw_ref[...])
            y_copy(b, slot).start()
            return 0

        lax.fori_loop(0, n_blk, block, 0)
        y_copy(n_blk - 2, n_blk % 2).wait()
        y_copy(n_blk - 1, (n_blk - 1) % 2).wait()


def _combine_final(rows, gates, ye, x1, mod, fw, cap, n_tok, *, nb, mod_base, mod_per_seq):
    full = lambda a: pl.BlockSpec(a.shape, lambda e, rows, gate: (0,) * a.ndim)
    grid_spec = pltpu.PrefetchScalarGridSpec(
        num_scalar_prefetch=2,
        grid=(N_EXPERTS,),
        in_specs=[pl.BlockSpec((cap * TILES_PER_TOKEN, LANES), lambda e, rows, gate: (e, 0)),
                  pl.BlockSpec(memory_space=pl.ANY), full(mod), full(fw)],
        out_specs=pl.BlockSpec(memory_space=pl.ANY),
        scratch_shapes=[pltpu.VMEM((n_tok * TILES_PER_TOKEN, LANES), F32),
                        pltpu.VMEM((2, TOKEN_BLOCK, D_MODEL), F32),
                        pltpu.VMEM((2, TOKEN_BLOCK, D_MODEL), F32),
                        pltpu.SemaphoreType.DMA((2,)),
                        pltpu.SemaphoreType.DMA((2,))],
    )
    return pl.pallas_call(
        functools.partial(_combine_kernel, cap, n_tok, mod_base, mod_per_seq, nb),
        grid_spec=grid_spec,
        out_shape=jax.ShapeDtypeStruct((n_tok, D_MODEL), F32),
        compiler_params=_compiler_params(("arbitrary",)),
        name="moe_combine_norm",
    )(rows, gates, ye, x1, mod, fw)


def _trunk_and_norm(x, e_tab, mod, s0_f, s0_b, prm, *, nseq, seq_len, add_pos, mod_base, mod_per_seq):
    nb = seq_len // TOKEN_BLOCK
    n_tok = nseq * seq_len
    cap = EC_CAPACITY_FACTOR * n_tok // N_EXPERTS
    kw = dict(nseq=nseq, nb=nb, add_pos=add_pos, mod_base=mod_base, mod_per_seq=mod_per_seq)
    qkv, g, lab, s, of, sfin_f = _mixer_fwd(
        x, e_tab, mod, prm["n1"], prm["win"], prm["wa"], prm["ba"], prm["snw"], prm["sws"],
        prm["sbs"], s0_f, **kw)
    x1, h2t, probs, sfin_b = _mixer_bwd(
        x, e_tab, mod, qkv, g, lab, s, of, prm["gnw"], prm["wout"], prm["n2"], prm["rw_t"],
        s0_b, **kw)
    x1 = x1.reshape(n_tok, D_MODEL)
    h2t = h2t.reshape(n_tok * TILES_PER_TOKEN, LANES)
    probs = probs.reshape(n_tok // TOKEN_BLOCK, N_EXPERTS, TOKEN_BLOCK)
    rows, gates = _route(probs, n_tok, cap)
    rows, gates = rows.reshape(-1), gates.reshape(-1)
    ye = _experts(rows, h2t, prm["w1"], prm["w3"], prm["w2"], cap)
    y = _combine_final(rows, gates, ye, x1, mod, prm["fw"], cap, n_tok,
                       nb=nb, mod_base=mod_base, mod_per_seq=mod_per_seq)
    return y.reshape(nseq, seq_len, D_MODEL), sfin_f, sfin_b


def kernel(x_prompt, x_sample, state_gla_fwd, state_gla_bwd, c, c_ctx, ada_w, ada_b, norm1_w, w_in, gla_wa2_f, gla_ba_f, gla_wa2_b, gla_ba_b, gla_norm_w, sgu_norm_w, sgu_ws, sgu_bs, w_out, norm2_w, router_w, exp_w1, exp_w3, exp_w2, final_norm_w):
    assert ada_w.shape[0] == 1, "single trunk layer"
    batch, seq, _ = x_prompt.shape
    dec_batch, dec_seq, _ = x_sample.shape

    off_af = 2 * QK_W + 2 * GLA_WIDTH
    off_u = off_af + 2 * GLA_LOWRANK
    win = _win_layout(w_in, off_af, off_u)
    wa = jnp.zeros((P_WIDTH - P_A, 2 * QK_W), F32)
    wa = wa.at[0:GLA_LOWRANK, 0:QK_W].set(gla_wa2_f[0])
    wa = wa.at[GLA_LOWRANK:2 * GLA_LOWRANK, QK_W:].set(gla_wa2_b[0]).astype(BF16)
    prm = dict(
        n1=norm1_w, win=win, wa=wa,
        ba=jnp.concatenate([gla_ba_f[0], gla_ba_b[0]])[None, :],
        snw=sgu_norm_w, sws=sgu_ws[0].astype(BF16),
        sbs=jnp.broadcast_to(sgu_bs[0][:, :, None], (SGU_GROUPS, SGU_CHUNK, SGU_CH)),
        gnw=gla_norm_w, wout=w_out[0].astype(BF16), n2=norm2_w, rw_t=router_w[0].T,
        w1=exp_w1[0], w3=exp_w3[0], w2=exp_w2[0], fw=final_norm_w[None, :])

    cvec = jnp.concatenate([c_ctx[None, :], c, jnp.zeros((SUBLANES - 1 - dec_batch, D_MODEL), F32)])
    mod = _modulation(cvec, ada_w[0], ada_b)
    e_tab = _pos_table()

    zero_state = jnp.zeros((min(SEQ_GROUP, batch), GLA_HEADS, GLA_DK, GLA_DV), F32)
    y_prompt, sf, sb = _trunk_and_norm(
        x_prompt, e_tab, mod, zero_state, zero_state, prm,
        nseq=batch, seq_len=seq, add_pos=False, mod_base=0, mod_per_seq=False)
    y_sample, _, _ = _trunk_and_norm(
        x_sample, e_tab, mod, state_gla_fwd[:, 0], state_gla_bwd[:, 0], prm,
        nseq=dec_batch, seq_len=dec_seq, add_pos=True, mod_base=1, mod_per_seq=True)
    return (y_prompt, y_sample, sf[:, None], sb[:, None])
```

```python
import functools
import math

import jax
import jax.numpy as jnp
from jax import lax
from jax.experimental import pallas as pl
from jax.experimental.pallas import tpu as pltpu

F32 = jnp.float32
BF16 = jnp.bfloat16
I32 = jnp.int32

D_MODEL = 1024
GRID_W = 64
GLA_HEADS = 4
GLA_DK = 64
GLA_DV = 128
GLA_WIDTH = GLA_HEADS * GLA_DV
QK_W = GLA_HEADS * GLA_DK
GLA_LOWRANK = 16
GLA_GATE_NORM = 16.0
GLA_CHUNK = 64
SGU_WIDTH = 512
SGU_GROUPS = 4
SGU_CH = 128
SGU_CHUNK = 128
N_EXPERTS = 16
EC_CAPACITY_FACTOR = 2
D_EXPERT = 2048
EPS = 1e-6

SUBLANES = 8
LANES = 128
TILES_PER_TOKEN = D_MODEL // LANES

TOKEN_BLOCK = 256
SEQ_GROUP = 4
P_Q, P_K, P_V, P_G, P_U, P_SV, P_A = 0, 256, 512, 1024, 1536, 2048, 2560
P_WIDTH = 2688
EXPERT_F_BLOCK = 512
EXPERT_N_BLOCK = 256
N_HID_STEPS = D_EXPERT // EXPERT_F_BLOCK
EXPERT_STEPS = N_HID_STEPS + D_MODEL // EXPERT_N_BLOCK
W_LOOKAHEAD = 2
W_RING = W_LOOKAHEAD + 1
VMEM_LIMIT = 56 * 1024 * 1024


def _dot(a, b):
    return jnp.dot(a.astype(BF16), b.astype(BF16), preferred_element_type=F32)


def _dot_nt(a, b):
    return lax.dot_general(a.astype(BF16), b.astype(BF16), (((1,), (1,)), ((), ())),
                           preferred_element_type=F32)


def _dot_tn(a, b):
    return lax.dot_general(a.astype(BF16), b.astype(BF16), (((0,), (0,)), ((), ())),
                           preferred_element_type=F32)


def _dot_f32(a, b, dims=(((1,), (0,)), ((), ()))):
    return lax.dot_general(a, b, dims, precision=lax.Precision.HIGHEST, preferred_element_type=F32)


def _split_bf16(x, terms):
    parts = []
    for _ in range(terms - 1):
        part = x.astype(BF16)
        parts.append(part)
        x = x - part.astype(F32)
    parts.append(x.astype(BF16))
    return parts


def _select_dot(sel, x):
    s = sel.astype(BF16)
    hi, mid, lo = _split_bf16(x, 3)
    return (jnp.dot(s, lo, preferred_element_type=F32) + jnp.dot(s, mid, preferred_element_type=F32)
            + jnp.dot(s, hi, preferred_element_type=F32))


def _dot_nt_3pass(a, b):
    a_hi, a_lo = _split_bf16(a, 2)
    b_hi, b_lo = _split_bf16(b, 2)
    nt = lambda x, y: lax.dot_general(x, y, (((1,), (1,)), ((), ())), preferred_element_type=F32)
    return (nt(a_hi, b_lo) + nt(a_lo, b_hi)) + nt(a_hi, b_hi)


def _rms(x, w):
    return x * lax.rsqrt(jnp.mean(x * x, axis=-1, keepdims=True) + EPS) * w


def _compiler_params(sem):
    return pltpu.CompilerParams(dimension_semantics=sem, vmem_limit_bytes=VMEM_LIMIT)


def _mod_kernel(c_ref, w_ref, b_ref, o_ref):
    o_ref[...] = _dot(jax.nn.silu(c_ref[...]), w_ref[...]) + b_ref[...]


def _modulation(cvec, ada_w, ada_b):
    n = ada_w.shape[1]
    bn = 1536
    return pl.pallas_call(
        _mod_kernel,
        grid=(n // bn,),
        in_specs=[pl.BlockSpec((SUBLANES, D_MODEL), lambda j: (0, 0)),
                  pl.BlockSpec((D_MODEL, bn), lambda j: (0, j)),
                  pl.BlockSpec((1, bn), lambda j: (0, j))],
        out_specs=pl.BlockSpec((SUBLANES, bn), lambda j: (0, j)),
        out_shape=jax.ShapeDtypeStruct((SUBLANES, n), F32),
        compiler_params=_compiler_params(("arbitrary",)),
        name="adaln_mod",
    )(cvec, ada_w, ada_b)


def _win_kernel(off_a, off_u, w_ref, o_ref):
    w = w_ref[0]
    n_in = w.shape[1]
    o_ref[:, 0:off_a] = w[:, 0:off_a].astype(BF16)
    o_ref[:, off_a:off_a + n_in - off_u] = w[:, off_u:n_in].astype(BF16)
    tail = jnp.concatenate(
        [w[:, off_a:off_u], jnp.zeros((w.shape[0], P_WIDTH - n_in), F32)], axis=1)
    o_ref[:, P_A:P_WIDTH] = tail.astype(BF16)


def _win_layout(w_in, off_a, off_u):
    rows = 256
    kdim, n_in = w_in.shape[1], w_in.shape[2]
    return pl.pallas_call(
        functools.partial(_win_kernel, off_a, off_u),
        grid=(kdim // rows,),
        in_specs=[pl.BlockSpec((1, rows, n_in), lambda i: (0, i, 0))],
        out_specs=pl.BlockSpec((rows, P_WIDTH), lambda i: (i, 0)),
        out_shape=jax.ShapeDtypeStruct((kdim, P_WIDTH), BF16),
        compiler_params=_compiler_params(("arbitrary",)),
        name="win_layout",
    )(w_in)


def _pos_kernel(o_ref):
    nf = D_MODEL // 4
    p = lax.broadcasted_iota(I32, (GRID_W, nf), 0).astype(F32)
    i = lax.broadcasted_iota(I32, (GRID_W, nf), 1).astype(F32)
    omega = jnp.exp(i * (-math.log(10000.0) / nf))
    a = p * omega
    o_ref[:, 0:nf] = jnp.sin(a)
    o_ref[:, nf:2 * nf] = jnp.cos(a)


def _pos_table():
    return pl.pallas_call(
        _pos_kernel,
        out_shape=jax.ShapeDtypeStruct((GRID_W, D_MODEL // 2), F32),
        name="sincos_table",
    )()


def _add_pos(x, e_ref, blk, add_pos):
    if not add_pos:
        return x
    half = D_MODEL // 2
    e_all = e_ref[...]
    rows = []
    for j in range(TOKEN_BLOCK // GRID_W):
        xj = x[j * GRID_W:(j + 1) * GRID_W]
        e_row = e_ref[pl.ds(blk * (TOKEN_BLOCK // GRID_W) + j, 1), :]
        rows.append(jnp.concatenate([xj[:, 0:half] + e_row, xj[:, half:] + e_all], axis=1))
    return jnp.concatenate(rows, axis=0)


def _chunk_masks():
    r = lax.broadcasted_iota(I32, (TOKEN_BLOCK, TOKEN_BLOCK), 0)
    c = lax.broadcasted_iota(I32, (TOKEN_BLOCK, TOKEN_BLOCK), 1)
    same = (r // GLA_CHUNK) == (c // GLA_CHUNK)
    return same & (c <= r), same & (c >= r)


def _gla_direction(q, k, v, cum, fwd, att_mask, st_ref, ready, done):
    qe = q * jnp.exp(cum)
    ke = k * jnp.exp(-cum)
    yield
    lane = lax.broadcasted_iota(I32, (1, LANES), 1)
    o_intra = []
    for pair in range(2):
        qp = qe[:, pair * LANES:(pair + 1) * LANES]
        kp = ke[:, pair * LANES:(pair + 1) * LANES]
        for hh in range(2):
            qm = jnp.where((lane // GLA_DK) == hh, qp, 0.0)
            att = jnp.where(att_mask, _dot_nt(qm, kp), 0.0)
            head = 2 * pair + hh
            o_intra.append(_dot(att, v[:, head * GLA_DV:(head + 1) * GLA_DV]))
            yield
    o_intra = jnp.concatenate(o_intra, axis=1)
    while not ready():
        yield

    er = lax.broadcasted_iota(I32, (2 * GLA_DV, 2 * GLA_DK), 0)
    dc = lax.broadcasted_iota(I32, (2 * GLA_DV, 2 * GLA_DK), 1)
    same_head = (er // GLA_DV) == (dc // GLA_DK)
    n_chunks = TOKEN_BLOCK // GLA_CHUNK
    o_inter = [None] * n_chunks
    for c in (range(n_chunks) if fwd else reversed(range(n_chunks))):
        r0 = c * GLA_CHUNK
        rows = slice(r0, r0 + GLA_CHUNK)
        last = cum[r0 + GLA_CHUNK - 1:r0 + GLA_CHUNK] if fwd else cum[r0:r0 + 1]
        kd = k[rows] * jnp.exp(last - cum[rows])
        dec = jnp.exp(last)
        parts = []
        for pair in range(2):
            dl = slice(pair * LANES, (pair + 1) * LANES)
            st = st_ref[pair]
            parts.append(_dot_nt(qe[rows, dl], st))
            ds_t = _dot_tn(v[rows, pair * 2 * GLA_DV:(pair + 1) * 2 * GLA_DV], kd[:, dl])
            st_ref[pair] = dec[:, dl] * st + jnp.where(same_head, ds_t, 0.0)
        o_inter[c] = jnp.concatenate(parts, axis=1)
        yield
    done()
    return o_intra + jnp.concatenate(o_inter, axis=0)


def _interleave(chains):
    chains = list(chains)
    done = [False] * len(chains)
    tick = 0
    while not all(done):
        for i, ch in enumerate(chains):
            if tick >= i and not done[i]:
                try:
                    next(ch)
                except StopIteration:
                    done[i] = True
        tick += 1


def _load_state(s0_ref, u, st_ref):
    zero = jnp.zeros((GLA_DV, GLA_DK), F32)
    for pair in range(2):
        a = s0_ref[u, 2 * pair].T
        b = s0_ref[u, 2 * pair + 1].T
        st_ref[pair] = jnp.concatenate(
            [jnp.concatenate([a, zero], axis=1), jnp.concatenate([zero, b], axis=1)], axis=0)


def _store_state(st_ref, sfin_ref, u):
    for pair in range(2):
        st = st_ref[pair]
        sfin_ref[u, 2 * pair] = st[0:GLA_DV, 0:GLA_DK].T
        sfin_ref[u, 2 * pair + 1] = st[GLA_DV:2 * GLA_DV, GLA_DK:2 * GLA_DK].T


def _mod_row(mod_ref, row, part):
    return mod_ref[pl.ds(row, 1), part * D_MODEL:(part + 1) * D_MODEL]


def _mixer_fwd_kernel(group, add_pos, mod_base, mod_per_seq, nb,
                      x_ref, e_ref, mod_ref, n1_ref, win_ref, wa_ref, ba_ref,
                      snw_ref, sws_ref, sbs_ref, s0_ref,
                      qkv_ref, g_ref, lab_ref, s_ref, of_ref, sfin_ref,
                      st_ref):
    grp = pl.program_id(0)
    blk = pl.program_id(1)
    sub = x_ref.shape[1] // TOKEN_BLOCK
    lo_mask, _ = _chunk_masks()
    state_done = set()

    @pl.when(blk == 0)
    def _():
        for u in range(group):
            _load_state(s0_ref, u, st_ref.at[u])

    def chain(u, b):
        row = mod_base + ((grp * group + u) if mod_per_seq else 0)
        st_u = st_ref.at[u]
        rows = slice(b * TOKEN_BLOCK, (b + 1) * TOKEN_BLOCK)
        xin = _add_pos(x_ref[u, rows, :], e_ref, blk * sub + b, add_pos)
        h = _rms(xin, n1_ref[...]) * (1.0 + _mod_row(mod_ref, row, 1)) + _mod_row(mod_ref, row, 0)
        yield
        hb = h.astype(BF16)
        p_parts = []
        for c0, c1 in ((P_Q, P_G), (P_G, P_SV), (P_SV, P_WIDTH)):
            p_parts.append(jnp.dot(hb, win_ref[:, c0:c1], preferred_element_type=F32))
            yield
        p = jnp.concatenate(p_parts, axis=1)
        q = p[:, P_Q:P_K] * (GLA_DK ** -0.5)
        k = p[:, P_K:P_V]
        v = p[:, P_V:P_G]
        z = _dot(p[:, P_A:P_WIDTH], wa_ref[...]) + ba_ref[...]
        la = (jnp.minimum(z, 0.0) - jnp.log1p(jnp.exp(-jnp.abs(z)))) * (1.0 / GLA_GATE_NORM)
        qkv_ref[u, rows, :] = jnp.concatenate([q, k, v], axis=1)
        g_ref[u, rows, :] = p[:, P_G:P_U]
        lab_ref[u, rows, :] = la[:, QK_W:2 * QK_W]
        yield

        ug = jax.nn.gelu(p[:, P_U:P_SV])
        yield
        vg = jax.nn.gelu(p[:, P_SV:P_A])
        yield
        s_cols = []
        for gi in range(SGU_GROUPS):
            cols = slice(gi * SGU_CH, (gi + 1) * SGU_CH)
            vn = _rms(vg[:, cols], snw_ref[:, cols])
            rhs = jnp.concatenate([vn[0:SGU_CHUNK], vn[SGU_CHUNK:2 * SGU_CHUNK]], axis=1)
            vm = _dot(sws_ref[gi], rhs) + jnp.concatenate([sbs_ref[gi], sbs_ref[gi]], axis=1)
            vm = jnp.concatenate([vm[:, 0:SGU_CH], vm[:, SGU_CH:2 * SGU_CH]], axis=0)
            s_cols.append(ug[:, cols] * vm)
        s_ref[u, rows, :] = jnp.concatenate(s_cols, axis=1)
        yield

        cum = _select_dot(lo_mask, la[:, 0:QK_W])
        yield
        of_ref[u, rows, :] = yield from _gla_direction(
            q, k, v, cum, True, lo_mask, st_u,
            ready=lambda: b == 0 or (u, b - 1) in state_done, done=lambda: state_done.add((u, b)))

    _interleave(chain(u, b) for b in range(sub) for u in range(group))

    @pl.when(blk == nb - 1)
    def _():
        for u in range(group):
            _store_state(st_ref.at[u], sfin_ref, u)


def _mixer_fwd(x, e_tab, mod, n1, win, wa, ba, snw, sws, sbs, s0, *, nseq, nb, add_pos,
               mod_base, mod_per_seq):
    seq_len = nb * TOKEN_BLOCK
    group = min(SEQ_GROUP, nseq)
    sub = min(SEQ_GROUP // group, nb)
    nb //= sub
    tok = lambda w: pl.BlockSpec((group, sub * TOKEN_BLOCK, w), lambda s, i: (s, i, 0))
    full = lambda a: pl.BlockSpec(a.shape, lambda s, i: (0,) * a.ndim)
    st_spec = pl.BlockSpec((group, GLA_HEADS, GLA_DK, GLA_DV), lambda s, i: (s, 0, 0, 0))
    s0_spec = st_spec if s0.shape[0] == nseq else pl.BlockSpec(s0.shape, lambda s, i: (0, 0, 0, 0))
    act = lambda w: jax.ShapeDtypeStruct((nseq, seq_len, w), F32)
    kern = functools.partial(_mixer_fwd_kernel, group, add_pos, mod_base, mod_per_seq, nb)
    return pl.pallas_call(
        kern,
        grid=(nseq // group, nb),
        in_specs=[tok(D_MODEL), full(e_tab), full(mod), full(n1), full(win), full(wa), full(ba),
                  full(snw), full(sws), full(sbs), s0_spec],
        out_specs=[tok(1024), tok(GLA_WIDTH), tok(QK_W), tok(SGU_WIDTH), tok(GLA_WIDTH), st_spec],
        out_shape=[act(1024), act(GLA_WIDTH), act(QK_W), act(SGU_WIDTH), act(GLA_WIDTH),
                   jax.ShapeDtypeStruct((nseq, GLA_HEADS, GLA_DK, GLA_DV), F32)],
        scratch_shapes=[pltpu.VMEM((group, 2, 2 * GLA_DV, 2 * GLA_DK), F32)],
        compiler_params=_compiler_params(("arbitrary", "arbitrary")),
        name="mixer_fwd",
    )(x, e_tab, mod, n1, win, wa, ba, snw, sws, sbs, s0)


def _mixer_bwd_kernel(group, add_pos, mod_base, mod_per_seq, nb,
                      x_ref, e_ref, mod_ref, qkv_ref, g_ref, lab_ref, s_ref, of_ref,
                      gnw_ref, wout_ref, n2_ref, rw_ref, s0_ref,
                      x1_ref, h2t_ref, probs_ref, sfin_ref,
                      st_ref):
    grp = pl.program_id(0)
    step = pl.program_id(1)
    sub = x_ref.shape[1] // TOKEN_BLOCK
    blk = nb - 1 - step
    _, hi_mask = _chunk_masks()
    state_done = set()

    @pl.when(step == 0)
    def _():
        for u in range(group):
            _load_state(s0_ref, u, st_ref.at[u])

    def chain(u, b):
        row = mod_base + ((grp * group + u) if mod_per_seq else 0)
        st_u = st_ref.at[u]
        rows = slice(b * TOKEN_BLOCK, (b + 1) * TOKEN_BLOCK)
        qkv = qkv_ref[u, rows, :]
        q, k, v = qkv[:, 0:QK_W], qkv[:, QK_W:2 * QK_W], qkv[:, 2 * QK_W:]
        cum = _select_dot(hi_mask, lab_ref[u, rows, :])
        yield
        o_b = yield from _gla_direction(
            q, k, v, cum, False, hi_mask, st_u,
            ready=lambda: b == sub - 1 or (u, b + 1) in state_done,
            done=lambda: state_done.add((u, b)))
        o = of_ref[u, rows, :] + o_b
        g = g_ref[u, rows, :]
        cols = []
        for head in range(GLA_HEADS):
            hs = slice(head * GLA_DV, (head + 1) * GLA_DV)
            cols.append(_rms(o[:, hs], gnw_ref[...]) * jax.nn.silu(g[:, hs]))
        cols.append(s_ref[u, rows, :])
        yield
        y = _dot(jnp.concatenate(cols, axis=1), wout_ref[...])
        yield

        xin = _add_pos(x_ref[u, rows, :], e_ref, blk * sub + b, add_pos)
        x1 = xin + _mod_row(mod_ref, row, 2) * y
        x1_ref[u, rows, :] = x1
        h2 = _rms(x1, n2_ref[...]) * (1.0 + _mod_row(mod_ref, row, 4)) + _mod_row(mod_ref, row, 3)
        for s in range(TILES_PER_TOKEN):
            h2t_ref[u, pl.ds(b * TOKEN_BLOCK * TILES_PER_TOKEN + s, TOKEN_BLOCK, stride=TILES_PER_TOKEN), :] = (
                h2[:, s * LANES:(s + 1) * LANES])
        yield

        logits = _dot_nt_3pass(rw_ref[...], h2)
        m = jnp.max(logits, axis=0, keepdims=True)
        ex = jnp.exp(logits - m)
        probs_ref[u, b] = ex / jnp.sum(ex, axis=0, keepdims=True)

    _interleave(chain(u, b) for b in reversed(range(sub)) for u in range(group))

    @pl.when(step == nb - 1)
    def _():
        for u in range(group):
            _store_state(st_ref.at[u], sfin_ref, u)


def _mixer_bwd(x, e_tab, mod, qkv, g, lab, s, of, gnw, wout, n2, rw_t, s0, *, nseq, nb, add_pos,
               mod_base, mod_per_seq):
    seq_len = nb * TOKEN_BLOCK
    group = min(SEQ_GROUP, nseq)
    sub = min(SEQ_GROUP // group, nb)
    nb //= sub
    tok = lambda w: pl.BlockSpec((group, sub * TOKEN_BLOCK, w), lambda s_, i: (s_, nb - 1 - i, 0))
    full = lambda a: pl.BlockSpec(a.shape, lambda s_, i: (0,) * a.ndim)
    st_spec = pl.BlockSpec((group, GLA_HEADS, GLA_DK, GLA_DV), lambda s_, i: (s_, 0, 0, 0))
    s0_spec = st_spec if s0.shape[0] == nseq else pl.BlockSpec(s0.shape, lambda s_, i: (0, 0, 0, 0))
    kern = functools.partial(_mixer_bwd_kernel, group, add_pos, mod_base, mod_per_seq, nb)
    return pl.pallas_call(
        kern,
        grid=(nseq // group, nb),
        in_specs=[tok(D_MODEL), full(e_tab), full(mod), tok(1024), tok(GLA_WIDTH), tok(QK_W),
                  tok(SGU_WIDTH), tok(GLA_WIDTH), full(gnw), full(wout), full(n2), full(rw_t),
                  s0_spec],
        out_specs=[tok(D_MODEL),
                   pl.BlockSpec((group, sub * TOKEN_BLOCK * TILES_PER_TOKEN, LANES),
                                lambda s_, i: (s_, nb - 1 - i, 0)),
                   pl.BlockSpec((group, sub, N_EXPERTS, TOKEN_BLOCK),
                                lambda s_, i: (s_, nb - 1 - i, 0, 0)),
                   st_spec],
        out_shape=[jax.ShapeDtypeStruct((nseq, seq_len, D_MODEL), F32),
                   jax.ShapeDtypeStruct((nseq, seq_len * TILES_PER_TOKEN, LANES), F32),
                   jax.ShapeDtypeStruct((nseq, nb * sub, N_EXPERTS, TOKEN_BLOCK), F32),
                   jax.ShapeDtypeStruct((nseq, GLA_HEADS, GLA_DK, GLA_DV), F32)],
        scratch_shapes=[pltpu.VMEM((group, 2, 2 * GLA_DV, 2 * GLA_DK), F32)],
        compiler_params=_compiler_params(("arbitrary", "arbitrary")),
        name="mixer_bwd",
    )(x, e_tab, mod, qkv, g, lab, s, of, gnw, wout, n2, rw_t, s0)


def _route_kernel(n_tok, cap, probs_ref, row_ref, gate_ref, xs_ref, ps_ref):
    n_blk = n_tok // TOKEN_BLOCK
    n_chunk = n_tok // LANES
    probs = jnp.concatenate([probs_ref[b] for b in range(n_blk)], axis=1)
    capf = jnp.float32(cap)

    def count(mask):
        return jnp.sum(mask.astype(F32), axis=1, keepdims=True)

    def as_f32(bits):
        return lax.bitcast_convert_type(bits, F32)

    def thr_step(_, lohi):
        lo, hi = lohi
        mid = lo + ((hi - lo + 1) >> 1)
        ok = count(probs >= as_f32(mid)) >= capf
        return jnp.where(ok, mid, lo), jnp.where(ok, hi, mid - 1)

    lo0 = jnp.zeros((N_EXPERTS, 1), I32)
    hi0 = jnp.full((N_EXPERTS, 1), 0x3F800000, I32)
    thr, _ = lax.fori_loop(0, 31, thr_step, (lo0, hi0))
    gt = probs >= as_f32(thr + 1)
    eq = (probs >= as_f32(thr)) & jnp.logical_not(gt)
    need = capf - count(gt)
    tok = lax.broadcasted_iota(I32, (N_EXPERTS, n_tok), 1)

    def tie_step(_, lohi):
        lo, hi = lohi
        mid = (lo + hi) >> 1
        ok = count(eq & (tok <= mid)) >= need
        return jnp.where(ok, lo, mid + 1), jnp.where(ok, mid, hi)

    n_bits = max(1, (n_tok - 1).bit_length())
    cut, _ = lax.fori_loop(0, n_bits, tie_step,
                           (jnp.zeros((N_EXPERTS, 1), I32), jnp.full((N_EXPERTS, 1), n_tok - 1, I32)))
    sel = (gt | (eq & (tok <= cut))).astype(F32)

    xs_ref[...] = jnp.concatenate([sel[:, c * LANES:(c + 1) * LANES] for c in range(n_chunk)], axis=0)
    ps_ref[...] = jnp.concatenate([probs[:, c * LANES:(c + 1) * LANES] for c in range(n_chunk)], axis=0)

    li = lax.broadcasted_iota(I32, (LANES, LANES), 0)
    lj = lax.broadcasted_iota(I32, (LANES, LANES), 1)
    upper = (li <= lj).astype(F32)
    ci = lax.broadcasted_iota(I32, (n_chunk, n_chunk), 0)
    cj = lax.broadcasted_iota(I32, (n_chunk, n_chunk), 1)
    lower = (cj <= ci).astype(F32)
    slot = lax.broadcasted_iota(I32, (1, cap), 1).astype(F32)
    chunk_id = lax.broadcasted_iota(I32, (n_chunk, cap), 0).astype(F32)
    lane_id = lax.broadcasted_iota(I32, (LANES, cap), 0).astype(F32)
    reps = cap // LANES

    def per_expert(e):
        x = xs_ref[pl.ds(e, n_chunk, stride=N_EXPERTS), :]
        pe = ps_ref[pl.ds(e, n_chunk, stride=N_EXPERTS), :]
        ploc = _dot(x, upper)
        tot = jnp.broadcast_to(ploc[:, LANES - 1:LANES], (n_chunk, LANES))
        cum = _dot(lower, tot)
        yield
        cum_w = jnp.concatenate([cum] * reps, axis=1)
        base_w = jnp.concatenate([cum - tot] * reps, axis=1)
        chunk_of = jnp.sum((cum_w <= slot).astype(F32), axis=0, keepdims=True)
        onehot = chunk_id == chunk_of
        local = slot - jnp.sum(jnp.where(onehot, base_w, 0.0), axis=0, keepdims=True)
        yield
        lhs = jnp.concatenate([ploc.astype(BF16)] + _split_bf16(pe, 3), axis=1)
        got = _dot_tn(lhs, onehot.astype(F32))
        yield
        pref = got[0:LANES]
        lane_of = jnp.sum((pref <= local).astype(F32), axis=0, keepdims=True)
        token = chunk_of * LANES + lane_of
        row_ref[pl.ds(e, 1), :] = (token * TILES_PER_TOKEN).astype(I32)
        yield
        pg = (got[3 * LANES:4 * LANES] + got[2 * LANES:3 * LANES]) + got[LANES:2 * LANES]
        gate_ref[pl.ds(e, 1), :] = jnp.sum(jnp.where(lane_id == lane_of, pg, 0.0), axis=0, keepdims=True)

    def expert_pair(i, _):
        _interleave(per_expert(2 * i + u) for u in range(2))
        return 0

    lax.fori_loop(0, N_EXPERTS // 2, expert_pair, 0)


def _route(probs, n_tok, cap):
    return pl.pallas_call(
        functools.partial(_route_kernel, n_tok, cap),
        out_shape=[jax.ShapeDtypeStruct((N_EXPERTS, cap), I32),
                   jax.ShapeDtypeStruct((N_EXPERTS, cap), F32)],
        scratch_shapes=[pltpu.VMEM((n_tok // LANES * N_EXPERTS, LANES), F32),
                        pltpu.VMEM((n_tok // LANES * N_EXPERTS, LANES), F32)],
        compiler_params=pltpu.CompilerParams(vmem_limit_bytes=VMEM_LIMIT),
        name="route_topk",
    )(probs)


def _expert_kernel(cap, row_ref, h2t_ref, w1_ref, w3_ref, w2_ref, ye_ref,
                   xe_ref, x2_ref, hid_ref, wring_ref, sem, sem_w):
    e = pl.program_id(0)
    f = pl.program_id(1)
    slot = e % 2
    rows_per_step = cap // EXPERT_STEPS
    hid_step = e * N_HID_STEPS + f

    def w_copies(step):
        ring = step % W_RING
        expert = step // N_HID_STEPS
        cols = pl.ds(pl.multiple_of((step % N_HID_STEPS) * EXPERT_F_BLOCK, EXPERT_F_BLOCK), EXPERT_F_BLOCK)
        return [pltpu.make_async_copy(w.at[expert, :, cols], wring_ref.at[ring, k], sem_w.at[ring, k])
                for k, w in enumerate((w1_ref, w3_ref))]

    def start_row(expert, buf, j):
        src = pl.multiple_of(row_ref[expert * cap + j], TILES_PER_TOKEN)
        dst = pl.multiple_of(j * TILES_PER_TOKEN, TILES_PER_TOKEN)
        pltpu.make_async_copy(h2t_ref.at[pl.ds(src, TILES_PER_TOKEN), :],
                              xe_ref.at[buf, pl.ds(dst, TILES_PER_TOKEN), :], sem.at[buf]).start()

    def wait_rows(buf):
        pltpu.make_async_copy(h2t_ref.at[pl.ds(0, cap * TILES_PER_TOKEN), :], xe_ref.at[buf],
                              sem.at[buf]).wait()

    def prefetch_next():
        nxt = jnp.minimum(e + 1, N_EXPERTS - 1)
        first = f * rows_per_step
        for j in range(rows_per_step):
            start_row(nxt, 1 - slot, first + j)

    @pl.when((e == 0) & (f == 0))
    def _():
        def issue(j, _):
            start_row(0, 0, j)
            return 0

        lax.fori_loop(0, cap, issue, 0, unroll=8)
        for step in range(W_LOOKAHEAD):
            for cp in w_copies(step):
                cp.start()

    @pl.when(f == 0)
    def _():
        wait_rows(slot)
        for s in range(TILES_PER_TOKEN):
            x2_ref[:, s * LANES:(s + 1) * LANES] = (
                xe_ref[slot, pl.ds(s, cap, stride=TILES_PER_TOKEN), :].astype(BF16))

    @pl.when((f < N_HID_STEPS) & (hid_step + W_LOOKAHEAD < N_EXPERTS * N_HID_STEPS))
    def _():
        for cp in w_copies(hid_step + W_LOOKAHEAD):
            cp.start()

    @pl.when(f < N_HID_STEPS)
    def _():
        for cp in w_copies(hid_step):
            cp.wait()
        prefetch_next()
        ring = hid_step % W_RING
        x2 = x2_ref[...]
        a = jnp.dot(x2, wring_ref[ring, 0].astype(BF16), preferred_element_type=F32)
        b = jnp.dot(x2, wring_ref[ring, 1].astype(BF16), preferred_element_type=F32)
        hid_ref[f] = (jax.nn.silu(a) * b).astype(BF16)

    @pl.when(f >= N_HID_STEPS)
    def _():
        prefetch_next()
        w2 = w2_ref[0].astype(BF16)
        out = jnp.dot(hid_ref[0], w2[0:EXPERT_F_BLOCK], preferred_element_type=F32)
        for kb in range(1, N_HID_STEPS):
            out += jnp.dot(hid_ref[kb], w2[kb * EXPERT_F_BLOCK:(kb + 1) * EXPERT_F_BLOCK],
                           preferred_element_type=F32)
        tile0 = (f - N_HID_STEPS) * (EXPERT_N_BLOCK // LANES)
        for i in range(EXPERT_N_BLOCK // LANES):
            ye_ref[pl.ds(tile0 + i, cap, stride=TILES_PER_TOKEN), :] = out[:, i * LANES:(i + 1) * LANES]

    @pl.when((e == N_EXPERTS - 1) & (f == EXPERT_STEPS - 1))
    def _():
        wait_rows(1 - slot)


def _experts(rows, h2t, w1, w3, w2, cap):
    n_out = D_MODEL // EXPERT_N_BLOCK

    def out_blk(e, f, idx):
        hold = (f == 0) & (e > 0)
        return (jnp.where(hold, e - 1, e), 0,
                jnp.where(hold, n_out - 1, jnp.maximum(f - N_HID_STEPS, 0)))

    grid_spec = pltpu.PrefetchScalarGridSpec(
        num_scalar_prefetch=1,
        grid=(N_EXPERTS, EXPERT_STEPS),
        in_specs=[pl.BlockSpec(memory_space=pl.ANY),
                  pl.BlockSpec(memory_space=pl.ANY),
                  pl.BlockSpec(memory_space=pl.ANY),
                  pl.BlockSpec((1, D_EXPERT, EXPERT_N_BLOCK), out_blk)],
        out_specs=pl.BlockSpec((cap * TILES_PER_TOKEN, LANES), lambda e, f, idx: (e, 0)),
        scratch_shapes=[pltpu.VMEM((2, cap * TILES_PER_TOKEN, LANES), F32),
                        pltpu.VMEM((cap, D_MODEL), BF16),
                        pltpu.VMEM((N_HID_STEPS, cap, EXPERT_F_BLOCK), BF16),
                        pltpu.VMEM((W_RING, 2, D_MODEL, EXPERT_F_BLOCK), F32),
                        pltpu.SemaphoreType.DMA((2,)),
                        pltpu.SemaphoreType.DMA((W_RING, 2))],
    )
    return pl.pallas_call(
        functools.partial(_expert_kernel, cap),
        grid_spec=grid_spec,
        out_shape=jax.ShapeDtypeStruct((N_EXPERTS * cap * TILES_PER_TOKEN, LANES), F32),
        compiler_params=_compiler_params(("arbitrary", "arbitrary")),
        name="expert_swiglu",
    )(rows, h2t, w1, w3, w2)


COMBINE_BATCH = 8
ZERO_ROWS = 512


def _combine_kernel(cap, n_tok, mod_base, mod_per_seq, nb,
                    row_ref, gate_ref, ye_ref, x1_ref, mod_ref, fw_ref, y_ref,
                    acc_ref, xbuf_ref, ybuf_ref, sem_x, sem_y):
    e = pl.program_id(0)

    @pl.when(e == 0)
    def _():
        def zero(i, _):
            r = pl.multiple_of(i * ZERO_ROWS, ZERO_ROWS)
            acc_ref[pl.ds(r, ZERO_ROWS), :] = jnp.zeros((ZERO_ROWS, LANES), F32)
            return 0

        lax.fori_loop(0, n_tok * TILES_PER_TOKEN // ZERO_ROWS, zero, 0)

    def batch(jb, _):
        vals = []
        for u in range(COMBINE_BATCH):
            j = jb * COMBINE_BATCH + u
            t = pl.multiple_of(row_ref[e * cap + j], TILES_PER_TOKEN)
            gate = gate_ref[e * cap + j]
            src = pl.multiple_of(j * TILES_PER_TOKEN, TILES_PER_TOKEN)
            vals.append((t, acc_ref[pl.ds(t, TILES_PER_TOKEN), :]
                         + ye_ref[pl.ds(src, TILES_PER_TOKEN), :] * gate))
        for t, val in vals:
            acc_ref[pl.ds(t, TILES_PER_TOKEN), :] = val
        return 0

    lax.fori_loop(0, cap // COMBINE_BATCH, batch, 0)

    @pl.when(e == N_EXPERTS - 1)
    def _():
        n_blk = n_tok // TOKEN_BLOCK

        def x1_copy(b, slot):
            rows = pl.ds(pl.multiple_of(b * TOKEN_BLOCK, TOKEN_BLOCK), TOKEN_BLOCK)
            return pltpu.make_async_copy(x1_ref.at[rows, :], xbuf_ref.at[slot], sem_x.at[slot])

        def y_copy(b, slot):
            rows = pl.ds(pl.multiple_of(b * TOKEN_BLOCK, TOKEN_BLOCK), TOKEN_BLOCK)
            return pltpu.make_async_copy(ybuf_ref.at[slot], y_ref.at[rows, :], sem_y.at[slot])

        x1_copy(0, 0).start()

        def block(b, _):
            slot = b % 2

            @pl.when(b + 1 < n_blk)
            def _():
                x1_copy(b + 1, 1 - slot).start()

            x1_copy(b, slot).wait()

            @pl.when(b >= 2)
            def _():
                y_copy(b - 2, slot).wait()

            base = pl.multiple_of(b * (TOKEN_BLOCK * TILES_PER_TOKEN), TOKEN_BLOCK * TILES_PER_TOKEN)
            moe = jnp.concatenate(
                [acc_ref[pl.ds(base + s, TOKEN_BLOCK, stride=TILES_PER_TOKEN), :]
                 for s in range(TILES_PER_TOKEN)], axis=1)
            row = mod_base + ((b // nb) if mod_per_seq else 0)
            ybuf_ref[slot] = _rms(xbuf_ref[slot] + _mod_row(mod_ref, row, 5) * moe, fw_ref[...])
            y_copy(b, slot).start()
            return 0

        lax.fori_loop(0, n_blk, block, 0)
        y_copy(n_blk - 2, n_blk % 2).wait()
        y_copy(n_blk - 1, (n_blk - 1) % 2).wait()


def _combine_final(rows, gates, ye, x1, mod, fw, cap, n_tok, *, nb, mod_base, mod_per_seq):
    full = lambda a: pl.BlockSpec(a.shape, lambda e, rows, gate: (0,) * a.ndim)
    grid_spec = pltpu.PrefetchScalarGridSpec(
        num_scalar_prefetch=2,
        grid=(N_EXPERTS,),
        in_specs=[pl.BlockSpec((cap * TILES_PER_TOKEN, LANES), lambda e, rows, gate: (e, 0)),
                  pl.BlockSpec(memory_space=pl.ANY), full(mod), full(fw)],
        out_specs=pl.BlockSpec(memory_space=pl.ANY),
        scratch_shapes=[pltpu.VMEM((n_tok * TILES_PER_TOKEN, LANES), F32),
                        pltpu.VMEM((2, TOKEN_BLOCK, D_MODEL), F32),
                        pltpu.VMEM((2, TOKEN_BLOCK, D_MODEL), F32),
                        pltpu.SemaphoreType.DMA((2,)),
                        pltpu.SemaphoreType.DMA((2,))],
    )
    return pl.pallas_call(
        functools.partial(_combine_kernel, cap, n_tok, mod_base, mod_per_seq, nb),
        grid_spec=grid_spec,
        out_shape=jax.ShapeDtypeStruct((n_tok, D_MODEL), F32),
        compiler_params=_compiler_params(("arbitrary",)),
        name="moe_combine_norm",
    )(rows, gates, ye, x1, mod, fw)


def _trunk_and_norm(x, e_tab, mod, s0_f, s0_b, prm, *, nseq, seq_len, add_pos, mod_base, mod_per_seq):
    nb = seq_len // TOKEN_BLOCK
    n_tok = nseq * seq_len
    cap = EC_CAPACITY_FACTOR * n_tok // N_EXPERTS
    kw = dict(nseq=nseq, nb=nb, add_pos=add_pos, mod_base=mod_base, mod_per_seq=mod_per_seq)
    qkv, g, lab, s, of, sfin_f = _mixer_fwd(
        x, e_tab, mod, prm["n1"], prm["win"], prm["wa"], prm["ba"], prm["snw"], prm["sws"],
        prm["sbs"], s0_f, **kw)
    x1, h2t, probs, sfin_b = _mixer_bwd(
        x, e_tab, mod, qkv, g, lab, s, of, prm["gnw"], prm["wout"], prm["n2"], prm["rw_t"],
        s0_b, **kw)
    x1 = x1.reshape(n_tok, D_MODEL)
    h2t = h2t.reshape(n_tok * TILES_PER_TOKEN, LANES)
    probs = probs.reshape(n_tok // TOKEN_BLOCK, N_EXPERTS, TOKEN_BLOCK)
    rows, gates = _route(probs, n_tok, cap)
    rows, gates = rows.reshape(-1), gates.reshape(-1)
    ye = _experts(rows, h2t, prm["w1"], prm["w3"], prm["w2"], cap)
    y = _combine_final(rows, gates, ye, x1, mod, prm["fw"], cap, n_tok,
                       nb=nb, mod_base=mod_base, mod_per_seq=mod_per_seq)
    return y.reshape(nseq, seq_len, D_MODEL), sfin_f, sfin_b


def kernel(x_prompt, x_sample, state_gla_fwd, state_gla_bwd, c, c_ctx, ada_w, ada_b, norm1_w, w_in, gla_wa2_f, gla_ba_f, gla_wa2_b, gla_ba_b, gla_norm_w, sgu_norm_w, sgu_ws, sgu_bs, w_out, norm2_w, router_w, exp_w1, exp_w3, exp_w2, final_norm_w):
    assert ada_w.shape[0] == 1, "single trunk layer"
    batch, seq, _ = x_prompt.shape
    dec_batch, dec_seq, _ = x_sample.shape

    off_af = 2 * QK_W + 2 * GLA_WIDTH
    off_u = off_af + 2 * GLA_LOWRANK
    win = _win_layout(w_in, off_af, off_u)
    wa = jnp.zeros((P_WIDTH - P_A, 2 * QK_W), F32)
    wa = wa.at[0:GLA_LOWRANK, 0:QK_W].set(gla_wa2_f[0])
    wa = wa.at[GLA_LOWRANK:2 * GLA_LOWRANK, QK_W:].set(gla_wa2_b[0]).astype(BF16)
    prm = dict(
        n1=norm1_w, win=win, wa=wa,
        ba=jnp.concatenate([gla_ba_f[0], gla_ba_b[0]])[None, :],
        snw=sgu_norm_w, sws=sgu_ws[0].astype(BF16),
        sbs=jnp.broadcast_to(sgu_bs[0][:, :, None], (SGU_GROUPS, SGU_CHUNK, SGU_CH)),
        gnw=gla_norm_w, wout=w_out[0].astype(BF16), n2=norm2_w, rw_t=router_w[0].T,
        w1=exp_w1[0], w3=exp_w3[0], w2=exp_w2[0], fw=final_norm_w[None, :])

    cvec = jnp.concatenate([c_ctx[None, :], c, jnp.zeros((SUBLANES - 1 - dec_batch, D_MODEL), F32)])
    mod = _modulation(cvec, ada_w[0], ada_b)
    e_tab = _pos_table()

    zero_state = jnp.zeros((min(SEQ_GROUP, batch), GLA_HEADS, GLA_DK, GLA_DV), F32)
    y_prompt, sf, sb = _trunk_and_norm(
        x_prompt, e_tab, mod, zero_state, zero_state, prm,
        nseq=batch, seq_len=seq, add_pos=False, mod_base=0, mod_per_seq=False)
    y_sample, _, _ = _trunk_and_norm(
        x_sample, e_tab, mod, state_gla_fwd[:, 0], state_gla_bwd[:, 0], prm,
        nseq=dec_batch, seq_len=dec_seq, add_pos=True, mod_base=1, mod_per_seq=True)
    return (y_prompt, y_sample, sf[:, None], sb[:, None])
```

```python
import functools
import math

import jax
import jax.numpy as jnp
from jax import lax
from jax.experimental import pallas as pl
from jax.experimental.pallas import tpu as pltpu

F32 = jnp.float32
BF16 = jnp.bfloat16
I32 = jnp.int32

D_MODEL = 1024
GRID_W = 64
GLA_HEADS = 4
GLA_DK = 64
GLA_DV = 128
GLA_WIDTH = GLA_HEADS * GLA_DV
QK_W = GLA_HEADS * GLA_DK
GLA_LOWRANK = 16
GLA_GATE_NORM = 16.0
GLA_CHUNK = 64
SGU_WIDTH = 512
SGU_GROUPS = 4
SGU_CH = 128
SGU_CHUNK = 128
N_EXPERTS = 16
EC_CAPACITY_FACTOR = 2
D_EXPERT = 2048
EPS = 1e-6

SUBLANES = 8
LANES = 128
TILES_PER_TOKEN = D_MODEL // LANES

TOKEN_BLOCK = 256
SEQ_GROUP = 4
P_Q, P_K, P_V, P_G, P_U, P_SV, P_A = 0, 256, 512, 1024, 1536, 2048, 2560
P_WIDTH = 2688
EXPERT_F_BLOCK = 512
EXPERT_N_BLOCK = 256
N_HID_STEPS = D_EXPERT // EXPERT_F_BLOCK
EXPERT_STEPS = N_HID_STEPS + D_MODEL // EXPERT_N_BLOCK
VMEM_LIMIT = 56 * 1024 * 1024


def _dot(a, b):
    return jnp.dot(a.astype(BF16), b.astype(BF16), preferred_element_type=F32)


def _dot_nt(a, b):
    return lax.dot_general(a.astype(BF16), b.astype(BF16), (((1,), (1,)), ((), ())),
                           preferred_element_type=F32)


def _dot_tn(a, b):
    return lax.dot_general(a.astype(BF16), b.astype(BF16), (((0,), (0,)), ((), ())),
                           preferred_element_type=F32)


def _dot_f32(a, b, dims=(((1,), (0,)), ((), ()))):
    return lax.dot_general(a, b, dims, precision=lax.Precision.HIGHEST, preferred_element_type=F32)


def _split_bf16(x, terms):
    parts = []
    for _ in range(terms - 1):
        part = x.astype(BF16)
        parts.append(part)
        x = x - part.astype(F32)
    parts.append(x.astype(BF16))
    return parts


def _select_dot(sel, x):
    s = sel.astype(BF16)
    hi, mid, lo = _split_bf16(x, 3)
    return (jnp.dot(s, lo, preferred_element_type=F32) + jnp.dot(s, mid, preferred_element_type=F32)
            + jnp.dot(s, hi, preferred_element_type=F32))


def _dot_nt_3pass(a, b):
    a_hi, a_lo = _split_bf16(a, 2)
    b_hi, b_lo = _split_bf16(b, 2)
    nt = lambda x, y: lax.dot_general(x, y, (((1,), (1,)), ((), ())), preferred_element_type=F32)
    return (nt(a_hi, b_lo) + nt(a_lo, b_hi)) + nt(a_hi, b_hi)


def _rms(x, w):
    return x * lax.rsqrt(jnp.mean(x * x, axis=-1, keepdims=True) + EPS) * w


def _compiler_params(sem):
    return pltpu.CompilerParams(dimension_semantics=sem, vmem_limit_bytes=VMEM_LIMIT)


def _mod_kernel(c_ref, w_ref, b_ref, o_ref):
    o_ref[...] = _dot(jax.nn.silu(c_ref[...]), w_ref[...]) + b_ref[...]


def _modulation(cvec, ada_w, ada_b):
    n = ada_w.shape[1]
    bn = 1536
    return pl.pallas_call(
        _mod_kernel,
        grid=(n // bn,),
        in_specs=[pl.BlockSpec((SUBLANES, D_MODEL), lambda j: (0, 0)),
                  pl.BlockSpec((D_MODEL, bn), lambda j: (0, j)),
                  pl.BlockSpec((1, bn), lambda j: (0, j))],
        out_specs=pl.BlockSpec((SUBLANES, bn), lambda j: (0, j)),
        out_shape=jax.ShapeDtypeStruct((SUBLANES, n), F32),
        compiler_params=_compiler_params(("arbitrary",)),
        name="adaln_mod",
    )(cvec, ada_w, ada_b)


def _win_kernel(off_a, off_u, w_ref, o_ref):
    w = w_ref[0]
    n_in = w.shape[1]
    o_ref[:, 0:off_a] = w[:, 0:off_a].astype(BF16)
    o_ref[:, off_a:off_a + n_in - off_u] = w[:, off_u:n_in].astype(BF16)
    tail = jnp.concatenate(
        [w[:, off_a:off_u], jnp.zeros((w.shape[0], P_WIDTH - n_in), F32)], axis=1)
    o_ref[:, P_A:P_WIDTH] = tail.astype(BF16)


def _win_layout(w_in, off_a, off_u):
    rows = 256
    kdim, n_in = w_in.shape[1], w_in.shape[2]
    return pl.pallas_call(
        functools.partial(_win_kernel, off_a, off_u),
        grid=(kdim // rows,),
        in_specs=[pl.BlockSpec((1, rows, n_in), lambda i: (0, i, 0))],
        out_specs=pl.BlockSpec((rows, P_WIDTH), lambda i: (i, 0)),
        out_shape=jax.ShapeDtypeStruct((kdim, P_WIDTH), BF16),
        compiler_params=_compiler_params(("arbitrary",)),
        name="win_layout",
    )(w_in)


def _pos_kernel(o_ref):
    nf = D_MODEL // 4
    p = lax.broadcasted_iota(I32, (GRID_W, nf), 0).astype(F32)
    i = lax.broadcasted_iota(I32, (GRID_W, nf), 1).astype(F32)
    omega = jnp.exp(i * (-math.log(10000.0) / nf))
    a = p * omega
    o_ref[:, 0:nf] = jnp.sin(a)
    o_ref[:, nf:2 * nf] = jnp.cos(a)


def _pos_table():
    return pl.pallas_call(
        _pos_kernel,
        out_shape=jax.ShapeDtypeStruct((GRID_W, D_MODEL // 2), F32),
        name="sincos_table",
    )()


def _add_pos(x, e_ref, blk, add_pos):
    if not add_pos:
        return x
    half = D_MODEL // 2
    e_all = e_ref[...]
    rows = []
    for j in range(TOKEN_BLOCK // GRID_W):
        xj = x[j * GRID_W:(j + 1) * GRID_W]
        e_row = e_ref[pl.ds(blk * (TOKEN_BLOCK // GRID_W) + j, 1), :]
        rows.append(jnp.concatenate([xj[:, 0:half] + e_row, xj[:, half:] + e_all], axis=1))
    return jnp.concatenate(rows, axis=0)


def _chunk_masks():
    r = lax.broadcasted_iota(I32, (TOKEN_BLOCK, TOKEN_BLOCK), 0)
    c = lax.broadcasted_iota(I32, (TOKEN_BLOCK, TOKEN_BLOCK), 1)
    same = (r // GLA_CHUNK) == (c // GLA_CHUNK)
    return same & (c <= r), same & (c >= r)


def _gla_direction(q, k, v, cum, fwd, att_mask, st_ref, ready, done):
    qe = q * jnp.exp(cum)
    ke = k * jnp.exp(-cum)
    yield
    lane = lax.broadcasted_iota(I32, (1, LANES), 1)
    o_intra = []
    for pair in range(2):
        qp = qe[:, pair * LANES:(pair + 1) * LANES]
        kp = ke[:, pair * LANES:(pair + 1) * LANES]
        for hh in range(2):
            qm = jnp.where((lane // GLA_DK) == hh, qp, 0.0)
            att = jnp.where(att_mask, _dot_nt(qm, kp), 0.0)
            head = 2 * pair + hh
            o_intra.append(_dot(att, v[:, head * GLA_DV:(head + 1) * GLA_DV]))
            yield
    o_intra = jnp.concatenate(o_intra, axis=1)
    while not ready():
        yield

    er = lax.broadcasted_iota(I32, (2 * GLA_DV, 2 * GLA_DK), 0)
    dc = lax.broadcasted_iota(I32, (2 * GLA_DV, 2 * GLA_DK), 1)
    same_head = (er // GLA_DV) == (dc // GLA_DK)
    n_chunks = TOKEN_BLOCK // GLA_CHUNK
    o_inter = [None] * n_chunks
    for c in (range(n_chunks) if fwd else reversed(range(n_chunks))):
        r0 = c * GLA_CHUNK
        rows = slice(r0, r0 + GLA_CHUNK)
        last = cum[r0 + GLA_CHUNK - 1:r0 + GLA_CHUNK] if fwd else cum[r0:r0 + 1]
        kd = k[rows] * jnp.exp(last - cum[rows])
        dec = jnp.exp(last)
        parts = []
        for pair in range(2):
            dl = slice(pair * LANES, (pair + 1) * LANES)
            st = st_ref[pair]
            parts.append(_dot_nt(qe[rows, dl], st))
            ds_t = _dot_tn(v[rows, pair * 2 * GLA_DV:(pair + 1) * 2 * GLA_DV], kd[:, dl])
            st_ref[pair] = dec[:, dl] * st + jnp.where(same_head, ds_t, 0.0)
        o_inter[c] = jnp.concatenate(parts, axis=1)
        yield
    done()
    return o_intra + jnp.concatenate(o_inter, axis=0)


def _interleave(chains):
    chains = list(chains)
    done = [False] * len(chains)
    tick = 0
    while not all(done):
        for i, ch in enumerate(chains):
            if tick >= i and not done[i]:
                try:
                    next(ch)
                except StopIteration:
                    done[i] = True
        tick += 1


def _load_state(s0_ref, u, st_ref):
    zero = jnp.zeros((GLA_DV, GLA_DK), F32)
    for pair in range(2):
        a = s0_ref[u, 2 * pair].T
        b = s0_ref[u, 2 * pair + 1].T
        st_ref[pair] = jnp.concatenate(
            [jnp.concatenate([a, zero], axis=1), jnp.concatenate([zero, b], axis=1)], axis=0)


def _store_state(st_ref, sfin_ref, u):
    for pair in range(2):
        st = st_ref[pair]
        sfin_ref[u, 2 * pair] = st[0:GLA_DV, 0:GLA_DK].T
        sfin_ref[u, 2 * pair + 1] = st[GLA_DV:2 * GLA_DV, GLA_DK:2 * GLA_DK].T


def _mod_row(mod_ref, row, part):
    return mod_ref[pl.ds(row, 1), part * D_MODEL:(part + 1) * D_MODEL]


def _front_stages(xin, row, mod_ref, n1_ref, win_ref, wa_ref, ba_ref, snw_ref, sws_ref, sbs_ref):
    h = _rms(xin, n1_ref[...]) * (1.0 + _mod_row(mod_ref, row, 1)) + _mod_row(mod_ref, row, 0)
    yield
    hb = h.astype(BF16)
    p_parts = []
    for c0, c1 in ((P_Q, P_G), (P_G, P_SV), (P_SV, P_WIDTH)):
        p_parts.append(jnp.dot(hb, win_ref[:, c0:c1], preferred_element_type=F32))
        yield
    p = jnp.concatenate(p_parts, axis=1)
    q = p[:, P_Q:P_K] * (GLA_DK ** -0.5)
    k = p[:, P_K:P_V]
    v = p[:, P_V:P_G]
    z = _dot(p[:, P_A:P_WIDTH], wa_ref[...]) + ba_ref[...]
    la = (jnp.minimum(z, 0.0) - jnp.log1p(jnp.exp(-jnp.abs(z)))) * (1.0 / GLA_GATE_NORM)
    yield

    ug = jax.nn.gelu(p[:, P_U:P_SV])
    yield
    vg = jax.nn.gelu(p[:, P_SV:P_A])
    yield
    s_cols = []
    for gi in range(SGU_GROUPS):
        cols = slice(gi * SGU_CH, (gi + 1) * SGU_CH)
        vn = _rms(vg[:, cols], snw_ref[:, cols])
        rhs = jnp.concatenate([vn[0:SGU_CHUNK], vn[SGU_CHUNK:2 * SGU_CHUNK]], axis=1)
        vm = _dot(sws_ref[gi], rhs) + jnp.concatenate([sbs_ref[gi], sbs_ref[gi]], axis=1)
        vm = jnp.concatenate([vm[:, 0:SGU_CH], vm[:, SGU_CH:2 * SGU_CH]], axis=0)
        s_cols.append(ug[:, cols] * vm)
    yield
    return q, k, v, p[:, P_G:P_U], la, jnp.concatenate(s_cols, axis=1)


def _back_stages(o, g, s_val, xin, row, mod_ref, gnw_ref, wout_ref, n2_ref, rw_ref):
    cols = []
    for head in range(GLA_HEADS):
        hs = slice(head * GLA_DV, (head + 1) * GLA_DV)
        cols.append(_rms(o[:, hs], gnw_ref[...]) * jax.nn.silu(g[:, hs]))
    cols.append(s_val)
    yield
    y = _dot(jnp.concatenate(cols, axis=1), wout_ref[...])
    yield
    x1 = xin + _mod_row(mod_ref, row, 2) * y
    h2 = _rms(x1, n2_ref[...]) * (1.0 + _mod_row(mod_ref, row, 4)) + _mod_row(mod_ref, row, 3)
    yield
    logits = _dot_nt_3pass(rw_ref[...], h2)
    m = jnp.max(logits, axis=0, keepdims=True)
    ex = jnp.exp(logits - m)
    return x1, h2, ex / jnp.sum(ex, axis=0, keepdims=True)


def _store_token_tiles(h2t_ref, u, b, h2):
    for s in range(TILES_PER_TOKEN):
        h2t_ref[u, pl.ds(b * TOKEN_BLOCK * TILES_PER_TOKEN + s, TOKEN_BLOCK, stride=TILES_PER_TOKEN), :] = (
            h2[:, s * LANES:(s + 1) * LANES])


def _zip_stages(gen_a, gen_b):
    out = [None, None]
    live = [gen_a, gen_b]
    while any(g is not None for g in live):
        for i, g in enumerate(live):
            if g is not None:
                try:
                    next(g)
                except StopIteration as stop:
                    out[i] = stop.value
                    live[i] = None
        yield
    return out


def _mixer_fwd_kernel(group, add_pos, mod_base, mod_per_seq, nb,
                      x_ref, e_ref, mod_ref, n1_ref, win_ref, wa_ref, ba_ref,
                      snw_ref, sws_ref, sbs_ref, s0_ref,
                      qkv_ref, g_ref, lab_ref, s_ref, of_ref, sfin_ref,
                      st_ref):
    grp = pl.program_id(0)
    blk = pl.program_id(1)
    sub = x_ref.shape[1] // TOKEN_BLOCK
    lo_mask, _ = _chunk_masks()
    state_done = set()

    @pl.when(blk == 0)
    def _():
        for u in range(group):
            _load_state(s0_ref, u, st_ref.at[u])

    def chain(u, b):
        row = mod_base + ((grp * group + u) if mod_per_seq else 0)
        st_u = st_ref.at[u]
        rows = slice(b * TOKEN_BLOCK, (b + 1) * TOKEN_BLOCK)
        xin = _add_pos(x_ref[u, rows, :], e_ref, blk * sub + b, add_pos)
        q, k, v, g, la, s_val = yield from _front_stages(
            xin, row, mod_ref, n1_ref, win_ref, wa_ref, ba_ref, snw_ref, sws_ref, sbs_ref)
        qkv_ref[u, rows, :] = jnp.concatenate([q, k, v], axis=1)
        g_ref[u, rows, :] = g
        lab_ref[u, rows, :] = la[:, QK_W:2 * QK_W]
        s_ref[u, rows, :] = s_val

        cum = _select_dot(lo_mask, la[:, 0:QK_W])
        yield
        of_ref[u, rows, :] = yield from _gla_direction(
            q, k, v, cum, True, lo_mask, st_u,
            ready=lambda: b == 0 or (u, b - 1) in state_done, done=lambda: state_done.add((u, b)))

    _interleave(chain(u, b) for b in range(sub) for u in range(group))

    @pl.when(blk == nb - 1)
    def _():
        for u in range(group):
            _store_state(st_ref.at[u], sfin_ref, u)


def _mixer_fwd(x, e_tab, mod, n1, win, wa, ba, snw, sws, sbs, s0, *, nseq, nb, add_pos,
               mod_base, mod_per_seq):
    seq_len = nb * TOKEN_BLOCK
    group = min(SEQ_GROUP, nseq)
    sub = min(SEQ_GROUP // group, nb)
    nb //= sub
    tok = lambda w: pl.BlockSpec((group, sub * TOKEN_BLOCK, w), lambda s, i: (s, i, 0))
    full = lambda a: pl.BlockSpec(a.shape, lambda s, i: (0,) * a.ndim)
    st_spec = pl.BlockSpec((group, GLA_HEADS, GLA_DK, GLA_DV), lambda s, i: (s, 0, 0, 0))
    s0_spec = st_spec if s0.shape[0] == nseq else pl.BlockSpec(s0.shape, lambda s, i: (0, 0, 0, 0))
    act = lambda w: jax.ShapeDtypeStruct((nseq, seq_len, w), F32)
    kern = functools.partial(_mixer_fwd_kernel, group, add_pos, mod_base, mod_per_seq, nb)
    return pl.pallas_call(
        kern,
        grid=(nseq // group, nb),
        in_specs=[tok(D_MODEL), full(e_tab), full(mod), full(n1), full(win), full(wa), full(ba),
                  full(snw), full(sws), full(sbs), s0_spec],
        out_specs=[tok(1024), tok(GLA_WIDTH), tok(QK_W), tok(SGU_WIDTH), tok(GLA_WIDTH), st_spec],
        out_shape=[act(1024), act(GLA_WIDTH), act(QK_W), act(SGU_WIDTH), act(GLA_WIDTH),
                   jax.ShapeDtypeStruct((nseq, GLA_HEADS, GLA_DK, GLA_DV), F32)],
        scratch_shapes=[pltpu.VMEM((group, 2, 2 * GLA_DV, 2 * GLA_DK), F32)],
        compiler_params=_compiler_params(("arbitrary", "arbitrary")),
        name="mixer_fwd",
    )(x, e_tab, mod, n1, win, wa, ba, snw, sws, sbs, s0)


def _mixer_bwd_kernel(group, add_pos, mod_base, mod_per_seq, nb,
                      x_ref, e_ref, mod_ref, qkv_ref, g_ref, lab_ref, s_ref, of_ref,
                      gnw_ref, wout_ref, n2_ref, rw_ref, s0_ref,
                      x1_ref, h2t_ref, probs_ref, sfin_ref,
                      st_ref):
    grp = pl.program_id(0)
    step = pl.program_id(1)
    sub = x_ref.shape[1] // TOKEN_BLOCK
    blk = nb - 1 - step
    _, hi_mask = _chunk_masks()
    state_done = set()

    @pl.when(step == 0)
    def _():
        for u in range(group):
            _load_state(s0_ref, u, st_ref.at[u])

    def chain(u, b):
        row = mod_base + ((grp * group + u) if mod_per_seq else 0)
        st_u = st_ref.at[u]
        rows = slice(b * TOKEN_BLOCK, (b + 1) * TOKEN_BLOCK)
        qkv = qkv_ref[u, rows, :]
        q, k, v = qkv[:, 0:QK_W], qkv[:, QK_W:2 * QK_W], qkv[:, 2 * QK_W:]
        cum = _select_dot(hi_mask, lab_ref[u, rows, :])
        yield
        o_b = yield from _gla_direction(
            q, k, v, cum, False, hi_mask, st_u,
            ready=lambda: b == sub - 1 or (u, b + 1) in state_done,
            done=lambda: state_done.add((u, b)))
        xin = _add_pos(x_ref[u, rows, :], e_ref, blk * sub + b, add_pos)
        x1, h2, probs = yield from _back_stages(
            of_ref[u, rows, :] + o_b, g_ref[u, rows, :], s_ref[u, rows, :], xin, row,
            mod_ref, gnw_ref, wout_ref, n2_ref, rw_ref)
        x1_ref[u, rows, :] = x1
        _store_token_tiles(h2t_ref, u, b, h2)
        probs_ref[u, b] = probs

    _interleave(chain(u, b) for b in reversed(range(sub)) for u in range(group))

    @pl.when(step == nb - 1)
    def _():
        for u in range(group):
            _store_state(st_ref.at[u], sfin_ref, u)


def _mixer_bwd(x, e_tab, mod, qkv, g, lab, s, of, gnw, wout, n2, rw_t, s0, *, nseq, nb, add_pos,
               mod_base, mod_per_seq):
    seq_len = nb * TOKEN_BLOCK
    group = min(SEQ_GROUP, nseq)
    sub = min(SEQ_GROUP // group, nb)
    nb //= sub
    tok = lambda w: pl.BlockSpec((group, sub * TOKEN_BLOCK, w), lambda s_, i: (s_, nb - 1 - i, 0))
    full = lambda a: pl.BlockSpec(a.shape, lambda s_, i: (0,) * a.ndim)
    st_spec = pl.BlockSpec((group, GLA_HEADS, GLA_DK, GLA_DV), lambda s_, i: (s_, 0, 0, 0))
    s0_spec = st_spec if s0.shape[0] == nseq else pl.BlockSpec(s0.shape, lambda s_, i: (0, 0, 0, 0))
    kern = functools.partial(_mixer_bwd_kernel, group, add_pos, mod_base, mod_per_seq, nb)
    return pl.pallas_call(
        kern,
        grid=(nseq // group, nb),
        in_specs=[tok(D_MODEL), full(e_tab), full(mod), tok(1024), tok(GLA_WIDTH), tok(QK_W),
                  tok(SGU_WIDTH), tok(GLA_WIDTH), full(gnw), full(wout), full(n2), full(rw_t),
                  s0_spec],
        out_specs=[tok(D_MODEL),
                   pl.BlockSpec((group, sub * TOKEN_BLOCK * TILES_PER_TOKEN, LANES),
                                lambda s_, i: (s_, nb - 1 - i, 0)),
                   pl.BlockSpec((group, sub, N_EXPERTS, TOKEN_BLOCK),
                                lambda s_, i: (s_, nb - 1 - i, 0, 0)),
                   st_spec],
        out_shape=[jax.ShapeDtypeStruct((nseq, seq_len, D_MODEL), F32),
                   jax.ShapeDtypeStruct((nseq, seq_len * TILES_PER_TOKEN, LANES), F32),
                   jax.ShapeDtypeStruct((nseq, nb * sub, N_EXPERTS, TOKEN_BLOCK), F32),
                   jax.ShapeDtypeStruct((nseq, GLA_HEADS, GLA_DK, GLA_DV), F32)],
        scratch_shapes=[pltpu.VMEM((group, 2, 2 * GLA_DV, 2 * GLA_DK), F32)],
        compiler_params=_compiler_params(("arbitrary", "arbitrary")),
        name="mixer_bwd",
    )(x, e_tab, mod, qkv, g, lab, s, of, gnw, wout, n2, rw_t, s0)


def _mixer_fused_kernel(group, mod_base, mod_per_seq,
                        x_ref, mod_ref, n1_ref, win_ref, wa_ref, ba_ref, snw_ref, sws_ref, sbs_ref,
                        gnw_ref, wout_ref, n2_ref, rw_ref, s0f_ref, s0b_ref,
                        x1_ref, h2t_ref, probs_ref, sfin_f_ref, sfin_b_ref,
                        stf_ref, stb_ref):
    grp = pl.program_id(0)
    lo_mask, hi_mask = _chunk_masks()
    always = lambda: True
    nothing = lambda: None
    for u in range(group):
        _load_state(s0f_ref, u, stf_ref.at[u])
        _load_state(s0b_ref, u, stb_ref.at[u])

    def chain(u):
        row = mod_base + ((grp * group + u) if mod_per_seq else 0)
        xin = x_ref[u]
        q, k, v, g, la, s_val = yield from _front_stages(
            xin, row, mod_ref, n1_ref, win_ref, wa_ref, ba_ref, snw_ref, sws_ref, sbs_ref)
        cum_f = _select_dot(lo_mask, la[:, 0:QK_W])
        cum_b = _select_dot(hi_mask, la[:, QK_W:2 * QK_W])
        yield
        o_f, o_b = yield from _zip_stages(
            _gla_direction(q, k, v, cum_f, True, lo_mask, stf_ref.at[u], always, nothing),
            _gla_direction(q, k, v, cum_b, False, hi_mask, stb_ref.at[u], always, nothing))
        x1, h2, probs = yield from _back_stages(
            o_f + o_b, g, s_val, xin, row, mod_ref, gnw_ref, wout_ref, n2_ref, rw_ref)
        x1_ref[u] = x1
        _store_token_tiles(h2t_ref, u, 0, h2)
        probs_ref[u, 0] = probs

    _interleave(chain(u) for u in range(group))
    for u in range(group):
        _store_state(stf_ref.at[u], sfin_f_ref, u)
        _store_state(stb_ref.at[u], sfin_b_ref, u)


def _mixer_fused(x, mod, prm, s0_f, s0_b, *, nseq, mod_base, mod_per_seq):
    group = min(SEQ_GROUP, nseq)
    consts = [mod, prm["n1"], prm["win"], prm["wa"], prm["ba"], prm["snw"], prm["sws"], prm["sbs"],
              prm["gnw"], prm["wout"], prm["n2"], prm["rw_t"]]
    tok = lambda w: pl.BlockSpec((group, TOKEN_BLOCK, w), lambda s: (s, 0, 0))
    full = lambda a: pl.BlockSpec(a.shape, lambda s: (0,) * a.ndim)
    st_spec = pl.BlockSpec((group, GLA_HEADS, GLA_DK, GLA_DV), lambda s: (s, 0, 0, 0))
    s0_spec = lambda a: st_spec if a.shape[0] == nseq else pl.BlockSpec(a.shape, lambda s: (0, 0, 0, 0))
    state = jax.ShapeDtypeStruct((nseq, GLA_HEADS, GLA_DK, GLA_DV), F32)
    st_scratch = pltpu.VMEM((group, 2, 2 * GLA_DV, 2 * GLA_DK), F32)
    return pl.pallas_call(
        functools.partial(_mixer_fused_kernel, group, mod_base, mod_per_seq),
        grid=(nseq // group,),
        in_specs=[tok(D_MODEL)] + [full(a) for a in consts] + [s0_spec(s0_f), s0_spec(s0_b)],
        out_specs=[tok(D_MODEL),
                   pl.BlockSpec((group, TOKEN_BLOCK * TILES_PER_TOKEN, LANES), lambda s: (s, 0, 0)),
                   pl.BlockSpec((group, 1, N_EXPERTS, TOKEN_BLOCK), lambda s: (s, 0, 0, 0)),
                   st_spec, st_spec],
        out_shape=[jax.ShapeDtypeStruct((nseq, TOKEN_BLOCK, D_MODEL), F32),
                   jax.ShapeDtypeStruct((nseq, TOKEN_BLOCK * TILES_PER_TOKEN, LANES), F32),
                   jax.ShapeDtypeStruct((nseq, 1, N_EXPERTS, TOKEN_BLOCK), F32),
                   state, state],
        scratch_shapes=[st_scratch, st_scratch],
        compiler_params=_compiler_params(("arbitrary",)),
        name="mixer_fused",
    )(x, *consts, s0_f, s0_b)


def _route_kernel(n_tok, cap, probs_ref, row_ref, gate_ref, xs_ref, ps_ref):
    n_blk = n_tok // TOKEN_BLOCK
    n_chunk = n_tok // LANES
    probs = jnp.concatenate([probs_ref[b] for b in range(n_blk)], axis=1)
    capf = jnp.float32(cap)

    def count(mask):
        return jnp.sum(mask.astype(F32), axis=1, keepdims=True)

    def as_f32(bits):
        return lax.bitcast_convert_type(bits, F32)

    def thr_step(_, lohi):
        lo, hi = lohi
        mid = lo + ((hi - lo + 1) >> 1)
        ok = count(probs >= as_f32(mid)) >= capf
        return jnp.where(ok, mid, lo), jnp.where(ok, hi, mid - 1)

    lo0 = jnp.zeros((N_EXPERTS, 1), I32)
    hi0 = jnp.full((N_EXPERTS, 1), 0x3F800000, I32)
    thr, _ = lax.fori_loop(0, 31, thr_step, (lo0, hi0))
    gt = probs >= as_f32(thr + 1)
    eq = (probs >= as_f32(thr)) & jnp.logical_not(gt)
    need = capf - count(gt)
    tok = lax.broadcasted_iota(I32, (N_EXPERTS, n_tok), 1)

    def tie_step(_, lohi):
        lo, hi = lohi
        mid = (lo + hi) >> 1
        ok = count(eq & (tok <= mid)) >= need
        return jnp.where(ok, lo, mid + 1), jnp.where(ok, mid, hi)

    n_bits = max(1, (n_tok - 1).bit_length())
    cut, _ = lax.fori_loop(0, n_bits, tie_step,
                           (jnp.zeros((N_EXPERTS, 1), I32), jnp.full((N_EXPERTS, 1), n_tok - 1, I32)))
    sel = (gt | (eq & (tok <= cut))).astype(F32)

    xs_ref[...] = jnp.concatenate([sel[:, c * LANES:(c + 1) * LANES] for c in range(n_chunk)], axis=0)
    ps_ref[...] = jnp.concatenate([probs[:, c * LANES:(c + 1) * LANES] for c in range(n_chunk)], axis=0)

    li = lax.broadcasted_iota(I32, (LANES, LANES), 0)
    lj = lax.broadcasted_iota(I32, (LANES, LANES), 1)
    upper = (li <= lj).astype(F32)
    ci = lax.broadcasted_iota(I32, (n_chunk, n_chunk), 0)
    cj = lax.broadcasted_iota(I32, (n_chunk, n_chunk), 1)
    lower = (cj <= ci).astype(F32)
    slot = lax.broadcasted_iota(I32, (1, cap), 1).astype(F32)
    chunk_id = lax.broadcasted_iota(I32, (n_chunk, cap), 0).astype(F32)
    lane_id = lax.broadcasted_iota(I32, (LANES, cap), 0).astype(F32)
    reps = cap // LANES

    def per_expert(e):
        x = xs_ref[pl.ds(e, n_chunk, stride=N_EXPERTS), :]
        pe = ps_ref[pl.ds(e, n_chunk, stride=N_EXPERTS), :]
        ploc = _dot(x, upper)
        tot = jnp.broadcast_to(ploc[:, LANES - 1:LANES], (n_chunk, LANES))
        cum = _dot(lower, tot)
        yield
        cum_w = jnp.concatenate([cum] * reps, axis=1)
        base_w = jnp.concatenate([cum - tot] * reps, axis=1)
        chunk_of = jnp.sum((cum_w <= slot).astype(F32), axis=0, keepdims=True)
        onehot = chunk_id == chunk_of
        local = slot - jnp.sum(jnp.where(onehot, base_w, 0.0), axis=0, keepdims=True)
        yield
        lhs = jnp.concatenate([ploc.astype(BF16)] + _split_bf16(pe, 3), axis=1)
        got = _dot_tn(lhs, onehot.astype(F32))
        yield
        pref = got[0:LANES]
        lane_of = jnp.sum((pref <= local).astype(F32), axis=0, keepdims=True)
        token = chunk_of * LANES + lane_of
        row_ref[pl.ds(e, 1), :] = (token * TILES_PER_TOKEN).astype(I32)
        yield
        pg = (got[3 * LANES:4 * LANES] + got[2 * LANES:3 * LANES]) + got[LANES:2 * LANES]
        gate_ref[pl.ds(e, 1), :] = jnp.sum(jnp.where(lane_id == lane_of, pg, 0.0), axis=0, keepdims=True)

    def expert_pair(i, _):
        _interleave(per_expert(2 * i + u) for u in range(2))
        return 0

    lax.fori_loop(0, N_EXPERTS // 2, expert_pair, 0)


def _route(probs, n_tok, cap):
    return pl.pallas_call(
        functools.partial(_route_kernel, n_tok, cap),
        out_shape=[jax.ShapeDtypeStruct((N_EXPERTS, cap), I32),
                   jax.ShapeDtypeStruct((N_EXPERTS, cap), F32)],
        scratch_shapes=[pltpu.VMEM((n_tok // LANES * N_EXPERTS, LANES), F32),
                        pltpu.VMEM((n_tok // LANES * N_EXPERTS, LANES), F32)],
        compiler_params=pltpu.CompilerParams(vmem_limit_bytes=VMEM_LIMIT),
        name="route_topk",
    )(probs)


def _expert_kernel(cap, row_ref, h2t_ref, w1_ref, w3_ref, w2_ref, ye_ref,
                   xe_ref, x2_ref, hid_ref, sem):
    e = pl.program_id(0)
    f = pl.program_id(1)
    slot = e % 2
    rows_per_step = cap // EXPERT_STEPS

    def start_row(expert, buf, j):
        src = pl.multiple_of(row_ref[expert * cap + j], TILES_PER_TOKEN)
        dst = pl.multiple_of(j * TILES_PER_TOKEN, TILES_PER_TOKEN)
        pltpu.make_async_copy(h2t_ref.at[pl.ds(src, TILES_PER_TOKEN), :],
                              xe_ref.at[buf, pl.ds(dst, TILES_PER_TOKEN), :], sem.at[buf]).start()

    def wait_rows(buf):
        pltpu.make_async_copy(h2t_ref.at[pl.ds(0, cap * TILES_PER_TOKEN), :], xe_ref.at[buf],
                              sem.at[buf]).wait()

    def prefetch_next():
        nxt = jnp.minimum(e + 1, N_EXPERTS - 1)
        first = f * rows_per_step
        for j in range(rows_per_step):
            start_row(nxt, 1 - slot, first + j)

    @pl.when((e == 0) & (f == 0))
    def _():
        def issue(j, _):
            start_row(0, 0, j)
            return 0

        lax.fori_loop(0, cap, issue, 0, unroll=8)

    @pl.when(f == 0)
    def _():
        wait_rows(slot)
        for s in range(TILES_PER_TOKEN):
            x2_ref[:, s * LANES:(s + 1) * LANES] = (
                xe_ref[slot, pl.ds(s, cap, stride=TILES_PER_TOKEN), :].astype(BF16))

    @pl.when(f < N_HID_STEPS)
    def _():
        prefetch_next()
        x2 = x2_ref[...]
        a = jnp.dot(x2, w1_ref[0].astype(BF16), preferred_element_type=F32)
        b = jnp.dot(x2, w3_ref[0].astype(BF16), preferred_element_type=F32)
        hid_ref[f] = (jax.nn.silu(a) * b).astype(BF16)

    @pl.when(f >= N_HID_STEPS)
    def _():
        prefetch_next()
        w2 = w2_ref[0].astype(BF16)
        out = jnp.dot(hid_ref[0], w2[0:EXPERT_F_BLOCK], preferred_element_type=F32)
        for kb in range(1, N_HID_STEPS):
            out += jnp.dot(hid_ref[kb], w2[kb * EXPERT_F_BLOCK:(kb + 1) * EXPERT_F_BLOCK],
                           preferred_element_type=F32)
        tile0 = (f - N_HID_STEPS) * (EXPERT_N_BLOCK // LANES)
        for i in range(EXPERT_N_BLOCK // LANES):
            ye_ref[pl.ds(tile0 + i, cap, stride=TILES_PER_TOKEN), :] = out[:, i * LANES:(i + 1) * LANES]

    @pl.when((e == N_EXPERTS - 1) & (f == EXPERT_STEPS - 1))
    def _():
        wait_rows(1 - slot)


def _experts(rows, h2t, w1, w3, w2, cap):
    hid_blk = lambda e, f, idx: (e, 0, jnp.minimum(f, N_HID_STEPS - 1))
    n_out = D_MODEL // EXPERT_N_BLOCK

    def out_blk(e, f, idx):
        hold = (f == 0) & (e > 0)
        return (jnp.where(hold, e - 1, e), 0,
                jnp.where(hold, n_out - 1, jnp.maximum(f - N_HID_STEPS, 0)))

    grid_spec = pltpu.PrefetchScalarGridSpec(
        num_scalar_prefetch=1,
        grid=(N_EXPERTS, EXPERT_STEPS),
        in_specs=[pl.BlockSpec(memory_space=pl.ANY),
                  pl.BlockSpec((1, D_MODEL, EXPERT_F_BLOCK), hid_blk),
                  pl.BlockSpec((1, D_MODEL, EXPERT_F_BLOCK), hid_blk),
                  pl.BlockSpec((1, D_EXPERT, EXPERT_N_BLOCK), out_blk)],
        out_specs=pl.BlockSpec((cap * TILES_PER_TOKEN, LANES), lambda e, f, idx: (e, 0)),
        scratch_shapes=[pltpu.VMEM((2, cap * TILES_PER_TOKEN, LANES), F32),
                        pltpu.VMEM((cap, D_MODEL), BF16),
                        pltpu.VMEM((N_HID_STEPS, cap, EXPERT_F_BLOCK), BF16),
                        pltpu.SemaphoreType.DMA((2,))],
    )
    return pl.pallas_call(
        functools.partial(_expert_kernel, cap),
        grid_spec=grid_spec,
        out_shape=jax.ShapeDtypeStruct((N_EXPERTS * cap * TILES_PER_TOKEN, LANES), F32),
        compiler_params=_compiler_params(("arbitrary", "arbitrary")),
        name="expert_swiglu",
    )(rows, h2t, w1, w3, w2)


COMBINE_BATCH = 8
ZERO_ROWS = 512


def _combine_kernel(cap, n_tok, mod_base, mod_per_seq, nb,
                    row_ref, gate_ref, ye_ref, x1_ref, mod_ref, fw_ref, y_ref,
                    acc_ref, xbuf_ref, ybuf_ref, sem_x, sem_y):
    e = pl.program_id(0)

    @pl.when(e == 0)
    def _():
        def zero(i, _):
            r = pl.multiple_of(i * ZERO_ROWS, ZERO_ROWS)
            acc_ref[pl.ds(r, ZERO_ROWS), :] = jnp.zeros((ZERO_ROWS, LANES), F32)
            return 0

        lax.fori_loop(0, n_tok * TILES_PER_TOKEN // ZERO_ROWS, zero, 0)

    def batch(jb, _):
        vals = []
        for u in range(COMBINE_BATCH):
            j = jb * COMBINE_BATCH + u
            t = pl.multiple_of(row_ref[e * cap + j], TILES_PER_TOKEN)
            gate = gate_ref[e * cap + j]
            src = pl.multiple_of(j * TILES_PER_TOKEN, TILES_PER_TOKEN)
            vals.append((t, acc_ref[pl.ds(t, TILES_PER_TOKEN), :]
                         + ye_ref[pl.ds(src, TILES_PER_TOKEN), :] * gate))
        for t, val in vals:
            acc_ref[pl.ds(t, TILES_PER_TOKEN), :] = val
        return 0

    lax.fori_loop(0, cap // COMBINE_BATCH, batch, 0)

    @pl.when(e == N_EXPERTS - 1)
    def _():
        n_blk = n_tok // TOKEN_BLOCK

        def x1_copy(b, slot):
            rows = pl.ds(pl.multiple_of(b * TOKEN_BLOCK, TOKEN_BLOCK), TOKEN_BLOCK)
            return pltpu.make_async_copy(x1_ref.at[rows, :], xbuf_ref.at[slot], sem_x.at[slot])

        def y_copy(b, slot):
            rows = pl.ds(pl.multiple_of(b * TOKEN_BLOCK, TOKEN_BLOCK), TOKEN_BLOCK)
            return pltpu.make_async_copy(ybuf_ref.at[slot], y_ref.at[rows, :], sem_y.at[slot])

        x1_copy(0, 0).start()

        def block(b, _):
            slot = b % 2

            @pl.when(b + 1 < n_blk)
            def _():
                x1_copy(b + 1, 1 - slot).start()

            x1_copy(b, slot).wait()

            @pl.when(b >= 2)
            def _():
                y_copy(b - 2, slot).wait()

            base = pl.multiple_of(b * (TOKEN_BLOCK * TILES_PER_TOKEN), TOKEN_BLOCK * TILES_PER_TOKEN)
            moe = jnp.concatenate(
                [acc_ref[pl.ds(base + s, TOKEN_BLOCK, stride=TILES_PER_TOKEN), :]
                 for s in range(TILES_PER_TOKEN)], axis=1)
            row = mod_base + ((b // nb) if mod_per_seq else 0)
            ybuf_ref[slot] = _rms(xbuf_ref[slot] + _mod_row(mod_ref, row, 5) * moe, fw_ref[...])
            y_copy(b, slot).start()
            return 0

        lax.fori_loop(0, n_blk, block, 0)
        y_copy(n_blk - 2, n_blk % 2).wait()
        y_copy(n_blk - 1, (n_blk - 1) % 2).wait()


def _combine_final(rows, gates, ye, x1, mod, fw, cap, n_tok, *, nb, mod_base, mod_per_seq):
    full = lambda a: pl.BlockSpec(a.shape, lambda e, rows, gate: (0,) * a.ndim)
    grid_spec = pltpu.PrefetchScalarGridSpec(
        num_scalar_prefetch=2,
        grid=(N_EXPERTS,),
        in_specs=[pl.BlockSpec((cap * TILES_PER_TOKEN, LANES), lambda e, rows, gate: (e, 0)),
                  pl.BlockSpec(memory_space=pl.ANY), full(mod), full(fw)],
        out_specs=pl.BlockSpec(memory_space=pl.ANY),
        scratch_shapes=[pltpu.VMEM((n_tok * TILES_PER_TOKEN, LANES), F32),
                        pltpu.VMEM((2, TOKEN_BLOCK, D_MODEL), F32),
                        pltpu.VMEM((2, TOKEN_BLOCK, D_MODEL), F32),
                        pltpu.SemaphoreType.DMA((2,)),
                        pltpu.SemaphoreType.DMA((2,))],
    )
    return pl.pallas_call(
        functools.partial(_combine_kernel, cap, n_tok, mod_base, mod_per_seq, nb),
        grid_spec=grid_spec,
        out_shape=jax.ShapeDtypeStruct((n_tok, D_MODEL), F32),
        compiler_params=_compiler_params(("arbitrary",)),
        name="moe_combine_norm",
    )(rows, gates, ye, x1, mod, fw)


def _trunk_and_norm(x, e_tab, mod, s0_f, s0_b, prm, *, nseq, seq_len, add_pos, mod_base, mod_per_seq):
    nb = seq_len // TOKEN_BLOCK
    n_tok = nseq * seq_len
    cap = EC_CAPACITY_FACTOR * n_tok // N_EXPERTS
    kw = dict(nseq=nseq, nb=nb, add_pos=add_pos, mod_base=mod_base, mod_per_seq=mod_per_seq)
    if nb == 1 and not add_pos:
        x1, h2t, probs, sfin_f, sfin_b = _mixer_fused(
            x, mod, prm, s0_f, s0_b, nseq=nseq, mod_base=mod_base, mod_per_seq=mod_per_seq)
    else:
        qkv, g, lab, s, of, sfin_f = _mixer_fwd(
            x, e_tab, mod, prm["n1"], prm["win"], prm["wa"], prm["ba"], prm["snw"], prm["sws"],
            prm["sbs"], s0_f, **kw)
        x1, h2t, probs, sfin_b = _mixer_bwd(
            x, e_tab, mod, qkv, g, lab, s, of, prm["gnw"], prm["wout"], prm["n2"], prm["rw_t"],
            s0_b, **kw)
    x1 = x1.reshape(n_tok, D_MODEL)
    h2t = h2t.reshape(n_tok * TILES_PER_TOKEN, LANES)
    probs = probs.reshape(n_tok // TOKEN_BLOCK, N_EXPERTS, TOKEN_BLOCK)
    rows, gates = _route(probs, n_tok, cap)
    rows, gates = rows.reshape(-1), gates.reshape(-1)
    ye = _experts(rows, h2t, prm["w1"], prm["w3"], prm["w2"], cap)
    y = _combine_final(rows, gates, ye, x1, mod, prm["fw"], cap, n_tok,
                       nb=nb, mod_base=mod_base, mod_per_seq=mod_per_seq)
    return y.reshape(nseq, seq_len, D_MODEL), sfin_f, sfin_b


def kernel(x_prompt, x_sample, state_gla_fwd, state_gla_bwd, c, c_ctx, ada_w, ada_b, norm1_w, w_in, gla_wa2_f, gla_ba_f, gla_wa2_b, gla_ba_b, gla_norm_w, sgu_norm_w, sgu_ws, sgu_bs, w_out, norm2_w, router_w, exp_w1, exp_w3, exp_w2, final_norm_w):
    assert ada_w.shape[0] == 1, "single trunk layer"
    batch, seq, _ = x_prompt.shape
    dec_batch, dec_seq, _ = x_sample.shape

    off_af = 2 * QK_W + 2 * GLA_WIDTH
    off_u = off_af + 2 * GLA_LOWRANK
    win = _win_layout(w_in, off_af, off_u)
    wa = jnp.zeros((P_WIDTH - P_A, 2 * QK_W), F32)
    wa = wa.at[0:GLA_LOWRANK, 0:QK_W].set(gla_wa2_f[0])
    wa = wa.at[GLA_LOWRANK:2 * GLA_LOWRANK, QK_W:].set(gla_wa2_b[0]).astype(BF16)
    prm = dict(
        n1=norm1_w, win=win, wa=wa,
        ba=jnp.concatenate([gla_ba_f[0], gla_ba_b[0]])[None, :],
        snw=sgu_norm_w, sws=sgu_ws[0].astype(BF16),
        sbs=jnp.broadcast_to(sgu_bs[0][:, :, None], (SGU_GROUPS, SGU_CHUNK, SGU_CH)),
        gnw=gla_norm_w, wout=w_out[0].astype(BF16), n2=norm2_w, rw_t=router_w[0].T,
        w1=exp_w1[0], w3=exp_w3[0], w2=exp_w2[0], fw=final_norm_w[None, :])

    cvec = jnp.concatenate([c_ctx[None, :], c, jnp.zeros((SUBLANES - 1 - dec_batch, D_MODEL), F32)])
    mod = _modulation(cvec, ada_w[0], ada_b)
    e_tab = _pos_table()

    zero_state = jnp.zeros((min(SEQ_GROUP, batch), GLA_HEADS, GLA_DK, GLA_DV), F32)
    y_prompt, sf, sb = _trunk_and_norm(
        x_prompt, e_tab, mod, zero_state, zero_state, prm,
        nseq=batch, seq_len=seq, add_pos=False, mod_base=0, mod_per_seq=False)
    y_sample, _, _ = _trunk_and_norm(
        x_sample, e_tab, mod, state_gla_fwd[:, 0], state_gla_bwd[:, 0], prm,
        nseq=dec_batch, seq_len=dec_seq, add_pos=True, mod_base=1, mod_per_seq=True)
    return (y_prompt, y_sample, sf[:, None], sb[:, None])
```

```python
import functools
import math

import jax
import jax.numpy as jnp
from jax import lax
from jax.experimental import pallas as pl
from jax.experimental.pallas import tpu as pltpu

F32 = jnp.float32
BF16 = jnp.bfloat16
I32 = jnp.int32

D_MODEL = 1024
GRID_W = 64
GLA_HEADS = 4
GLA_DK = 64
GLA_DV = 128
GLA_WIDTH = GLA_HEADS * GLA_DV
QK_W = GLA_HEADS * GLA_DK
GLA_LOWRANK = 16
GLA_GATE_NORM = 16.0
GLA_CHUNK = 64
SGU_WIDTH = 512
SGU_GROUPS = 4
SGU_CH = 128
SGU_CHUNK = 128
N_EXPERTS = 16
EC_CAPACITY_FACTOR = 2
D_EXPERT = 2048
EPS = 1e-6

SUBLANES = 8
LANES = 128
TILES_PER_TOKEN = D_MODEL // LANES

TOKEN_BLOCK = 256
SEQ_GROUP = 4
P_Q, P_K, P_V, P_G, P_U, P_SV, P_A = 0, 256, 512, 1024, 1536, 2048, 2560
P_WIDTH = 2688
EXPERT_F_BLOCK = 512
EXPERT_N_BLOCK = 256
N_HID_STEPS = D_EXPERT // EXPERT_F_BLOCK
EXPERT_STEPS = N_HID_STEPS + D_MODEL // EXPERT_N_BLOCK
VMEM_LIMIT = 56 * 1024 * 1024


def _dot(a, b):
    return jnp.dot(a.astype(BF16), b.astype(BF16), preferred_element_type=F32)


def _dot_nt(a, b):
    return lax.dot_general(a.astype(BF16), b.astype(BF16), (((1,), (1,)), ((), ())),
                           preferred_element_type=F32)


def _dot_tn(a, b):
    return lax.dot_general(a.astype(BF16), b.astype(BF16), (((0,), (0,)), ((), ())),
                           preferred_element_type=F32)


def _dot_f32(a, b, dims=(((1,), (0,)), ((), ()))):
    return lax.dot_general(a, b, dims, precision=lax.Precision.HIGHEST, preferred_element_type=F32)


def _split_bf16(x, terms):
    parts = []
    for _ in range(terms - 1):
        part = x.astype(BF16)
        parts.append(part)
        x = x - part.astype(F32)
    parts.append(x.astype(BF16))
    return parts


def _select_dot(sel, x):
    s = sel.astype(BF16)
    hi, mid, lo = _split_bf16(x, 3)
    return (jnp.dot(s, lo, preferred_element_type=F32) + jnp.dot(s, mid, preferred_element_type=F32)
            + jnp.dot(s, hi, preferred_element_type=F32))


def _dot_nt_3pass(a, b):
    a_hi, a_lo = _split_bf16(a, 2)
    b_hi, b_lo = _split_bf16(b, 2)
    nt = lambda x, y: lax.dot_general(x, y, (((1,), (1,)), ((), ())), preferred_element_type=F32)
    return (nt(a_hi, b_lo) + nt(a_lo, b_hi)) + nt(a_hi, b_hi)


def _rms(x, w):
    return x * lax.rsqrt(jnp.mean(x * x, axis=-1, keepdims=True) + EPS) * w


def _compiler_params(sem):
    return pltpu.CompilerParams(dimension_semantics=sem, vmem_limit_bytes=VMEM_LIMIT)


def _mod_kernel(c_ref, w_ref, b_ref, o_ref):
    o_ref[...] = _dot(jax.nn.silu(c_ref[...]), w_ref[...]) + b_ref[...]


def _modulation(cvec, ada_w, ada_b):
    n = ada_w.shape[1]
    bn = 1536
    return pl.pallas_call(
        _mod_kernel,
        grid=(n // bn,),
        in_specs=[pl.BlockSpec((SUBLANES, D_MODEL), lambda j: (0, 0)),
                  pl.BlockSpec((D_MODEL, bn), lambda j: (0, j)),
                  pl.BlockSpec((1, bn), lambda j: (0, j))],
        out_specs=pl.BlockSpec((SUBLANES, bn), lambda j: (0, j)),
        out_shape=jax.ShapeDtypeStruct((SUBLANES, n), F32),
        compiler_params=_compiler_params(("arbitrary",)),
        name="adaln_mod",
    )(cvec, ada_w, ada_b)


def _win_kernel(off_a, off_u, w_hbm, o_ref, w_ref, sem):
    rows = w_ref.shape[0]
    chunk = pl.ds(pl.multiple_of(pl.program_id(0) * rows, rows), rows)
    cp = pltpu.make_async_copy(w_hbm.at[0, chunk, :], w_ref, sem)
    cp.start()
    cp.wait()
    w = w_ref[...]
    n_in = w.shape[1]
    o_ref[:, 0:off_a] = w[:, 0:off_a].astype(BF16)
    o_ref[:, off_a:off_a + n_in - off_u] = w[:, off_u:n_in].astype(BF16)
    tail = jnp.concatenate(
        [w[:, off_a:off_u], jnp.zeros((w.shape[0], P_WIDTH - n_in), F32)], axis=1)
    o_ref[:, P_A:P_WIDTH] = tail.astype(BF16)


def _win_layout(w_in, off_a, off_u):
    rows = 256
    kdim, n_in = w_in.shape[1], w_in.shape[2]
    return pl.pallas_call(
        functools.partial(_win_kernel, off_a, off_u),
        grid=(kdim // rows,),
        in_specs=[pl.BlockSpec(memory_space=pl.ANY)],
        out_specs=pl.BlockSpec((rows, P_WIDTH), lambda i: (i, 0)),
        out_shape=jax.ShapeDtypeStruct((kdim, P_WIDTH), BF16),
        scratch_shapes=[pltpu.VMEM((rows, n_in), F32), pltpu.SemaphoreType.DMA],
        compiler_params=_compiler_params(("arbitrary",)),
        name="win_layout",
    )(w_in)


def _pos_kernel(o_ref):
    nf = D_MODEL // 4
    p = lax.broadcasted_iota(I32, (GRID_W, nf), 0).astype(F32)
    i = lax.broadcasted_iota(I32, (GRID_W, nf), 1).astype(F32)
    omega = jnp.exp(i * (-math.log(10000.0) / nf))
    a = p * omega
    o_ref[:, 0:nf] = jnp.sin(a)
    o_ref[:, nf:2 * nf] = jnp.cos(a)


def _pos_table():
    return pl.pallas_call(
        _pos_kernel,
        out_shape=jax.ShapeDtypeStruct((GRID_W, D_MODEL // 2), F32),
        name="sincos_table",
    )()


def _add_pos(x, e_ref, blk, add_pos):
    if not add_pos:
        return x
    half = D_MODEL // 2
    e_all = e_ref[...]
    rows = []
    for j in range(TOKEN_BLOCK // GRID_W):
        xj = x[j * GRID_W:(j + 1) * GRID_W]
        e_row = e_ref[pl.ds(blk * (TOKEN_BLOCK // GRID_W) + j, 1), :]
        rows.append(jnp.concatenate([xj[:, 0:half] + e_row, xj[:, half:] + e_all], axis=1))
    return jnp.concatenate(rows, axis=0)


def _chunk_masks():
    r = lax.broadcasted_iota(I32, (TOKEN_BLOCK, TOKEN_BLOCK), 0)
    c = lax.broadcasted_iota(I32, (TOKEN_BLOCK, TOKEN_BLOCK), 1)
    same = (r // GLA_CHUNK) == (c // GLA_CHUNK)
    return same & (c <= r), same & (c >= r)


def _gla_direction(q, k, v, cum, fwd, att_mask, st_ref, ready, done):
    qe = q * jnp.exp(cum)
    ke = k * jnp.exp(-cum)
    yield
    lane = lax.broadcasted_iota(I32, (1, LANES), 1)
    o_intra = []
    for pair in range(2):
        qp = qe[:, pair * LANES:(pair + 1) * LANES]
        kp = ke[:, pair * LANES:(pair + 1) * LANES]
        for hh in range(2):
            qm = jnp.where((lane // GLA_DK) == hh, qp, 0.0)
            att = jnp.where(att_mask, _dot_nt(qm, kp), 0.0)
            head = 2 * pair + hh
            o_intra.append(_dot(att, v[:, head * GLA_DV:(head + 1) * GLA_DV]))
            yield
    o_intra = jnp.concatenate(o_intra, axis=1)
    while not ready():
        yield

    er = lax.broadcasted_iota(I32, (2 * GLA_DV, 2 * GLA_DK), 0)
    dc = lax.broadcasted_iota(I32, (2 * GLA_DV, 2 * GLA_DK), 1)
    same_head = (er // GLA_DV) == (dc // GLA_DK)
    n_chunks = TOKEN_BLOCK // GLA_CHUNK
    o_inter = [None] * n_chunks
    for c in (range(n_chunks) if fwd else reversed(range(n_chunks))):
        r0 = c * GLA_CHUNK
        rows = slice(r0, r0 + GLA_CHUNK)
        last = cum[r0 + GLA_CHUNK - 1:r0 + GLA_CHUNK] if fwd else cum[r0:r0 + 1]
        kd = k[rows] * jnp.exp(last - cum[rows])
        dec = jnp.exp(last)
        parts = []
        for pair in range(2):
            dl = slice(pair * LANES, (pair + 1) * LANES)
            st = st_ref[pair]
            parts.append(_dot_nt(qe[rows, dl], st))
            ds_t = _dot_tn(v[rows, pair * 2 * GLA_DV:(pair + 1) * 2 * GLA_DV], kd[:, dl])
            st_ref[pair] = dec[:, dl] * st + jnp.where(same_head, ds_t, 0.0)
        o_inter[c] = jnp.concatenate(parts, axis=1)
        yield
    done()
    return o_intra + jnp.concatenate(o_inter, axis=0)


def _interleave(chains):
    chains = list(chains)
    done = [False] * len(chains)
    tick = 0
    while not all(done):
        for i, ch in enumerate(chains):
            if tick >= i and not done[i]:
                try:
                    next(ch)
                except StopIteration:
                    done[i] = True
        tick += 1


def _load_state(s0_ref, u, st_ref):
    zero = jnp.zeros((GLA_DV, GLA_DK), F32)
    for pair in range(2):
        a = s0_ref[u, 2 * pair].T
        b = s0_ref[u, 2 * pair + 1].T
        st_ref[pair] = jnp.concatenate(
            [jnp.concatenate([a, zero], axis=1), jnp.concatenate([zero, b], axis=1)], axis=0)


def _store_state(st_ref, sfin_ref, u):
    for pair in range(2):
        st = st_ref[pair]
        sfin_ref[u, 2 * pair] = st[0:GLA_DV, 0:GLA_DK].T
        sfin_ref[u, 2 * pair + 1] = st[GLA_DV:2 * GLA_DV, GLA_DK:2 * GLA_DK].T


def _mod_row(mod_ref, row, part):
    return mod_ref[pl.ds(row, 1), part * D_MODEL:(part + 1) * D_MODEL]


def _front_stages(xin, row, mod_ref, n1_ref, win_ref, wa_ref, ba_ref, snw_ref, sws_ref, sbs_ref):
    h = _rms(xin, n1_ref[...]) * (1.0 + _mod_row(mod_ref, row, 1)) + _mod_row(mod_ref, row, 0)
    yield
    hb = h.astype(BF16)
    p_parts = []
    for c0, c1 in ((P_Q, P_G), (P_G, P_SV), (P_SV, P_WIDTH)):
        p_parts.append(jnp.dot(hb, win_ref[:, c0:c1], preferred_element_type=F32))
        yield
    p = jnp.concatenate(p_parts, axis=1)
    q = p[:, P_Q:P_K] * (GLA_DK ** -0.5)
    k = p[:, P_K:P_V]
    v = p[:, P_V:P_G]
    z = _dot(p[:, P_A:P_WIDTH], wa_ref[...]) + ba_ref[...]
    la = (jnp.minimum(z, 0.0) - jnp.log1p(jnp.exp(-jnp.abs(z)))) * (1.0 / GLA_GATE_NORM)
    yield

    ug = jax.nn.gelu(p[:, P_U:P_SV])
    yield
    vg = jax.nn.gelu(p[:, P_SV:P_A])
    yield
    s_cols = []
    for gi in range(SGU_GROUPS):
        cols = slice(gi * SGU_CH, (gi + 1) * SGU_CH)
        vn = _rms(vg[:, cols], snw_ref[:, cols])
        rhs = jnp.concatenate([vn[0:SGU_CHUNK], vn[SGU_CHUNK:2 * SGU_CHUNK]], axis=1)
        vm = _dot(sws_ref[gi], rhs) + jnp.concatenate([sbs_ref[gi], sbs_ref[gi]], axis=1)
        vm = jnp.concatenate([vm[:, 0:SGU_CH], vm[:, SGU_CH:2 * SGU_CH]], axis=0)
        s_cols.append(ug[:, cols] * vm)
    yield
    return q, k, v, p[:, P_G:P_U], la, jnp.concatenate(s_cols, axis=1)


def _back_stages(o, g, s_val, xin, row, mod_ref, gnw_ref, wout_ref, n2_ref, rw_ref):
    cols = []
    for head in range(GLA_HEADS):
        hs = slice(head * GLA_DV, (head + 1) * GLA_DV)
        cols.append(_rms(o[:, hs], gnw_ref[...]) * jax.nn.silu(g[:, hs]))
    cols.append(s_val)
    yield
    y = _dot(jnp.concatenate(cols, axis=1), wout_ref[...])
    yield
    x1 = xin + _mod_row(mod_ref, row, 2) * y
    h2 = _rms(x1, n2_ref[...]) * (1.0 + _mod_row(mod_ref, row, 4)) + _mod_row(mod_ref, row, 3)
    yield
    logits = _dot_nt_3pass(rw_ref[...], h2)
    m = jnp.max(logits, axis=0, keepdims=True)
    ex = jnp.exp(logits - m)
    return x1, h2, ex / jnp.sum(ex, axis=0, keepdims=True)


def _store_token_tiles(h2t_ref, u, b, h2):
    for s in range(TILES_PER_TOKEN):
        h2t_ref[u, pl.ds(b * TOKEN_BLOCK * TILES_PER_TOKEN + s, TOKEN_BLOCK, stride=TILES_PER_TOKEN), :] = (
            h2[:, s * LANES:(s + 1) * LANES])


def _zip_stages(gen_a, gen_b):
    out = [None, None]
    live = [gen_a, gen_b]
    while any(g is not None for g in live):
        for i, g in enumerate(live):
            if g is not None:
                try:
                    next(g)
                except StopIteration as stop:
                    out[i] = stop.value
                    live[i] = None
        yield
    return out


def _mixer_fwd_kernel(group, add_pos, mod_base, mod_per_seq, nb,
                      x_ref, e_ref, mod_ref, n1_ref, win_ref, wa_ref, ba_ref,
                      snw_ref, sws_ref, sbs_ref, s0_ref,
                      qkv_ref, g_ref, lab_ref, s_ref, of_ref, sfin_ref,
                      st_ref):
    grp = pl.program_id(0)
    blk = pl.program_id(1)
    sub = x_ref.shape[1] // TOKEN_BLOCK
    lo_mask, _ = _chunk_masks()
    state_done = set()

    @pl.when(blk == 0)
    def _():
        for u in range(group):
            _load_state(s0_ref, u, st_ref.at[u])

    def chain(u, b):
        row = mod_base + ((grp * group + u) if mod_per_seq else 0)
        st_u = st_ref.at[u]
        rows = slice(b * TOKEN_BLOCK, (b + 1) * TOKEN_BLOCK)
        xin = _add_pos(x_ref[u, rows, :], e_ref, blk * sub + b, add_pos)
        q, k, v, g, la, s_val = yield from _front_stages(
            xin, row, mod_ref, n1_ref, win_ref, wa_ref, ba_ref, snw_ref, sws_ref, sbs_ref)
        qkv_ref[u, rows, :] = jnp.concatenate([q, k, v], axis=1)
        g_ref[u, rows, :] = g
        lab_ref[u, rows, :] = la[:, QK_W:2 * QK_W]
        s_ref[u, rows, :] = s_val

        cum = _select_dot(lo_mask, la[:, 0:QK_W])
        yield
        of_ref[u, rows, :] = yield from _gla_direction(
            q, k, v, cum, True, lo_mask, st_u,
            ready=lambda: b == 0 or (u, b - 1) in state_done, done=lambda: state_done.add((u, b)))

    _interleave(chain(u, b) for b in range(sub) for u in range(group))

    @pl.when(blk == nb - 1)
    def _():
        for u in range(group):
            _store_state(st_ref.at[u], sfin_ref, u)


def _mixer_fwd(x, e_tab, mod, n1, win, wa, ba, snw, sws, sbs, s0, *, nseq, nb, add_pos,
               mod_base, mod_per_seq):
    seq_len = nb * TOKEN_BLOCK
    group = min(SEQ_GROUP, nseq)
    sub = min(SEQ_GROUP // group, nb)
    nb //= sub
    tok = lambda w: pl.BlockSpec((group, sub * TOKEN_BLOCK, w), lambda s, i: (s, i, 0))
    full = lambda a: pl.BlockSpec(a.shape, lambda s, i: (0,) * a.ndim)
    st_spec = pl.BlockSpec((group, GLA_HEADS, GLA_DK, GLA_DV), lambda s, i: (s, 0, 0, 0))
    s0_spec = st_spec if s0.shape[0] == nseq else pl.BlockSpec(s0.shape, lambda s, i: (0, 0, 0, 0))
    act = lambda w: jax.ShapeDtypeStruct((nseq, seq_len, w), F32)
    kern = functools.partial(_mixer_fwd_kernel, group, add_pos, mod_base, mod_per_seq, nb)
    return pl.pallas_call(
        kern,
        grid=(nseq // group, nb),
        in_specs=[tok(D_MODEL), full(e_tab), full(mod), full(n1), full(win), full(wa), full(ba),
                  full(snw), full(sws), full(sbs), s0_spec],
        out_specs=[tok(1024), tok(GLA_WIDTH), tok(QK_W), tok(SGU_WIDTH), tok(GLA_WIDTH), st_spec],
        out_shape=[act(1024), act(GLA_WIDTH), act(QK_W), act(SGU_WIDTH), act(GLA_WIDTH),
                   jax.ShapeDtypeStruct((nseq, GLA_HEADS, GLA_DK, GLA_DV), F32)],
        scratch_shapes=[pltpu.VMEM((group, 2, 2 * GLA_DV, 2 * GLA_DK), F32)],
        compiler_params=_compiler_params(("arbitrary", "arbitrary")),
        name="mixer_fwd",
    )(x, e_tab, mod, n1, win, wa, ba, snw, sws, sbs, s0)


def _mixer_bwd_kernel(group, add_pos, mod_base, mod_per_seq, nb,
                      x_ref, e_ref, mod_ref, qkv_ref, g_ref, lab_ref, s_ref, of_ref,
                      gnw_ref, wout_ref, n2_ref, rw_ref, s0_ref,
                      x1_ref, h2t_ref, probs_ref, sfin_ref,
                      st_ref):
    grp = pl.program_id(0)
    step = pl.program_id(1)
    sub = x_ref.shape[1] // TOKEN_BLOCK
    blk = nb - 1 - step
    _, hi_mask = _chunk_masks()
    state_done = set()

    @pl.when(step == 0)
    def _():
        for u in range(group):
            _load_state(s0_ref, u, st_ref.at[u])

    def chain(u, b):
        row = mod_base + ((grp * group + u) if mod_per_seq else 0)
        st_u = st_ref.at[u]
        rows = slice(b * TOKEN_BLOCK, (b + 1) * TOKEN_BLOCK)
        qkv = qkv_ref[u, rows, :]
        q, k, v = qkv[:, 0:QK_W], qkv[:, QK_W:2 * QK_W], qkv[:, 2 * QK_W:]
        cum = _select_dot(hi_mask, lab_ref[u, rows, :])
        yield
        o_b = yield from _gla_direction(
            q, k, v, cum, False, hi_mask, st_u,
            ready=lambda: b == sub - 1 or (u, b + 1) in state_done,
            done=lambda: state_done.add((u, b)))
        xin = _add_pos(x_ref[u, rows, :], e_ref, blk * sub + b, add_pos)
        x1, h2, probs = yield from _back_stages(
            of_ref[u, rows, :] + o_b, g_ref[u, rows, :], s_ref[u, rows, :], xin, row,
            mod_ref, gnw_ref, wout_ref, n2_ref, rw_ref)
        x1_ref[u, rows, :] = x1
        _store_token_tiles(h2t_ref, u, b, h2)
        probs_ref[u, b] = probs

    _interleave(chain(u, b) for b in reversed(range(sub)) for u in range(group))

    @pl.when(step == nb - 1)
    def _():
        for u in range(group):
            _store_state(st_ref.at[u], sfin_ref, u)


def _mixer_bwd(x, e_tab, mod, qkv, g, lab, s, of, gnw, wout, n2, rw_t, s0, *, nseq, nb, add_pos,
               mod_base, mod_per_seq):
    seq_len = nb * TOKEN_BLOCK
    group = min(SEQ_GROUP, nseq)
    sub = min(SEQ_GROUP // group, nb)
    nb //= sub
    tok = lambda w: pl.BlockSpec((group, sub * TOKEN_BLOCK, w), lambda s_, i: (s_, nb - 1 - i, 0))
    full = lambda a: pl.BlockSpec(a.shape, lambda s_, i: (0,) * a.ndim)
    st_spec = pl.BlockSpec((group, GLA_HEADS, GLA_DK, GLA_DV), lambda s_, i: (s_, 0, 0, 0))
    s0_spec = st_spec if s0.shape[0] == nseq else pl.BlockSpec(s0.shape, lambda s_, i: (0, 0, 0, 0))
    kern = functools.partial(_mixer_bwd_kernel, group, add_pos, mod_base, mod_per_seq, nb)
    return pl.pallas_call(
        kern,
        grid=(nseq // group, nb),
        in_specs=[tok(D_MODEL), full(e_tab), full(mod), tok(1024), tok(GLA_WIDTH), tok(QK_W),
                  tok(SGU_WIDTH), tok(GLA_WIDTH), full(gnw), full(wout), full(n2), full(rw_t),
                  s0_spec],
        out_specs=[tok(D_MODEL),
                   pl.BlockSpec((group, sub * TOKEN_BLOCK * TILES_PER_TOKEN, LANES),
                                lambda s_, i: (s_, nb - 1 - i, 0)),
                   pl.BlockSpec((group, sub, N_EXPERTS, TOKEN_BLOCK),
                                lambda s_, i: (s_, nb - 1 - i, 0, 0)),
                   st_spec],
        out_shape=[jax.ShapeDtypeStruct((nseq, seq_len, D_MODEL), F32),
                   jax.ShapeDtypeStruct((nseq, seq_len * TILES_PER_TOKEN, LANES), F32),
                   jax.ShapeDtypeStruct((nseq, nb * sub, N_EXPERTS, TOKEN_BLOCK), F32),
                   jax.ShapeDtypeStruct((nseq, GLA_HEADS, GLA_DK, GLA_DV), F32)],
        scratch_shapes=[pltpu.VMEM((group, 2, 2 * GLA_DV, 2 * GLA_DK), F32)],
        compiler_params=_compiler_params(("arbitrary", "arbitrary")),
        name="mixer_bwd",
    )(x, e_tab, mod, qkv, g, lab, s, of, gnw, wout, n2, rw_t, s0)


def _mixer_fused_kernel(group, mod_base, mod_per_seq,
                        x_ref, mod_ref, n1_ref, win_ref, wa_ref, ba_ref, snw_ref, sws_ref, sbs_ref,
                        gnw_ref, wout_ref, n2_ref, rw_ref, s0f_ref, s0b_ref,
                        x1_ref, h2t_ref, probs_ref, sfin_f_ref, sfin_b_ref,
                        stf_ref, stb_ref):
    grp = pl.program_id(0)
    lo_mask, hi_mask = _chunk_masks()
    always = lambda: True
    nothing = lambda: None
    for u in range(group):
        _load_state(s0f_ref, u, stf_ref.at[u])
        _load_state(s0b_ref, u, stb_ref.at[u])

    def chain(u):
        row = mod_base + ((grp * group + u) if mod_per_seq else 0)
        xin = x_ref[u]
        q, k, v, g, la, s_val = yield from _front_stages(
            xin, row, mod_ref, n1_ref, win_ref, wa_ref, ba_ref, snw_ref, sws_ref, sbs_ref)
        cum_f = _select_dot(lo_mask, la[:, 0:QK_W])
        cum_b = _select_dot(hi_mask, la[:, QK_W:2 * QK_W])
        yield
        o_f, o_b = yield from _zip_stages(
            _gla_direction(q, k, v, cum_f, True, lo_mask, stf_ref.at[u], always, nothing),
            _gla_direction(q, k, v, cum_b, False, hi_mask, stb_ref.at[u], always, nothing))
        x1, h2, probs = yield from _back_stages(
            o_f + o_b, g, s_val, xin, row, mod_ref, gnw_ref, wout_ref, n2_ref, rw_ref)
        x1_ref[u] = x1
        _store_token_tiles(h2t_ref, u, 0, h2)
        probs_ref[u, 0] = probs

    _interleave(chain(u) for u in range(group))
    for u in range(group):
        _store_state(stf_ref.at[u], sfin_f_ref, u)
        _store_state(stb_ref.at[u], sfin_b_ref, u)


def _mixer_fused(x, mod, prm, s0_f, s0_b, *, nseq, mod_base, mod_per_seq):
    group = min(SEQ_GROUP, nseq)
    consts = [mod, prm["n1"], prm["win"], prm["wa"], prm["ba"], prm["snw"], prm["sws"], prm["sbs"],
              prm["gnw"], prm["wout"], prm["n2"], prm["rw_t"]]
    tok = lambda w: pl.BlockSpec((group, TOKEN_BLOCK, w), lambda s: (s, 0, 0))
    full = lambda a: pl.BlockSpec(a.shape, lambda s: (0,) * a.ndim)
    st_spec = pl.BlockSpec((group, GLA_HEADS, GLA_DK, GLA_DV), lambda s: (s, 0, 0, 0))
    s0_spec = lambda a: st_spec if a.shape[0] == nseq else pl.BlockSpec(a.shape, lambda s: (0, 0, 0, 0))
    state = jax.ShapeDtypeStruct((nseq, GLA_HEADS, GLA_DK, GLA_DV), F32)
    st_scratch = pltpu.VMEM((group, 2, 2 * GLA_DV, 2 * GLA_DK), F32)
    return pl.pallas_call(
        functools.partial(_mixer_fused_kernel, group, mod_base, mod_per_seq),
        grid=(nseq // group,),
        in_specs=[tok(D_MODEL)] + [full(a) for a in consts] + [s0_spec(s0_f), s0_spec(s0_b)],
        out_specs=[tok(D_MODEL),
                   pl.BlockSpec((group, TOKEN_BLOCK * TILES_PER_TOKEN, LANES), lambda s: (s, 0, 0)),
                   pl.BlockSpec((group, 1, N_EXPERTS, TOKEN_BLOCK), lambda s: (s, 0, 0, 0)),
                   st_spec, st_spec],
        out_shape=[jax.ShapeDtypeStruct((nseq, TOKEN_BLOCK, D_MODEL), F32),
                   jax.ShapeDtypeStruct((nseq, TOKEN_BLOCK * TILES_PER_TOKEN, LANES), F32),
                   jax.ShapeDtypeStruct((nseq, 1, N_EXPERTS, TOKEN_BLOCK), F32),
                   state, state],
        scratch_shapes=[st_scratch, st_scratch],
        compiler_params=_compiler_params(("arbitrary",)),
        name="mixer_fused",
    )(x, *consts, s0_f, s0_b)


def _route_kernel(n_tok, cap, probs_ref, row_ref, gate_ref, xs_ref, ps_ref):
    n_blk = n_tok // TOKEN_BLOCK
    n_chunk = n_tok // LANES
    probs = jnp.concatenate([probs_ref[b] for b in range(n_blk)], axis=1)
    capf = jnp.float32(cap)

    def count(mask):
        return jnp.sum(mask.astype(F32), axis=1, keepdims=True)

    def as_f32(bits):
        return lax.bitcast_convert_type(bits, F32)

    def thr_step(_, lohi):
        lo, hi = lohi
        mid = lo + ((hi - lo + 1) >> 1)
        ok = count(probs >= as_f32(mid)) >= capf
        return jnp.where(ok, mid, lo), jnp.where(ok, hi, mid - 1)

    lo0 = jnp.zeros((N_EXPERTS, 1), I32)
    hi0 = jnp.full((N_EXPERTS, 1), 0x3F800000, I32)
    thr, _ = lax.fori_loop(0, 31, thr_step, (lo0, hi0))
    gt = probs >= as_f32(thr + 1)
    eq = (probs >= as_f32(thr)) & jnp.logical_not(gt)
    need = capf - count(gt)
    tok = lax.broadcasted_iota(I32, (N_EXPERTS, n_tok), 1)

    def tie_step(_, lohi):
        lo, hi = lohi
        mid = (lo + hi) >> 1
        ok = count(eq & (tok <= mid)) >= need
        return jnp.where(ok, lo, mid + 1), jnp.where(ok, mid, hi)

    n_bits = max(1, (n_tok - 1).bit_length())
    cut, _ = lax.fori_loop(0, n_bits, tie_step,
                           (jnp.zeros((N_EXPERTS, 1), I32), jnp.full((N_EXPERTS, 1), n_tok - 1, I32)))
    sel = (gt | (eq & (tok <= cut))).astype(F32)

    xs_ref[...] = jnp.concatenate([sel[:, c * LANES:(c + 1) * LANES] for c in range(n_chunk)], axis=0)
    ps_ref[...] = jnp.concatenate([probs[:, c * LANES:(c + 1) * LANES] for c in range(n_chunk)], axis=0)

    li = lax.broadcasted_iota(I32, (LANES, LANES), 0)
    lj = lax.broadcasted_iota(I32, (LANES, LANES), 1)
    upper = (li <= lj).astype(F32)
    ci = lax.broadcasted_iota(I32, (n_chunk, n_chunk), 0)
    cj = lax.broadcasted_iota(I32, (n_chunk, n_chunk), 1)
    lower = (cj <= ci).astype(F32)
    slot = lax.broadcasted_iota(I32, (1, cap), 1).astype(F32)
    chunk_id = lax.broadcasted_iota(I32, (n_chunk, cap), 0).astype(F32)
    lane_id = lax.broadcasted_iota(I32, (LANES, cap), 0).astype(F32)
    reps = cap // LANES

    def per_expert(e):
        x = xs_ref[pl.ds(e, n_chunk, stride=N_EXPERTS), :]
        pe = ps_ref[pl.ds(e, n_chunk, stride=N_EXPERTS), :]
        ploc = _dot(x, upper)
        tot = jnp.broadcast_to(ploc[:, LANES - 1:LANES], (n_chunk, LANES))
        cum = _dot(lower, tot)
        yield
        cum_w = jnp.concatenate([cum] * reps, axis=1)
        base_w = jnp.concatenate([cum - tot] * reps, axis=1)
        chunk_of = jnp.sum((cum_w <= slot).astype(F32), axis=0, keepdims=True)
        onehot = chunk_id == chunk_of
        local = slot - jnp.sum(jnp.where(onehot, base_w, 0.0), axis=0, keepdims=True)
        yield
        lhs = jnp.concatenate([ploc.astype(BF16)] + _split_bf16(pe, 3), axis=1)
        got = _dot_tn(lhs, onehot.astype(F32))
        yield
        pref = got[0:LANES]
        lane_of = jnp.sum((pref <= local).astype(F32), axis=0, keepdims=True)
        token = chunk_of * LANES + lane_of
        row_ref[pl.ds(e, 1), :] = (token * TILES_PER_TOKEN).astype(I32)
        yield
        pg = (got[3 * LANES:4 * LANES] + got[2 * LANES:3 * LANES]) + got[LANES:2 * LANES]
        gate_ref[pl.ds(e, 1), :] = jnp.sum(jnp.where(lane_id == lane_of, pg, 0.0), axis=0, keepdims=True)

    def expert_pair(i, _):
        _interleave(per_expert(2 * i + u) for u in range(2))
        return 0

    lax.fori_loop(0, N_EXPERTS // 2, expert_pair, 0)


def _route(probs, n_tok, cap):
    return pl.pallas_call(
        functools.partial(_route_kernel, n_tok, cap),
        out_shape=[jax.ShapeDtypeStruct((N_EXPERTS, cap), I32),
                   jax.ShapeDtypeStruct((N_EXPERTS, cap), F32)],
        scratch_shapes=[pltpu.VMEM((n_tok // LANES * N_EXPERTS, LANES), F32),
                        pltpu.VMEM((n_tok // LANES * N_EXPERTS, LANES), F32)],
        compiler_params=pltpu.CompilerParams(vmem_limit_bytes=VMEM_LIMIT),
        name="route_topk",
    )(probs)


def _expert_kernel(cap, row_ref, h2t_ref, w1_ref, w3_ref, w2_ref, ye_ref,
                   xe_ref, x2_ref, hid_ref, sem):
    e = pl.program_id(0)
    f = pl.program_id(1)
    slot = e % 2
    rows_per_step = cap // EXPERT_STEPS

    def start_row(expert, buf, j):
        src = pl.multiple_of(row_ref[expert * cap + j], TILES_PER_TOKEN)
        dst = pl.multiple_of(j * TILES_PER_TOKEN, TILES_PER_TOKEN)
        pltpu.make_async_copy(h2t_ref.at[pl.ds(src, TILES_PER_TOKEN), :],
                              xe_ref.at[buf, pl.ds(dst, TILES_PER_TOKEN), :], sem.at[buf]).start()

    def wait_rows(buf):
        pltpu.make_async_copy(h2t_ref.at[pl.ds(0, cap * TILES_PER_TOKEN), :], xe_ref.at[buf],
                              sem.at[buf]).wait()

    def prefetch_next():
        nxt = jnp.minimum(e + 1, N_EXPERTS - 1)
        first = f * rows_per_step
        for j in range(rows_per_step):
            start_row(nxt, 1 - slot, first + j)

    @pl.when((e == 0) & (f == 0))
    def _():
        def issue(j, _):
            start_row(0, 0, j)
            return 0

        lax.fori_loop(0, cap, issue, 0, unroll=8)

    @pl.when(f == 0)
    def _():
        wait_rows(slot)
        for s in range(TILES_PER_TOKEN):
            x2_ref[:, s * LANES:(s + 1) * LANES] = (
                xe_ref[slot, pl.ds(s, cap, stride=TILES_PER_TOKEN), :].astype(BF16))

    @pl.when(f < N_HID_STEPS)
    def _():
        prefetch_next()
        x2 = x2_ref[...]
        a = jnp.dot(x2, w1_ref[0].astype(BF16), preferred_element_type=F32)
        b = jnp.dot(x2, w3_ref[0].astype(BF16), preferred_element_type=F32)
        hid_ref[f] = (jax.nn.silu(a) * b).astype(BF16)

    @pl.when(f >= N_HID_STEPS)
    def _():
        prefetch_next()
        w2 = w2_ref[0].astype(BF16)
        out = jnp.dot(hid_ref[0], w2[0:EXPERT_F_BLOCK], preferred_element_type=F32)
        for kb in range(1, N_HID_STEPS):
            out += jnp.dot(hid_ref[kb], w2[kb * EXPERT_F_BLOCK:(kb + 1) * EXPERT_F_BLOCK],
                           preferred_element_type=F32)
        tile0 = (f - N_HID_STEPS) * (EXPERT_N_BLOCK // LANES)
        for i in range(EXPERT_N_BLOCK // LANES):
            ye_ref[pl.ds(tile0 + i, cap, stride=TILES_PER_TOKEN), :] = out[:, i * LANES:(i + 1) * LANES]

    @pl.when((e == N_EXPERTS - 1) & (f == EXPERT_STEPS - 1))
    def _():
        wait_rows(1 - slot)


def _experts(rows, h2t, w1, w3, w2, cap):
    hid_blk = lambda e, f, idx: (e, 0, jnp.minimum(f, N_HID_STEPS - 1))
    n_out = D_MODEL // EXPERT_N_BLOCK

    def out_blk(e, f, idx):
        hold = (f == 0) & (e > 0)
        return (jnp.where(hold, e - 1, e), 0,
                jnp.where(hold, n_out - 1, jnp.maximum(f - N_HID_STEPS, 0)))

    grid_spec = pltpu.PrefetchScalarGridSpec(
        num_scalar_prefetch=1,
        grid=(N_EXPERTS, EXPERT_STEPS),
        in_specs=[pl.BlockSpec(memory_space=pl.ANY),
                  pl.BlockSpec((1, D_MODEL, EXPERT_F_BLOCK), hid_blk),
                  pl.BlockSpec((1, D_MODEL, EXPERT_F_BLOCK), hid_blk),
                  pl.BlockSpec((1, D_EXPERT, EXPERT_N_BLOCK), out_blk)],
        out_specs=pl.BlockSpec((cap * TILES_PER_TOKEN, LANES), lambda e, f, idx: (e, 0)),
        scratch_shapes=[pltpu.VMEM((2, cap * TILES_PER_TOKEN, LANES), F32),
                        pltpu.VMEM((cap, D_MODEL), BF16),
                        pltpu.VMEM((N_HID_STEPS, cap, EXPERT_F_BLOCK), BF16),
                        pltpu.SemaphoreType.DMA((2,))],
    )
    return pl.pallas_call(
        functools.partial(_expert_kernel, cap),
        grid_spec=grid_spec,
        out_shape=jax.ShapeDtypeStruct((N_EXPERTS * cap * TILES_PER_TOKEN, LANES), F32),
        compiler_params=_compiler_params(("arbitrary", "arbitrary")),
        name="expert_swiglu",
    )(rows, h2t, w1, w3, w2)


COMBINE_BATCH = 8
ZERO_ROWS = 512


def _combine_kernel(cap, n_tok, mod_base, mod_per_seq, nb,
                    row_ref, gate_ref, ye_ref, x1_ref, mod_ref, fw_ref, y_ref,
                    acc_ref, xbuf_ref, ybuf_ref, sem_x, sem_y):
    e = pl.program_id(0)

    @pl.when(e == 0)
    def _():
        def zero(i, _):
            r = pl.multiple_of(i * ZERO_ROWS, ZERO_ROWS)
            acc_ref[pl.ds(r, ZERO_ROWS), :] = jnp.zeros((ZERO_ROWS, LANES), F32)
            return 0

        lax.fori_loop(0, n_tok * TILES_PER_TOKEN // ZERO_ROWS, zero, 0)

    def batch(jb, _):
        vals = []
        for u in range(COMBINE_BATCH):
            j = jb * COMBINE_BATCH + u
            t = pl.multiple_of(row_ref[e * cap + j], TILES_PER_TOKEN)
            gate = gate_ref[e * cap + j]
            src = pl.multiple_of(j * TILES_PER_TOKEN, TILES_PER_TOKEN)
            vals.append((t, acc_ref[pl.ds(t, TILES_PER_TOKEN), :]
                         + ye_ref[pl.ds(src, TILES_PER_TOKEN), :] * gate))
        for t, val in vals:
            acc_ref[pl.ds(t, TILES_PER_TOKEN), :] = val
        return 0

    lax.fori_loop(0, cap // COMBINE_BATCH, batch, 0)

    @pl.when(e == N_EXPERTS - 1)
    def _():
        n_blk = n_tok // TOKEN_BLOCK

        def x1_copy(b, slot):
            rows = pl.ds(pl.multiple_of(b * TOKEN_BLOCK, TOKEN_BLOCK), TOKEN_BLOCK)
            return pltpu.make_async_copy(x1_ref.at[rows, :], xbuf_ref.at[slot], sem_x.at[slot])

        def y_copy(b, slot):
            rows = pl.ds(pl.multiple_of(b * TOKEN_BLOCK, TOKEN_BLOCK), TOKEN_BLOCK)
            return pltpu.make_async_copy(ybuf_ref.at[slot], y_ref.at[rows, :], sem_y.at[slot])

        x1_copy(0, 0).start()

        def block(b, _):
            slot = b % 2

            @pl.when(b + 1 < n_blk)
            def _():
                x1_copy(b + 1, 1 - slot).start()

            x1_copy(b, slot).wait()

            @pl.when(b >= 2)
            def _():
                y_copy(b - 2, slot).wait()

            base = pl.multiple_of(b * (TOKEN_BLOCK * TILES_PER_TOKEN), TOKEN_BLOCK * TILES_PER_TOKEN)
            moe = jnp.concatenate(
                [acc_ref[pl.ds(base + s, TOKEN_BLOCK, stride=TILES_PER_TOKEN), :]
                 for s in range(TILES_PER_TOKEN)], axis=1)
            row = mod_base + ((b // nb) if mod_per_seq else 0)
            ybuf_ref[slot] = _rms(xbuf_ref[slot] + _mod_row(mod_ref, row, 5) * moe, fw_ref[...])
            y_copy(b, slot).start()
            return 0

        lax.fori_loop(0, n_blk, block, 0)
        y_copy(n_blk - 2, n_blk % 2).wait()
        y_copy(n_blk - 1, (n_blk - 1) % 2).wait()


def _combine_final(rows, gates, ye, x1, mod, fw, cap, n_tok, *, nb, mod_base, mod_per_seq):
    full = lambda a: pl.BlockSpec(a.shape, lambda e, rows, gate: (0,) * a.ndim)
    grid_spec = pltpu.PrefetchScalarGridSpec(
        num_scalar_prefetch=2,
        grid=(N_EXPERTS,),
        in_specs=[pl.BlockSpec((cap * TILES_PER_TOKEN, LANES), lambda e, rows, gate: (e, 0)),
                  pl.BlockSpec(memory_space=pl.ANY), full(mod), full(fw)],
        out_specs=pl.BlockSpec(memory_space=pl.ANY),
        scratch_shapes=[pltpu.VMEM((n_tok * TILES_PER_TOKEN, LANES), F32),
                        pltpu.VMEM((2, TOKEN_BLOCK, D_MODEL), F32),
                        pltpu.VMEM((2, TOKEN_BLOCK, D_MODEL), F32),
                        pltpu.SemaphoreType.DMA((2,)),
                        pltpu.SemaphoreType.DMA((2,))],
    )
    return pl.pallas_call(
        functools.partial(_combine_kernel, cap, n_tok, mod_base, mod_per_seq, nb),
        grid_spec=grid_spec,
        out_shape=jax.ShapeDtypeStruct((n_tok, D_MODEL), F32),
        compiler_params=_compiler_params(("arbitrary",)),
        name="moe_combine_norm",
    )(rows, gates, ye, x1, mod, fw)


def _trunk_and_norm(x, e_tab, mod, s0_f, s0_b, prm, *, nseq, seq_len, add_pos, mod_base, mod_per_seq):
    nb = seq_len // TOKEN_BLOCK
    n_tok = nseq * seq_len
    cap = EC_CAPACITY_FACTOR * n_tok // N_EXPERTS
    kw = dict(nseq=nseq, nb=nb, add_pos=add_pos, mod_base=mod_base, mod_per_seq=mod_per_seq)
    if nb == 1 and not add_pos:
        x1, h2t, probs, sfin_f, sfin_b = _mixer_fused(
            x, mod, prm, s0_f, s0_b, nseq=nseq, mod_base=mod_base, mod_per_seq=mod_per_seq)
    else:
        qkv, g, lab, s, of, sfin_f = _mixer_fwd(
            x, e_tab, mod, prm["n1"], prm["win"], prm["wa"], prm["ba"], prm["snw"], prm["sws"],
            prm["sbs"], s0_f, **kw)
        x1, h2t, probs, sfin_b = _mixer_bwd(
            x, e_tab, mod, qkv, g, lab, s, of, prm["gnw"], prm["wout"], prm["n2"], prm["rw_t"],
            s0_b, **kw)
    x1 = x1.reshape(n_tok, D_MODEL)
    h2t = h2t.reshape(n_tok * TILES_PER_TOKEN, LANES)
    probs = probs.reshape(n_tok // TOKEN_BLOCK, N_EXPERTS, TOKEN_BLOCK)
    rows, gates = _route(probs, n_tok, cap)
    rows, gates = rows.reshape(-1), gates.reshape(-1)
    ye = _experts(rows, h2t, prm["w1"], prm["w3"], prm["w2"], cap)
    y = _combine_final(rows, gates, ye, x1, mod, prm["fw"], cap, n_tok,
                       nb=nb, mod_base=mod_base, mod_per_seq=mod_per_seq)
    return y.reshape(nseq, seq_len, D_MODEL), sfin_f, sfin_b


def kernel(x_prompt, x_sample, state_gla_fwd, state_gla_bwd, c, c_ctx, ada_w, ada_b, norm1_w, w_in, gla_wa2_f, gla_ba_f, gla_wa2_b, gla_ba_b, gla_norm_w, sgu_norm_w, sgu_ws, sgu_bs, w_out, norm2_w, router_w, exp_w1, exp_w3, exp_w2, final_norm_w):
    assert ada_w.shape[0] == 1, "single trunk layer"
    batch, seq, _ = x_prompt.shape
    dec_batch, dec_seq, _ = x_sample.shape

    off_af = 2 * QK_W + 2 * GLA_WIDTH
    off_u = off_af + 2 * GLA_LOWRANK
    win = _win_layout(w_in, off_af, off_u)
    wa = jnp.zeros((P_WIDTH - P_A, 2 * QK_W), F32)
    wa = wa.at[0:GLA_LOWRANK, 0:QK_W].set(gla_wa2_f[0])
    wa = wa.at[GLA_LOWRANK:2 * GLA_LOWRANK, QK_W:].set(gla_wa2_b[0]).astype(BF16)
    prm = dict(
        n1=norm1_w, win=win, wa=wa,
        ba=jnp.concatenate([gla_ba_f[0], gla_ba_b[0]])[None, :],
        snw=sgu_norm_w, sws=sgu_ws[0].astype(BF16),
        sbs=jnp.broadcast_to(sgu_bs[0][:, :, None], (SGU_GROUPS, SGU_CHUNK, SGU_CH)),
        gnw=gla_norm_w, wout=w_out[0].astype(BF16), n2=norm2_w, rw_t=router_w[0].T,
        w1=exp_w1[0], w3=exp_w3[0], w2=exp_w2[0], fw=final_norm_w[None, :])

    cvec = jnp.concatenate([c_ctx[None, :], c, jnp.zeros((SUBLANES - 1 - dec_batch, D_MODEL), F32)])
    mod = _modulation(cvec, ada_w[0], ada_b)
    e_tab = _pos_table()

    zero_state = jnp.zeros((min(SEQ_GROUP, batch), GLA_HEADS, GLA_DK, GLA_DV), F32)
    y_prompt, sf, sb = _trunk_and_norm(
        x_prompt, e_tab, mod, zero_state, zero_state, prm,
        nseq=batch, seq_len=seq, add_pos=False, mod_base=0, mod_per_seq=False)
    y_sample, _, _ = _trunk_and_norm(
        x_sample, e_tab, mod, state_gla_fwd[:, 0], state_gla_bwd[:, 0], prm,
        nseq=dec_batch, seq_len=dec_seq, add_pos=True, mod_base=1, mod_per_seq=True)
    return (y_prompt, y_sample, sf[:, None], sb[:, None])
```

```python
import functools
import math

import jax
import jax.numpy as jnp
from jax import lax
from jax.experimental import pallas as pl
from jax.experimental.pallas import tpu as pltpu

F32 = jnp.float32
BF16 = jnp.bfloat16
I32 = jnp.int32

D_MODEL = 1024
GRID_W = 64
GLA_HEADS = 4
GLA_DK = 64
GLA_DV = 128
GLA_WIDTH = GLA_HEADS * GLA_DV
QK_W = GLA_HEADS * GLA_DK
GLA_LOWRANK = 16
GLA_GATE_NORM = 16.0
GLA_CHUNK = 64
SGU_WIDTH = 512
SGU_GROUPS = 4
SGU_CH = 128
SGU_CHUNK = 128
N_EXPERTS = 16
EC_CAPACITY_FACTOR = 2
D_EXPERT = 2048
EPS = 1e-6

SUBLANES = 8
LANES = 128
TILES_PER_TOKEN = D_MODEL // LANES

TOKEN_BLOCK = 256
SEQ_GROUP = 4
P_Q, P_K, P_V, P_G, P_U, P_SV, P_A = 0, 256, 512, 1024, 1536, 2048, 2560
P_WIDTH = 2688
EXPERT_F_BLOCK = 512
EXPERT_N_BLOCK = 256
N_HID_STEPS = D_EXPERT // EXPERT_F_BLOCK
EXPERT_STEPS = N_HID_STEPS + D_MODEL // EXPERT_N_BLOCK
VMEM_LIMIT = 56 * 1024 * 1024


def _dot(a, b):
    return jnp.dot(a.astype(BF16), b.astype(BF16), preferred_element_type=F32)


def _dot_nt(a, b):
    return lax.dot_general(a.astype(BF16), b.astype(BF16), (((1,), (1,)), ((), ())),
                           preferred_element_type=F32)


def _dot_tn(a, b):
    return lax.dot_general(a.astype(BF16), b.astype(BF16), (((0,), (0,)), ((), ())),
                           preferred_element_type=F32)


def _dot_f32(a, b, dims=(((1,), (0,)), ((), ()))):
    return lax.dot_general(a, b, dims, precision=lax.Precision.HIGHEST, preferred_element_type=F32)


def _split_bf16(x, terms):
    parts = []
    for _ in range(terms - 1):
        part = x.astype(BF16)
        parts.append(part)
        x = x - part.astype(F32)
    parts.append(x.astype(BF16))
    return parts


def _select_dot(sel, x):
    s = sel.astype(BF16)
    hi, mid, lo = _split_bf16(x, 3)
    return (jnp.dot(s, lo, preferred_element_type=F32) + jnp.dot(s, mid, preferred_element_type=F32)
            + jnp.dot(s, hi, preferred_element_type=F32))


def _dot_nt_3pass(a, b):
    a_hi, a_lo = _split_bf16(a, 2)
    b_hi, b_lo = _split_bf16(b, 2)
    nt = lambda x, y: lax.dot_general(x, y, (((1,), (1,)), ((), ())), preferred_element_type=F32)
    return (nt(a_hi, b_lo) + nt(a_lo, b_hi)) + nt(a_hi, b_hi)


def _rms(x, w):
    return x * lax.rsqrt(jnp.mean(x * x, axis=-1, keepdims=True) + EPS) * w


def _compiler_params(sem):
    return pltpu.CompilerParams(dimension_semantics=sem, vmem_limit_bytes=VMEM_LIMIT)


def _mod_kernel(c_ref, w_ref, b_ref, o_ref):
    o_ref[...] = _dot(jax.nn.silu(c_ref[...]), w_ref[...]) + b_ref[...]


def _modulation(cvec, ada_w, ada_b):
    n = ada_w.shape[1]
    bn = 1536
    return pl.pallas_call(
        _mod_kernel,
        grid=(n // bn,),
        in_specs=[pl.BlockSpec((SUBLANES, D_MODEL), lambda j: (0, 0)),
                  pl.BlockSpec((D_MODEL, bn), lambda j: (0, j)),
                  pl.BlockSpec((1, bn), lambda j: (0, j))],
        out_specs=pl.BlockSpec((SUBLANES, bn), lambda j: (0, j)),
        out_shape=jax.ShapeDtypeStruct((SUBLANES, n), F32),
        compiler_params=_compiler_params(("arbitrary",)),
        name="adaln_mod",
    )(cvec, ada_w, ada_b)


def _win_kernel(off_a, off_u, w_ref, o_ref):
    w = w_ref[...]
    n_in = w.shape[1]
    o_ref[:, 0:off_a] = w[:, 0:off_a].astype(BF16)
    o_ref[:, off_a:off_a + n_in - off_u] = w[:, off_u:n_in].astype(BF16)
    tail = jnp.concatenate(
        [w[:, off_a:off_u], jnp.zeros((w.shape[0], P_WIDTH - n_in), F32)], axis=1)
    o_ref[:, P_A:P_WIDTH] = tail.astype(BF16)


def _win_layout(w_in, off_a, off_u):
    rows = 256
    kdim, n_in = w_in.shape
    return pl.pallas_call(
        functools.partial(_win_kernel, off_a, off_u),
        grid=(kdim // rows,),
        in_specs=[pl.BlockSpec((rows, n_in), lambda i: (i, 0))],
        out_specs=pl.BlockSpec((rows, P_WIDTH), lambda i: (i, 0)),
        out_shape=jax.ShapeDtypeStruct((kdim, P_WIDTH), BF16),
        compiler_params=_compiler_params(("arbitrary",)),
        name="win_layout",
    )(w_in)


def _pos_kernel(o_ref):
    nf = D_MODEL // 4
    p = lax.broadcasted_iota(I32, (GRID_W, nf), 0).astype(F32)
    i = lax.broadcasted_iota(I32, (GRID_W, nf), 1).astype(F32)
    omega = jnp.exp(i * (-math.log(10000.0) / nf))
    a = p * omega
    o_ref[:, 0:nf] = jnp.sin(a)
    o_ref[:, nf:2 * nf] = jnp.cos(a)


def _pos_table():
    return pl.pallas_call(
        _pos_kernel,
        out_shape=jax.ShapeDtypeStruct((GRID_W, D_MODEL // 2), F32),
        name="sincos_table",
    )()


def _add_pos(x, e_ref, blk, add_pos):
    if not add_pos:
        return x
    half = D_MODEL // 2
    e_all = e_ref[...]
    rows = []
    for j in range(TOKEN_BLOCK // GRID_W):
        xj = x[j * GRID_W:(j + 1) * GRID_W]
        e_row = e_ref[pl.ds(blk * (TOKEN_BLOCK // GRID_W) + j, 1), :]
        rows.append(jnp.concatenate([xj[:, 0:half] + e_row, xj[:, half:] + e_all], axis=1))
    return jnp.concatenate(rows, axis=0)


def _chunk_masks():
    r = lax.broadcasted_iota(I32, (TOKEN_BLOCK, TOKEN_BLOCK), 0)
    c = lax.broadcasted_iota(I32, (TOKEN_BLOCK, TOKEN_BLOCK), 1)
    same = (r // GLA_CHUNK) == (c // GLA_CHUNK)
    return same & (c <= r), same & (c >= r)


def _gla_direction(q, k, v, cum, fwd, att_mask, st_ref, ready, done):
    qe = q * jnp.exp(cum)
    ke = k * jnp.exp(-cum)
    yield
    lane = lax.broadcasted_iota(I32, (1, LANES), 1)
    o_intra = []
    for pair in range(2):
        qp = qe[:, pair * LANES:(pair + 1) * LANES]
        kp = ke[:, pair * LANES:(pair + 1) * LANES]
        for hh in range(2):
            qm = jnp.where((lane // GLA_DK) == hh, qp, 0.0)
            att = jnp.where(att_mask, _dot_nt(qm, kp), 0.0)
            head = 2 * pair + hh
            o_intra.append(_dot(att, v[:, head * GLA_DV:(head + 1) * GLA_DV]))
            yield
    o_intra = jnp.concatenate(o_intra, axis=1)
    while not ready():
        yield

    er = lax.broadcasted_iota(I32, (2 * GLA_DV, 2 * GLA_DK), 0)
    dc = lax.broadcasted_iota(I32, (2 * GLA_DV, 2 * GLA_DK), 1)
    same_head = (er // GLA_DV) == (dc // GLA_DK)
    n_chunks = TOKEN_BLOCK // GLA_CHUNK
    o_inter = [None] * n_chunks
    for c in (range(n_chunks) if fwd else reversed(range(n_chunks))):
        r0 = c * GLA_CHUNK
        rows = slice(r0, r0 + GLA_CHUNK)
        last = cum[r0 + GLA_CHUNK - 1:r0 + GLA_CHUNK] if fwd else cum[r0:r0 + 1]
        kd = k[rows] * jnp.exp(last - cum[rows])
        dec = jnp.exp(last)
        parts = []
        for pair in range(2):
            dl = slice(pair * LANES, (pair + 1) * LANES)
            st = st_ref[pair]
            parts.append(_dot_nt(qe[rows, dl], st))
            ds_t = _dot_tn(v[rows, pair * 2 * GLA_DV:(pair + 1) * 2 * GLA_DV], kd[:, dl])
            st_ref[pair] = dec[:, dl] * st + jnp.where(same_head, ds_t, 0.0)
        o_inter[c] = jnp.concatenate(parts, axis=1)
        yield
    done()
    return o_intra + jnp.concatenate(o_inter, axis=0)


def _interleave(chains):
    chains = list(chains)
    done = [False] * len(chains)
    tick = 0
    while not all(done):
        for i, ch in enumerate(chains):
            if tick >= i and not done[i]:
                try:
                    next(ch)
                except StopIteration:
                    done[i] = True
        tick += 1


def _load_state(s0_ref, u, st_ref):
    zero = jnp.zeros((GLA_DV, GLA_DK), F32)
    for pair in range(2):
        a = s0_ref[u, 2 * pair].T
        b = s0_ref[u, 2 * pair + 1].T
        st_ref[pair] = jnp.concatenate(
            [jnp.concatenate([a, zero], axis=1), jnp.concatenate([zero, b], axis=1)], axis=0)


def _store_state(st_ref, sfin_ref, u):
    for pair in range(2):
        st = st_ref[pair]
        sfin_ref[u, 2 * pair] = st[0:GLA_DV, 0:GLA_DK].T
        sfin_ref[u, 2 * pair + 1] = st[GLA_DV:2 * GLA_DV, GLA_DK:2 * GLA_DK].T


def _mod_row(mod_ref, row, part):
    return mod_ref[pl.ds(row, 1), part * D_MODEL:(part + 1) * D_MODEL]


def _front_stages(xin, row, mod_ref, n1_ref, win_ref, wa_ref, ba_ref, snw_ref, sws_ref, sbs_ref):
    h = _rms(xin, n1_ref[...]) * (1.0 + _mod_row(mod_ref, row, 1)) + _mod_row(mod_ref, row, 0)
    yield
    hb = h.astype(BF16)
    p_parts = []
    for c0, c1 in ((P_Q, P_G), (P_G, P_SV), (P_SV, P_WIDTH)):
        p_parts.append(jnp.dot(hb, win_ref[:, c0:c1], preferred_element_type=F32))
        yield
    p = jnp.concatenate(p_parts, axis=1)
    q = p[:, P_Q:P_K] * (GLA_DK ** -0.5)
    k = p[:, P_K:P_V]
    v = p[:, P_V:P_G]
    z = _dot(p[:, P_A:P_WIDTH], wa_ref[...]) + ba_ref[...]
    la = (jnp.minimum(z, 0.0) - jnp.log1p(jnp.exp(-jnp.abs(z)))) * (1.0 / GLA_GATE_NORM)
    yield

    ug = jax.nn.gelu(p[:, P_U:P_SV])
    yield
    vg = jax.nn.gelu(p[:, P_SV:P_A])
    yield
    s_cols = []
    for gi in range(SGU_GROUPS):
        cols = slice(gi * SGU_CH, (gi + 1) * SGU_CH)
        vn = _rms(vg[:, cols], snw_ref[:, cols])
        rhs = jnp.concatenate([vn[0:SGU_CHUNK], vn[SGU_CHUNK:2 * SGU_CHUNK]], axis=1)
        vm = _dot(sws_ref[gi], rhs) + jnp.concatenate([sbs_ref[gi], sbs_ref[gi]], axis=1)
        vm = jnp.concatenate([vm[:, 0:SGU_CH], vm[:, SGU_CH:2 * SGU_CH]], axis=0)
        s_cols.append(ug[:, cols] * vm)
    yield
    return q, k, v, p[:, P_G:P_U], la, jnp.concatenate(s_cols, axis=1)


def _back_stages(o, g, s_val, xin, row, mod_ref, gnw_ref, wout_ref, n2_ref, rw_ref):
    cols = []
    for head in range(GLA_HEADS):
        hs = slice(head * GLA_DV, (head + 1) * GLA_DV)
        cols.append(_rms(o[:, hs], gnw_ref[...]) * jax.nn.silu(g[:, hs]))
    cols.append(s_val)
    yield
    y = _dot(jnp.concatenate(cols, axis=1), wout_ref[...])
    yield
    x1 = xin + _mod_row(mod_ref, row, 2) * y
    h2 = _rms(x1, n2_ref[...]) * (1.0 + _mod_row(mod_ref, row, 4)) + _mod_row(mod_ref, row, 3)
    yield
    logits = _dot_nt_3pass(rw_ref[...], h2)
    m = jnp.max(logits, axis=0, keepdims=True)
    ex = jnp.exp(logits - m)
    return x1, h2, ex / jnp.sum(ex, axis=0, keepdims=True)


def _store_token_tiles(h2t_ref, u, b, h2):
    for s in range(TILES_PER_TOKEN):
        h2t_ref[u, pl.ds(b * TOKEN_BLOCK * TILES_PER_TOKEN + s, TOKEN_BLOCK, stride=TILES_PER_TOKEN), :] = (
            h2[:, s * LANES:(s + 1) * LANES])


def _zip_stages(gen_a, gen_b):
    out = [None, None]
    live = [gen_a, gen_b]
    while any(g is not None for g in live):
        for i, g in enumerate(live):
            if g is not None:
                try:
                    next(g)
                except StopIteration as stop:
                    out[i] = stop.value
                    live[i] = None
        yield
    return out


def _mixer_fwd_kernel(group, add_pos, mod_base, mod_per_seq, nb,
                      x_ref, e_ref, mod_ref, n1_ref, win_ref, wa_ref, ba_ref,
                      snw_ref, sws_ref, sbs_ref, s0_ref,
                      qkv_ref, g_ref, lab_ref, s_ref, of_ref, sfin_ref,
                      st_ref):
    grp = pl.program_id(0)
    blk = pl.program_id(1)
    sub = x_ref.shape[1] // TOKEN_BLOCK
    lo_mask, _ = _chunk_masks()
    state_done = set()

    @pl.when(blk == 0)
    def _():
        for u in range(group):
            _load_state(s0_ref, u, st_ref.at[u])

    def chain(u, b):
        row = mod_base + ((grp * group + u) if mod_per_seq else 0)
        st_u = st_ref.at[u]
        rows = slice(b * TOKEN_BLOCK, (b + 1) * TOKEN_BLOCK)
        xin = _add_pos(x_ref[u, rows, :], e_ref, blk * sub + b, add_pos)
        q, k, v, g, la, s_val = yield from _front_stages(
            xin, row, mod_ref, n1_ref, win_ref, wa_ref, ba_ref, snw_ref, sws_ref, sbs_ref)
        qkv_ref[u, rows, :] = jnp.concatenate([q, k, v], axis=1)
        g_ref[u, rows, :] = g
        lab_ref[u, rows, :] = la[:, QK_W:2 * QK_W]
        s_ref[u, rows, :] = s_val

        cum = _select_dot(lo_mask, la[:, 0:QK_W])
        yield
        of_ref[u, rows, :] = yield from _gla_direction(
            q, k, v, cum, True, lo_mask, st_u,
            ready=lambda: b == 0 or (u, b - 1) in state_done, done=lambda: state_done.add((u, b)))

    _interleave(chain(u, b) for b in range(sub) for u in range(group))

    @pl.when(blk == nb - 1)
    def _():
        for u in range(group):
            _store_state(st_ref.at[u], sfin_ref, u)


def _mixer_fwd(x, e_tab, mod, n1, win, wa, ba, snw, sws, sbs, s0, *, nseq, nb, add_pos,
               mod_base, mod_per_seq):
    seq_len = nb * TOKEN_BLOCK
    group = min(SEQ_GROUP, nseq)
    sub = min(SEQ_GROUP // group, nb)
    nb //= sub
    tok = lambda w: pl.BlockSpec((group, sub * TOKEN_BLOCK, w), lambda s, i: (s, i, 0))
    full = lambda a: pl.BlockSpec(a.shape, lambda s, i: (0,) * a.ndim)
    st_spec = pl.BlockSpec((group, GLA_HEADS, GLA_DK, GLA_DV), lambda s, i: (s, 0, 0, 0))
    s0_spec = st_spec if s0.shape[0] == nseq else pl.BlockSpec(s0.shape, lambda s, i: (0, 0, 0, 0))
    act = lambda w: jax.ShapeDtypeStruct((nseq, seq_len, w), F32)
    kern = functools.partial(_mixer_fwd_kernel, group, add_pos, mod_base, mod_per_seq, nb)
    return pl.pallas_call(
        kern,
        grid=(nseq // group, nb),
        in_specs=[tok(D_MODEL), full(e_tab), full(mod), full(n1), full(win), full(wa), full(ba),
                  full(snw), full(sws), full(sbs), s0_spec],
        out_specs=[tok(1024), tok(GLA_WIDTH), tok(QK_W), tok(SGU_WIDTH), tok(GLA_WIDTH), st_spec],
        out_shape=[act(1024), act(GLA_WIDTH), act(QK_W), act(SGU_WIDTH), act(GLA_WIDTH),
                   jax.ShapeDtypeStruct((nseq, GLA_HEADS, GLA_DK, GLA_DV), F32)],
        scratch_shapes=[pltpu.VMEM((group, 2, 2 * GLA_DV, 2 * GLA_DK), F32)],
        compiler_params=_compiler_params(("arbitrary", "arbitrary")),
        name="mixer_fwd",
    )(x, e_tab, mod, n1, win, wa, ba, snw, sws, sbs, s0)


def _mixer_bwd_kernel(group, add_pos, mod_base, mod_per_seq, nb,
                      x_ref, e_ref, mod_ref, qkv_ref, g_ref, lab_ref, s_ref, of_ref,
                      gnw_ref, wout_ref, n2_ref, rw_ref, s0_ref,
                      x1_ref, h2t_ref, probs_ref, sfin_ref,
                      st_ref):
    grp = pl.program_id(0)
    step = pl.program_id(1)
    sub = x_ref.shape[1] // TOKEN_BLOCK
    blk = nb - 1 - step
    _, hi_mask = _chunk_masks()
    state_done = set()

    @pl.when(step == 0)
    def _():
        for u in range(group):
            _load_state(s0_ref, u, st_ref.at[u])

    def chain(u, b):
        row = mod_base + ((grp * group + u) if mod_per_seq else 0)
        st_u = st_ref.at[u]
        rows = slice(b * TOKEN_BLOCK, (b + 1) * TOKEN_BLOCK)
        qkv = qkv_ref[u, rows, :]
        q, k, v = qkv[:, 0:QK_W], qkv[:, QK_W:2 * QK_W], qkv[:, 2 * QK_W:]
        cum = _select_dot(hi_mask, lab_ref[u, rows, :])
        yield
        o_b = yield from _gla_direction(
            q, k, v, cum, False, hi_mask, st_u,
            ready=lambda: b == sub - 1 or (u, b + 1) in state_done,
            done=lambda: state_done.add((u, b)))
        xin = _add_pos(x_ref[u, rows, :], e_ref, blk * sub + b, add_pos)
        x1, h2, probs = yield from _back_stages(
            of_ref[u, rows, :] + o_b, g_ref[u, rows, :], s_ref[u, rows, :], xin, row,
            mod_ref, gnw_ref, wout_ref, n2_ref, rw_ref)
        x1_ref[u, rows, :] = x1
        _store_token_tiles(h2t_ref, u, b, h2)
        probs_ref[u, b] = probs

    _interleave(chain(u, b) for b in reversed(range(sub)) for u in range(group))

    @pl.when(step == nb - 1)
    def _():
        for u in range(group):
            _store_state(st_ref.at[u], sfin_ref, u)


def _mixer_bwd(x, e_tab, mod, qkv, g, lab, s, of, gnw, wout, n2, rw_t, s0, *, nseq, nb, add_pos,
               mod_base, mod_per_seq):
    seq_len = nb * TOKEN_BLOCK
    group = min(SEQ_GROUP, nseq)
    sub = min(SEQ_GROUP // group, nb)
    nb //= sub
    tok = lambda w: pl.BlockSpec((group, sub * TOKEN_BLOCK, w), lambda s_, i: (s_, nb - 1 - i, 0))
    full = lambda a: pl.BlockSpec(a.shape, lambda s_, i: (0,) * a.ndim)
    st_spec = pl.BlockSpec((group, GLA_HEADS, GLA_DK, GLA_DV), lambda s_, i: (s_, 0, 0, 0))
    s0_spec = st_spec if s0.shape[0] == nseq else pl.BlockSpec(s0.shape, lambda s_, i: (0, 0, 0, 0))
    kern = functools.partial(_mixer_bwd_kernel, group, add_pos, mod_base, mod_per_seq, nb)
    return pl.pallas_call(
        kern,
        grid=(nseq // group, nb),
        in_specs=[tok(D_MODEL), full(e_tab), full(mod), tok(1024), tok(GLA_WIDTH), tok(QK_W),
                  tok(SGU_WIDTH), tok(GLA_WIDTH), full(gnw), full(wout), full(n2), full(rw_t),
                  s0_spec],
        out_specs=[tok(D_MODEL),
                   pl.BlockSpec((group, sub * TOKEN_BLOCK * TILES_PER_TOKEN, LANES),
                                lambda s_, i: (s_, nb - 1 - i, 0)),
                   pl.BlockSpec((group, sub, N_EXPERTS, TOKEN_BLOCK),
                                lambda s_, i: (s_, nb - 1 - i, 0, 0)),
                   st_spec],
        out_shape=[jax.ShapeDtypeStruct((nseq, seq_len, D_MODEL), F32),
                   jax.ShapeDtypeStruct((nseq, seq_len * TILES_PER_TOKEN, LANES), F32),
                   jax.ShapeDtypeStruct((nseq, nb * sub, N_EXPERTS, TOKEN_BLOCK), F32),
                   jax.ShapeDtypeStruct((nseq, GLA_HEADS, GLA_DK, GLA_DV), F32)],
        scratch_shapes=[pltpu.VMEM((group, 2, 2 * GLA_DV, 2 * GLA_DK), F32)],
        compiler_params=_compiler_params(("arbitrary", "arbitrary")),
        name="mixer_bwd",
    )(x, e_tab, mod, qkv, g, lab, s, of, gnw, wout, n2, rw_t, s0)


def _mixer_fused_kernel(group, mod_base, mod_per_seq,
                        x_ref, mod_ref, n1_ref, win_ref, wa_ref, ba_ref, snw_ref, sws_ref, sbs_ref,
                        gnw_ref, wout_ref, n2_ref, rw_ref, s0f_ref, s0b_ref,
                        x1_ref, h2t_ref, probs_ref, sfin_f_ref, sfin_b_ref,
                        stf_ref, stb_ref):
    grp = pl.program_id(0)
    lo_mask, hi_mask = _chunk_masks()
    always = lambda: True
    nothing = lambda: None
    for u in range(group):
        _load_state(s0f_ref, u, stf_ref.at[u])
        _load_state(s0b_ref, u, stb_ref.at[u])

    def chain(u):
        row = mod_base + ((grp * group + u) if mod_per_seq else 0)
        xin = x_ref[u]
        q, k, v, g, la, s_val = yield from _front_stages(
            xin, row, mod_ref, n1_ref, win_ref, wa_ref, ba_ref, snw_ref, sws_ref, sbs_ref)
        cum_f = _select_dot(lo_mask, la[:, 0:QK_W])
        cum_b = _select_dot(hi_mask, la[:, QK_W:2 * QK_W])
        yield
        o_f, o_b = yield from _zip_stages(
            _gla_direction(q, k, v, cum_f, True, lo_mask, stf_ref.at[u], always, nothing),
            _gla_direction(q, k, v, cum_b, False, hi_mask, stb_ref.at[u], always, nothing))
        x1, h2, probs = yield from _back_stages(
            o_f + o_b, g, s_val, xin, row, mod_ref, gnw_ref, wout_ref, n2_ref, rw_ref)
        x1_ref[u] = x1
        _store_token_tiles(h2t_ref, u, 0, h2)
        probs_ref[u, 0] = probs

    _interleave(chain(u) for u in range(group))
    for u in range(group):
        _store_state(stf_ref.at[u], sfin_f_ref, u)
        _store_state(stb_ref.at[u], sfin_b_ref, u)


def _mixer_fused(x, mod, prm, s0_f, s0_b, *, nseq, mod_base, mod_per_seq):
    group = min(SEQ_GROUP, nseq)
    consts = [mod, prm["n1"], prm["win"], prm["wa"], prm["ba"], prm["snw"], prm["sws"], prm["sbs"],
              prm["gnw"], prm["wout"], prm["n2"], prm["rw_t"]]
    tok = lambda w: pl.BlockSpec((group, TOKEN_BLOCK, w), lambda s: (s, 0, 0))
    full = lambda a: pl.BlockSpec(a.shape, lambda s: (0,) * a.ndim)
    st_spec = pl.BlockSpec((group, GLA_HEADS, GLA_DK, GLA_DV), lambda s: (s, 0, 0, 0))
    s0_spec = lambda a: st_spec if a.shape[0] == nseq else pl.BlockSpec(a.shape, lambda s: (0, 0, 0, 0))
    state = jax.ShapeDtypeStruct((nseq, GLA_HEADS, GLA_DK, GLA_DV), F32)
    st_scratch = pltpu.VMEM((group, 2, 2 * GLA_DV, 2 * GLA_DK), F32)
    return pl.pallas_call(
        functools.partial(_mixer_fused_kernel, group, mod_base, mod_per_seq),
        grid=(nseq // group,),
        in_specs=[tok(D_MODEL)] + [full(a) for a in consts] + [s0_spec(s0_f), s0_spec(s0_b)],
        out_specs=[tok(D_MODEL),
                   pl.BlockSpec((group, TOKEN_BLOCK * TILES_PER_TOKEN, LANES), lambda s: (s, 0, 0)),
                   pl.BlockSpec((group, 1, N_EXPERTS, TOKEN_BLOCK), lambda s: (s, 0, 0, 0)),
                   st_spec, st_spec],
        out_shape=[jax.ShapeDtypeStruct((nseq, TOKEN_BLOCK, D_MODEL), F32),
                   jax.ShapeDtypeStruct((nseq, TOKEN_BLOCK * TILES_PER_TOKEN, LANES), F32),
                   jax.ShapeDtypeStruct((nseq, 1, N_EXPERTS, TOKEN_BLOCK), F32),
                   state, state],
        scratch_shapes=[st_scratch, st_scratch],
        compiler_params=_compiler_params(("arbitrary",)),
        name="mixer_fused",
    )(x, *consts, s0_f, s0_b)


def _route_kernel(n_tok, cap, probs_ref, row_ref, gate_ref, xs_ref, ps_ref):
    n_blk = n_tok // TOKEN_BLOCK
    n_chunk = n_tok // LANES
    probs = jnp.concatenate([probs_ref[b] for b in range(n_blk)], axis=1)
    capf = jnp.float32(cap)

    def count(mask):
        return jnp.sum(mask.astype(F32), axis=1, keepdims=True)

    def as_f32(bits):
        return lax.bitcast_convert_type(bits, F32)

    def thr_step(_, lohi):
        lo, hi = lohi
        mid = lo + ((hi - lo + 1) >> 1)
        ok = count(probs >= as_f32(mid)) >= capf
        return jnp.where(ok, mid, lo), jnp.where(ok, hi, mid - 1)

    lo0 = jnp.zeros((N_EXPERTS, 1), I32)
    hi0 = jnp.full((N_EXPERTS, 1), 0x3F800000, I32)
    thr, _ = lax.fori_loop(0, 31, thr_step, (lo0, hi0))
    gt = probs >= as_f32(thr + 1)
    eq = (probs >= as_f32(thr)) & jnp.logical_not(gt)
    need = capf - count(gt)
    tok = lax.broadcasted_iota(I32, (N_EXPERTS, n_tok), 1)

    def tie_step(_, lohi):
        lo, hi = lohi
        mid = (lo + hi) >> 1
        ok = count(eq & (tok <= mid)) >= need
        return jnp.where(ok, lo, mid + 1), jnp.where(ok, mid, hi)

    n_bits = max(1, (n_tok - 1).bit_length())
    cut, _ = lax.fori_loop(0, n_bits, tie_step,
                           (jnp.zeros((N_EXPERTS, 1), I32), jnp.full((N_EXPERTS, 1), n_tok - 1, I32)))
    sel = (gt | (eq & (tok <= cut))).astype(F32)

    xs_ref[...] = jnp.concatenate([sel[:, c * LANES:(c + 1) * LANES] for c in range(n_chunk)], axis=0)
    ps_ref[...] = jnp.concatenate([probs[:, c * LANES:(c + 1) * LANES] for c in range(n_chunk)], axis=0)

    li = lax.broadcasted_iota(I32, (LANES, LANES), 0)
    lj = lax.broadcasted_iota(I32, (LANES, LANES), 1)
    upper = (li <= lj).astype(F32)
    ci = lax.broadcasted_iota(I32, (n_chunk, n_chunk), 0)
    cj = lax.broadcasted_iota(I32, (n_chunk, n_chunk), 1)
    lower = (cj <= ci).astype(F32)
    slot = lax.broadcasted_iota(I32, (1, cap), 1).astype(F32)
    chunk_id = lax.broadcasted_iota(I32, (n_chunk, cap), 0).astype(F32)
    lane_id = lax.broadcasted_iota(I32, (LANES, cap), 0).astype(F32)
    reps = cap // LANES

    def per_expert(e):
        x = xs_ref[pl.ds(e, n_chunk, stride=N_EXPERTS), :]
        pe = ps_ref[pl.ds(e, n_chunk, stride=N_EXPERTS), :]
        ploc = _dot(x, upper)
        tot = jnp.broadcast_to(ploc[:, LANES - 1:LANES], (n_chunk, LANES))
        cum = _dot(lower, tot)
        yield
        cum_w = jnp.concatenate([cum] * reps, axis=1)
        base_w = jnp.concatenate([cum - tot] * reps, axis=1)
        chunk_of = jnp.sum((cum_w <= slot).astype(F32), axis=0, keepdims=True)
        onehot = chunk_id == chunk_of
        local = slot - jnp.sum(jnp.where(onehot, base_w, 0.0), axis=0, keepdims=True)
        yield
        lhs = jnp.concatenate([ploc.astype(BF16)] + _split_bf16(pe, 3), axis=1)
        got = _dot_tn(lhs, onehot.astype(F32))
        yield
        pref = got[0:LANES]
        lane_of = jnp.sum((pref <= local).astype(F32), axis=0, keepdims=True)
        token = chunk_of * LANES + lane_of
        row_ref[pl.ds(e, 1), :] = (token * TILES_PER_TOKEN).astype(I32)
        yield
        pg = (got[3 * LANES:4 * LANES] + got[2 * LANES:3 * LANES]) + got[LANES:2 * LANES]
        gate_ref[pl.ds(e, 1), :] = jnp.sum(jnp.where(lane_id == lane_of, pg, 0.0), axis=0, keepdims=True)

    def expert_pair(i, _):
        _interleave(per_expert(2 * i + u) for u in range(2))
        return 0

    lax.fori_loop(0, N_EXPERTS // 2, expert_pair, 0)


def _route(probs, n_tok, cap):
    return pl.pallas_call(
        functools.partial(_route_kernel, n_tok, cap),
        out_shape=[jax.ShapeDtypeStruct((N_EXPERTS, cap), I32),
                   jax.ShapeDtypeStruct((N_EXPERTS, cap), F32)],
        scratch_shapes=[pltpu.VMEM((n_tok // LANES * N_EXPERTS, LANES), F32),
                        pltpu.VMEM((n_tok // LANES * N_EXPERTS, LANES), F32)],
        compiler_params=pltpu.CompilerParams(vmem_limit_bytes=VMEM_LIMIT),
        name="route_topk",
    )(probs)


def _expert_kernel(cap, row_ref, h2t_ref, w1_ref, w3_ref, w2_ref, ye_ref,
                   xe_ref, x2_ref, hid_ref, sem):
    e = pl.program_id(0)
    f = pl.program_id(1)
    slot = e % 2
    rows_per_step = cap // EXPERT_STEPS

    def start_row(expert, buf, j):
        src = pl.multiple_of(row_ref[expert * cap + j], TILES_PER_TOKEN)
        dst = pl.multiple_of(j * TILES_PER_TOKEN, TILES_PER_TOKEN)
        pltpu.make_async_copy(h2t_ref.at[pl.ds(src, TILES_PER_TOKEN), :],
                              xe_ref.at[buf, pl.ds(dst, TILES_PER_TOKEN), :], sem.at[buf]).start()

    def wait_rows(buf):
        pltpu.make_async_copy(h2t_ref.at[pl.ds(0, cap * TILES_PER_TOKEN), :], xe_ref.at[buf],
                              sem.at[buf]).wait()

    def prefetch_next():
        nxt = jnp.minimum(e + 1, N_EXPERTS - 1)
        first = f * rows_per_step
        for j in range(rows_per_step):
            start_row(nxt, 1 - slot, first + j)

    @pl.when((e == 0) & (f == 0))
    def _():
        def issue(j, _):
            start_row(0, 0, j)
            return 0

        lax.fori_loop(0, cap, issue, 0, unroll=8)

    @pl.when(f == 0)
    def _():
        wait_rows(slot)
        for s in range(TILES_PER_TOKEN):
            x2_ref[:, s * LANES:(s + 1) * LANES] = (
                xe_ref[slot, pl.ds(s, cap, stride=TILES_PER_TOKEN), :].astype(BF16))

    @pl.when(f < N_HID_STEPS)
    def _():
        prefetch_next()
        x2 = x2_ref[...]
        a = jnp.dot(x2, w1_ref[0].astype(BF16), preferred_element_type=F32)
        b = jnp.dot(x2, w3_ref[0].astype(BF16), preferred_element_type=F32)
        hid_ref[f] = (jax.nn.silu(a) * b).astype(BF16)

    @pl.when(f >= N_HID_STEPS)
    def _():
        prefetch_next()
        w2 = w2_ref[0].astype(BF16)
        out = jnp.dot(hid_ref[0], w2[0:EXPERT_F_BLOCK], preferred_element_type=F32)
        for kb in range(1, N_HID_STEPS):
            out += jnp.dot(hid_ref[kb], w2[kb * EXPERT_F_BLOCK:(kb + 1) * EXPERT_F_BLOCK],
                           preferred_element_type=F32)
        tile0 = (f - N_HID_STEPS) * (EXPERT_N_BLOCK // LANES)
        for i in range(EXPERT_N_BLOCK // LANES):
            ye_ref[pl.ds(tile0 + i, cap, stride=TILES_PER_TOKEN), :] = out[:, i * LANES:(i + 1) * LANES]

    @pl.when((e == N_EXPERTS - 1) & (f == EXPERT_STEPS - 1))
    def _():
        wait_rows(1 - slot)


def _experts(rows, h2t, w1, w3, w2, cap):
    hid_blk = lambda e, f, idx: (e, 0, jnp.minimum(f, N_HID_STEPS - 1))
    n_out = D_MODEL // EXPERT_N_BLOCK

    def out_blk(e, f, idx):
        hold = (f == 0) & (e > 0)
        return (jnp.where(hold, e - 1, e), 0,
                jnp.where(hold, n_out - 1, jnp.maximum(f - N_HID_STEPS, 0)))

    grid_spec = pltpu.PrefetchScalarGridSpec(
        num_scalar_prefetch=1,
        grid=(N_EXPERTS, EXPERT_STEPS),
        in_specs=[pl.BlockSpec(memory_space=pl.ANY),
                  pl.BlockSpec((1, D_MODEL, EXPERT_F_BLOCK), hid_blk),
                  pl.BlockSpec((1, D_MODEL, EXPERT_F_BLOCK), hid_blk),
                  pl.BlockSpec((1, D_EXPERT, EXPERT_N_BLOCK), out_blk)],
        out_specs=pl.BlockSpec((cap * TILES_PER_TOKEN, LANES), lambda e, f, idx: (e, 0)),
        scratch_shapes=[pltpu.VMEM((2, cap * TILES_PER_TOKEN, LANES), F32),
                        pltpu.VMEM((cap, D_MODEL), BF16),
                        pltpu.VMEM((N_HID_STEPS, cap, EXPERT_F_BLOCK), BF16),
                        pltpu.SemaphoreType.DMA((2,))],
    )
    return pl.pallas_call(
        functools.partial(_expert_kernel, cap),
        grid_spec=grid_spec,
        out_shape=jax.ShapeDtypeStruct((N_EXPERTS * cap * TILES_PER_TOKEN, LANES), F32),
        compiler_params=_compiler_params(("arbitrary", "arbitrary")),
        name="expert_swiglu",
    )(rows, h2t, w1, w3, w2)


COMBINE_BATCH = 8
ZERO_ROWS = 512


def _combine_kernel(cap, n_tok, mod_base, mod_per_seq, nb,
                    row_ref, gate_ref, ye_ref, x1_ref, mod_ref, fw_ref, y_ref,
                    acc_ref, xbuf_ref, ybuf_ref, sem_x, sem_y):
    e = pl.program_id(0)

    @pl.when(e == 0)
    def _():
        def zero(i, _):
            r = pl.multiple_of(i * ZERO_ROWS, ZERO_ROWS)
            acc_ref[pl.ds(r, ZERO_ROWS), :] = jnp.zeros((ZERO_ROWS, LANES), F32)
            return 0

        lax.fori_loop(0, n_tok * TILES_PER_TOKEN // ZERO_ROWS, zero, 0)

    def batch(jb, _):
        vals = []
        for u in range(COMBINE_BATCH):
            j = jb * COMBINE_BATCH + u
            t = pl.multiple_of(row_ref[e * cap + j], TILES_PER_TOKEN)
            gate = gate_ref[e * cap + j]
            src = pl.multiple_of(j * TILES_PER_TOKEN, TILES_PER_TOKEN)
            vals.append((t, acc_ref[pl.ds(t, TILES_PER_TOKEN), :]
                         + ye_ref[pl.ds(src, TILES_PER_TOKEN), :] * gate))
        for t, val in vals:
            acc_ref[pl.ds(t, TILES_PER_TOKEN), :] = val
        return 0

    lax.fori_loop(0, cap // COMBINE_BATCH, batch, 0)

    @pl.when(e == N_EXPERTS - 1)
    def _():
        n_blk = n_tok // TOKEN_BLOCK

        def x1_copy(b, slot):
            rows = pl.ds(pl.multiple_of(b * TOKEN_BLOCK, TOKEN_BLOCK), TOKEN_BLOCK)
            return pltpu.make_async_copy(x1_ref.at[rows, :], xbuf_ref.at[slot], sem_x.at[slot])

        def y_copy(b, slot):
            rows = pl.ds(pl.multiple_of(b * TOKEN_BLOCK, TOKEN_BLOCK), TOKEN_BLOCK)
            return pltpu.make_async_copy(ybuf_ref.at[slot], y_ref.at[rows, :], sem_y.at[slot])

        x1_copy(0, 0).start()

        def block(b, _):
            slot = b % 2

            @pl.when(b + 1 < n_blk)
            def _():
                x1_copy(b + 1, 1 - slot).start()

            x1_copy(b, slot).wait()

            @pl.when(b >= 2)
            def _():
                y_copy(b - 2, slot).wait()

            base = pl.multiple_of(b * (TOKEN_BLOCK * TILES_PER_TOKEN), TOKEN_BLOCK * TILES_PER_TOKEN)
            moe = jnp.concatenate(
                [acc_ref[pl.ds(base + s, TOKEN_BLOCK, stride=TILES_PER_TOKEN), :]
                 for s in range(TILES_PER_TOKEN)], axis=1)
            row = mod_base + ((b // nb) if mod_per_seq else 0)
            ybuf_ref[slot] = _rms(xbuf_ref[slot] + _mod_row(mod_ref, row, 5) * moe, fw_ref[...])
            y_copy(b, slot).start()
            return 0

        lax.fori_loop(0, n_blk, block, 0)
        y_copy(n_blk - 2, n_blk % 2).wait()
        y_copy(n_blk - 1, (n_blk - 1) % 2).wait()


def _combine_final(rows, gates, ye, x1, mod, fw, cap, n_tok, *, nb, mod_base, mod_per_seq):
    full = lambda a: pl.BlockSpec(a.shape, lambda e, rows, gate: (0,) * a.ndim)
    grid_spec = pltpu.PrefetchScalarGridSpec(
        num_scalar_prefetch=2,
        grid=(N_EXPERTS,),
        in_specs=[pl.BlockSpec((cap * TILES_PER_TOKEN, LANES), lambda e, rows, gate: (e, 0)),
                  pl.BlockSpec(memory_space=pl.ANY), full(mod), full(fw)],
        out_specs=pl.BlockSpec(memory_space=pl.ANY),
        scratch_shapes=[pltpu.VMEM((n_tok * TILES_PER_TOKEN, LANES), F32),
                        pltpu.VMEM((2, TOKEN_BLOCK, D_MODEL), F32),
                        pltpu.VMEM((2, TOKEN_BLOCK, D_MODEL), F32),
                        pltpu.SemaphoreType.DMA((2,)),
                        pltpu.SemaphoreType.DMA((2,))],
    )
    return pl.pallas_call(
        functools.partial(_combine_kernel, cap, n_tok, mod_base, mod_per_seq, nb),
        grid_spec=grid_spec,
        out_shape=jax.ShapeDtypeStruct((n_tok, D_MODEL), F32),
        compiler_params=_compiler_params(("arbitrary",)),
        name="moe_combine_norm",
    )(rows, gates, ye, x1, mod, fw)


def _trunk_and_norm(x, e_tab, mod, s0_f, s0_b, prm, *, nseq, seq_len, add_pos, mod_base, mod_per_seq):
    nb = seq_len // TOKEN_BLOCK
    n_tok = nseq * seq_len
    cap = EC_CAPACITY_FACTOR * n_tok // N_EXPERTS
    kw = dict(nseq=nseq, nb=nb, add_pos=add_pos, mod_base=mod_base, mod_per_seq=mod_per_seq)
    if nb == 1 and not add_pos:
        x1, h2t, probs, sfin_f, sfin_b = _mixer_fused(
            x, mod, prm, s0_f, s0_b, nseq=nseq, mod_base=mod_base, mod_per_seq=mod_per_seq)
    else:
        qkv, g, lab, s, of, sfin_f = _mixer_fwd(
            x, e_tab, mod, prm["n1"], prm["win"], prm["wa"], prm["ba"], prm["snw"], prm["sws"],
            prm["sbs"], s0_f, **kw)
        x1, h2t, probs, sfin_b = _mixer_bwd(
            x, e_tab, mod, qkv, g, lab, s, of, prm["gnw"], prm["wout"], prm["n2"], prm["rw_t"],
            s0_b, **kw)
    x1 = x1.reshape(n_tok, D_MODEL)
    h2t = h2t.reshape(n_tok * TILES_PER_TOKEN, LANES)
    probs = probs.reshape(n_tok // TOKEN_BLOCK, N_EXPERTS, TOKEN_BLOCK)
    rows, gates = _route(probs, n_tok, cap)
    rows, gates = rows.reshape(-1), gates.reshape(-1)
    ye = _experts(rows, h2t, prm["w1"], prm["w3"], prm["w2"], cap)
    y = _combine_final(rows, gates, ye, x1, mod, prm["fw"], cap, n_tok,
                       nb=nb, mod_base=mod_base, mod_per_seq=mod_per_seq)
    return y.reshape(nseq, seq_len, D_MODEL), sfin_f, sfin_b


def kernel(x_prompt, x_sample, state_gla_fwd, state_gla_bwd, c, c_ctx, ada_w, ada_b, norm1_w, w_in, gla_wa2_f, gla_ba_f, gla_wa2_b, gla_ba_b, gla_norm_w, sgu_norm_w, sgu_ws, sgu_bs, w_out, norm2_w, router_w, exp_w1, exp_w3, exp_w2, final_norm_w):
    assert ada_w.shape[0] == 1, "single trunk layer"
    batch, seq, _ = x_prompt.shape
    dec_batch, dec_seq, _ = x_sample.shape

    off_af = 2 * QK_W + 2 * GLA_WIDTH
    off_u = off_af + 2 * GLA_LOWRANK
    win = _win_layout(w_in[0], off_af, off_u)
    wa = jnp.zeros((P_WIDTH - P_A, 2 * QK_W), F32)
    wa = wa.at[0:GLA_LOWRANK, 0:QK_W].set(gla_wa2_f[0])
    wa = wa.at[GLA_LOWRANK:2 * GLA_LOWRANK, QK_W:].set(gla_wa2_b[0]).astype(BF16)
    prm = dict(
        n1=norm1_w, win=win, wa=wa,
        ba=jnp.concatenate([gla_ba_f[0], gla_ba_b[0]])[None, :],
        snw=sgu_norm_w, sws=sgu_ws[0].astype(BF16),
        sbs=jnp.broadcast_to(sgu_bs[0][:, :, None], (SGU_GROUPS, SGU_CHUNK, SGU_CH)),
        gnw=gla_norm_w, wout=w_out[0].astype(BF16), n2=norm2_w, rw_t=router_w[0].T,
        w1=exp_w1[0], w3=exp_w3[0], w2=exp_w2[0], fw=final_norm_w[None, :])

    cvec = jnp.concatenate([c_ctx[None, :], c, jnp.zeros((SUBLANES - 1 - dec_batch, D_MODEL), F32)])
    mod = _modulation(cvec, ada_w[0], ada_b)
    e_tab = _pos_table()

    zero_state = jnp.zeros((min(SEQ_GROUP, batch), GLA_HEADS, GLA_DK, GLA_DV), F32)
    y_prompt, sf, sb = _trunk_and_norm(
        x_prompt, e_tab, mod, zero_state, zero_state, prm,
        nseq=batch, seq_len=seq, add_pos=False, mod_base=0, mod_per_seq=False)
    y_sample, _, _ = _trunk_and_norm(
        x_sample, e_tab, mod, state_gla_fwd[:, 0], state_gla_bwd[:, 0], prm,
        nseq=dec_batch, seq_len=dec_seq, add_pos=True, mod_base=1, mod_per_seq=True)
    return (y_prompt, y_sample, sf[:, None], sb[:, None])
```

```python
import functools
import math

import jax
import jax.numpy as jnp
from jax import lax
from jax.experimental import pallas as pl
from jax.experimental.pallas import tpu as pltpu

F32 = jnp.float32
BF16 = jnp.bfloat16
I32 = jnp.int32

D_MODEL = 1024
GRID_W = 64
GLA_HEADS = 4
GLA_DK = 64
GLA_DV = 128
GLA_WIDTH = GLA_HEADS * GLA_DV
QK_W = GLA_HEADS * GLA_DK
GLA_LOWRANK = 16
GLA_GATE_NORM = 16.0
GLA_CHUNK = 64
SGU_WIDTH = 512
SGU_GROUPS = 4
SGU_CH = 128
SGU_CHUNK = 128
N_EXPERTS = 16
EC_CAPACITY_FACTOR = 2
D_EXPERT = 2048
EPS = 1e-6

SUBLANES = 8
LANES = 128
TILES_PER_TOKEN = D_MODEL // LANES

TOKEN_BLOCK = 256
SEQ_GROUP = 4
OFF_A = 2 * QK_W + 2 * GLA_WIDTH
OFF_U = OFF_A + 2 * GLA_LOWRANK
D_IN = OFF_U + 2 * SGU_WIDTH
EXPERT_F_BLOCK = 512
EXPERT_N_BLOCK = 256
N_HID_STEPS = D_EXPERT // EXPERT_F_BLOCK
EXPERT_STEPS = N_HID_STEPS + D_MODEL // EXPERT_N_BLOCK
VMEM_LIMIT = 56 * 1024 * 1024


def _dot(a, b):
    return jnp.dot(a.astype(BF16), b.astype(BF16), preferred_element_type=F32)


def _dot_nt(a, b):
    return lax.dot_general(a.astype(BF16), b.astype(BF16), (((1,), (1,)), ((), ())),
                           preferred_element_type=F32)


def _dot_tn(a, b):
    return lax.dot_general(a.astype(BF16), b.astype(BF16), (((0,), (0,)), ((), ())),
                           preferred_element_type=F32)


def _dot_f32(a, b, dims=(((1,), (0,)), ((), ()))):
    return lax.dot_general(a, b, dims, precision=lax.Precision.HIGHEST, preferred_element_type=F32)


def _split_bf16(x, terms):
    parts = []
    for _ in range(terms - 1):
        part = x.astype(BF16)
        parts.append(part)
        x = x - part.astype(F32)
    parts.append(x.astype(BF16))
    return parts


def _select_dot(sel, x):
    s = sel.astype(BF16)
    hi, mid, lo = _split_bf16(x, 3)
    return (jnp.dot(s, lo, preferred_element_type=F32) + jnp.dot(s, mid, preferred_element_type=F32)
            + jnp.dot(s, hi, preferred_element_type=F32))


def _dot_nt_3pass(a, b):
    a_hi, a_lo = _split_bf16(a, 2)
    b_hi, b_lo = _split_bf16(b, 2)
    nt = lambda x, y: lax.dot_general(x, y, (((1,), (1,)), ((), ())), preferred_element_type=F32)
    return (nt(a_hi, b_lo) + nt(a_lo, b_hi)) + nt(a_hi, b_hi)


def _rms(x, w):
    return x * lax.rsqrt(jnp.mean(x * x, axis=-1, keepdims=True) + EPS) * w


def _compiler_params(sem):
    return pltpu.CompilerParams(dimension_semantics=sem, vmem_limit_bytes=VMEM_LIMIT)


def _mod_kernel(c_ref, w_ref, b_ref, o_ref):
    o_ref[...] = _dot(jax.nn.silu(c_ref[...]), w_ref[...]) + b_ref[...]


def _modulation(cvec, ada_w, ada_b):
    n = ada_w.shape[1]
    bn = 768
    return pl.pallas_call(
        _mod_kernel,
        grid=(n // bn,),
        in_specs=[pl.BlockSpec((SUBLANES, D_MODEL), lambda j: (0, 0)),
                  pl.BlockSpec((D_MODEL, bn), lambda j: (0, j)),
                  pl.BlockSpec((1, bn), lambda j: (0, j))],
        out_specs=pl.BlockSpec((SUBLANES, bn), lambda j: (0, j)),
        out_shape=jax.ShapeDtypeStruct((SUBLANES, n), F32),
        compiler_params=_compiler_params(("arbitrary",)),
        name="adaln_mod",
    )(cvec, ada_w, ada_b)


def _cast_kernel(w_ref, o_ref):
    o_ref[...] = w_ref[...].astype(BF16)


def _win_transposed_bf16(w_t):
    rows = D_IN // 3
    return pl.pallas_call(
        _cast_kernel,
        grid=(D_IN // rows,),
        in_specs=[pl.BlockSpec((rows, D_MODEL), lambda i: (i, 0))],
        out_specs=pl.BlockSpec((rows, D_MODEL), lambda i: (i, 0)),
        out_shape=jax.ShapeDtypeStruct((D_IN, D_MODEL), BF16),
        compiler_params=_compiler_params(("arbitrary",)),
        name="win_cast",
    )(w_t)


def _pos_kernel(o_ref):
    nf = D_MODEL // 4
    p = lax.broadcasted_iota(I32, (GRID_W, nf), 0).astype(F32)
    i = lax.broadcasted_iota(I32, (GRID_W, nf), 1).astype(F32)
    omega = jnp.exp(i * (-math.log(10000.0) / nf))
    a = p * omega
    o_ref[:, 0:nf] = jnp.sin(a)
    o_ref[:, nf:2 * nf] = jnp.cos(a)


def _pos_table():
    return pl.pallas_call(
        _pos_kernel,
        out_shape=jax.ShapeDtypeStruct((GRID_W, D_MODEL // 2), F32),
        name="sincos_table",
    )()


def _add_pos(x, e_ref, blk, add_pos):
    if not add_pos:
        return x
    half = D_MODEL // 2
    e_all = e_ref[...]
    rows = []
    for j in range(TOKEN_BLOCK // GRID_W):
        xj = x[j * GRID_W:(j + 1) * GRID_W]
        e_row = e_ref[pl.ds(blk * (TOKEN_BLOCK // GRID_W) + j, 1), :]
        rows.append(jnp.concatenate([xj[:, 0:half] + e_row, xj[:, half:] + e_all], axis=1))
    return jnp.concatenate(rows, axis=0)


def _chunk_masks():
    r = lax.broadcasted_iota(I32, (TOKEN_BLOCK, TOKEN_BLOCK), 0)
    c = lax.broadcasted_iota(I32, (TOKEN_BLOCK, TOKEN_BLOCK), 1)
    same = (r // GLA_CHUNK) == (c // GLA_CHUNK)
    return same & (c <= r), same & (c >= r)


def _gla_direction(q, k, v, cum, fwd, att_mask, st_ref, ready, done):
    qe = q * jnp.exp(cum)
    ke = k * jnp.exp(-cum)
    yield
    lane = lax.broadcasted_iota(I32, (1, LANES), 1)
    o_intra = []
    for pair in range(2):
        qp = qe[:, pair * LANES:(pair + 1) * LANES]
        kp = ke[:, pair * LANES:(pair + 1) * LANES]
        for hh in range(2):
            qm = jnp.where((lane // GLA_DK) == hh, qp, 0.0)
            att = jnp.where(att_mask, _dot_nt(qm, kp), 0.0)
            head = 2 * pair + hh
            o_intra.append(_dot(att, v[:, head * GLA_DV:(head + 1) * GLA_DV]))
            yield
    o_intra = jnp.concatenate(o_intra, axis=1)
    while not ready():
        yield

    er = lax.broadcasted_iota(I32, (2 * GLA_DV, 2 * GLA_DK), 0)
    dc = lax.broadcasted_iota(I32, (2 * GLA_DV, 2 * GLA_DK), 1)
    same_head = (er // GLA_DV) == (dc // GLA_DK)
    n_chunks = TOKEN_BLOCK // GLA_CHUNK
    o_inter = [None] * n_chunks
    for c in (range(n_chunks) if fwd else reversed(range(n_chunks))):
        r0 = c * GLA_CHUNK
        rows = slice(r0, r0 + GLA_CHUNK)
        last = cum[r0 + GLA_CHUNK - 1:r0 + GLA_CHUNK] if fwd else cum[r0:r0 + 1]
        kd = k[rows] * jnp.exp(last - cum[rows])
        dec = jnp.exp(last)
        parts = []
        for pair in range(2):
            dl = slice(pair * LANES, (pair + 1) * LANES)
            st = st_ref[pair]
            parts.append(_dot_nt(qe[rows, dl], st))
            ds_t = _dot_tn(v[rows, pair * 2 * GLA_DV:(pair + 1) * 2 * GLA_DV], kd[:, dl])
            st_ref[pair] = dec[:, dl] * st + jnp.where(same_head, ds_t, 0.0)
        o_inter[c] = jnp.concatenate(parts, axis=1)
        yield
    done()
    return o_intra + jnp.concatenate(o_inter, axis=0)


def _interleave(chains):
    chains = list(chains)
    done = [False] * len(chains)
    tick = 0
    while not all(done):
        for i, ch in enumerate(chains):
            if tick >= i and not done[i]:
                try:
                    next(ch)
                except StopIteration:
                    done[i] = True
        tick += 1


def _load_state(s0_ref, u, st_ref):
    zero = jnp.zeros((GLA_DV, GLA_DK), F32)
    for pair in range(2):
        a = s0_ref[u, 2 * pair].T
        b = s0_ref[u, 2 * pair + 1].T
        st_ref[pair] = jnp.concatenate(
            [jnp.concatenate([a, zero], axis=1), jnp.concatenate([zero, b], axis=1)], axis=0)


def _store_state(st_ref, sfin_ref, u):
    for pair in range(2):
        st = st_ref[pair]
        sfin_ref[u, 2 * pair] = st[0:GLA_DV, 0:GLA_DK].T
        sfin_ref[u, 2 * pair + 1] = st[GLA_DV:2 * GLA_DV, GLA_DK:2 * GLA_DK].T


def _mod_row(mod_ref, row, part):
    return mod_ref[pl.ds(row, 1), part * D_MODEL:(part + 1) * D_MODEL]


def _front_stages(xin, row, mod_ref, n1_ref, win_ref, wa_ref, ba_ref, snw_ref, sws_ref, sbs_ref):
    h = _rms(xin, n1_ref[...]) * (1.0 + _mod_row(mod_ref, row, 1)) + _mod_row(mod_ref, row, 0)
    yield
    hb = h.astype(BF16)
    parts = []
    for r0, r1 in ((0, OFF_A), (OFF_U, D_IN), (OFF_A, OFF_U)):
        parts.append(lax.dot_general(hb, win_ref[r0:r1, :], (((1,), (1,)), ((), ())),
                                     preferred_element_type=F32))
        yield
    main, gate, low = parts
    q = main[:, 0:QK_W] * (GLA_DK ** -0.5)
    k = main[:, QK_W:2 * QK_W]
    v = main[:, 2 * QK_W:2 * QK_W + GLA_WIDTH]
    z = _dot(low, wa_ref[...]) + ba_ref[...]
    la = (jnp.minimum(z, 0.0) - jnp.log1p(jnp.exp(-jnp.abs(z)))) * (1.0 / GLA_GATE_NORM)
    yield

    ug = jax.nn.gelu(gate[:, 0:SGU_WIDTH])
    yield
    vg = jax.nn.gelu(gate[:, SGU_WIDTH:2 * SGU_WIDTH])
    yield
    s_cols = []
    for gi in range(SGU_GROUPS):
        cols = slice(gi * SGU_CH, (gi + 1) * SGU_CH)
        vn = _rms(vg[:, cols], snw_ref[:, cols])
        rhs = jnp.concatenate([vn[0:SGU_CHUNK], vn[SGU_CHUNK:2 * SGU_CHUNK]], axis=1)
        vm = _dot(sws_ref[gi], rhs) + jnp.concatenate([sbs_ref[gi], sbs_ref[gi]], axis=1)
        vm = jnp.concatenate([vm[:, 0:SGU_CH], vm[:, SGU_CH:2 * SGU_CH]], axis=0)
        s_cols.append(ug[:, cols] * vm)
    yield
    return q, k, v, main[:, 2 * QK_W + GLA_WIDTH:OFF_A], la, jnp.concatenate(s_cols, axis=1)


def _back_stages(o, g, s_val, xin, row, mod_ref, gnw_ref, wout_ref, n2_ref, rw_ref):
    cols = []
    for head in range(GLA_HEADS):
        hs = slice(head * GLA_DV, (head + 1) * GLA_DV)
        cols.append(_rms(o[:, hs], gnw_ref[...]) * jax.nn.silu(g[:, hs]))
    cols.append(s_val)
    yield
    y = _dot(jnp.concatenate(cols, axis=1), wout_ref[...])
    yield
    x1 = xin + _mod_row(mod_ref, row, 2) * y
    h2 = _rms(x1, n2_ref[...]) * (1.0 + _mod_row(mod_ref, row, 4)) + _mod_row(mod_ref, row, 3)
    yield
    logits = _dot_nt_3pass(rw_ref[...], h2)
    m = jnp.max(logits, axis=0, keepdims=True)
    ex = jnp.exp(logits - m)
    return x1, h2, ex / jnp.sum(ex, axis=0, keepdims=True)


def _store_token_tiles(h2t_ref, u, b, h2):
    for s in range(TILES_PER_TOKEN):
        h2t_ref[u, pl.ds(b * TOKEN_BLOCK * TILES_PER_TOKEN + s, TOKEN_BLOCK, stride=TILES_PER_TOKEN), :] = (
            h2[:, s * LANES:(s + 1) * LANES])


def _zip_stages(gen_a, gen_b):
    out = [None, None]
    live = [gen_a, gen_b]
    while any(g is not None for g in live):
        for i, g in enumerate(live):
            if g is not None:
                try:
                    next(g)
                except StopIteration as stop:
                    out[i] = stop.value
                    live[i] = None
        yield
    return out


def _mixer_fwd_kernel(group, add_pos, mod_base, mod_per_seq, nb,
                      x_ref, e_ref, mod_ref, n1_ref, win_ref, wa_ref, ba_ref,
                      snw_ref, sws_ref, sbs_ref, s0_ref,
                      qkv_ref, g_ref, lab_ref, s_ref, of_ref, sfin_ref,
                      st_ref):
    grp = pl.program_id(0)
    blk = pl.program_id(1)
    sub = x_ref.shape[1] // TOKEN_BLOCK
    lo_mask, _ = _chunk_masks()
    state_done = set()

    @pl.when(blk == 0)
    def _():
        for u in range(group):
            _load_state(s0_ref, u, st_ref.at[u])

    def chain(u, b):
        row = mod_base + ((grp * group + u) if mod_per_seq else 0)
        st_u = st_ref.at[u]
        rows = slice(b * TOKEN_BLOCK, (b + 1) * TOKEN_BLOCK)
        xin = _add_pos(x_ref[u, rows, :], e_ref, blk * sub + b, add_pos)
        q, k, v, g, la, s_val = yield from _front_stages(
            xin, row, mod_ref, n1_ref, win_ref, wa_ref, ba_ref, snw_ref, sws_ref, sbs_ref)
        qkv_ref[u, rows, :] = jnp.concatenate([q, k, v], axis=1)
        g_ref[u, rows, :] = g
        lab_ref[u, rows, :] = la[:, QK_W:2 * QK_W]
        s_ref[u, rows, :] = s_val

        cum = _select_dot(lo_mask, la[:, 0:QK_W])
        yield
        of_ref[u, rows, :] = yield from _gla_direction(
            q, k, v, cum, True, lo_mask, st_u,
            ready=lambda: b == 0 or (u, b - 1) in state_done, done=lambda: state_done.add((u, b)))

    _interleave(chain(u, b) for b in range(sub) for u in range(group))

    @pl.when(blk == nb - 1)
    def _():
        for u in range(group):
            _store_state(st_ref.at[u], sfin_ref, u)


def _mixer_fwd(x, e_tab, mod, n1, win, wa, ba, snw, sws, sbs, s0, *, nseq, nb, add_pos,
               mod_base, mod_per_seq):
    seq_len = nb * TOKEN_BLOCK
    group = min(SEQ_GROUP, nseq)
    sub = min(SEQ_GROUP // group, nb)
    nb //= sub
    tok = lambda w: pl.BlockSpec((group, sub * TOKEN_BLOCK, w), lambda s, i: (s, i, 0))
    full = lambda a: pl.BlockSpec(a.shape, lambda s, i: (0,) * a.ndim)
    st_spec = pl.BlockSpec((group, GLA_HEADS, GLA_DK, GLA_DV), lambda s, i: (s, 0, 0, 0))
    s0_spec = st_spec if s0.shape[0] == nseq else pl.BlockSpec(s0.shape, lambda s, i: (0, 0, 0, 0))
    act = lambda w: jax.ShapeDtypeStruct((nseq, seq_len, w), F32)
    kern = functools.partial(_mixer_fwd_kernel, group, add_pos, mod_base, mod_per_seq, nb)
    return pl.pallas_call(
        kern,
        grid=(nseq // group, nb),
        in_specs=[tok(D_MODEL), full(e_tab), full(mod), full(n1), full(win), full(wa), full(ba),
                  full(snw), full(sws), full(sbs), s0_spec],
        out_specs=[tok(1024), tok(GLA_WIDTH), tok(QK_W), tok(SGU_WIDTH), tok(GLA_WIDTH), st_spec],
        out_shape=[act(1024), act(GLA_WIDTH), act(QK_W), act(SGU_WIDTH), act(GLA_WIDTH),
                   jax.ShapeDtypeStruct((nseq, GLA_HEADS, GLA_DK, GLA_DV), F32)],
        scratch_shapes=[pltpu.VMEM((group, 2, 2 * GLA_DV, 2 * GLA_DK), F32)],
        compiler_params=_compiler_params(("arbitrary", "arbitrary")),
        name="mixer_fwd",
    )(x, e_tab, mod, n1, win, wa, ba, snw, sws, sbs, s0)


def _mixer_bwd_kernel(group, add_pos, mod_base, mod_per_seq, nb,
                      x_ref, e_ref, mod_ref, qkv_ref, g_ref, lab_ref, s_ref, of_ref,
                      gnw_ref, wout_ref, n2_ref, rw_ref, s0_ref,
                      x1_ref, h2t_ref, probs_ref, sfin_ref,
                      st_ref):
    grp = pl.program_id(0)
    step = pl.program_id(1)
    sub = x_ref.shape[1] // TOKEN_BLOCK
    blk = nb - 1 - step
    _, hi_mask = _chunk_masks()
    state_done = set()

    @pl.when(step == 0)
    def _():
        for u in range(group):
            _load_state(s0_ref, u, st_ref.at[u])

    def chain(u, b):
        row = mod_base + ((grp * group + u) if mod_per_seq else 0)
        st_u = st_ref.at[u]
        rows = slice(b * TOKEN_BLOCK, (b + 1) * TOKEN_BLOCK)
        qkv = qkv_ref[u, rows, :]
        q, k, v = qkv[:, 0:QK_W], qkv[:, QK_W:2 * QK_W], qkv[:, 2 * QK_W:]
        cum = _select_dot(hi_mask, lab_ref[u, rows, :])
        yield
        o_b = yield from _gla_direction(
            q, k, v, cum, False, hi_mask, st_u,
            ready=lambda: b == sub - 1 or (u, b + 1) in state_done,
            done=lambda: state_done.add((u, b)))
        xin = _add_pos(x_ref[u, rows, :], e_ref, blk * sub + b, add_pos)
        x1, h2, probs = yield from _back_stages(
            of_ref[u, rows, :] + o_b, g_ref[u, rows, :], s_ref[u, rows, :], xin, row,
            mod_ref, gnw_ref, wout_ref, n2_ref, rw_ref)
        x1_ref[u, rows, :] = x1
        _store_token_tiles(h2t_ref, u, b, h2)
        probs_ref[u, b] = probs

    _interleave(chain(u, b) for b in reversed(range(sub)) for u in range(group))

    @pl.when(step == nb - 1)
    def _():
        for u in range(group):
            _store_state(st_ref.at[u], sfin_ref, u)


def _mixer_bwd(x, e_tab, mod, qkv, g, lab, s, of, gnw, wout, n2, rw_t, s0, *, nseq, nb, add_pos,
               mod_base, mod_per_seq):
    seq_len = nb * TOKEN_BLOCK
    group = min(SEQ_GROUP, nseq)
    sub = min(SEQ_GROUP // group, nb)
    nb //= sub
    tok = lambda w: pl.BlockSpec((group, sub * TOKEN_BLOCK, w), lambda s_, i: (s_, nb - 1 - i, 0))
    full = lambda a: pl.BlockSpec(a.shape, lambda s_, i: (0,) * a.ndim)
    st_spec = pl.BlockSpec((group, GLA_HEADS, GLA_DK, GLA_DV), lambda s_, i: (s_, 0, 0, 0))
    s0_spec = st_spec if s0.shape[0] == nseq else pl.BlockSpec(s0.shape, lambda s_, i: (0, 0, 0, 0))
    kern = functools.partial(_mixer_bwd_kernel, group, add_pos, mod_base, mod_per_seq, nb)
    return pl.pallas_call(
        kern,
        grid=(nseq // group, nb),
        in_specs=[tok(D_MODEL), full(e_tab), full(mod), tok(1024), tok(GLA_WIDTH), tok(QK_W),
                  tok(SGU_WIDTH), tok(GLA_WIDTH), full(gnw), full(wout), full(n2), full(rw_t),
                  s0_spec],
        out_specs=[tok(D_MODEL),
                   pl.BlockSpec((group, sub * TOKEN_BLOCK * TILES_PER_TOKEN, LANES),
                                lambda s_, i: (s_, nb - 1 - i, 0)),
                   pl.BlockSpec((group, sub, N_EXPERTS, TOKEN_BLOCK),
                                lambda s_, i: (s_, nb - 1 - i, 0, 0)),
                   st_spec],
        out_shape=[jax.ShapeDtypeStruct((nseq, seq_len, D_MODEL), F32),
                   jax.ShapeDtypeStruct((nseq, seq_len * TILES_PER_TOKEN, LANES), F32),
                   jax.ShapeDtypeStruct((nseq, nb * sub, N_EXPERTS, TOKEN_BLOCK), F32),
                   jax.ShapeDtypeStruct((nseq, GLA_HEADS, GLA_DK, GLA_DV), F32)],
        scratch_shapes=[pltpu.VMEM((group, 2, 2 * GLA_DV, 2 * GLA_DK), F32)],
        compiler_params=_compiler_params(("arbitrary", "arbitrary")),
        name="mixer_bwd",
    )(x, e_tab, mod, qkv, g, lab, s, of, gnw, wout, n2, rw_t, s0)


def _mixer_fused_kernel(group, mod_base, mod_per_seq,
                        x_ref, mod_ref, n1_ref, win_ref, wa_ref, ba_ref, snw_ref, sws_ref, sbs_ref,
                        gnw_ref, wout_ref, n2_ref, rw_ref, s0f_ref, s0b_ref,
                        x1_ref, h2t_ref, probs_ref, sfin_f_ref, sfin_b_ref,
                        stf_ref, stb_ref):
    grp = pl.program_id(0)
    lo_mask, hi_mask = _chunk_masks()
    always = lambda: True
    nothing = lambda: None
    for u in range(group):
        _load_state(s0f_ref, u, stf_ref.at[u])
        _load_state(s0b_ref, u, stb_ref.at[u])

    def chain(u):
        row = mod_base + ((grp * group + u) if mod_per_seq else 0)
        xin = x_ref[u]
        q, k, v, g, la, s_val = yield from _front_stages(
            xin, row, mod_ref, n1_ref, win_ref, wa_ref, ba_ref, snw_ref, sws_ref, sbs_ref)
        cum_f = _select_dot(lo_mask, la[:, 0:QK_W])
        cum_b = _select_dot(hi_mask, la[:, QK_W:2 * QK_W])
        yield
        o_f, o_b = yield from _zip_stages(
            _gla_direction(q, k, v, cum_f, True, lo_mask, stf_ref.at[u], always, nothing),
            _gla_direction(q, k, v, cum_b, False, hi_mask, stb_ref.at[u], always, nothing))
        x1, h2, probs = yield from _back_stages(
            o_f + o_b, g, s_val, xin, row, mod_ref, gnw_ref, wout_ref, n2_ref, rw_ref)
        x1_ref[u] = x1
        _store_token_tiles(h2t_ref, u, 0, h2)
        probs_ref[u, 0] = probs

    _interleave(chain(u) for u in range(group))
    for u in range(group):
        _store_state(stf_ref.at[u], sfin_f_ref, u)
        _store_state(stb_ref.at[u], sfin_b_ref, u)


def _mixer_fused(x, mod, prm, s0_f, s0_b, *, nseq, mod_base, mod_per_seq):
    group = min(SEQ_GROUP, nseq)
    consts = [mod, prm["n1"], prm["win"], prm["wa"], prm["ba"], prm["snw"], prm["sws"], prm["sbs"],
              prm["gnw"], prm["wout"], prm["n2"], prm["rw_t"]]
    tok = lambda w: pl.BlockSpec((group, TOKEN_BLOCK, w), lambda s: (s, 0, 0))
    full = lambda a: pl.BlockSpec(a.shape, lambda s: (0,) * a.ndim)
    st_spec = pl.BlockSpec((group, GLA_HEADS, GLA_DK, GLA_DV), lambda s: (s, 0, 0, 0))
    s0_spec = lambda a: st_spec if a.shape[0] == nseq else pl.BlockSpec(a.shape, lambda s: (0, 0, 0, 0))
    state = jax.ShapeDtypeStruct((nseq, GLA_HEADS, GLA_DK, GLA_DV), F32)
    st_scratch = pltpu.VMEM((group, 2, 2 * GLA_DV, 2 * GLA_DK), F32)
    return pl.pallas_call(
        functools.partial(_mixer_fused_kernel, group, mod_base, mod_per_seq),
        grid=(nseq // group,),
        in_specs=[tok(D_MODEL)] + [full(a) for a in consts] + [s0_spec(s0_f), s0_spec(s0_b)],
        out_specs=[tok(D_MODEL),
                   pl.BlockSpec((group, TOKEN_BLOCK * TILES_PER_TOKEN, LANES), lambda s: (s, 0, 0)),
                   pl.BlockSpec((group, 1, N_EXPERTS, TOKEN_BLOCK), lambda s: (s, 0, 0, 0)),
                   st_spec, st_spec],
        out_shape=[jax.ShapeDtypeStruct((nseq, TOKEN_BLOCK, D_MODEL), F32),
                   jax.ShapeDtypeStruct((nseq, TOKEN_BLOCK * TILES_PER_TOKEN, LANES), F32),
                   jax.ShapeDtypeStruct((nseq, 1, N_EXPERTS, TOKEN_BLOCK), F32),
                   state, state],
        scratch_shapes=[st_scratch, st_scratch],
        compiler_params=_compiler_params(("arbitrary",)),
        name="mixer_fused",
    )(x, *consts, s0_f, s0_b)


def _route_kernel(n_tok, cap, probs_ref, row_ref, gate_ref, xs_ref, ps_ref):
    n_blk = n_tok // TOKEN_BLOCK
    n_chunk = n_tok // LANES
    probs = jnp.concatenate([probs_ref[b] for b in range(n_blk)], axis=1)
    capf = jnp.float32(cap)

    def count(mask):
        return jnp.sum(mask.astype(F32), axis=1, keepdims=True)

    def as_f32(bits):
        return lax.bitcast_convert_type(bits, F32)

    def thr_step(_, lohi):
        lo, hi = lohi
        mid = lo + ((hi - lo + 1) >> 1)
        ok = count(probs >= as_f32(mid)) >= capf
        return jnp.where(ok, mid, lo), jnp.where(ok, hi, mid - 1)

    lo0 = jnp.zeros((N_EXPERTS, 1), I32)
    hi0 = jnp.full((N_EXPERTS, 1), 0x3F800000, I32)
    thr, _ = lax.fori_loop(0, 31, thr_step, (lo0, hi0))
    gt = probs >= as_f32(thr + 1)
    eq = (probs >= as_f32(thr)) & jnp.logical_not(gt)
    need = capf - count(gt)
    tok = lax.broadcasted_iota(I32, (N_EXPERTS, n_tok), 1)

    def tie_step(_, lohi):
        lo, hi = lohi
        mid = (lo + hi) >> 1
        ok = count(eq & (tok <= mid)) >= need
        return jnp.where(ok, lo, mid + 1), jnp.where(ok, mid, hi)

    n_bits = max(1, (n_tok - 1).bit_length())
    cut, _ = lax.fori_loop(0, n_bits, tie_step,
                           (jnp.zeros((N_EXPERTS, 1), I32), jnp.full((N_EXPERTS, 1), n_tok - 1, I32)))
    sel = (gt | (eq & (tok <= cut))).astype(F32)

    xs_ref[...] = jnp.concatenate([sel[:, c * LANES:(c + 1) * LANES] for c in range(n_chunk)], axis=0)
    ps_ref[...] = jnp.concatenate([probs[:, c * LANES:(c + 1) * LANES] for c in range(n_chunk)], axis=0)

    li = lax.broadcasted_iota(I32, (LANES, LANES), 0)
    lj = lax.broadcasted_iota(I32, (LANES, LANES), 1)
    upper = (li <= lj).astype(F32)
    ci = lax.broadcasted_iota(I32, (n_chunk, n_chunk), 0)
    cj = lax.broadcasted_iota(I32, (n_chunk, n_chunk), 1)
    lower = (cj <= ci).astype(F32)
    slot = lax.broadcasted_iota(I32, (1, cap), 1).astype(F32)
    chunk_id = lax.broadcasted_iota(I32, (n_chunk, cap), 0).astype(F32)
    lane_id = lax.broadcasted_iota(I32, (LANES, cap), 0).astype(F32)
    reps = cap // LANES

    def store_flat(ref, e, val):
        for i in range(reps):
            ref[pl.ds(e * reps + i, 1), :] = val[:, i * LANES:(i + 1) * LANES]

    def per_expert(e):
        x = xs_ref[pl.ds(e, n_chunk, stride=N_EXPERTS), :]
        pe = ps_ref[pl.ds(e, n_chunk, stride=N_EXPERTS), :]
        ploc = _dot(x, upper)
        tot = jnp.broadcast_to(ploc[:, LANES - 1:LANES], (n_chunk, LANES))
        cum = _dot(lower, tot)
        yield
        cum_w = jnp.concatenate([cum] * reps, axis=1)
        base_w = jnp.concatenate([cum - tot] * reps, axis=1)
        chunk_of = jnp.sum((cum_w <= slot).astype(F32), axis=0, keepdims=True)
        onehot = chunk_id == chunk_of
        local = slot - jnp.sum(jnp.where(onehot, base_w, 0.0), axis=0, keepdims=True)
        yield
        lhs = jnp.concatenate([ploc.astype(BF16)] + _split_bf16(pe, 3), axis=1)
        got = _dot_tn(lhs, onehot.astype(F32))
        yield
        pref = got[0:LANES]
        lane_of = jnp.sum((pref <= local).astype(F32), axis=0, keepdims=True)
        token = chunk_of * LANES + lane_of
        store_flat(row_ref, e, (token * TILES_PER_TOKEN).astype(I32))
        yield
        pg = (got[3 * LANES:4 * LANES] + got[2 * LANES:3 * LANES]) + got[LANES:2 * LANES]
        store_flat(gate_ref, e, jnp.sum(jnp.where(lane_id == lane_of, pg, 0.0), axis=0, keepdims=True))

    def expert_pair(i, _):
        _interleave(per_expert(2 * i + u) for u in range(2))
        return 0

    lax.fori_loop(0, N_EXPERTS // 2, expert_pair, 0)


def _route(probs, n_tok, cap):
    return pl.pallas_call(
        functools.partial(_route_kernel, n_tok, cap),
        out_shape=[jax.ShapeDtypeStruct((N_EXPERTS * cap // LANES, LANES), I32),
                   jax.ShapeDtypeStruct((N_EXPERTS * cap // LANES, LANES), F32)],
        scratch_shapes=[pltpu.VMEM((n_tok // LANES * N_EXPERTS, LANES), F32),
                        pltpu.VMEM((n_tok // LANES * N_EXPERTS, LANES), F32)],
        compiler_params=pltpu.CompilerParams(vmem_limit_bytes=VMEM_LIMIT),
        name="route_topk",
    )(probs)


def _expert_kernel(cap, row_ref, h2t_ref, w1_ref, w3_ref, w2_ref, ye_ref,
                   xe_ref, x2_ref, hid_ref, sem):
    e = pl.program_id(0)
    f = pl.program_id(1)
    slot = e % 2
    rows_per_step = cap // EXPERT_STEPS

    def start_row(expert, buf, j):
        src = pl.multiple_of(row_ref[expert * cap + j], TILES_PER_TOKEN)
        dst = pl.multiple_of(j * TILES_PER_TOKEN, TILES_PER_TOKEN)
        pltpu.make_async_copy(h2t_ref.at[pl.ds(src, TILES_PER_TOKEN), :],
                              xe_ref.at[buf, pl.ds(dst, TILES_PER_TOKEN), :], sem.at[buf]).start()

    def wait_rows(buf):
        pltpu.make_async_copy(h2t_ref.at[pl.ds(0, cap * TILES_PER_TOKEN), :], xe_ref.at[buf],
                              sem.at[buf]).wait()

    def prefetch_next():
        nxt = jnp.minimum(e + 1, N_EXPERTS - 1)
        first = f * rows_per_step
        for j in range(rows_per_step):
            start_row(nxt, 1 - slot, first + j)

    @pl.when((e == 0) & (f == 0))
    def _():
        def issue(j, _):
            start_row(0, 0, j)
            return 0

        lax.fori_loop(0, cap, issue, 0, unroll=8)

    @pl.when(f == 0)
    def _():
        wait_rows(slot)
        for s in range(TILES_PER_TOKEN):
            x2_ref[:, s * LANES:(s + 1) * LANES] = (
                xe_ref[slot, pl.ds(s, cap, stride=TILES_PER_TOKEN), :].astype(BF16))

    @pl.when(f < N_HID_STEPS)
    def _():
        prefetch_next()
        x2 = x2_ref[...]
        a = jnp.dot(x2, w1_ref[0].astype(BF16), preferred_element_type=F32)
        b = jnp.dot(x2, w3_ref[0].astype(BF16), preferred_element_type=F32)
        hid_ref[f] = (jax.nn.silu(a) * b).astype(BF16)

    @pl.when(f >= N_HID_STEPS)
    def _():
        prefetch_next()
        w2 = w2_ref[0].astype(BF16)
        out = jnp.dot(hid_ref[0], w2[0:EXPERT_F_BLOCK], preferred_element_type=F32)
        for kb in range(1, N_HID_STEPS):
            out += jnp.dot(hid_ref[kb], w2[kb * EXPERT_F_BLOCK:(kb + 1) * EXPERT_F_BLOCK],
                           preferred_element_type=F32)
        tile0 = (f - N_HID_STEPS) * (EXPERT_N_BLOCK // LANES)
        for i in range(EXPERT_N_BLOCK // LANES):
            ye_ref[pl.ds(tile0 + i, cap, stride=TILES_PER_TOKEN), :] = out[:, i * LANES:(i + 1) * LANES]

    @pl.when((e == N_EXPERTS - 1) & (f == EXPERT_STEPS - 1))
    def _():
        wait_rows(1 - slot)


def _experts(rows, h2t, w1, w3, w2, cap):
    hid_blk = lambda e, f, idx: (e, 0, jnp.minimum(f, N_HID_STEPS - 1))
    n_out = D_MODEL // EXPERT_N_BLOCK

    def out_blk(e, f, idx):
        hold = (f == 0) & (e > 0)
        return (jnp.where(hold, e - 1, e), 0,
                jnp.where(hold, n_out - 1, jnp.maximum(f - N_HID_STEPS, 0)))

    grid_spec = pltpu.PrefetchScalarGridSpec(
        num_scalar_prefetch=1,
        grid=(N_EXPERTS, EXPERT_STEPS),
        in_specs=[pl.BlockSpec(memory_space=pl.ANY),
                  pl.BlockSpec((1, D_MODEL, EXPERT_F_BLOCK), hid_blk),
                  pl.BlockSpec((1, D_MODEL, EXPERT_F_BLOCK), hid_blk),
                  pl.BlockSpec((1, D_EXPERT, EXPERT_N_BLOCK), out_blk)],
        out_specs=pl.BlockSpec((cap * TILES_PER_TOKEN, LANES), lambda e, f, idx: (e, 0)),
        scratch_shapes=[pltpu.VMEM((2, cap * TILES_PER_TOKEN, LANES), F32),
                        pltpu.VMEM((cap, D_MODEL), BF16),
                        pltpu.VMEM((N_HID_STEPS, cap, EXPERT_F_BLOCK), BF16),
                        pltpu.SemaphoreType.DMA((2,))],
    )
    return pl.pallas_call(
        functools.partial(_expert_kernel, cap),
        grid_spec=grid_spec,
        out_shape=jax.ShapeDtypeStruct((N_EXPERTS * cap * TILES_PER_TOKEN, LANES), F32),
        compiler_params=_compiler_params(("arbitrary", "arbitrary")),
        name="expert_swiglu",
    )(rows, h2t, w1, w3, w2)


COMBINE_BATCH = 8
ZERO_ROWS = 512


def _combine_kernel(cap, n_tok, mod_base, mod_per_seq, nb,
                    row_ref, gate_ref, ye_ref, x1_ref, mod_ref, fw_ref, y_ref,
                    acc_ref, xbuf_ref, ybuf_ref, sem_x, sem_y):
    e = pl.program_id(0)

    @pl.when(e == 0)
    def _():
        def zero(i, _):
            r = pl.multiple_of(i * ZERO_ROWS, ZERO_ROWS)
            acc_ref[pl.ds(r, ZERO_ROWS), :] = jnp.zeros((ZERO_ROWS, LANES), F32)
            return 0

        lax.fori_loop(0, n_tok * TILES_PER_TOKEN // ZERO_ROWS, zero, 0)

    def batch(jb, _):
        vals = []
        for u in range(COMBINE_BATCH):
            j = jb * COMBINE_BATCH + u
            t = pl.multiple_of(row_ref[e * cap + j], TILES_PER_TOKEN)
            gate = gate_ref[e * cap + j]
            src = pl.multiple_of(j * TILES_PER_TOKEN, TILES_PER_TOKEN)
            vals.append((t, acc_ref[pl.ds(t, TILES_PER_TOKEN), :]
                         + ye_ref[pl.ds(src, TILES_PER_TOKEN), :] * gate))
        for t, val in vals:
            acc_ref[pl.ds(t, TILES_PER_TOKEN), :] = val
        return 0

    lax.fori_loop(0, cap // COMBINE_BATCH, batch, 0)

    @pl.when(e == N_EXPERTS - 1)
    def _():
        n_blk = n_tok // TOKEN_BLOCK

        def x1_copy(b, slot):
            rows = pl.ds(pl.multiple_of(b * TOKEN_BLOCK, TOKEN_BLOCK), TOKEN_BLOCK)
            return pltpu.make_async_copy(x1_ref.at[rows, :], xbuf_ref.at[slot], sem_x.at[slot])

        def y_copy(b, slot):
            rows = pl.ds(pl.multiple_of(b * TOKEN_BLOCK, TOKEN_BLOCK), TOKEN_BLOCK)
            return pltpu.make_async_copy(ybuf_ref.at[slot], y_ref.at[rows, :], sem_y.at[slot])

        x1_copy(0, 0).start()

        def block(b, _):
            slot = b % 2

            @pl.when(b + 1 < n_blk)
            def _():
                x1_copy(b + 1, 1 - slot).start()

            x1_copy(b, slot).wait()

            @pl.when(b >= 2)
            def _():
                y_copy(b - 2, slot).wait()

            base = pl.multiple_of(b * (TOKEN_BLOCK * TILES_PER_TOKEN), TOKEN_BLOCK * TILES_PER_TOKEN)
            moe = jnp.concatenate(
                [acc_ref[pl.ds(base + s, TOKEN_BLOCK, stride=TILES_PER_TOKEN), :]
                 for s in range(TILES_PER_TOKEN)], axis=1)
            row = mod_base + ((b // nb) if mod_per_seq else 0)
            ybuf_ref[slot] = _rms(xbuf_ref[slot] + _mod_row(mod_ref, row, 5) * moe, fw_ref[...])
            y_copy(b, slot).start()
            return 0

        lax.fori_loop(0, n_blk, block, 0)
        y_copy(n_blk - 2, n_blk % 2).wait()
        y_copy(n_blk - 1, (n_blk - 1) % 2).wait()


def _combine_final(rows, gates, ye, x1, mod, fw, cap, n_tok, *, nb, mod_base, mod_per_seq):
    full = lambda a: pl.BlockSpec(a.shape, lambda e, rows, gate: (0,) * a.ndim)
    grid_spec = pltpu.PrefetchScalarGridSpec(
        num_scalar_prefetch=2,
        grid=(N_EXPERTS,),
        in_specs=[pl.BlockSpec((cap * TILES_PER_TOKEN, LANES), lambda e, rows, gate: (e, 0)),
                  pl.BlockSpec(memory_space=pl.ANY), full(mod), full(fw)],
        out_specs=pl.BlockSpec(memory_space=pl.ANY),
        scratch_shapes=[pltpu.VMEM((n_tok * TILES_PER_TOKEN, LANES), F32),
                        pltpu.VMEM((2, TOKEN_BLOCK, D_MODEL), F32),
                        pltpu.VMEM((2, TOKEN_BLOCK, D_MODEL), F32),
                        pltpu.SemaphoreType.DMA((2,)),
                        pltpu.SemaphoreType.DMA((2,))],
    )
    return pl.pallas_call(
        functools.partial(_combine_kernel, cap, n_tok, mod_base, mod_per_seq, nb),
        grid_spec=grid_spec,
        out_shape=jax.ShapeDtypeStruct((n_tok, D_MODEL), F32),
        compiler_params=_compiler_params(("arbitrary",)),
        name="moe_combine_norm",
    )(rows, gates, ye, x1, mod, fw)


def _trunk_and_norm(x, e_tab, mod, s0_f, s0_b, prm, *, nseq, seq_len, add_pos, mod_base, mod_per_seq):
    nb = seq_len // TOKEN_BLOCK
    n_tok = nseq * seq_len
    cap = EC_CAPACITY_FACTOR * n_tok // N_EXPERTS
    kw = dict(nseq=nseq, nb=nb, add_pos=add_pos, mod_base=mod_base, mod_per_seq=mod_per_seq)
    if nb == 1 and not add_pos:
        x1, h2t, probs, sfin_f, sfin_b = _mixer_fused(
            x, mod, prm, s0_f, s0_b, nseq=nseq, mod_base=mod_base, mod_per_seq=mod_per_seq)
    else:
        qkv, g, lab, s, of, sfin_f = _mixer_fwd(
            x, e_tab, mod, prm["n1"], prm["win"], prm["wa"], prm["ba"], prm["snw"], prm["sws"],
            prm["sbs"], s0_f, **kw)
        x1, h2t, probs, sfin_b = _mixer_bwd(
            x, e_tab, mod, qkv, g, lab, s, of, prm["gnw"], prm["wout"], prm["n2"], prm["rw_t"],
            s0_b, **kw)
    x1 = x1.reshape(n_tok, D_MODEL)
    h2t = h2t.reshape(n_tok * TILES_PER_TOKEN, LANES)
    probs = probs.reshape(n_tok // TOKEN_BLOCK, N_EXPERTS, TOKEN_BLOCK)
    rows, gates = _route(probs, n_tok, cap)
    rows, gates = rows.reshape(-1), gates.reshape(-1)
    ye = _experts(rows, h2t, prm["w1"], prm["w3"], prm["w2"], cap)
    y = _combine_final(rows, gates, ye, x1, mod, prm["fw"], cap, n_tok,
                       nb=nb, mod_base=mod_base, mod_per_seq=mod_per_seq)
    return y.reshape(nseq, seq_len, D_MODEL), sfin_f, sfin_b


def kernel(x_prompt, x_sample, state_gla_fwd, state_gla_bwd, c, c_ctx, ada_w, ada_b, norm1_w, w_in, gla_wa2_f, gla_ba_f, gla_wa2_b, gla_ba_b, gla_norm_w, sgu_norm_w, sgu_ws, sgu_bs, w_out, norm2_w, router_w, exp_w1, exp_w3, exp_w2, final_norm_w):
    assert ada_w.shape[0] == 1, "single trunk layer"
    batch, seq, _ = x_prompt.shape
    dec_batch, dec_seq, _ = x_sample.shape

    assert w_in.shape[2] == D_IN
    win = _win_transposed_bf16(w_in[0].T)
    wa = jnp.zeros((2 * GLA_LOWRANK, 2 * QK_W), F32)
    wa = wa.at[0:GLA_LOWRANK, 0:QK_W].set(gla_wa2_f[0])
    wa = wa.at[GLA_LOWRANK:2 * GLA_LOWRANK, QK_W:].set(gla_wa2_b[0]).astype(BF16)
    prm = dict(
        n1=norm1_w, win=win, wa=wa,
        ba=jnp.concatenate([gla_ba_f[0], gla_ba_b[0]])[None, :],
        snw=sgu_norm_w, sws=sgu_ws[0].astype(BF16),
        sbs=jnp.broadcast_to(sgu_bs[0][:, :, None], (SGU_GROUPS, SGU_CHUNK, SGU_CH)),
        gnw=gla_norm_w, wout=w_out[0].astype(BF16), n2=norm2_w, rw_t=router_w[0].T,
        w1=exp_w1[0], w3=exp_w3[0], w2=exp_w2[0], fw=final_norm_w[None, :])

    cvec = jnp.concatenate([c_ctx[None, :], c, jnp.zeros((SUBLANES - 1 - dec_batch, D_MODEL), F32)])
    mod = _modulation(cvec, ada_w[0], ada_b)
    e_tab = _pos_table()

    zero_state = jnp.zeros((min(SEQ_GROUP, batch), GLA_HEADS, GLA_DK, GLA_DV), F32)
    y_prompt, sf, sb = _trunk_and_norm(
        x_prompt, e_tab, mod, zero_state, zero_state, prm,
        nseq=batch, seq_len=seq, add_pos=False, mod_base=0, mod_per_seq=False)
    y_sample, _, _ = _trunk_and_norm(
        x_sample, e_tab, mod, state_gla_fwd[:, 0], state_gla_bwd[:, 0], prm,
        nseq=dec_batch, seq_len=dec_seq, add_pos=True, mod_base=1, mod_per_seq=True)
    return (y_prompt, y_sample, sf[:, None], sb[:, None])
```

```python
import functools
import math

import jax
import jax.numpy as jnp
from jax import lax
from jax.experimental import pallas as pl
from jax.experimental.pallas import tpu as pltpu

F32 = jnp.float32
BF16 = jnp.bfloat16
I32 = jnp.int32

D_MODEL = 1024
GRID_W = 64
GLA_HEADS = 4
GLA_DK = 64
GLA_DV = 128
GLA_WIDTH = GLA_HEADS * GLA_DV
QK_W = GLA_HEADS * GLA_DK
GLA_LOWRANK = 16
GLA_GATE_NORM = 16.0
GLA_CHUNK = 64
SGU_WIDTH = 512
SGU_GROUPS = 4
SGU_CH = 128
SGU_CHUNK = 128
N_EXPERTS = 16
EC_CAPACITY_FACTOR = 2
D_EXPERT = 2048
EPS = 1e-6

SUBLANES = 8
LANES = 128
TILES_PER_TOKEN = D_MODEL // LANES

TOKEN_BLOCK = 256
SEQ_GROUP = 4
OFF_A = 2 * QK_W + 2 * GLA_WIDTH
OFF_U = OFF_A + 2 * GLA_LOWRANK
D_IN = OFF_U + 2 * SGU_WIDTH
EXPERT_F_BLOCK = 512
EXPERT_N_BLOCK = 512
N_HID_STEPS = D_EXPERT // EXPERT_F_BLOCK
EXPERT_STEPS = N_HID_STEPS + D_MODEL // EXPERT_N_BLOCK
VMEM_LIMIT = 56 * 1024 * 1024


def _dot(a, b):
    return jnp.dot(a.astype(BF16), b.astype(BF16), preferred_element_type=F32)


def _dot_nt(a, b):
    return lax.dot_general(a.astype(BF16), b.astype(BF16), (((1,), (1,)), ((), ())),
                           preferred_element_type=F32)


def _dot_tn(a, b):
    return lax.dot_general(a.astype(BF16), b.astype(BF16), (((0,), (0,)), ((), ())),
                           preferred_element_type=F32)


def _dot_f32(a, b, dims=(((1,), (0,)), ((), ()))):
    return lax.dot_general(a, b, dims, precision=lax.Precision.HIGHEST, preferred_element_type=F32)


def _split_bf16(x, terms):
    parts = []
    for _ in range(terms - 1):
        part = x.astype(BF16)
        parts.append(part)
        x = x - part.astype(F32)
    parts.append(x.astype(BF16))
    return parts


def _select_dot(sel, x):
    s = sel.astype(BF16)
    hi, mid, lo = _split_bf16(x, 3)
    return (jnp.dot(s, lo, preferred_element_type=F32) + jnp.dot(s, mid, preferred_element_type=F32)
            + jnp.dot(s, hi, preferred_element_type=F32))


def _dot_nt_3pass(a, b):
    a_hi, a_lo = _split_bf16(a, 2)
    b_hi, b_lo = _split_bf16(b, 2)
    nt = lambda x, y: lax.dot_general(x, y, (((1,), (1,)), ((), ())), preferred_element_type=F32)
    return (nt(a_hi, b_lo) + nt(a_lo, b_hi)) + nt(a_hi, b_hi)


def _rms(x, w):
    return x * lax.rsqrt(jnp.mean(x * x, axis=-1, keepdims=True) + EPS) * w


def _compiler_params(sem):
    return pltpu.CompilerParams(dimension_semantics=sem, vmem_limit_bytes=VMEM_LIMIT)


def _mod_kernel(c_ref, w_ref, b_ref, o_ref):
    o_ref[...] = _dot(jax.nn.silu(c_ref[...]), w_ref[...]) + b_ref[...]


def _modulation(cvec, ada_w, ada_b):
    n = ada_w.shape[1]
    bn = 1536
    return pl.pallas_call(
        _mod_kernel,
        grid=(n // bn,),
        in_specs=[pl.BlockSpec((SUBLANES, D_MODEL), lambda j: (0, 0)),
                  pl.BlockSpec((D_MODEL, bn), lambda j: (0, j)),
                  pl.BlockSpec((1, bn), lambda j: (0, j))],
        out_specs=pl.BlockSpec((SUBLANES, bn), lambda j: (0, j)),
        out_shape=jax.ShapeDtypeStruct((SUBLANES, n), F32),
        compiler_params=_compiler_params(("arbitrary",)),
        name="adaln_mod",
    )(cvec, ada_w, ada_b)


def _cast_kernel(w_ref, o_ref):
    o_ref[...] = w_ref[...].astype(BF16)


def _win_transposed_bf16(w_t):
    rows = D_IN // 3
    return pl.pallas_call(
        _cast_kernel,
        grid=(D_IN // rows,),
        in_specs=[pl.BlockSpec((rows, D_MODEL), lambda i: (i, 0))],
        out_specs=pl.BlockSpec((rows, D_MODEL), lambda i: (i, 0)),
        out_shape=jax.ShapeDtypeStruct((D_IN, D_MODEL), BF16),
        compiler_params=_compiler_params(("arbitrary",)),
        name="win_cast",
    )(w_t)


def _pos_kernel(o_ref):
    nf = D_MODEL // 4
    p = lax.broadcasted_iota(I32, (GRID_W, nf), 0).astype(F32)
    i = lax.broadcasted_iota(I32, (GRID_W, nf), 1).astype(F32)
    omega = jnp.exp(i * (-math.log(10000.0) / nf))
    a = p * omega
    o_ref[:, 0:nf] = jnp.sin(a)
    o_ref[:, nf:2 * nf] = jnp.cos(a)


def _pos_table():
    return pl.pallas_call(
        _pos_kernel,
        out_shape=jax.ShapeDtypeStruct((GRID_W, D_MODEL // 2), F32),
        name="sincos_table",
    )()


def _add_pos(x, e_ref, blk, add_pos):
    if not add_pos:
        return x
    half = D_MODEL // 2
    e_all = e_ref[...]
    rows = []
    for j in range(TOKEN_BLOCK // GRID_W):
        xj = x[j * GRID_W:(j + 1) * GRID_W]
        e_row = e_ref[pl.ds(blk * (TOKEN_BLOCK // GRID_W) + j, 1), :]
        rows.append(jnp.concatenate([xj[:, 0:half] + e_row, xj[:, half:] + e_all], axis=1))
    return jnp.concatenate(rows, axis=0)


def _chunk_masks():
    r = lax.broadcasted_iota(I32, (TOKEN_BLOCK, TOKEN_BLOCK), 0)
    c = lax.broadcasted_iota(I32, (TOKEN_BLOCK, TOKEN_BLOCK), 1)
    same = (r // GLA_CHUNK) == (c // GLA_CHUNK)
    return same & (c <= r), same & (c >= r)


def _gla_direction(q, k, v, cum, fwd, att_mask, st_ref, ready, done):
    qe = q * jnp.exp(cum)
    ke = k * jnp.exp(-cum)
    yield
    lane = lax.broadcasted_iota(I32, (1, LANES), 1)
    o_intra = []
    for pair in range(2):
        qp = qe[:, pair * LANES:(pair + 1) * LANES]
        kp = ke[:, pair * LANES:(pair + 1) * LANES]
        for hh in range(2):
            qm = jnp.where((lane // GLA_DK) == hh, qp, 0.0)
            att = jnp.where(att_mask, _dot_nt(qm, kp), 0.0)
            head = 2 * pair + hh
            o_intra.append(_dot(att, v[:, head * GLA_DV:(head + 1) * GLA_DV]))
            yield
    o_intra = jnp.concatenate(o_intra, axis=1)
    while not ready():
        yield

    er = lax.broadcasted_iota(I32, (2 * GLA_DV, 2 * GLA_DK), 0)
    dc = lax.broadcasted_iota(I32, (2 * GLA_DV, 2 * GLA_DK), 1)
    same_head = (er // GLA_DV) == (dc // GLA_DK)
    n_chunks = TOKEN_BLOCK // GLA_CHUNK
    o_inter = [None] * n_chunks
    for c in (range(n_chunks) if fwd else reversed(range(n_chunks))):
        r0 = c * GLA_CHUNK
        rows = slice(r0, r0 + GLA_CHUNK)
        last = cum[r0 + GLA_CHUNK - 1:r0 + GLA_CHUNK] if fwd else cum[r0:r0 + 1]
        kd = k[rows] * jnp.exp(last - cum[rows])
        dec = jnp.exp(last)
        parts = []
        for pair in range(2):
            dl = slice(pair * LANES, (pair + 1) * LANES)
            st = st_ref[pair]
            parts.append(_dot_nt(qe[rows, dl], st))
            ds_t = _dot_tn(v[rows, pair * 2 * GLA_DV:(pair + 1) * 2 * GLA_DV], kd[:, dl])
            st_ref[pair] = dec[:, dl] * st + jnp.where(same_head, ds_t, 0.0)
        o_inter[c] = jnp.concatenate(parts, axis=1)
        yield
    done()
    return o_intra + jnp.concatenate(o_inter, axis=0)


def _interleave(chains):
    chains = list(chains)
    done = [False] * len(chains)
    tick = 0
    while not all(done):
        for i, ch in enumerate(chains):
            if tick >= i and not done[i]:
                try:
                    next(ch)
                except StopIteration:
                    done[i] = True
        tick += 1


def _load_state(s0_ref, u, st_ref):
    zero = jnp.zeros((GLA_DV, GLA_DK), F32)
    for pair in range(2):
        a = s0_ref[u, 2 * pair].T
        b = s0_ref[u, 2 * pair + 1].T
        st_ref[pair] = jnp.concatenate(
            [jnp.concatenate([a, zero], axis=1), jnp.concatenate([zero, b], axis=1)], axis=0)


def _store_state(st_ref, sfin_ref, u):
    for pair in range(2):
        st = st_ref[pair]
        sfin_ref[u, 2 * pair] = st[0:GLA_DV, 0:GLA_DK].T
        sfin_ref[u, 2 * pair + 1] = st[GLA_DV:2 * GLA_DV, GLA_DK:2 * GLA_DK].T


def _mod_row(mod_ref, row, part):
    return mod_ref[pl.ds(row, 1), part * D_MODEL:(part + 1) * D_MODEL]


def _front_stages(xin, row, mod_ref, n1_ref, win_ref, wa_ref, ba_ref, snw_ref, sws_ref, sbs_ref):
    h = _rms(xin, n1_ref[...]) * (1.0 + _mod_row(mod_ref, row, 1)) + _mod_row(mod_ref, row, 0)
    yield
    hb = h.astype(BF16)
    parts = []
    for r0, r1 in ((0, OFF_A), (OFF_U, D_IN), (OFF_A, OFF_U)):
        parts.append(lax.dot_general(hb, win_ref[r0:r1, :], (((1,), (1,)), ((), ())),
                                     preferred_element_type=F32))
        yield
    main, gate, low = parts
    q = main[:, 0:QK_W] * (GLA_DK ** -0.5)
    k = main[:, QK_W:2 * QK_W]
    v = main[:, 2 * QK_W:2 * QK_W + GLA_WIDTH]
    z = _dot(low, wa_ref[...]) + ba_ref[...]
    la = (jnp.minimum(z, 0.0) - jnp.log1p(jnp.exp(-jnp.abs(z)))) * (1.0 / GLA_GATE_NORM)
    yield

    ug = jax.nn.gelu(gate[:, 0:SGU_WIDTH])
    yield
    vg = jax.nn.gelu(gate[:, SGU_WIDTH:2 * SGU_WIDTH])
    yield
    s_cols = []
    for gi in range(SGU_GROUPS):
        cols = slice(gi * SGU_CH, (gi + 1) * SGU_CH)
        vn = _rms(vg[:, cols], snw_ref[:, cols])
        rhs = jnp.concatenate([vn[0:SGU_CHUNK], vn[SGU_CHUNK:2 * SGU_CHUNK]], axis=1)
        vm = _dot(sws_ref[gi], rhs) + jnp.concatenate([sbs_ref[gi], sbs_ref[gi]], axis=1)
        vm = jnp.concatenate([vm[:, 0:SGU_CH], vm[:, SGU_CH:2 * SGU_CH]], axis=0)
        s_cols.append(ug[:, cols] * vm)
    yield
    return q, k, v, main[:, 2 * QK_W + GLA_WIDTH:OFF_A], la, jnp.concatenate(s_cols, axis=1)


def _back_stages(o, g, s_val, xin, row, mod_ref, gnw_ref, wout_ref, n2_ref, rw_ref):
    cols = []
    for head in range(GLA_HEADS):
        hs = slice(head * GLA_DV, (head + 1) * GLA_DV)
        cols.append(_rms(o[:, hs], gnw_ref[...]) * jax.nn.silu(g[:, hs]))
    cols.append(s_val)
    yield
    y = _dot(jnp.concatenate(cols, axis=1), wout_ref[...])
    yield
    x1 = xin + _mod_row(mod_ref, row, 2) * y
    h2 = _rms(x1, n2_ref[...]) * (1.0 + _mod_row(mod_ref, row, 4)) + _mod_row(mod_ref, row, 3)
    yield
    logits = _dot_nt_3pass(rw_ref[...], h2)
    m = jnp.max(logits, axis=0, keepdims=True)
    ex = jnp.exp(logits - m)
    return x1, h2, ex / jnp.sum(ex, axis=0, keepdims=True)


def _store_token_tiles(h2t_ref, u, b, h2):
    for s in range(TILES_PER_TOKEN):
        h2t_ref[u, pl.ds(b * TOKEN_BLOCK * TILES_PER_TOKEN + s, TOKEN_BLOCK, stride=TILES_PER_TOKEN), :] = (
            h2[:, s * LANES:(s + 1) * LANES])


def _zip_stages(gen_a, gen_b):
    out = [None, None]
    live = [gen_a, gen_b]
    while any(g is not None for g in live):
        for i, g in enumerate(live):
            if g is not None:
                try:
                    next(g)
                except StopIteration as stop:
                    out[i] = stop.value
                    live[i] = None
        yield
    return out


def _mixer_fwd_kernel(group, add_pos, mod_base, mod_per_seq, nb,
                      x_ref, e_ref, mod_ref, n1_ref, win_ref, wa_ref, ba_ref,
                      snw_ref, sws_ref, sbs_ref, s0_ref,
                      qkv_ref, g_ref, lab_ref, s_ref, of_ref, sfin_ref,
                      st_ref):
    grp = pl.program_id(0)
    blk = pl.program_id(1)
    sub = x_ref.shape[1] // TOKEN_BLOCK
    lo_mask, _ = _chunk_masks()
    state_done = set()

    @pl.when(blk == 0)
    def _():
        for u in range(group):
            _load_state(s0_ref, u, st_ref.at[u])

    def chain(u, b):
        row = mod_base + ((grp * group + u) if mod_per_seq else 0)
        st_u = st_ref.at[u]
        rows = slice(b * TOKEN_BLOCK, (b + 1) * TOKEN_BLOCK)
        xin = _add_pos(x_ref[u, rows, :], e_ref, blk * sub + b, add_pos)
        q, k, v, g, la, s_val = yield from _front_stages(
            xin, row, mod_ref, n1_ref, win_ref, wa_ref, ba_ref, snw_ref, sws_ref, sbs_ref)
        qkv_ref[u, rows, :] = jnp.concatenate([q, k, v], axis=1)
        g_ref[u, rows, :] = g
        lab_ref[u, rows, :] = la[:, QK_W:2 * QK_W]
        s_ref[u, rows, :] = s_val

        cum = _select_dot(lo_mask, la[:, 0:QK_W])
        yield
        of_ref[u, rows, :] = yield from _gla_direction(
            q, k, v, cum, True, lo_mask, st_u,
            ready=lambda: b == 0 or (u, b - 1) in state_done, done=lambda: state_done.add((u, b)))

    _interleave(chain(u, b) for b in range(sub) for u in range(group))

    @pl.when(blk == nb - 1)
    def _():
        for u in range(group):
            _store_state(st_ref.at[u], sfin_ref, u)


def _mixer_fwd(x, e_tab, mod, n1, win, wa, ba, snw, sws, sbs, s0, *, nseq, nb, add_pos,
               mod_base, mod_per_seq):
    seq_len = nb * TOKEN_BLOCK
    group = min(SEQ_GROUP, nseq)
    sub = min(SEQ_GROUP // group, nb)
    nb //= sub
    tok = lambda w: pl.BlockSpec((group, sub * TOKEN_BLOCK, w), lambda s, i: (s, i, 0))
    full = lambda a: pl.BlockSpec(a.shape, lambda s, i: (0,) * a.ndim)
    st_spec = pl.BlockSpec((group, GLA_HEADS, GLA_DK, GLA_DV), lambda s, i: (s, 0, 0, 0))
    s0_spec = st_spec if s0.shape[0] == nseq else pl.BlockSpec(s0.shape, lambda s, i: (0, 0, 0, 0))
    act = lambda w: jax.ShapeDtypeStruct((nseq, seq_len, w), F32)
    kern = functools.partial(_mixer_fwd_kernel, group, add_pos, mod_base, mod_per_seq, nb)
    return pl.pallas_call(
        kern,
        grid=(nseq // group, nb),
        in_specs=[tok(D_MODEL), full(e_tab), full(mod), full(n1), full(win), full(wa), full(ba),
                  full(snw), full(sws), full(sbs), s0_spec],
        out_specs=[tok(1024), tok(GLA_WIDTH), tok(QK_W), tok(SGU_WIDTH), tok(GLA_WIDTH), st_spec],
        out_shape=[act(1024), act(GLA_WIDTH), act(QK_W), act(SGU_WIDTH), act(GLA_WIDTH),
                   jax.ShapeDtypeStruct((nseq, GLA_HEADS, GLA_DK, GLA_DV), F32)],
        scratch_shapes=[pltpu.VMEM((group, 2, 2 * GLA_DV, 2 * GLA_DK), F32)],
        compiler_params=_compiler_params(("arbitrary", "arbitrary")),
        name="mixer_fwd",
    )(x, e_tab, mod, n1, win, wa, ba, snw, sws, sbs, s0)


def _mixer_bwd_kernel(group, add_pos, mod_base, mod_per_seq, nb,
                      x_ref, e_ref, mod_ref, qkv_ref, g_ref, lab_ref, s_ref, of_ref,
                      gnw_ref, wout_ref, n2_ref, rw_ref, s0_ref,
                      x1_ref, h2t_ref, probs_ref, sfin_ref,
                      st_ref):
    grp = pl.program_id(0)
    step = pl.program_id(1)
    sub = x_ref.shape[1] // TOKEN_BLOCK
    blk = nb - 1 - step
    _, hi_mask = _chunk_masks()
    state_done = set()

    @pl.when(step == 0)
    def _():
        for u in range(group):
            _load_state(s0_ref, u, st_ref.at[u])

    def chain(u, b):
        row = mod_base + ((grp * group + u) if mod_per_seq else 0)
        st_u = st_ref.at[u]
        rows = slice(b * TOKEN_BLOCK, (b + 1) * TOKEN_BLOCK)
        qkv = qkv_ref[u, rows, :]
        q, k, v = qkv[:, 0:QK_W], qkv[:, QK_W:2 * QK_W], qkv[:, 2 * QK_W:]
        cum = _select_dot(hi_mask, lab_ref[u, rows, :])
        yield
        o_b = yield from _gla_direction(
            q, k, v, cum, False, hi_mask, st_u,
            ready=lambda: b == sub - 1 or (u, b + 1) in state_done,
            done=lambda: state_done.add((u, b)))
        xin = _add_pos(x_ref[u, rows, :], e_ref, blk * sub + b, add_pos)
        x1, h2, probs = yield from _back_stages(
            of_ref[u, rows, :] + o_b, g_ref[u, rows, :], s_ref[u, rows, :], xin, row,
            mod_ref, gnw_ref, wout_ref, n2_ref, rw_ref)
        x1_ref[u, rows, :] = x1
        _store_token_tiles(h2t_ref, u, b, h2)
        probs_ref[u, b] = probs

    _interleave(chain(u, b) for b in reversed(range(sub)) for u in range(group))

    @pl.when(step == nb - 1)
    def _():
        for u in range(group):
            _store_state(st_ref.at[u], sfin_ref, u)


def _mixer_bwd(x, e_tab, mod, qkv, g, lab, s, of, gnw, wout, n2, rw_t, s0, *, nseq, nb, add_pos,
               mod_base, mod_per_seq):
    seq_len = nb * TOKEN_BLOCK
    group = min(SEQ_GROUP, nseq)
    sub = min(SEQ_GROUP // group, nb)
    nb //= sub
    tok = lambda w: pl.BlockSpec((group, sub * TOKEN_BLOCK, w), lambda s_, i: (s_, nb - 1 - i, 0))
    full = lambda a: pl.BlockSpec(a.shape, lambda s_, i: (0,) * a.ndim)
    st_spec = pl.BlockSpec((group, GLA_HEADS, GLA_DK, GLA_DV), lambda s_, i: (s_, 0, 0, 0))
    s0_spec = st_spec if s0.shape[0] == nseq else pl.BlockSpec(s0.shape, lambda s_, i: (0, 0, 0, 0))
    kern = functools.partial(_mixer_bwd_kernel, group, add_pos, mod_base, mod_per_seq, nb)
    return pl.pallas_call(
        kern,
        grid=(nseq // group, nb),
        in_specs=[tok(D_MODEL), full(e_tab), full(mod), tok(1024), tok(GLA_WIDTH), tok(QK_W),
                  tok(SGU_WIDTH), tok(GLA_WIDTH), full(gnw), full(wout), full(n2), full(rw_t),
                  s0_spec],
        out_specs=[tok(D_MODEL),
                   pl.BlockSpec((group, sub * TOKEN_BLOCK * TILES_PER_TOKEN, LANES),
                                lambda s_, i: (s_, nb - 1 - i, 0)),
                   pl.BlockSpec((group, sub, N_EXPERTS, TOKEN_BLOCK),
                                lambda s_, i: (s_, nb - 1 - i, 0, 0)),
                   st_spec],
        out_shape=[jax.ShapeDtypeStruct((nseq, seq_len, D_MODEL), F32),
                   jax.ShapeDtypeStruct((nseq, seq_len * TILES_PER_TOKEN, LANES), F32),
                   jax.ShapeDtypeStruct((nseq, nb * sub, N_EXPERTS, TOKEN_BLOCK), F32),
                   jax.ShapeDtypeStruct((nseq, GLA_HEADS, GLA_DK, GLA_DV), F32)],
        scratch_shapes=[pltpu.VMEM((group, 2, 2 * GLA_DV, 2 * GLA_DK), F32)],
        compiler_params=_compiler_params(("arbitrary", "arbitrary")),
        name="mixer_bwd",
    )(x, e_tab, mod, qkv, g, lab, s, of, gnw, wout, n2, rw_t, s0)


def _mixer_fused_kernel(group, mod_base, mod_per_seq,
                        x_ref, mod_ref, n1_ref, win_ref, wa_ref, ba_ref, snw_ref, sws_ref, sbs_ref,
                        gnw_ref, wout_ref, n2_ref, rw_ref, s0f_ref, s0b_ref,
                        x1_ref, h2t_ref, probs_ref, sfin_f_ref, sfin_b_ref,
                        stf_ref, stb_ref):
    grp = pl.program_id(0)
    lo_mask, hi_mask = _chunk_masks()
    always = lambda: True
    nothing = lambda: None
    for u in range(group):
        _load_state(s0f_ref, u, stf_ref.at[u])
        _load_state(s0b_ref, u, stb_ref.at[u])

    def chain(u):
        row = mod_base + ((grp * group + u) if mod_per_seq else 0)
        xin = x_ref[u]
        q, k, v, g, la, s_val = yield from _front_stages(
            xin, row, mod_ref, n1_ref, win_ref, wa_ref, ba_ref, snw_ref, sws_ref, sbs_ref)
        cum_f = _select_dot(lo_mask, la[:, 0:QK_W])
        cum_b = _select_dot(hi_mask, la[:, QK_W:2 * QK_W])
        yield
        o_f, o_b = yield from _zip_stages(
            _gla_direction(q, k, v, cum_f, True, lo_mask, stf_ref.at[u], always, nothing),
            _gla_direction(q, k, v, cum_b, False, hi_mask, stb_ref.at[u], always, nothing))
        x1, h2, probs = yield from _back_stages(
            o_f + o_b, g, s_val, xin, row, mod_ref, gnw_ref, wout_ref, n2_ref, rw_ref)
        x1_ref[u] = x1
        _store_token_tiles(h2t_ref, u, 0, h2)
        probs_ref[u, 0] = probs

    _interleave(chain(u) for u in range(group))
    for u in range(group):
        _store_state(stf_ref.at[u], sfin_f_ref, u)
        _store_state(stb_ref.at[u], sfin_b_ref, u)


def _mixer_fused(x, mod, prm, s0_f, s0_b, *, nseq, mod_base, mod_per_seq):
    group = min(SEQ_GROUP, nseq)
    consts = [mod, prm["n1"], prm["win"], prm["wa"], prm["ba"], prm["snw"], prm["sws"], prm["sbs"],
              prm["gnw"], prm["wout"], prm["n2"], prm["rw_t"]]
    tok = lambda w: pl.BlockSpec((group, TOKEN_BLOCK, w), lambda s: (s, 0, 0))
    full = lambda a: pl.BlockSpec(a.shape, lambda s: (0,) * a.ndim)
    st_spec = pl.BlockSpec((group, GLA_HEADS, GLA_DK, GLA_DV), lambda s: (s, 0, 0, 0))
    s0_spec = lambda a: st_spec if a.shape[0] == nseq else pl.BlockSpec(a.shape, lambda s: (0, 0, 0, 0))
    state = jax.ShapeDtypeStruct((nseq, GLA_HEADS, GLA_DK, GLA_DV), F32)
    st_scratch = pltpu.VMEM((group, 2, 2 * GLA_DV, 2 * GLA_DK), F32)
    return pl.pallas_call(
        functools.partial(_mixer_fused_kernel, group, mod_base, mod_per_seq),
        grid=(nseq // group,),
        in_specs=[tok(D_MODEL)] + [full(a) for a in consts] + [s0_spec(s0_f), s0_spec(s0_b)],
        out_specs=[tok(D_MODEL),
                   pl.BlockSpec((group, TOKEN_BLOCK * TILES_PER_TOKEN, LANES), lambda s: (s, 0, 0)),
                   pl.BlockSpec((group, 1, N_EXPERTS, TOKEN_BLOCK), lambda s: (s, 0, 0, 0)),
                   st_spec, st_spec],
        out_shape=[jax.ShapeDtypeStruct((nseq, TOKEN_BLOCK, D_MODEL), F32),
                   jax.ShapeDtypeStruct((nseq, TOKEN_BLOCK * TILES_PER_TOKEN, LANES), F32),
                   jax.ShapeDtypeStruct((nseq, 1, N_EXPERTS, TOKEN_BLOCK), F32),
                   state, state],
        scratch_shapes=[st_scratch, st_scratch],
        compiler_params=_compiler_params(("arbitrary",)),
        name="mixer_fused",
    )(x, *consts, s0_f, s0_b)


def _route_kernel(n_tok, cap, probs_ref, row_ref, gate_ref, xs_ref, ps_ref):
    n_blk = n_tok // TOKEN_BLOCK
    n_chunk = n_tok // LANES
    probs = jnp.concatenate([probs_ref[b] for b in range(n_blk)], axis=1)
    capf = jnp.float32(cap)

    def count(mask):
        return jnp.sum(mask.astype(F32), axis=1, keepdims=True)

    def as_f32(bits):
        return lax.bitcast_convert_type(bits, F32)

    def thr_step(_, lohi):
        lo, hi = lohi
        mid = lo + ((hi - lo + 1) >> 1)
        ok = count(probs >= as_f32(mid)) >= capf
        return jnp.where(ok, mid, lo), jnp.where(ok, hi, mid - 1)

    lo0 = jnp.zeros((N_EXPERTS, 1), I32)
    hi0 = jnp.full((N_EXPERTS, 1), 0x3F800000, I32)
    thr, _ = lax.fori_loop(0, 31, thr_step, (lo0, hi0))
    gt = probs >= as_f32(thr + 1)
    eq = (probs >= as_f32(thr)) & jnp.logical_not(gt)
    need = capf - count(gt)
    tok = lax.broadcasted_iota(I32, (N_EXPERTS, n_tok), 1)

    def tie_step(_, lohi):
        lo, hi = lohi
        mid = (lo + hi) >> 1
        ok = count(eq & (tok <= mid)) >= need
        return jnp.where(ok, lo, mid + 1), jnp.where(ok, mid, hi)

    n_bits = max(1, (n_tok - 1).bit_length())
    cut, _ = lax.fori_loop(0, n_bits, tie_step,
                           (jnp.zeros((N_EXPERTS, 1), I32), jnp.full((N_EXPERTS, 1), n_tok - 1, I32)))
    sel = (gt | (eq & (tok <= cut))).astype(F32)

    xs_ref[...] = jnp.concatenate([sel[:, c * LANES:(c + 1) * LANES] for c in range(n_chunk)], axis=0)
    ps_ref[...] = jnp.concatenate([probs[:, c * LANES:(c + 1) * LANES] for c in range(n_chunk)], axis=0)

    li = lax.broadcasted_iota(I32, (LANES, LANES), 0)
    lj = lax.broadcasted_iota(I32, (LANES, LANES), 1)
    upper = (li <= lj).astype(F32)
    ci = lax.broadcasted_iota(I32, (n_chunk, n_chunk), 0)
    cj = lax.broadcasted_iota(I32, (n_chunk, n_chunk), 1)
    lower = (cj <= ci).astype(F32)
    slot = lax.broadcasted_iota(I32, (1, cap), 1).astype(F32)
    chunk_id = lax.broadcasted_iota(I32, (n_chunk, cap), 0).astype(F32)
    lane_id = lax.broadcasted_iota(I32, (LANES, cap), 0).astype(F32)
    reps = cap // LANES

    def store_flat(ref, e, val):
        for i in range(reps):
            ref[pl.ds(e * reps + i, 1), :] = val[:, i * LANES:(i + 1) * LANES]

    def per_expert(e):
        x = xs_ref[pl.ds(e, n_chunk, stride=N_EXPERTS), :]
        pe = ps_ref[pl.ds(e, n_chunk, stride=N_EXPERTS), :]
        ploc = _dot(x, upper)
        tot = jnp.broadcast_to(ploc[:, LANES - 1:LANES], (n_chunk, LANES))
        cum = _dot(lower, tot)
        yield
        cum_w = jnp.concatenate([cum] * reps, axis=1)
        base_w = jnp.concatenate([cum - tot] * reps, axis=1)
        chunk_of = jnp.sum((cum_w <= slot).astype(F32), axis=0, keepdims=True)
        onehot = chunk_id == chunk_of
        local = slot - jnp.sum(jnp.where(onehot, base_w, 0.0), axis=0, keepdims=True)
        yield
        lhs = jnp.concatenate([ploc.astype(BF16)] + _split_bf16(pe, 3), axis=1)
        got = _dot_tn(lhs, onehot.astype(F32))
        yield
        pref = got[0:LANES]
        lane_of = jnp.sum((pref <= local).astype(F32), axis=0, keepdims=True)
        token = chunk_of * LANES + lane_of
        store_flat(row_ref, e, (token * TILES_PER_TOKEN).astype(I32))
        yield
        pg = (got[3 * LANES:4 * LANES] + got[2 * LANES:3 * LANES]) + got[LANES:2 * LANES]
        gate = jnp.sum(jnp.where(lane_id == lane_of, pg, 0.0), axis=0, keepdims=True)
        gate_ref[e] = jnp.transpose(jnp.broadcast_to(gate, (LANES, cap)))

    def expert_pair(i, _):
        _interleave(per_expert(2 * i + u) for u in range(2))
        return 0

    lax.fori_loop(0, N_EXPERTS // 2, expert_pair, 0)


def _route(probs, n_tok, cap):
    return pl.pallas_call(
        functools.partial(_route_kernel, n_tok, cap),
        out_shape=[jax.ShapeDtypeStruct((N_EXPERTS * cap // LANES, LANES), I32),
                   jax.ShapeDtypeStruct((N_EXPERTS, cap, LANES), F32)],
        scratch_shapes=[pltpu.VMEM((n_tok // LANES * N_EXPERTS, LANES), F32),
                        pltpu.VMEM((n_tok // LANES * N_EXPERTS, LANES), F32)],
        compiler_params=pltpu.CompilerParams(vmem_limit_bytes=VMEM_LIMIT),
        name="route_topk",
    )(probs)


def _expert_kernel(cap, row_ref, h2t_ref, gate_ref, w1_ref, w3_ref, w2_ref, ye_ref,
                   xe_ref, x2_ref, hid_ref, sem):
    e = pl.program_id(0)
    f = pl.program_id(1)
    slot = e % 2
    rows_per_step = cap // N_HID_STEPS

    def start_row(expert, buf, j):
        src = pl.multiple_of(row_ref[expert * cap + j], TILES_PER_TOKEN)
        dst = pl.multiple_of(j * TILES_PER_TOKEN, TILES_PER_TOKEN)
        pltpu.make_async_copy(h2t_ref.at[pl.ds(src, TILES_PER_TOKEN), :],
                              xe_ref.at[buf, pl.ds(dst, TILES_PER_TOKEN), :], sem.at[buf]).start()

    def wait_rows(buf):
        pltpu.make_async_copy(h2t_ref.at[pl.ds(0, cap * TILES_PER_TOKEN), :], xe_ref.at[buf],
                              sem.at[buf]).wait()

    def prefetch_next():
        nxt = jnp.minimum(e + 1, N_EXPERTS - 1)
        first = f * rows_per_step
        for j in range(rows_per_step):
            start_row(nxt, 1 - slot, first + j)

    @pl.when((e == 0) & (f == 0))
    def _():
        def issue(j, _):
            start_row(0, 0, j)
            return 0

        lax.fori_loop(0, cap, issue, 0, unroll=8)

    @pl.when(f == 0)
    def _():
        wait_rows(slot)
        for s in range(TILES_PER_TOKEN):
            x2_ref[:, s * LANES:(s + 1) * LANES] = (
                xe_ref[slot, pl.ds(s, cap, stride=TILES_PER_TOKEN), :].astype(BF16))

    @pl.when(f < N_HID_STEPS)
    def _():
        prefetch_next()
        x2 = x2_ref[...]
        a = jnp.dot(x2, w1_ref[0].astype(BF16), preferred_element_type=F32)
        b = jnp.dot(x2, w3_ref[0].astype(BF16), preferred_element_type=F32)
        hid_ref[f] = (jax.nn.silu(a) * b).astype(BF16)

    @pl.when(f >= N_HID_STEPS)
    def _():
        w2 = w2_ref[0].astype(BF16)
        out = jnp.dot(hid_ref[0], w2[0:EXPERT_F_BLOCK], preferred_element_type=F32)
        for kb in range(1, N_HID_STEPS):
            out += jnp.dot(hid_ref[kb], w2[kb * EXPERT_F_BLOCK:(kb + 1) * EXPERT_F_BLOCK],
                           preferred_element_type=F32)
        tile0 = (f - N_HID_STEPS) * (EXPERT_N_BLOCK // LANES)
        gate = gate_ref[0]
        for i in range(EXPERT_N_BLOCK // LANES):
            ye_ref[pl.ds(tile0 + i, cap, stride=TILES_PER_TOKEN), :] = out[:, i * LANES:(i + 1) * LANES] * gate

    @pl.when((e == N_EXPERTS - 1) & (f == EXPERT_STEPS - 1))
    def _():
        wait_rows(1 - slot)


def _experts(rows, h2t, gates, w1, w3, w2, cap):
    hid_blk = lambda e, f, idx: (e, 0, jnp.minimum(f, N_HID_STEPS - 1))
    n_out = D_MODEL // EXPERT_N_BLOCK

    def out_blk(e, f, idx):
        hold = (f == 0) & (e > 0)
        return (jnp.where(hold, e - 1, e), 0,
                jnp.where(hold, n_out - 1, jnp.maximum(f - N_HID_STEPS, 0)))

    grid_spec = pltpu.PrefetchScalarGridSpec(
        num_scalar_prefetch=1,
        grid=(N_EXPERTS, EXPERT_STEPS),
        in_specs=[pl.BlockSpec(memory_space=pl.ANY),
                  pl.BlockSpec((1, cap, LANES), lambda e, f, idx: (e, 0, 0)),
                  pl.BlockSpec((1, D_MODEL, EXPERT_F_BLOCK), hid_blk),
                  pl.BlockSpec((1, D_MODEL, EXPERT_F_BLOCK), hid_blk),
                  pl.BlockSpec((1, D_EXPERT, EXPERT_N_BLOCK), out_blk)],
        out_specs=pl.BlockSpec((cap * TILES_PER_TOKEN, LANES), lambda e, f, idx: (e, 0)),
        scratch_shapes=[pltpu.VMEM((2, cap * TILES_PER_TOKEN, LANES), F32),
                        pltpu.VMEM((cap, D_MODEL), BF16),
                        pltpu.VMEM((N_HID_STEPS, cap, EXPERT_F_BLOCK), BF16),
                        pltpu.SemaphoreType.DMA((2,))],
    )
    return pl.pallas_call(
        functools.partial(_expert_kernel, cap),
        grid_spec=grid_spec,
        out_shape=jax.ShapeDtypeStruct((N_EXPERTS * cap * TILES_PER_TOKEN, LANES), F32),
        compiler_params=_compiler_params(("arbitrary", "arbitrary")),
        name="expert_swiglu",
    )(rows, h2t, gates, w1, w3, w2)


COMBINE_BATCH = 8
ZERO_ROWS = 512


def _combine_kernel(cap, n_tok, mod_base, mod_per_seq, nb,
                    row_ref, ye_ref, x1_ref, mod_ref, fw_ref, y_ref,
                    acc_ref, xbuf_ref, ybuf_ref, sem_x, sem_y):
    e = pl.program_id(0)

    @pl.when(e == 0)
    def _():
        def zero(i, _):
            r = pl.multiple_of(i * ZERO_ROWS, ZERO_ROWS)
            acc_ref[pl.ds(r, ZERO_ROWS), :] = jnp.zeros((ZERO_ROWS, LANES), F32)
            return 0

        lax.fori_loop(0, n_tok * TILES_PER_TOKEN // ZERO_ROWS, zero, 0)

    def batch(jb, _):
        vals = []
        for u in range(COMBINE_BATCH):
            j = jb * COMBINE_BATCH + u
            t = pl.multiple_of(row_ref[e * cap + j], TILES_PER_TOKEN)
            src = pl.multiple_of(j * TILES_PER_TOKEN, TILES_PER_TOKEN)
            vals.append((t, acc_ref[pl.ds(t, TILES_PER_TOKEN), :] + ye_ref[pl.ds(src, TILES_PER_TOKEN), :]))
        for t, val in vals:
            acc_ref[pl.ds(t, TILES_PER_TOKEN), :] = val
        return 0

    lax.fori_loop(0, cap // COMBINE_BATCH, batch, 0)

    @pl.when(e == N_EXPERTS - 1)
    def _():
        n_blk = n_tok // TOKEN_BLOCK

        def x1_copy(b, slot):
            rows = pl.ds(pl.multiple_of(b * TOKEN_BLOCK, TOKEN_BLOCK), TOKEN_BLOCK)
            return pltpu.make_async_copy(x1_ref.at[rows, :], xbuf_ref.at[slot], sem_x.at[slot])

        def y_copy(b, slot):
            rows = pl.ds(pl.multiple_of(b * TOKEN_BLOCK, TOKEN_BLOCK), TOKEN_BLOCK)
            return pltpu.make_async_copy(ybuf_ref.at[slot], y_ref.at[rows, :], sem_y.at[slot])

        x1_copy(0, 0).start()

        def block(b, _):
            slot = b % 2

            @pl.when(b + 1 < n_blk)
            def _():
                x1_copy(b + 1, 1 - slot).start()

            x1_copy(b, slot).wait()

            @pl.when(b >= 2)
            def _():
                y_copy(b - 2, slot).wait()

            base = pl.multiple_of(b * (TOKEN_BLOCK * TILES_PER_TOKEN), TOKEN_BLOCK * TILES_PER_TOKEN)
            moe = jnp.concatenate(
                [acc_ref[pl.ds(base + s, TOKEN_BLOCK, stride=TILES_PER_TOKEN), :]
                 for s in range(TILES_PER_TOKEN)], axis=1)
            row = mod_base + ((b // nb) if mod_per_seq else 0)
            ybuf_ref[slot] = _rms(xbuf_ref[slot] + _mod_row(mod_ref, row, 5) * moe, fw_ref[...])
            y_copy(b, slot).start()
            return 0

        lax.fori_loop(0, n_blk, block, 0)
        y_copy(n_blk - 2, n_blk % 2).wait()
        y_copy(n_blk - 1, (n_blk - 1) % 2).wait()


def _combine_final(rows, ye, x1, mod, fw, cap, n_tok, *, nb, mod_base, mod_per_seq):
    full = lambda a: pl.BlockSpec(a.shape, lambda e, rows: (0,) * a.ndim)
    grid_spec = pltpu.PrefetchScalarGridSpec(
        num_scalar_prefetch=1,
        grid=(N_EXPERTS,),
        in_specs=[pl.BlockSpec((cap * TILES_PER_TOKEN, LANES), lambda e, rows: (e, 0)),
                  pl.BlockSpec(memory_space=pl.ANY), full(mod), full(fw)],
        out_specs=pl.BlockSpec(memory_space=pl.ANY),
        scratch_shapes=[pltpu.VMEM((n_tok * TILES_PER_TOKEN, LANES), F32),
                        pltpu.VMEM((2, TOKEN_BLOCK, D_MODEL), F32),
                        pltpu.VMEM((2, TOKEN_BLOCK, D_MODEL), F32),
                        pltpu.SemaphoreType.DMA((2,)),
                        pltpu.SemaphoreType.DMA((2,))],
    )
    return pl.pallas_call(
        functools.partial(_combine_kernel, cap, n_tok, mod_base, mod_per_seq, nb),
        grid_spec=grid_spec,
        out_shape=jax.ShapeDtypeStruct((n_tok, D_MODEL), F32),
        compiler_params=_compiler_params(("arbitrary",)),
        name="moe_combine_norm",
    )(rows, ye, x1, mod, fw)


def _trunk_and_norm(x, e_tab, mod, s0_f, s0_b, prm, *, nseq, seq_len, add_pos, mod_base, mod_per_seq):
    nb = seq_len // TOKEN_BLOCK
    n_tok = nseq * seq_len
    cap = EC_CAPACITY_FACTOR * n_tok // N_EXPERTS
    kw = dict(nseq=nseq, nb=nb, add_pos=add_pos, mod_base=mod_base, mod_per_seq=mod_per_seq)
    if nb == 1 and not add_pos:
        x1, h2t, probs, sfin_f, sfin_b = _mixer_fused(
            x, mod, prm, s0_f, s0_b, nseq=nseq, mod_base=mod_base, mod_per_seq=mod_per_seq)
    else:
        qkv, g, lab, s, of, sfin_f = _mixer_fwd(
            x, e_tab, mod, prm["n1"], prm["win"], prm["wa"], prm["ba"], prm["snw"], prm["sws"],
            prm["sbs"], s0_f, **kw)
        x1, h2t, probs, sfin_b = _mixer_bwd(
            x, e_tab, mod, qkv, g, lab, s, of, prm["gnw"], prm["wout"], prm["n2"], prm["rw_t"],
            s0_b, **kw)
    x1 = x1.reshape(n_tok, D_MODEL)
    h2t = h2t.reshape(n_tok * TILES_PER_TOKEN, LANES)
    probs = probs.reshape(n_tok // TOKEN_BLOCK, N_EXPERTS, TOKEN_BLOCK)
    rows, gates = _route(probs, n_tok, cap)
    rows = rows.reshape(-1)
    ye = _experts(rows, h2t, gates, prm["w1"], prm["w3"], prm["w2"], cap)
    y = _combine_final(rows, ye, x1, mod, prm["fw"], cap, n_tok,
                       nb=nb, mod_base=mod_base, mod_per_seq=mod_per_seq)
    return y.reshape(nseq, seq_len, D_MODEL), sfin_f, sfin_b


def kernel(x_prompt, x_sample, state_gla_fwd, state_gla_bwd, c, c_ctx, ada_w, ada_b, norm1_w, w_in, gla_wa2_f, gla_ba_f, gla_wa2_b, gla_ba_b, gla_norm_w, sgu_norm_w, sgu_ws, sgu_bs, w_out, norm2_w, router_w, exp_w1, exp_w3, exp_w2, final_norm_w):
    assert ada_w.shape[0] == 1, "single trunk layer"
    batch, seq, _ = x_prompt.shape
    dec_batch, dec_seq, _ = x_sample.shape

    assert w_in.shape[2] == D_IN
    win = _win_transposed_bf16(w_in[0].T)
    wa = jnp.zeros((2 * GLA_LOWRANK, 2 * QK_W), F32)
    wa = wa.at[0:GLA_LOWRANK, 0:QK_W].set(gla_wa2_f[0])
    wa = wa.at[GLA_LOWRANK:2 * GLA_LOWRANK, QK_W:].set(gla_wa2_b[0]).astype(BF16)
    prm = dict(
        n1=norm1_w, win=win, wa=wa,
        ba=jnp.concatenate([gla_ba_f[0], gla_ba_b[0]])[None, :],
        snw=sgu_norm_w, sws=sgu_ws[0].astype(BF16),
        sbs=jnp.broadcast_to(sgu_bs[0][:, :, None], (SGU_GROUPS, SGU_CHUNK, SGU_CH)),
        gnw=gla_norm_w, wout=w_out[0].astype(BF16), n2=norm2_w, rw_t=router_w[0].T,
        w1=exp_w1[0], w3=exp_w3[0], w2=exp_w2[0], fw=final_norm_w[None, :])

    cvec = jnp.concatenate([c_ctx[None, :], c, jnp.zeros((SUBLANES - 1 - dec_batch, D_MODEL), F32)])
    mod = _modulation(cvec, ada_w[0], ada_b)
    e_tab = _pos_table()

    zero_state = jnp.zeros((min(SEQ_GROUP, batch), GLA_HEADS, GLA_DK, GLA_DV), F32)
    y_prompt, sf, sb = _trunk_and_norm(
        x_prompt, e_tab, mod, zero_state, zero_state, prm,
        nseq=batch, seq_len=seq, add_pos=False, mod_base=0, mod_per_seq=False)
    y_sample, _, _ = _trunk_and_norm(
        x_sample, e_tab, mod, state_gla_fwd[:, 0], state_gla_bwd[:, 0], prm,
        nseq=dec_batch, seq_len=dec_seq, add_pos=True, mod_base=1, mod_per_seq=True)
    return (y_prompt, y_sample, sf[:, None], sb[:, None])
```

```python
import functools
import math

import jax
import jax.numpy as jnp
from jax import lax
from jax.experimental import pallas as pl
from jax.experimental.pallas import tpu as pltpu

F32 = jnp.float32
BF16 = jnp.bfloat16
I32 = jnp.int32

D_MODEL = 1024
GRID_W = 64
GLA_HEADS = 4
GLA_DK = 64
GLA_DV = 128
GLA_WIDTH = GLA_HEADS * GLA_DV
QK_W = GLA_HEADS * GLA_DK
GLA_LOWRANK = 16
GLA_GATE_NORM = 16.0
GLA_CHUNK = 64
SGU_WIDTH = 512
SGU_GROUPS = 4
SGU_CH = 128
SGU_CHUNK = 128
N_EXPERTS = 16
EC_CAPACITY_FACTOR = 2
D_EXPERT = 2048
EPS = 1e-6

SUBLANES = 8
LANES = 128
TILES_PER_TOKEN = D_MODEL // LANES

TOKEN_BLOCK = 256
SEQ_GROUP = 4
OFF_A = 2 * QK_W + 2 * GLA_WIDTH
OFF_U = OFF_A + 2 * GLA_LOWRANK
D_IN = OFF_U + 2 * SGU_WIDTH
EXPERT_F_BLOCK = 512
EXPERT_N_BLOCK = 1024
N_HID_STEPS = D_EXPERT // EXPERT_F_BLOCK
EXPERT_STEPS = N_HID_STEPS + D_MODEL // EXPERT_N_BLOCK
VMEM_LIMIT = 56 * 1024 * 1024


def _dot(a, b):
    return jnp.dot(a.astype(BF16), b.astype(BF16), preferred_element_type=F32)


def _dot_nt(a, b):
    return lax.dot_general(a.astype(BF16), b.astype(BF16), (((1,), (1,)), ((), ())),
                           preferred_element_type=F32)


def _dot_tn(a, b):
    return lax.dot_general(a.astype(BF16), b.astype(BF16), (((0,), (0,)), ((), ())),
                           preferred_element_type=F32)


def _dot_f32(a, b, dims=(((1,), (0,)), ((), ()))):
    return lax.dot_general(a, b, dims, precision=lax.Precision.HIGHEST, preferred_element_type=F32)


def _split_bf16(x, terms):
    parts = []
    for _ in range(terms - 1):
        part = x.astype(BF16)
        parts.append(part)
        x = x - part.astype(F32)
    parts.append(x.astype(BF16))
    return parts


def _select_dot(sel, x):
    s = sel.astype(BF16)
    hi, mid, lo = _split_bf16(x, 3)
    return (jnp.dot(s, lo, preferred_element_type=F32) + jnp.dot(s, mid, preferred_element_type=F32)
            + jnp.dot(s, hi, preferred_element_type=F32))


def _dot_nt_3pass(a, b):
    a_hi, a_lo = _split_bf16(a, 2)
    b_hi, b_lo = _split_bf16(b, 2)
    nt = lambda x, y: lax.dot_general(x, y, (((1,), (1,)), ((), ())), preferred_element_type=F32)
    return (nt(a_hi, b_lo) + nt(a_lo, b_hi)) + nt(a_hi, b_hi)


def _rms(x, w):
    return x * lax.rsqrt(jnp.mean(x * x, axis=-1, keepdims=True) + EPS) * w


def _compiler_params(sem):
    return pltpu.CompilerParams(dimension_semantics=sem, vmem_limit_bytes=VMEM_LIMIT)


def _mod_kernel(c_ref, w_ref, b_ref, o_ref):
    o_ref[...] = _dot(jax.nn.silu(c_ref[...]), w_ref[...]) + b_ref[...]


def _modulation(cvec, ada_w, ada_b):
    n = ada_w.shape[1]
    bn = 1536
    return pl.pallas_call(
        _mod_kernel,
        grid=(n // bn,),
        in_specs=[pl.BlockSpec((SUBLANES, D_MODEL), lambda j: (0, 0)),
                  pl.BlockSpec((D_MODEL, bn), lambda j: (0, j)),
                  pl.BlockSpec((1, bn), lambda j: (0, j))],
        out_specs=pl.BlockSpec((SUBLANES, bn), lambda j: (0, j)),
        out_shape=jax.ShapeDtypeStruct((SUBLANES, n), F32),
        compiler_params=_compiler_params(("arbitrary",)),
        name="adaln_mod",
    )(cvec, ada_w, ada_b)


def _cast_kernel(w_ref, o_ref):
    o_ref[...] = w_ref[...].astype(BF16)


def _win_transposed_bf16(w_t):
    rows = D_IN // 3
    return pl.pallas_call(
        _cast_kernel,
        grid=(D_IN // rows,),
        in_specs=[pl.BlockSpec((rows, D_MODEL), lambda i: (i, 0))],
        out_specs=pl.BlockSpec((rows, D_MODEL), lambda i: (i, 0)),
        out_shape=jax.ShapeDtypeStruct((D_IN, D_MODEL), BF16),
        compiler_params=_compiler_params(("arbitrary",)),
        name="win_cast",
    )(w_t)


def _pos_kernel(o_ref):
    nf = D_MODEL // 4
    p = lax.broadcasted_iota(I32, (GRID_W, nf), 0).astype(F32)
    i = lax.broadcasted_iota(I32, (GRID_W, nf), 1).astype(F32)
    omega = jnp.exp(i * (-math.log(10000.0) / nf))
    a = p * omega
    o_ref[:, 0:nf] = jnp.sin(a)
    o_ref[:, nf:2 * nf] = jnp.cos(a)


def _pos_table():
    return pl.pallas_call(
        _pos_kernel,
        out_shape=jax.ShapeDtypeStruct((GRID_W, D_MODEL // 2), F32),
        name="sincos_table",
    )()


def _add_pos(x, e_ref, blk, add_pos):
    if not add_pos:
        return x
    half = D_MODEL // 2
    e_all = e_ref[...]
    rows = []
    for j in range(TOKEN_BLOCK // GRID_W):
        xj = x[j * GRID_W:(j + 1) * GRID_W]
        e_row = e_ref[pl.ds(blk * (TOKEN_BLOCK // GRID_W) + j, 1), :]
        rows.append(jnp.concatenate([xj[:, 0:half] + e_row, xj[:, half:] + e_all], axis=1))
    return jnp.concatenate(rows, axis=0)


def _chunk_masks():
    r = lax.broadcasted_iota(I32, (TOKEN_BLOCK, TOKEN_BLOCK), 0)
    c = lax.broadcasted_iota(I32, (TOKEN_BLOCK, TOKEN_BLOCK), 1)
    same = (r // GLA_CHUNK) == (c // GLA_CHUNK)
    return same & (c <= r), same & (c >= r)


def _gla_direction(q, k, v, cum, fwd, att_mask, st_ref, ready, done):
    qe = q * jnp.exp(cum)
    ke = k * jnp.exp(-cum)
    yield
    lane = lax.broadcasted_iota(I32, (1, LANES), 1)
    o_intra = []
    for pair in range(2):
        qp = qe[:, pair * LANES:(pair + 1) * LANES]
        kp = ke[:, pair * LANES:(pair + 1) * LANES]
        for hh in range(2):
            qm = jnp.where((lane // GLA_DK) == hh, qp, 0.0)
            att = jnp.where(att_mask, _dot_nt(qm, kp), 0.0)
            head = 2 * pair + hh
            o_intra.append(_dot(att, v[:, head * GLA_DV:(head + 1) * GLA_DV]))
            yield
    o_intra = jnp.concatenate(o_intra, axis=1)
    while not ready():
        yield

    er = lax.broadcasted_iota(I32, (2 * GLA_DV, 2 * GLA_DK), 0)
    dc = lax.broadcasted_iota(I32, (2 * GLA_DV, 2 * GLA_DK), 1)
    same_head = (er // GLA_DV) == (dc // GLA_DK)
    n_chunks = TOKEN_BLOCK // GLA_CHUNK
    o_inter = [None] * n_chunks
    for c in (range(n_chunks) if fwd else reversed(range(n_chunks))):
        r0 = c * GLA_CHUNK
        rows = slice(r0, r0 + GLA_CHUNK)
        last = cum[r0 + GLA_CHUNK - 1:r0 + GLA_CHUNK] if fwd else cum[r0:r0 + 1]
        kd = k[rows] * jnp.exp(last - cum[rows])
        dec = jnp.exp(last)
        parts = []
        for pair in range(2):
            dl = slice(pair * LANES, (pair + 1) * LANES)
            st = st_ref[pair]
            parts.append(_dot_nt(qe[rows, dl], st))
            ds_t = _dot_tn(v[rows, pair * 2 * GLA_DV:(pair + 1) * 2 * GLA_DV], kd[:, dl])
            st_ref[pair] = dec[:, dl] * st + jnp.where(same_head, ds_t, 0.0)
        o_inter[c] = jnp.concatenate(parts, axis=1)
        yield
    done()
    return o_intra + jnp.concatenate(o_inter, axis=0)


def _interleave(chains):
    chains = list(chains)
    done = [False] * len(chains)
    tick = 0
    while not all(done):
        for i, ch in enumerate(chains):
            if tick >= i and not done[i]:
                try:
                    next(ch)
                except StopIteration:
                    done[i] = True
        tick += 1


def _load_state(s0_ref, u, st_ref):
    zero = jnp.zeros((GLA_DV, GLA_DK), F32)
    for pair in range(2):
        a = s0_ref[u, 2 * pair].T
        b = s0_ref[u, 2 * pair + 1].T
        st_ref[pair] = jnp.concatenate(
            [jnp.concatenate([a, zero], axis=1), jnp.concatenate([zero, b], axis=1)], axis=0)


def _store_state(st_ref, sfin_ref, u):
    for pair in range(2):
        st = st_ref[pair]
        sfin_ref[u, 2 * pair] = st[0:GLA_DV, 0:GLA_DK].T
        sfin_ref[u, 2 * pair + 1] = st[GLA_DV:2 * GLA_DV, GLA_DK:2 * GLA_DK].T


def _mod_row(mod_ref, row, part):
    return mod_ref[pl.ds(row, 1), part * D_MODEL:(part + 1) * D_MODEL]


def _front_stages(xin, row, mod_ref, n1_ref, win_ref, wa_ref, ba_ref, snw_ref, sws_ref, sbs_ref):
    h = _rms(xin, n1_ref[...]) * (1.0 + _mod_row(mod_ref, row, 1)) + _mod_row(mod_ref, row, 0)
    yield
    hb = h.astype(BF16)
    parts = []
    for r0, r1 in ((0, OFF_A), (OFF_U, D_IN), (OFF_A, OFF_U)):
        parts.append(lax.dot_general(hb, win_ref[r0:r1, :], (((1,), (1,)), ((), ())),
                                     preferred_element_type=F32))
        yield
    main, gate, low = parts
    q = main[:, 0:QK_W] * (GLA_DK ** -0.5)
    k = main[:, QK_W:2 * QK_W]
    v = main[:, 2 * QK_W:2 * QK_W + GLA_WIDTH]
    z = _dot(low, wa_ref[...]) + ba_ref[...]
    la = (jnp.minimum(z, 0.0) - jnp.log1p(jnp.exp(-jnp.abs(z)))) * (1.0 / GLA_GATE_NORM)
    yield

    ug = jax.nn.gelu(gate[:, 0:SGU_WIDTH])
    yield
    vg = jax.nn.gelu(gate[:, SGU_WIDTH:2 * SGU_WIDTH])
    yield
    s_cols = []
    for gi in range(SGU_GROUPS):
        cols = slice(gi * SGU_CH, (gi + 1) * SGU_CH)
        vn = _rms(vg[:, cols], snw_ref[:, cols])
        rhs = jnp.concatenate([vn[0:SGU_CHUNK], vn[SGU_CHUNK:2 * SGU_CHUNK]], axis=1)
        vm = _dot(sws_ref[gi], rhs) + jnp.concatenate([sbs_ref[gi], sbs_ref[gi]], axis=1)
        vm = jnp.concatenate([vm[:, 0:SGU_CH], vm[:, SGU_CH:2 * SGU_CH]], axis=0)
        s_cols.append(ug[:, cols] * vm)
    yield
    return q, k, v, main[:, 2 * QK_W + GLA_WIDTH:OFF_A], la, jnp.concatenate(s_cols, axis=1)


def _back_stages(o, g, s_val, xin, row, mod_ref, gnw_ref, wout_ref, n2_ref, rw_ref):
    cols = []
    for head in range(GLA_HEADS):
        hs = slice(head * GLA_DV, (head + 1) * GLA_DV)
        cols.append(_rms(o[:, hs], gnw_ref[...]) * jax.nn.silu(g[:, hs]))
    cols.append(s_val)
    yield
    y = _dot(jnp.concatenate(cols, axis=1), wout_ref[...])
    yield
    x1 = xin + _mod_row(mod_ref, row, 2) * y
    h2 = _rms(x1, n2_ref[...]) * (1.0 + _mod_row(mod_ref, row, 4)) + _mod_row(mod_ref, row, 3)
    yield
    logits = _dot_nt_3pass(rw_ref[...], h2)
    m = jnp.max(logits, axis=0, keepdims=True)
    ex = jnp.exp(logits - m)
    return x1, h2, ex / jnp.sum(ex, axis=0, keepdims=True)


def _store_token_tiles(h2t_ref, u, b, h2):
    for s in range(TILES_PER_TOKEN):
        h2t_ref[u, pl.ds(b * TOKEN_BLOCK * TILES_PER_TOKEN + s, TOKEN_BLOCK, stride=TILES_PER_TOKEN), :] = (
            h2[:, s * LANES:(s + 1) * LANES])


def _zip_stages(gen_a, gen_b):
    out = [None, None]
    live = [gen_a, gen_b]
    while any(g is not None for g in live):
        for i, g in enumerate(live):
            if g is not None:
                try:
                    next(g)
                except StopIteration as stop:
                    out[i] = stop.value
                    live[i] = None
        yield
    return out


def _mixer_fwd_kernel(group, add_pos, mod_base, mod_per_seq, nb,
                      x_ref, e_ref, mod_ref, n1_ref, win_ref, wa_ref, ba_ref,
                      snw_ref, sws_ref, sbs_ref, s0_ref,
                      qkv_ref, g_ref, lab_ref, s_ref, of_ref, sfin_ref,
                      st_ref):
    grp = pl.program_id(0)
    blk = pl.program_id(1)
    sub = x_ref.shape[1] // TOKEN_BLOCK
    lo_mask, _ = _chunk_masks()
    state_done = set()

    @pl.when(blk == 0)
    def _():
        for u in range(group):
            _load_state(s0_ref, u, st_ref.at[u])

    def chain(u, b):
        row = mod_base + ((grp * group + u) if mod_per_seq else 0)
        st_u = st_ref.at[u]
        rows = slice(b * TOKEN_BLOCK, (b + 1) * TOKEN_BLOCK)
        xin = _add_pos(x_ref[u, rows, :], e_ref, blk * sub + b, add_pos)
        q, k, v, g, la, s_val = yield from _front_stages(
            xin, row, mod_ref, n1_ref, win_ref, wa_ref, ba_ref, snw_ref, sws_ref, sbs_ref)
        qkv_ref[u, rows, :] = jnp.concatenate([q, k, v], axis=1)
        g_ref[u, rows, :] = g
        lab_ref[u, rows, :] = la[:, QK_W:2 * QK_W]
        s_ref[u, rows, :] = s_val

        cum = _select_dot(lo_mask, la[:, 0:QK_W])
        yield
        of_ref[u, rows, :] = yield from _gla_direction(
            q, k, v, cum, True, lo_mask, st_u,
            ready=lambda: b == 0 or (u, b - 1) in state_done, done=lambda: state_done.add((u, b)))

    _interleave(chain(u, b) for b in range(sub) for u in range(group))

    @pl.when(blk == nb - 1)
    def _():
        for u in range(group):
            _store_state(st_ref.at[u], sfin_ref, u)


def _mixer_fwd(x, e_tab, mod, n1, win, wa, ba, snw, sws, sbs, s0, *, nseq, nb, add_pos,
               mod_base, mod_per_seq):
    seq_len = nb * TOKEN_BLOCK
    group = min(SEQ_GROUP, nseq)
    sub = min(SEQ_GROUP // group, nb)
    nb //= sub
    tok = lambda w: pl.BlockSpec((group, sub * TOKEN_BLOCK, w), lambda s, i: (s, i, 0))
    full = lambda a: pl.BlockSpec(a.shape, lambda s, i: (0,) * a.ndim)
    st_spec = pl.BlockSpec((group, GLA_HEADS, GLA_DK, GLA_DV), lambda s, i: (s, 0, 0, 0))
    s0_spec = st_spec if s0.shape[0] == nseq else pl.BlockSpec(s0.shape, lambda s, i: (0, 0, 0, 0))
    act = lambda w: jax.ShapeDtypeStruct((nseq, seq_len, w), F32)
    kern = functools.partial(_mixer_fwd_kernel, group, add_pos, mod_base, mod_per_seq, nb)
    return pl.pallas_call(
        kern,
        grid=(nseq // group, nb),
        in_specs=[tok(D_MODEL), full(e_tab), full(mod), full(n1), full(win), full(wa), full(ba),
                  full(snw), full(sws), full(sbs), s0_spec],
        out_specs=[tok(1024), tok(GLA_WIDTH), tok(QK_W), tok(SGU_WIDTH), tok(GLA_WIDTH), st_spec],
        out_shape=[act(1024), act(GLA_WIDTH), act(QK_W), act(SGU_WIDTH), act(GLA_WIDTH),
                   jax.ShapeDtypeStruct((nseq, GLA_HEADS, GLA_DK, GLA_DV), F32)],
        scratch_shapes=[pltpu.VMEM((group, 2, 2 * GLA_DV, 2 * GLA_DK), F32)],
        compiler_params=_compiler_params(("arbitrary", "arbitrary")),
        name="mixer_fwd",
    )(x, e_tab, mod, n1, win, wa, ba, snw, sws, sbs, s0)


def _mixer_bwd_kernel(group, add_pos, mod_base, mod_per_seq, nb,
                      x_ref, e_ref, mod_ref, qkv_ref, g_ref, lab_ref, s_ref, of_ref,
                      gnw_ref, wout_ref, n2_ref, rw_ref, s0_ref,
                      x1_ref, h2t_ref, probs_ref, sfin_ref,
                      st_ref):
    grp = pl.program_id(0)
    step = pl.program_id(1)
    sub = x_ref.shape[1] // TOKEN_BLOCK
    blk = nb - 1 - step
    _, hi_mask = _chunk_masks()
    state_done = set()

    @pl.when(step == 0)
    def _():
        for u in range(group):
            _load_state(s0_ref, u, st_ref.at[u])

    def chain(u, b):
        row = mod_base + ((grp * group + u) if mod_per_seq else 0)
        st_u = st_ref.at[u]
        rows = slice(b * TOKEN_BLOCK, (b + 1) * TOKEN_BLOCK)
        qkv = qkv_ref[u, rows, :]
        q, k, v = qkv[:, 0:QK_W], qkv[:, QK_W:2 * QK_W], qkv[:, 2 * QK_W:]
        cum = _select_dot(hi_mask, lab_ref[u, rows, :])
        yield
        o_b = yield from _gla_direction(
            q, k, v, cum, False, hi_mask, st_u,
            ready=lambda: b == sub - 1 or (u, b + 1) in state_done,
            done=lambda: state_done.add((u, b)))
        xin = _add_pos(x_ref[u, rows, :], e_ref, blk * sub + b, add_pos)
        x1, h2, probs = yield from _back_stages(
            of_ref[u, rows, :] + o_b, g_ref[u, rows, :], s_ref[u, rows, :], xin, row,
            mod_ref, gnw_ref, wout_ref, n2_ref, rw_ref)
        x1_ref[u, rows, :] = x1
        _store_token_tiles(h2t_ref, u, b, h2)
        probs_ref[u, b] = probs

    _interleave(chain(u, b) for b in reversed(range(sub)) for u in range(group))

    @pl.when(step == nb - 1)
    def _():
        for u in range(group):
            _store_state(st_ref.at[u], sfin_ref, u)


def _mixer_bwd(x, e_tab, mod, qkv, g, lab, s, of, gnw, wout, n2, rw_t, s0, *, nseq, nb, add_pos,
               mod_base, mod_per_seq):
    seq_len = nb * TOKEN_BLOCK
    group = min(SEQ_GROUP, nseq)
    sub = min(SEQ_GROUP // group, nb)
    nb //= sub
    tok = lambda w: pl.BlockSpec((group, sub * TOKEN_BLOCK, w), lambda s_, i: (s_, nb - 1 - i, 0))
    full = lambda a: pl.BlockSpec(a.shape, lambda s_, i: (0,) * a.ndim)
    st_spec = pl.BlockSpec((group, GLA_HEADS, GLA_DK, GLA_DV), lambda s_, i: (s_, 0, 0, 0))
    s0_spec = st_spec if s0.shape[0] == nseq else pl.BlockSpec(s0.shape, lambda s_, i: (0, 0, 0, 0))
    kern = functools.partial(_mixer_bwd_kernel, group, add_pos, mod_base, mod_per_seq, nb)
    return pl.pallas_call(
        kern,
        grid=(nseq // group, nb),
        in_specs=[tok(D_MODEL), full(e_tab), full(mod), tok(1024), tok(GLA_WIDTH), tok(QK_W),
                  tok(SGU_WIDTH), tok(GLA_WIDTH), full(gnw), full(wout), full(n2), full(rw_t),
                  s0_spec],
        out_specs=[tok(D_MODEL),
                   pl.BlockSpec((group, sub * TOKEN_BLOCK * TILES_PER_TOKEN, LANES),
                                lambda s_, i: (s_, nb - 1 - i, 0)),
                   pl.BlockSpec((group, sub, N_EXPERTS, TOKEN_BLOCK),
                                lambda s_, i: (s_, nb - 1 - i, 0, 0)),
                   st_spec],
        out_shape=[jax.ShapeDtypeStruct((nseq, seq_len, D_MODEL), F32),
                   jax.ShapeDtypeStruct((nseq, seq_len * TILES_PER_TOKEN, LANES), F32),
                   jax.ShapeDtypeStruct((nseq, nb * sub, N_EXPERTS, TOKEN_BLOCK), F32),
                   jax.ShapeDtypeStruct((nseq, GLA_HEADS, GLA_DK, GLA_DV), F32)],
        scratch_shapes=[pltpu.VMEM((group, 2, 2 * GLA_DV, 2 * GLA_DK), F32)],
        compiler_params=_compiler_params(("arbitrary", "arbitrary")),
        name="mixer_bwd",
    )(x, e_tab, mod, qkv, g, lab, s, of, gnw, wout, n2, rw_t, s0)


def _mixer_fused_kernel(group, mod_base, mod_per_seq,
                        x_ref, mod_ref, n1_ref, win_ref, wa_ref, ba_ref, snw_ref, sws_ref, sbs_ref,
                        gnw_ref, wout_ref, n2_ref, rw_ref, s0f_ref, s0b_ref,
                        x1_ref, h2t_ref, probs_ref, sfin_f_ref, sfin_b_ref,
                        stf_ref, stb_ref):
    grp = pl.program_id(0)
    lo_mask, hi_mask = _chunk_masks()
    always = lambda: True
    nothing = lambda: None
    for u in range(group):
        _load_state(s0f_ref, u, stf_ref.at[u])
        _load_state(s0b_ref, u, stb_ref.at[u])

    def chain(u):
        row = mod_base + ((grp * group + u) if mod_per_seq else 0)
        xin = x_ref[u]
        q, k, v, g, la, s_val = yield from _front_stages(
            xin, row, mod_ref, n1_ref, win_ref, wa_ref, ba_ref, snw_ref, sws_ref, sbs_ref)
        cum_f = _select_dot(lo_mask, la[:, 0:QK_W])
        cum_b = _select_dot(hi_mask, la[:, QK_W:2 * QK_W])
        yield
        o_f, o_b = yield from _zip_stages(
            _gla_direction(q, k, v, cum_f, True, lo_mask, stf_ref.at[u], always, nothing),
            _gla_direction(q, k, v, cum_b, False, hi_mask, stb_ref.at[u], always, nothing))
        x1, h2, probs = yield from _back_stages(
            o_f + o_b, g, s_val, xin, row, mod_ref, gnw_ref, wout_ref, n2_ref, rw_ref)
        x1_ref[u] = x1
        _store_token_tiles(h2t_ref, u, 0, h2)
        probs_ref[u, 0] = probs

    _interleave(chain(u) for u in range(group))
    for u in range(group):
        _store_state(stf_ref.at[u], sfin_f_ref, u)
        _store_state(stb_ref.at[u], sfin_b_ref, u)


def _mixer_fused(x, mod, prm, s0_f, s0_b, *, nseq, mod_base, mod_per_seq):
    group = min(SEQ_GROUP, nseq)
    consts = [mod, prm["n1"], prm["win"], prm["wa"], prm["ba"], prm["snw"], prm["sws"], prm["sbs"],
              prm["gnw"], prm["wout"], prm["n2"], prm["rw_t"]]
    tok = lambda w: pl.BlockSpec((group, TOKEN_BLOCK, w), lambda s: (s, 0, 0))
    full = lambda a: pl.BlockSpec(a.shape, lambda s: (0,) * a.ndim)
    st_spec = pl.BlockSpec((group, GLA_HEADS, GLA_DK, GLA_DV), lambda s: (s, 0, 0, 0))
    s0_spec = lambda a: st_spec if a.shape[0] == nseq else pl.BlockSpec(a.shape, lambda s: (0, 0, 0, 0))
    state = jax.ShapeDtypeStruct((nseq, GLA_HEADS, GLA_DK, GLA_DV), F32)
    st_scratch = pltpu.VMEM((group, 2, 2 * GLA_DV, 2 * GLA_DK), F32)
    return pl.pallas_call(
        functools.partial(_mixer_fused_kernel, group, mod_base, mod_per_seq),
        grid=(nseq // group,),
        in_specs=[tok(D_MODEL)] + [full(a) for a in consts] + [s0_spec(s0_f), s0_spec(s0_b)],
        out_specs=[tok(D_MODEL),
                   pl.BlockSpec((group, TOKEN_BLOCK * TILES_PER_TOKEN, LANES), lambda s: (s, 0, 0)),
                   pl.BlockSpec((group, 1, N_EXPERTS, TOKEN_BLOCK), lambda s: (s, 0, 0, 0)),
                   st_spec, st_spec],
        out_shape=[jax.ShapeDtypeStruct((nseq, TOKEN_BLOCK, D_MODEL), F32),
                   jax.ShapeDtypeStruct((nseq, TOKEN_BLOCK * TILES_PER_TOKEN, LANES), F32),
                   jax.ShapeDtypeStruct((nseq, 1, N_EXPERTS, TOKEN_BLOCK), F32),
                   state, state],
        scratch_shapes=[st_scratch, st_scratch],
        compiler_params=_compiler_params(("arbitrary",)),
        name="mixer_fused",
    )(x, *consts, s0_f, s0_b)


def _route_kernel(n_tok, cap, probs_ref, row_ref, gate_ref, xs_ref, ps_ref):
    n_blk = n_tok // TOKEN_BLOCK
    n_chunk = n_tok // LANES
    probs = jnp.concatenate([probs_ref[b] for b in range(n_blk)], axis=1)
    capf = jnp.float32(cap)

    def count(mask):
        return jnp.sum(mask.astype(F32), axis=1, keepdims=True)

    def as_f32(bits):
        return lax.bitcast_convert_type(bits, F32)

    def thr_step(_, lohi):
        lo, hi = lohi
        mid = lo + ((hi - lo + 1) >> 1)
        ok = count(probs >= as_f32(mid)) >= capf
        return jnp.where(ok, mid, lo), jnp.where(ok, hi, mid - 1)

    lo0 = jnp.zeros((N_EXPERTS, 1), I32)
    hi0 = jnp.full((N_EXPERTS, 1), 0x3F800000, I32)
    thr, _ = lax.fori_loop(0, 31, thr_step, (lo0, hi0))
    gt = probs >= as_f32(thr + 1)
    eq = (probs >= as_f32(thr)) & jnp.logical_not(gt)
    need = capf - count(gt)
    tok = lax.broadcasted_iota(I32, (N_EXPERTS, n_tok), 1)

    def tie_step(_, lohi):
        lo, hi = lohi
        mid = (lo + hi) >> 1
        ok = count(eq & (tok <= mid)) >= need
        return jnp.where(ok, lo, mid + 1), jnp.where(ok, mid, hi)

    n_bits = max(1, (n_tok - 1).bit_length())
    cut, _ = lax.fori_loop(0, n_bits, tie_step,
                           (jnp.zeros((N_EXPERTS, 1), I32), jnp.full((N_EXPERTS, 1), n_tok - 1, I32)))
    sel = (gt | (eq & (tok <= cut))).astype(F32)

    xs_ref[...] = jnp.concatenate([sel[:, c * LANES:(c + 1) * LANES] for c in range(n_chunk)], axis=0)
    ps_ref[...] = jnp.concatenate([probs[:, c * LANES:(c + 1) * LANES] for c in range(n_chunk)], axis=0)

    li = lax.broadcasted_iota(I32, (LANES, LANES), 0)
    lj = lax.broadcasted_iota(I32, (LANES, LANES), 1)
    upper = (li <= lj).astype(F32)
    ci = lax.broadcasted_iota(I32, (n_chunk, n_chunk), 0)
    cj = lax.broadcasted_iota(I32, (n_chunk, n_chunk), 1)
    lower = (cj <= ci).astype(F32)
    slot = lax.broadcasted_iota(I32, (1, cap), 1).astype(F32)
    chunk_id = lax.broadcasted_iota(I32, (n_chunk, cap), 0).astype(F32)
    lane_id = lax.broadcasted_iota(I32, (LANES, cap), 0).astype(F32)
    reps = cap // LANES

    def store_flat(ref, e, val):
        for i in range(reps):
            ref[pl.ds(e * reps + i, 1), :] = val[:, i * LANES:(i + 1) * LANES]

    def per_expert(e):
        x = xs_ref[pl.ds(e, n_chunk, stride=N_EXPERTS), :]
        pe = ps_ref[pl.ds(e, n_chunk, stride=N_EXPERTS), :]
        ploc = _dot(x, upper)
        tot = jnp.broadcast_to(ploc[:, LANES - 1:LANES], (n_chunk, LANES))
        cum = _dot(lower, tot)
        yield
        cum_w = jnp.concatenate([cum] * reps, axis=1)
        base_w = jnp.concatenate([cum - tot] * reps, axis=1)
        chunk_of = jnp.sum((cum_w <= slot).astype(F32), axis=0, keepdims=True)
        onehot = chunk_id == chunk_of
        local = slot - jnp.sum(jnp.where(onehot, base_w, 0.0), axis=0, keepdims=True)
        yield
        lhs = jnp.concatenate([ploc.astype(BF16)] + _split_bf16(pe, 3), axis=1)
        got = _dot_tn(lhs, onehot.astype(F32))
        yield
        pref = got[0:LANES]
        lane_of = jnp.sum((pref <= local).astype(F32), axis=0, keepdims=True)
        token = chunk_of * LANES + lane_of
        store_flat(row_ref, e, (token * TILES_PER_TOKEN).astype(I32))
        yield
        pg = (got[3 * LANES:4 * LANES] + got[2 * LANES:3 * LANES]) + got[LANES:2 * LANES]
        gate = jnp.sum(jnp.where(lane_id == lane_of, pg, 0.0), axis=0, keepdims=True)
        gate_ref[e] = jnp.transpose(jnp.broadcast_to(gate, (LANES, cap)))

    def expert_pair(i, _):
        _interleave(per_expert(2 * i + u) for u in range(2))
        return 0

    lax.fori_loop(0, N_EXPERTS // 2, expert_pair, 0)


def _route(probs, n_tok, cap):
    return pl.pallas_call(
        functools.partial(_route_kernel, n_tok, cap),
        out_shape=[jax.ShapeDtypeStruct((N_EXPERTS * cap // LANES, LANES), I32),
                   jax.ShapeDtypeStruct((N_EXPERTS, cap, LANES), F32)],
        scratch_shapes=[pltpu.VMEM((n_tok // LANES * N_EXPERTS, LANES), F32),
                        pltpu.VMEM((n_tok // LANES * N_EXPERTS, LANES), F32)],
        compiler_params=pltpu.CompilerParams(vmem_limit_bytes=VMEM_LIMIT),
        name="route_topk",
    )(probs)


def _expert_kernel(cap, row_ref, h2t_ref, gate_ref, w1_ref, w3_ref, w2_ref, ye_ref,
                   xe_ref, x2_ref, hid_ref, sem):
    e = pl.program_id(0)
    f = pl.program_id(1)
    slot = e % 2
    rows_per_step = cap // N_HID_STEPS

    def start_row(expert, buf, j):
        src = pl.multiple_of(row_ref[expert * cap + j], TILES_PER_TOKEN)
        dst = pl.multiple_of(j * TILES_PER_TOKEN, TILES_PER_TOKEN)
        pltpu.make_async_copy(h2t_ref.at[pl.ds(src, TILES_PER_TOKEN), :],
                              xe_ref.at[buf, pl.ds(dst, TILES_PER_TOKEN), :], sem.at[buf]).start()

    def wait_rows(buf):
        pltpu.make_async_copy(h2t_ref.at[pl.ds(0, cap * TILES_PER_TOKEN), :], xe_ref.at[buf],
                              sem.at[buf]).wait()

    def prefetch_next():
        nxt = jnp.minimum(e + 1, N_EXPERTS - 1)
        first = f * rows_per_step
        for j in range(rows_per_step):
            start_row(nxt, 1 - slot, first + j)

    @pl.when((e == 0) & (f == 0))
    def _():
        def issue(j, _):
            start_row(0, 0, j)
            return 0

        lax.fori_loop(0, cap, issue, 0, unroll=8)

    @pl.when(f == 0)
    def _():
        wait_rows(slot)
        for s in range(TILES_PER_TOKEN):
            x2_ref[:, s * LANES:(s + 1) * LANES] = (
                xe_ref[slot, pl.ds(s, cap, stride=TILES_PER_TOKEN), :].astype(BF16))

    @pl.when(f < N_HID_STEPS)
    def _():
        prefetch_next()
        x2 = x2_ref[...]
        a = jnp.dot(x2, w1_ref[0].astype(BF16), preferred_element_type=F32)
        b = jnp.dot(x2, w3_ref[0].astype(BF16), preferred_element_type=F32)
        hid_ref[f] = (jax.nn.silu(a) * b).astype(BF16)

    @pl.when(f >= N_HID_STEPS)
    def _():
        w2 = w2_ref[0].astype(BF16)
        out = jnp.dot(hid_ref[0], w2[0:EXPERT_F_BLOCK], preferred_element_type=F32)
        for kb in range(1, N_HID_STEPS):
            out += jnp.dot(hid_ref[kb], w2[kb * EXPERT_F_BLOCK:(kb + 1) * EXPERT_F_BLOCK],
                           preferred_element_type=F32)
        tile0 = (f - N_HID_STEPS) * (EXPERT_N_BLOCK // LANES)
        gate = gate_ref[0]
        for i in range(EXPERT_N_BLOCK // LANES):
            ye_ref[pl.ds(tile0 + i, cap, stride=TILES_PER_TOKEN), :] = out[:, i * LANES:(i + 1) * LANES] * gate

    @pl.when((e == N_EXPERTS - 1) & (f == EXPERT_STEPS - 1))
    def _():
        wait_rows(1 - slot)


def _experts(rows, h2t, gates, w1, w3, w2, cap):
    hid_blk = lambda e, f, idx: (e, 0, jnp.minimum(f, N_HID_STEPS - 1))
    n_out = D_MODEL // EXPERT_N_BLOCK

    def out_blk(e, f, idx):
        hold = (f == 0) & (e > 0)
        return (jnp.where(hold, e - 1, e), 0,
                jnp.where(hold, n_out - 1, jnp.maximum(f - N_HID_STEPS, 0)))

    grid_spec = pltpu.PrefetchScalarGridSpec(
        num_scalar_prefetch=1,
        grid=(N_EXPERTS, EXPERT_STEPS),
        in_specs=[pl.BlockSpec(memory_space=pl.ANY),
                  pl.BlockSpec((1, cap, LANES), lambda e, f, idx: (e, 0, 0)),
                  pl.BlockSpec((1, D_MODEL, EXPERT_F_BLOCK), hid_blk),
                  pl.BlockSpec((1, D_MODEL, EXPERT_F_BLOCK), hid_blk),
                  pl.BlockSpec((1, D_EXPERT, EXPERT_N_BLOCK), out_blk)],
        out_specs=pl.BlockSpec((cap * TILES_PER_TOKEN, LANES), lambda e, f, idx: (e, 0)),
        scratch_shapes=[pltpu.VMEM((2, cap * TILES_PER_TOKEN, LANES), F32),
                        pltpu.VMEM((cap, D_MODEL), BF16),
                        pltpu.VMEM((N_HID_STEPS, cap, EXPERT_F_BLOCK), BF16),
                        pltpu.SemaphoreType.DMA((2,))],
    )
    return pl.pallas_call(
        functools.partial(_expert_kernel, cap),
        grid_spec=grid_spec,
        out_shape=jax.ShapeDtypeStruct((N_EXPERTS * cap * TILES_PER_TOKEN, LANES), F32),
        compiler_params=_compiler_params(("arbitrary", "arbitrary")),
        name="expert_swiglu",
    )(rows, h2t, gates, w1, w3, w2)


COMBINE_BATCH = 8
ZERO_ROWS = 512


def _combine_kernel(cap, n_tok, mod_base, mod_per_seq, nb,
                    row_ref, ye_ref, x1_ref, mod_ref, fw_ref, y_ref,
                    acc_ref, xbuf_ref, ybuf_ref, sem_x, sem_y):
    e = pl.program_id(0)

    @pl.when(e == 0)
    def _():
        def zero(i, _):
            r = pl.multiple_of(i * ZERO_ROWS, ZERO_ROWS)
            acc_ref[pl.ds(r, ZERO_ROWS), :] = jnp.zeros((ZERO_ROWS, LANES), F32)
            return 0

        lax.fori_loop(0, n_tok * TILES_PER_TOKEN // ZERO_ROWS, zero, 0)

    def batch(jb, _):
        vals = []
        for u in range(COMBINE_BATCH):
            j = jb * COMBINE_BATCH + u
            t = pl.multiple_of(row_ref[e * cap + j], TILES_PER_TOKEN)
            src = pl.multiple_of(j * TILES_PER_TOKEN, TILES_PER_TOKEN)
            vals.append((t, acc_ref[pl.ds(t, TILES_PER_TOKEN), :] + ye_ref[pl.ds(src, TILES_PER_TOKEN), :]))
        for t, val in vals:
            acc_ref[pl.ds(t, TILES_PER_TOKEN), :] = val
        return 0

    lax.fori_loop(0, cap // COMBINE_BATCH, batch, 0)

    @pl.when(e == N_EXPERTS - 1)
    def _():
        n_blk = n_tok // TOKEN_BLOCK

        def x1_copy(b, slot):
            rows = pl.ds(pl.multiple_of(b * TOKEN_BLOCK, TOKEN_BLOCK), TOKEN_BLOCK)
            return pltpu.make_async_copy(x1_ref.at[rows, :], xbuf_ref.at[slot], sem_x.at[slot])

        def y_copy(b, slot):
            rows = pl.ds(pl.multiple_of(b * TOKEN_BLOCK, TOKEN_BLOCK), TOKEN_BLOCK)
            return pltpu.make_async_copy(ybuf_ref.at[slot], y_ref.at[rows, :], sem_y.at[slot])

        x1_copy(0, 0).start()

        def block(b, _):
            slot = b % 2

            @pl.when(b + 1 < n_blk)
            def _():
                x1_copy(b + 1, 1 - slot).start()

            x1_copy(b, slot).wait()

            @pl.when(b >= 2)
            def _():
                y_copy(b - 2, slot).wait()

            base = pl.multiple_of(b * (TOKEN_BLOCK * TILES_PER_TOKEN), TOKEN_BLOCK * TILES_PER_TOKEN)
            moe = jnp.concatenate(
                [acc_ref[pl.ds(base + s, TOKEN_BLOCK, stride=TILES_PER_TOKEN), :]
                 for s in range(TILES_PER_TOKEN)], axis=1)
            row = mod_base + ((b // nb) if mod_per_seq else 0)
            ybuf_ref[slot] = _rms(xbuf_ref[slot] + _mod_row(mod_ref, row, 5) * moe, fw_ref[...])
            y_copy(b, slot).start()
            return 0

        lax.fori_loop(0, n_blk, block, 0)
        y_copy(n_blk - 2, n_blk % 2).wait()
        y_copy(n_blk - 1, (n_blk - 1) % 2).wait()


def _combine_final(rows, ye, x1, mod, fw, cap, n_tok, *, nb, mod_base, mod_per_seq):
    full = lambda a: pl.BlockSpec(a.shape, lambda e, rows: (0,) * a.ndim)
    grid_spec = pltpu.PrefetchScalarGridSpec(
        num_scalar_prefetch=1,
        grid=(N_EXPERTS,),
        in_specs=[pl.BlockSpec((cap * TILES_PER_TOKEN, LANES), lambda e, rows: (e, 0)),
                  pl.BlockSpec(memory_space=pl.ANY), full(mod), full(fw)],
        out_specs=pl.BlockSpec(memory_space=pl.ANY),
        scratch_shapes=[pltpu.VMEM((n_tok * TILES_PER_TOKEN, LANES), F32),
                        pltpu.VMEM((2, TOKEN_BLOCK, D_MODEL), F32),
                        pltpu.VMEM((2, TOKEN_BLOCK, D_MODEL), F32),
                        pltpu.SemaphoreType.DMA((2,)),
                        pltpu.SemaphoreType.DMA((2,))],
    )
    return pl.pallas_call(
        functools.partial(_combine_kernel, cap, n_tok, mod_base, mod_per_seq, nb),
        grid_spec=grid_spec,
        out_shape=jax.ShapeDtypeStruct((n_tok, D_MODEL), F32),
        compiler_params=_compiler_params(("arbitrary",)),
        name="moe_combine_norm",
    )(rows, ye, x1, mod, fw)


def _trunk_and_norm(x, e_tab, mod, s0_f, s0_b, prm, *, nseq, seq_len, add_pos, mod_base, mod_per_seq):
    nb = seq_len // TOKEN_BLOCK
    n_tok = nseq * seq_len
    cap = EC_CAPACITY_FACTOR * n_tok // N_EXPERTS
    kw = dict(nseq=nseq, nb=nb, add_pos=add_pos, mod_base=mod_base, mod_per_seq=mod_per_seq)
    if nb == 1 and not add_pos:
        x1, h2t, probs, sfin_f, sfin_b = _mixer_fused(
            x, mod, prm, s0_f, s0_b, nseq=nseq, mod_base=mod_base, mod_per_seq=mod_per_seq)
    else:
        qkv, g, lab, s, of, sfin_f = _mixer_fwd(
            x, e_tab, mod, prm["n1"], prm["win"], prm["wa"], prm["ba"], prm["snw"], prm["sws"],
            prm["sbs"], s0_f, **kw)
        x1, h2t, probs, sfin_b = _mixer_bwd(
            x, e_tab, mod, qkv, g, lab, s, of, prm["gnw"], prm["wout"], prm["n2"], prm["rw_t"],
            s0_b, **kw)
    x1 = x1.reshape(n_tok, D_MODEL)
    h2t = h2t.reshape(n_tok * TILES_PER_TOKEN, LANES)
    probs = probs.reshape(n_tok // TOKEN_BLOCK, N_EXPERTS, TOKEN_BLOCK)
    rows, gates = _route(probs, n_tok, cap)
    rows = rows.reshape(-1)
    ye = _experts(rows, h2t, gates, prm["w1"], prm["w3"], prm["w2"], cap)
    y = _combine_final(rows, ye, x1, mod, prm["fw"], cap, n_tok,
                       nb=nb, mod_base=mod_base, mod_per_seq=mod_per_seq)
    return y.reshape(nseq, seq_len, D_MODEL), sfin_f, sfin_b


def kernel(x_prompt, x_sample, state_gla_fwd, state_gla_bwd, c, c_ctx, ada_w, ada_b, norm1_w, w_in, gla_wa2_f, gla_ba_f, gla_wa2_b, gla_ba_b, gla_norm_w, sgu_norm_w, sgu_ws, sgu_bs, w_out, norm2_w, router_w, exp_w1, exp_w3, exp_w2, final_norm_w):
    assert ada_w.shape[0] == 1, "single trunk layer"
    batch, seq, _ = x_prompt.shape
    dec_batch, dec_seq, _ = x_sample.shape

    assert w_in.shape[2] == D_IN
    win = _win_transposed_bf16(w_in[0].T)
    wa = jnp.zeros((2 * GLA_LOWRANK, 2 * QK_W), F32)
    wa = wa.at[0:GLA_LOWRANK, 0:QK_W].set(gla_wa2_f[0])
    wa = wa.at[GLA_LOWRANK:2 * GLA_LOWRANK, QK_W:].set(gla_wa2_b[0]).astype(BF16)
    prm = dict(
        n1=norm1_w, win=win, wa=wa,
        ba=jnp.concatenate([gla_ba_f[0], gla_ba_b[0]])[None, :],
        snw=sgu_norm_w, sws=sgu_ws[0].astype(BF16),
        sbs=jnp.broadcast_to(sgu_bs[0][:, :, None], (SGU_GROUPS, SGU_CHUNK, SGU_CH)),
        gnw=gla_norm_w, wout=w_out[0].astype(BF16), n2=norm2_w, rw_t=router_w[0].T,
        w1=exp_w1[0], w3=exp_w3[0], w2=exp_w2[0], fw=final_norm_w[None, :])

    cvec = jnp.concatenate([c_ctx[None, :], c, jnp.zeros((SUBLANES - 1 - dec_batch, D_MODEL), F32)])
    mod = _modulation(cvec, ada_w[0], ada_b)
    e_tab = _pos_table()

    zero_state = jnp.zeros((min(SEQ_GROUP, batch), GLA_HEADS, GLA_DK, GLA_DV), F32)
    y_prompt, sf, sb = _trunk_and_norm(
        x_prompt, e_tab, mod, zero_state, zero_state, prm,
        nseq=batch, seq_len=seq, add_pos=False, mod_base=0, mod_per_seq=False)
    y_sample, _, _ = _trunk_and_norm(
        x_sample, e_tab, mod, state_gla_fwd[:, 0], state_gla_bwd[:, 0], prm,
        nseq=dec_batch, seq_len=dec_seq, add_pos=True, mod_base=1, mod_per_seq=True)
    return (y_prompt, y_sample, sf[:, None], sb[:, None])
```

```python
import functools
import math

import jax
import jax.numpy as jnp
from jax import lax
from jax.experimental import pallas as pl
from jax.experimental.pallas import tpu as pltpu

F32 = jnp.float32
BF16 = jnp.bfloat16
I32 = jnp.int32

D_MODEL = 1024
GRID_W = 64
GLA_HEADS = 4
GLA_DK = 64
GLA_DV = 128
GLA_WIDTH = GLA_HEADS * GLA_DV
QK_W = GLA_HEADS * GLA_DK
GLA_LOWRANK = 16
GLA_GATE_NORM = 16.0
GLA_CHUNK = 64
SGU_WIDTH = 512
SGU_GROUPS = 4
SGU_CH = 128
SGU_CHUNK = 128
N_EXPERTS = 16
EC_CAPACITY_FACTOR = 2
D_EXPERT = 2048
EPS = 1e-6

SUBLANES = 8
LANES = 128
TILES_PER_TOKEN = D_MODEL // LANES

TOKEN_BLOCK = 256
SEQ_GROUP = 4
OFF_A = 2 * QK_W + 2 * GLA_WIDTH
OFF_U = OFF_A + 2 * GLA_LOWRANK
D_IN = OFF_U + 2 * SGU_WIDTH
EXPERT_F_BLOCK = 512
EXPERT_N_BLOCK = 512
N_HID_STEPS = D_EXPERT // EXPERT_F_BLOCK
EXPERT_STEPS = N_HID_STEPS + D_MODEL // EXPERT_N_BLOCK
VMEM_LIMIT = 56 * 1024 * 1024
F32_ONE_BITS = 0x3F800000
THRESHOLD_STEPS = F32_ONE_BITS.bit_length() + 1


def _dot(a, b):
    return jnp.dot(a.astype(BF16), b.astype(BF16), preferred_element_type=F32)


def _dot_nt(a, b):
    return lax.dot_general(a.astype(BF16), b.astype(BF16), (((1,), (1,)), ((), ())),
                           preferred_element_type=F32)


def _dot_tn(a, b):
    return lax.dot_general(a.astype(BF16), b.astype(BF16), (((0,), (0,)), ((), ())),
                           preferred_element_type=F32)


def _dot_f32(a, b, dims=(((1,), (0,)), ((), ()))):
    return lax.dot_general(a, b, dims, precision=lax.Precision.HIGHEST, preferred_element_type=F32)


def _split_bf16(x, terms):
    parts = []
    for _ in range(terms - 1):
        part = x.astype(BF16)
        parts.append(part)
        x = x - part.astype(F32)
    parts.append(x.astype(BF16))
    return parts


def _select_dot(sel, x):
    s = sel.astype(BF16)
    hi, mid, lo = _split_bf16(x, 3)
    return (jnp.dot(s, lo, preferred_element_type=F32) + jnp.dot(s, mid, preferred_element_type=F32)
            + jnp.dot(s, hi, preferred_element_type=F32))


def _dot_nt_3pass(a, b):
    a_hi, a_lo = _split_bf16(a, 2)
    b_hi, b_lo = _split_bf16(b, 2)
    nt = lambda x, y: lax.dot_general(x, y, (((1,), (1,)), ((), ())), preferred_element_type=F32)
    return (nt(a_hi, b_lo) + nt(a_lo, b_hi)) + nt(a_hi, b_hi)


def _rms(x, w):
    return x * lax.rsqrt(jnp.mean(x * x, axis=-1, keepdims=True) + EPS) * w


def _compiler_params(sem):
    return pltpu.CompilerParams(dimension_semantics=sem, vmem_limit_bytes=VMEM_LIMIT)


def _mod_kernel(c_ref, w_ref, b_ref, o_ref):
    o_ref[...] = _dot(jax.nn.silu(c_ref[...]), w_ref[...]) + b_ref[...]


def _modulation(cvec, ada_w, ada_b):
    n = ada_w.shape[1]
    bn = 1536
    return pl.pallas_call(
        _mod_kernel,
        grid=(n // bn,),
        in_specs=[pl.BlockSpec((SUBLANES, D_MODEL), lambda j: (0, 0)),
                  pl.BlockSpec((D_MODEL, bn), lambda j: (0, j)),
                  pl.BlockSpec((1, bn), lambda j: (0, j))],
        out_specs=pl.BlockSpec((SUBLANES, bn), lambda j: (0, j)),
        out_shape=jax.ShapeDtypeStruct((SUBLANES, n), F32),
        compiler_params=_compiler_params(("arbitrary",)),
        name="adaln_mod",
    )(cvec, ada_w, ada_b)


def _cast_kernel(w_ref, o_ref):
    o_ref[...] = w_ref[...].astype(BF16)


def _win_transposed_bf16(w_t):
    rows = D_IN // 3
    return pl.pallas_call(
        _cast_kernel,
        grid=(D_IN // rows,),
        in_specs=[pl.BlockSpec((rows, D_MODEL), lambda i: (i, 0))],
        out_specs=pl.BlockSpec((rows, D_MODEL), lambda i: (i, 0)),
        out_shape=jax.ShapeDtypeStruct((D_IN, D_MODEL), BF16),
        compiler_params=_compiler_params(("arbitrary",)),
        name="win_cast",
    )(w_t)


def _pos_kernel(o_ref):
    nf = D_MODEL // 4
    p = lax.broadcasted_iota(I32, (GRID_W, nf), 0).astype(F32)
    i = lax.broadcasted_iota(I32, (GRID_W, nf), 1).astype(F32)
    omega = jnp.exp(i * (-math.log(10000.0) / nf))
    a = p * omega
    o_ref[:, 0:nf] = jnp.sin(a)
    o_ref[:, nf:2 * nf] = jnp.cos(a)


def _pos_table():
    return pl.pallas_call(
        _pos_kernel,
        out_shape=jax.ShapeDtypeStruct((GRID_W, D_MODEL // 2), F32),
        name="sincos_table",
    )()


def _add_pos(x, e_ref, blk, add_pos):
    if not add_pos:
        return x
    half = D_MODEL // 2
    e_all = e_ref[...]
    rows = []
    for j in range(TOKEN_BLOCK // GRID_W):
        xj = x[j * GRID_W:(j + 1) * GRID_W]
        e_row = e_ref[pl.ds(blk * (TOKEN_BLOCK // GRID_W) + j, 1), :]
        rows.append(jnp.concatenate([xj[:, 0:half] + e_row, xj[:, half:] + e_all], axis=1))
    return jnp.concatenate(rows, axis=0)


def _chunk_masks():
    r = lax.broadcasted_iota(I32, (TOKEN_BLOCK, TOKEN_BLOCK), 0)
    c = lax.broadcasted_iota(I32, (TOKEN_BLOCK, TOKEN_BLOCK), 1)
    same = (r // GLA_CHUNK) == (c // GLA_CHUNK)
    return same & (c <= r), same & (c >= r)


def _gla_direction(q, k, v, cum, fwd, att_mask, st_ref, ready, done):
    qe = q * jnp.exp(cum)
    ke = k * jnp.exp(-cum)
    yield
    lane = lax.broadcasted_iota(I32, (1, LANES), 1)
    o_intra = []
    for pair in range(2):
        qp = qe[:, pair * LANES:(pair + 1) * LANES]
        kp = ke[:, pair * LANES:(pair + 1) * LANES]
        for hh in range(2):
            qm = jnp.where((lane // GLA_DK) == hh, qp, 0.0)
            att = jnp.where(att_mask, _dot_nt(qm, kp), 0.0)
            head = 2 * pair + hh
            o_intra.append(_dot(att, v[:, head * GLA_DV:(head + 1) * GLA_DV]))
            yield
    o_intra = jnp.concatenate(o_intra, axis=1)
    while not ready():
        yield

    er = lax.broadcasted_iota(I32, (2 * GLA_DV, 2 * GLA_DK), 0)
    dc = lax.broadcasted_iota(I32, (2 * GLA_DV, 2 * GLA_DK), 1)
    same_head = (er // GLA_DV) == (dc // GLA_DK)
    n_chunks = TOKEN_BLOCK // GLA_CHUNK
    o_inter = [None] * n_chunks
    for c in (range(n_chunks) if fwd else reversed(range(n_chunks))):
        r0 = c * GLA_CHUNK
        rows = slice(r0, r0 + GLA_CHUNK)
        last = cum[r0 + GLA_CHUNK - 1:r0 + GLA_CHUNK] if fwd else cum[r0:r0 + 1]
        kd = k[rows] * jnp.exp(last - cum[rows])
        dec = jnp.exp(last)
        parts = []
        for pair in range(2):
            dl = slice(pair * LANES, (pair + 1) * LANES)
            st = st_ref[pair]
            parts.append(_dot_nt(qe[rows, dl], st))
            ds_t = _dot_tn(v[rows, pair * 2 * GLA_DV:(pair + 1) * 2 * GLA_DV], kd[:, dl])
            st_ref[pair] = dec[:, dl] * st + jnp.where(same_head, ds_t, 0.0)
        o_inter[c] = jnp.concatenate(parts, axis=1)
        yield
    done()
    return o_intra + jnp.concatenate(o_inter, axis=0)


def _interleave(chains):
    chains = list(chains)
    done = [False] * len(chains)
    tick = 0
    while not all(done):
        for i, ch in enumerate(chains):
            if tick >= i and not done[i]:
                try:
                    next(ch)
                except StopIteration:
                    done[i] = True
        tick += 1


def _load_state(s0_ref, u, st_ref):
    zero = jnp.zeros((GLA_DV, GLA_DK), F32)
    for pair in range(2):
        a = s0_ref[u, 2 * pair].T
        b = s0_ref[u, 2 * pair + 1].T
        st_ref[pair] = jnp.concatenate(
            [jnp.concatenate([a, zero], axis=1), jnp.concatenate([zero, b], axis=1)], axis=0)


def _store_state(st_ref, sfin_ref, u):
    for pair in range(2):
        st = st_ref[pair]
        sfin_ref[u, 2 * pair] = st[0:GLA_DV, 0:GLA_DK].T
        sfin_ref[u, 2 * pair + 1] = st[GLA_DV:2 * GLA_DV, GLA_DK:2 * GLA_DK].T


def _mod_row(mod_ref, row, part):
    return mod_ref[pl.ds(row, 1), part * D_MODEL:(part + 1) * D_MODEL]


def _front_stages(xin, row, mod_ref, n1_ref, win_ref, wa_ref, ba_ref, snw_ref, sws_ref, sbs_ref):
    h = _rms(xin, n1_ref[...]) * (1.0 + _mod_row(mod_ref, row, 1)) + _mod_row(mod_ref, row, 0)
    yield
    hb = h.astype(BF16)
    parts = []
    for r0, r1 in ((0, OFF_A), (OFF_U, D_IN), (OFF_A, OFF_U)):
        parts.append(lax.dot_general(hb, win_ref[r0:r1, :], (((1,), (1,)), ((), ())),
                                     preferred_element_type=F32))
        yield
    main, gate, low = parts
    q = main[:, 0:QK_W] * (GLA_DK ** -0.5)
    k = main[:, QK_W:2 * QK_W]
    v = main[:, 2 * QK_W:2 * QK_W + GLA_WIDTH]
    z = _dot(low, wa_ref[...]) + ba_ref[...]
    la = (jnp.minimum(z, 0.0) - jnp.log1p(jnp.exp(-jnp.abs(z)))) * (1.0 / GLA_GATE_NORM)
    yield

    ug = jax.nn.gelu(gate[:, 0:SGU_WIDTH])
    yield
    vg = jax.nn.gelu(gate[:, SGU_WIDTH:2 * SGU_WIDTH])
    yield
    s_cols = []
    for gi in range(SGU_GROUPS):
        cols = slice(gi * SGU_CH, (gi + 1) * SGU_CH)
        vn = _rms(vg[:, cols], snw_ref[:, cols])
        rhs = jnp.concatenate([vn[0:SGU_CHUNK], vn[SGU_CHUNK:2 * SGU_CHUNK]], axis=1)
        vm = _dot(sws_ref[gi], rhs) + jnp.concatenate([sbs_ref[gi], sbs_ref[gi]], axis=1)
        vm = jnp.concatenate([vm[:, 0:SGU_CH], vm[:, SGU_CH:2 * SGU_CH]], axis=0)
        s_cols.append(ug[:, cols] * vm)
    yield
    return q, k, v, main[:, 2 * QK_W + GLA_WIDTH:OFF_A], la, jnp.concatenate(s_cols, axis=1)


def _back_stages(o, g, s_val, xin, row, mod_ref, gnw_ref, wout_ref, n2_ref, rw_ref):
    cols = []
    for head in range(GLA_HEADS):
        hs = slice(head * GLA_DV, (head + 1) * GLA_DV)
        cols.append(_rms(o[:, hs], gnw_ref[...]) * jax.nn.silu(g[:, hs]))
    cols.append(s_val)
    yield
    y = _dot(jnp.concatenate(cols, axis=1), wout_ref[...])
    yield
    x1 = xin + _mod_row(mod_ref, row, 2) * y
    h2 = _rms(x1, n2_ref[...]) * (1.0 + _mod_row(mod_ref, row, 4)) + _mod_row(mod_ref, row, 3)
    yield
    logits = _dot_nt_3pass(rw_ref[...], h2)
    m = jnp.max(logits, axis=0, keepdims=True)
    ex = jnp.exp(logits - m)
    return x1, h2, ex / jnp.sum(ex, axis=0, keepdims=True)


def _store_token_tiles(h2t_ref, u, b, h2):
    for s in range(TILES_PER_TOKEN):
        h2t_ref[u, pl.ds(b * TOKEN_BLOCK * TILES_PER_TOKEN + s, TOKEN_BLOCK, stride=TILES_PER_TOKEN), :] = (
            h2[:, s * LANES:(s + 1) * LANES])


def _zip_stages(gen_a, gen_b):
    out = [None, None]
    live = [gen_a, gen_b]
    while any(g is not None for g in live):
        for i, g in enumerate(live):
            if g is not None:
                try:
                    next(g)
                except StopIteration as stop:
                    out[i] = stop.value
                    live[i] = None
        yield
    return out


def _mixer_fwd_kernel(group, add_pos, mod_base, mod_per_seq, nb,
                      x_ref, e_ref, mod_ref, n1_ref, win_ref, wa_ref, ba_ref,
                      snw_ref, sws_ref, sbs_ref, s0_ref,
                      qkv_ref, g_ref, lab_ref, s_ref, of_ref, sfin_ref,
                      st_ref):
    grp = pl.program_id(0)
    blk = pl.program_id(1)
    sub = x_ref.shape[1] // TOKEN_BLOCK
    lo_mask, _ = _chunk_masks()
    state_done = set()

    @pl.when(blk == 0)
    def _():
        for u in range(group):
            _load_state(s0_ref, u, st_ref.at[u])

    def chain(u, b):
        row = mod_base + ((grp * group + u) if mod_per_seq else 0)
        st_u = st_ref.at[u]
        rows = slice(b * TOKEN_BLOCK, (b + 1) * TOKEN_BLOCK)
        xin = _add_pos(x_ref[u, rows, :], e_ref, blk * sub + b, add_pos)
        q, k, v, g, la, s_val = yield from _front_stages(
            xin, row, mod_ref, n1_ref, win_ref, wa_ref, ba_ref, snw_ref, sws_ref, sbs_ref)
        qkv_ref[u, rows, :] = jnp.concatenate([q, k, v], axis=1)
        g_ref[u, rows, :] = g
        lab_ref[u, rows, :] = la[:, QK_W:2 * QK_W]
        s_ref[u, rows, :] = s_val

        cum = _select_dot(lo_mask, la[:, 0:QK_W])
        yield
        of_ref[u, rows, :] = yield from _gla_direction(
            q, k, v, cum, True, lo_mask, st_u,
            ready=lambda: b == 0 or (u, b - 1) in state_done, done=lambda: state_done.add((u, b)))

    _interleave(chain(u, b) for b in range(sub) for u in range(group))

    @pl.when(blk == nb - 1)
    def _():
        for u in range(group):
            _store_state(st_ref.at[u], sfin_ref, u)


def _mixer_fwd(x, e_tab, mod, n1, win, wa, ba, snw, sws, sbs, s0, *, nseq, nb, add_pos,
               mod_base, mod_per_seq):
    seq_len = nb * TOKEN_BLOCK
    group = min(SEQ_GROUP, nseq)
    sub = min(SEQ_GROUP // group, nb)
    nb //= sub
    tok = lambda w: pl.BlockSpec((group, sub * TOKEN_BLOCK, w), lambda s, i: (s, i, 0))
    full = lambda a: pl.BlockSpec(a.shape, lambda s, i: (0,) * a.ndim)
    st_spec = pl.BlockSpec((group, GLA_HEADS, GLA_DK, GLA_DV), lambda s, i: (s, 0, 0, 0))
    s0_spec = st_spec if s0.shape[0] == nseq else pl.BlockSpec(s0.shape, lambda s, i: (0, 0, 0, 0))
    act = lambda w: jax.ShapeDtypeStruct((nseq, seq_len, w), F32)
    kern = functools.partial(_mixer_fwd_kernel, group, add_pos, mod_base, mod_per_seq, nb)
    return pl.pallas_call(
        kern,
        grid=(nseq // group, nb),
        in_specs=[tok(D_MODEL), full(e_tab), full(mod), full(n1), full(win), full(wa), full(ba),
                  full(snw), full(sws), full(sbs), s0_spec],
        out_specs=[tok(1024), tok(GLA_WIDTH), tok(QK_W), tok(SGU_WIDTH), tok(GLA_WIDTH), st_spec],
        out_shape=[act(1024), act(GLA_WIDTH), act(QK_W), act(SGU_WIDTH), act(GLA_WIDTH),
                   jax.ShapeDtypeStruct((nseq, GLA_HEADS, GLA_DK, GLA_DV), F32)],
        scratch_shapes=[pltpu.VMEM((group, 2, 2 * GLA_DV, 2 * GLA_DK), F32)],
        compiler_params=_compiler_params(("arbitrary", "arbitrary")),
        name="mixer_fwd",
    )(x, e_tab, mod, n1, win, wa, ba, snw, sws, sbs, s0)


def _mixer_bwd_kernel(group, add_pos, mod_base, mod_per_seq, nb,
                      x_ref, e_ref, mod_ref, qkv_ref, g_ref, lab_ref, s_ref, of_ref,
                      gnw_ref, wout_ref, n2_ref, rw_ref, s0_ref,
                      x1_ref, h2t_ref, probs_ref, sfin_ref,
                      st_ref):
    grp = pl.program_id(0)
    step = pl.program_id(1)
    sub = x_ref.shape[1] // TOKEN_BLOCK
    blk = nb - 1 - step
    _, hi_mask = _chunk_masks()
    state_done = set()

    @pl.when(step == 0)
    def _():
        for u in range(group):
            _load_state(s0_ref, u, st_ref.at[u])

    def chain(u, b):
        row = mod_base + ((grp * group + u) if mod_per_seq else 0)
        st_u = st_ref.at[u]
        rows = slice(b * TOKEN_BLOCK, (b + 1) * TOKEN_BLOCK)
        qkv = qkv_ref[u, rows, :]
        q, k, v = qkv[:, 0:QK_W], qkv[:, QK_W:2 * QK_W], qkv[:, 2 * QK_W:]
        cum = _select_dot(hi_mask, lab_ref[u, rows, :])
        yield
        o_b = yield from _gla_direction(
            q, k, v, cum, False, hi_mask, st_u,
            ready=lambda: b == sub - 1 or (u, b + 1) in state_done,
            done=lambda: state_done.add((u, b)))
        xin = _add_pos(x_ref[u, rows, :], e_ref, blk * sub + b, add_pos)
        x1, h2, probs = yield from _back_stages(
            of_ref[u, rows, :] + o_b, g_ref[u, rows, :], s_ref[u, rows, :], xin, row,
            mod_ref, gnw_ref, wout_ref, n2_ref, rw_ref)
        x1_ref[u, rows, :] = x1
        _store_token_tiles(h2t_ref, u, b, h2)
        probs_ref[u, b] = probs

    _interleave(chain(u, b) for b in reversed(range(sub)) for u in range(group))

    @pl.when(step == nb - 1)
    def _():
        for u in range(group):
            _store_state(st_ref.at[u], sfin_ref, u)


def _mixer_bwd(x, e_tab, mod, qkv, g, lab, s, of, gnw, wout, n2, rw_t, s0, *, nseq, nb, add_pos,
               mod_base, mod_per_seq):
    seq_len = nb * TOKEN_BLOCK
    group = min(SEQ_GROUP, nseq)
    sub = min(SEQ_GROUP // group, nb)
    nb //= sub
    tok = lambda w: pl.BlockSpec((group, sub * TOKEN_BLOCK, w), lambda s_, i: (s_, nb - 1 - i, 0))
    full = lambda a: pl.BlockSpec(a.shape, lambda s_, i: (0,) * a.ndim)
    st_spec = pl.BlockSpec((group, GLA_HEADS, GLA_DK, GLA_DV), lambda s_, i: (s_, 0, 0, 0))
    s0_spec = st_spec if s0.shape[0] == nseq else pl.BlockSpec(s0.shape, lambda s_, i: (0, 0, 0, 0))
    kern = functools.partial(_mixer_bwd_kernel, group, add_pos, mod_base, mod_per_seq, nb)
    return pl.pallas_call(
        kern,
        grid=(nseq // group, nb),
        in_specs=[tok(D_MODEL), full(e_tab), full(mod), tok(1024), tok(GLA_WIDTH), tok(QK_W),
                  tok(SGU_WIDTH), tok(GLA_WIDTH), full(gnw), full(wout), full(n2), full(rw_t),
                  s0_spec],
        out_specs=[tok(D_MODEL),
                   pl.BlockSpec((group, sub * TOKEN_BLOCK * TILES_PER_TOKEN, LANES),
                                lambda s_, i: (s_, nb - 1 - i, 0)),
                   pl.BlockSpec((group, sub, N_EXPERTS, TOKEN_BLOCK),
                                lambda s_, i: (s_, nb - 1 - i, 0, 0)),
                   st_spec],
        out_shape=[jax.ShapeDtypeStruct((nseq, seq_len, D_MODEL), F32),
                   jax.ShapeDtypeStruct((nseq, seq_len * TILES_PER_TOKEN, LANES), F32),
                   jax.ShapeDtypeStruct((nseq, nb * sub, N_EXPERTS, TOKEN_BLOCK), F32),
                   jax.ShapeDtypeStruct((nseq, GLA_HEADS, GLA_DK, GLA_DV), F32)],
        scratch_shapes=[pltpu.VMEM((group, 2, 2 * GLA_DV, 2 * GLA_DK), F32)],
        compiler_params=_compiler_params(("arbitrary", "arbitrary")),
        name="mixer_bwd",
    )(x, e_tab, mod, qkv, g, lab, s, of, gnw, wout, n2, rw_t, s0)


def _mixer_fused_kernel(group, mod_base, mod_per_seq,
                        x_ref, mod_ref, n1_ref, win_ref, wa_ref, ba_ref, snw_ref, sws_ref, sbs_ref,
                        gnw_ref, wout_ref, n2_ref, rw_ref, s0f_ref, s0b_ref,
                        x1_ref, h2t_ref, probs_ref, sfin_f_ref, sfin_b_ref,
                        stf_ref, stb_ref):
    grp = pl.program_id(0)
    lo_mask, hi_mask = _chunk_masks()
    always = lambda: True
    nothing = lambda: None
    for u in range(group):
        _load_state(s0f_ref, u, stf_ref.at[u])
        _load_state(s0b_ref, u, stb_ref.at[u])

    def chain(u):
        row = mod_base + ((grp * group + u) if mod_per_seq else 0)
        xin = x_ref[u]
        q, k, v, g, la, s_val = yield from _front_stages(
            xin, row, mod_ref, n1_ref, win_ref, wa_ref, ba_ref, snw_ref, sws_ref, sbs_ref)
        cum_f = _select_dot(lo_mask, la[:, 0:QK_W])
        cum_b = _select_dot(hi_mask, la[:, QK_W:2 * QK_W])
        yield
        o_f, o_b = yield from _zip_stages(
            _gla_direction(q, k, v, cum_f, True, lo_mask, stf_ref.at[u], always, nothing),
            _gla_direction(q, k, v, cum_b, False, hi_mask, stb_ref.at[u], always, nothing))
        x1, h2, probs = yield from _back_stages(
            o_f + o_b, g, s_val, xin, row, mod_ref, gnw_ref, wout_ref, n2_ref, rw_ref)
        x1_ref[u] = x1
        _store_token_tiles(h2t_ref, u, 0, h2)
        probs_ref[u, 0] = probs

    _interleave(chain(u) for u in range(group))
    for u in range(group):
        _store_state(stf_ref.at[u], sfin_f_ref, u)
        _store_state(stb_ref.at[u], sfin_b_ref, u)


def _mixer_fused(x, mod, prm, s0_f, s0_b, *, nseq, mod_base, mod_per_seq):
    group = min(SEQ_GROUP, nseq)
    consts = [mod, prm["n1"], prm["win"], prm["wa"], prm["ba"], prm["snw"], prm["sws"], prm["sbs"],
              prm["gnw"], prm["wout"], prm["n2"], prm["rw_t"]]
    tok = lambda w: pl.BlockSpec((group, TOKEN_BLOCK, w), lambda s: (s, 0, 0))
    full = lambda a: pl.BlockSpec(a.shape, lambda s: (0,) * a.ndim)
    st_spec = pl.BlockSpec((group, GLA_HEADS, GLA_DK, GLA_DV), lambda s: (s, 0, 0, 0))
    s0_spec = lambda a: st_spec if a.shape[0] == nseq else pl.BlockSpec(a.shape, lambda s: (0, 0, 0, 0))
    state = jax.ShapeDtypeStruct((nseq, GLA_HEADS, GLA_DK, GLA_DV), F32)
    st_scratch = pltpu.VMEM((group, 2, 2 * GLA_DV, 2 * GLA_DK), F32)
    return pl.pallas_call(
        functools.partial(_mixer_fused_kernel, group, mod_base, mod_per_seq),
        grid=(nseq // group,),
        in_specs=[tok(D_MODEL)] + [full(a) for a in consts] + [s0_spec(s0_f), s0_spec(s0_b)],
        out_specs=[tok(D_MODEL),
                   pl.BlockSpec((group, TOKEN_BLOCK * TILES_PER_TOKEN, LANES), lambda s: (s, 0, 0)),
                   pl.BlockSpec((group, 1, N_EXPERTS, TOKEN_BLOCK), lambda s: (s, 0, 0, 0)),
                   st_spec, st_spec],
        out_shape=[jax.ShapeDtypeStruct((nseq, TOKEN_BLOCK, D_MODEL), F32),
                   jax.ShapeDtypeStruct((nseq, TOKEN_BLOCK * TILES_PER_TOKEN, LANES), F32),
                   jax.ShapeDtypeStruct((nseq, 1, N_EXPERTS, TOKEN_BLOCK), F32),
                   state, state],
        scratch_shapes=[st_scratch, st_scratch],
        compiler_params=_compiler_params(("arbitrary",)),
        name="mixer_fused",
    )(x, *consts, s0_f, s0_b)


def _route_kernel(n_tok, cap, probs_ref, row_ref, gate_ref, xs_ref, ps_ref):
    n_blk = n_tok // TOKEN_BLOCK
    n_chunk = n_tok // LANES
    probs = jnp.concatenate([probs_ref[b] for b in range(n_blk)], axis=1)
    capf = jnp.float32(cap)

    def count(mask):
        return jnp.sum(mask.astype(F32), axis=1, keepdims=True)

    def as_f32(bits):
        return lax.bitcast_convert_type(bits, F32)

    def thr_step(_, lohi):
        lo, hi = lohi
        mid = lo + ((hi - lo + 1) >> 1)
        ok = count(probs >= as_f32(mid)) >= capf
        return jnp.where(ok, mid, lo), jnp.where(ok, hi, mid - 1)

    lo0 = jnp.zeros((N_EXPERTS, 1), I32)
    hi0 = jnp.full((N_EXPERTS, 1), F32_ONE_BITS, I32)
    thr, _ = lax.fori_loop(0, THRESHOLD_STEPS, thr_step, (lo0, hi0))
    gt = probs >= as_f32(thr + 1)
    eq = (probs >= as_f32(thr)) & jnp.logical_not(gt)
    need = capf - count(gt)
    tok = lax.broadcasted_iota(I32, (N_EXPERTS, n_tok), 1)

    def tie_step(_, lohi):
        lo, hi = lohi
        mid = (lo + hi) >> 1
        ok = count(eq & (tok <= mid)) >= need
        return jnp.where(ok, lo, mid + 1), jnp.where(ok, mid, hi)

    n_bits = max(1, (n_tok - 1).bit_length())
    cut, _ = lax.fori_loop(0, n_bits, tie_step,
                           (jnp.zeros((N_EXPERTS, 1), I32), jnp.full((N_EXPERTS, 1), n_tok - 1, I32)))
    sel = (gt | (eq & (tok <= cut))).astype(F32)

    xs_ref[...] = jnp.concatenate([sel[:, c * LANES:(c + 1) * LANES] for c in range(n_chunk)], axis=0)
    ps_ref[...] = jnp.concatenate([probs[:, c * LANES:(c + 1) * LANES] for c in range(n_chunk)], axis=0)

    li = lax.broadcasted_iota(I32, (LANES, LANES), 0)
    lj = lax.broadcasted_iota(I32, (LANES, LANES), 1)
    upper = (li <= lj).astype(F32)
    ci = lax.broadcasted_iota(I32, (n_chunk, n_chunk), 0)
    cj = lax.broadcasted_iota(I32, (n_chunk, n_chunk), 1)
    lower = (cj <= ci).astype(F32)
    slot = lax.broadcasted_iota(I32, (1, cap), 1).astype(F32)
    chunk_id = lax.broadcasted_iota(I32, (n_chunk, cap), 0).astype(F32)
    lane_id = lax.broadcasted_iota(I32, (LANES, cap), 0).astype(F32)
    reps = cap // LANES

    def store_flat(ref, e, val):
        for i in range(reps):
            ref[pl.ds(e * reps + i, 1), :] = val[:, i * LANES:(i + 1) * LANES]

    def per_expert(e):
        x = xs_ref[pl.ds(e, n_chunk, stride=N_EXPERTS), :]
        pe = ps_ref[pl.ds(e, n_chunk, stride=N_EXPERTS), :]
        ploc = _dot(x, upper)
        tot = jnp.broadcast_to(ploc[:, LANES - 1:LANES], (n_chunk, LANES))
        cum = _dot(lower, tot)
        yield
        cum_w = jnp.concatenate([cum] * reps, axis=1)
        base_w = jnp.concatenate([cum - tot] * reps, axis=1)
        chunk_of = jnp.sum((cum_w <= slot).astype(F32), axis=0, keepdims=True)
        onehot = chunk_id == chunk_of
        local = slot - jnp.sum(jnp.where(onehot, base_w, 0.0), axis=0, keepdims=True)
        yield
        lhs = jnp.concatenate([ploc.astype(BF16)] + _split_bf16(pe, 3), axis=1)
        got = _dot_tn(lhs, onehot.astype(F32))
        yield
        pref = got[0:LANES]
        lane_of = jnp.sum((pref <= local).astype(F32), axis=0, keepdims=True)
        token = chunk_of * LANES + lane_of
        store_flat(row_ref, e, (token * TILES_PER_TOKEN).astype(I32))
        yield
        pg = (got[3 * LANES:4 * LANES] + got[2 * LANES:3 * LANES]) + got[LANES:2 * LANES]
        gate = jnp.sum(jnp.where(lane_id == lane_of, pg, 0.0), axis=0, keepdims=True)
        gate_ref[e] = jnp.transpose(jnp.broadcast_to(gate, (LANES, cap)))

    def expert_pair(i, _):
        _interleave(per_expert(2 * i + u) for u in range(2))
        return 0

    lax.fori_loop(0, N_EXPERTS // 2, expert_pair, 0)


def _route(probs, n_tok, cap):
    return pl.pallas_call(
        functools.partial(_route_kernel, n_tok, cap),
        out_shape=[jax.ShapeDtypeStruct((N_EXPERTS * cap // LANES, LANES), I32),
                   jax.ShapeDtypeStruct((N_EXPERTS, cap, LANES), F32)],
        scratch_shapes=[pltpu.VMEM((n_tok // LANES * N_EXPERTS, LANES), F32),
                        pltpu.VMEM((n_tok // LANES * N_EXPERTS, LANES), F32)],
        compiler_params=pltpu.CompilerParams(vmem_limit_bytes=VMEM_LIMIT),
        name="route_topk",
    )(probs)


def _expert_kernel(cap, row_ref, h2t_ref, gate_ref, w1_ref, w3_ref, w2_ref, ye_ref,
                   xe_ref, x2_ref, hid_ref, sem):
    e = pl.program_id(0)
    f = pl.program_id(1)
    slot = e % 2
    rows_per_step = cap // N_HID_STEPS

    def start_row(expert, buf, j):
        src = pl.multiple_of(row_ref[expert * cap + j], TILES_PER_TOKEN)
        dst = pl.multiple_of(j * TILES_PER_TOKEN, TILES_PER_TOKEN)
        pltpu.make_async_copy(h2t_ref.at[pl.ds(src, TILES_PER_TOKEN), :],
                              xe_ref.at[buf, pl.ds(dst, TILES_PER_TOKEN), :], sem.at[buf]).start()

    def wait_rows(buf):
        pltpu.make_async_copy(h2t_ref.at[pl.ds(0, cap * TILES_PER_TOKEN), :], xe_ref.at[buf],
                              sem.at[buf]).wait()

    def prefetch_next():
        nxt = jnp.minimum(e + 1, N_EXPERTS - 1)
        first = f * rows_per_step
        for j in range(rows_per_step):
            start_row(nxt, 1 - slot, first + j)

    @pl.when((e == 0) & (f == 0))
    def _():
        def issue(j, _):
            start_row(0, 0, j)
            return 0

        lax.fori_loop(0, cap, issue, 0, unroll=8)

    @pl.when(f == 0)
    def _():
        wait_rows(slot)
        for s in range(TILES_PER_TOKEN):
            x2_ref[:, s * LANES:(s + 1) * LANES] = (
                xe_ref[slot, pl.ds(s, cap, stride=TILES_PER_TOKEN), :].astype(BF16))

    @pl.when(f < N_HID_STEPS)
    def _():
        prefetch_next()
        x2 = x2_ref[...]
        a = jnp.dot(x2, w1_ref[0].astype(BF16), preferred_element_type=F32)
        b = jnp.dot(x2, w3_ref[0].astype(BF16), preferred_element_type=F32)
        hid_ref[f] = (jax.nn.silu(a) * b).astype(BF16)

    @pl.when(f >= N_HID_STEPS)
    def _():
        w2 = w2_ref[0].astype(BF16)
        out = jnp.dot(hid_ref[0], w2[0:EXPERT_F_BLOCK], preferred_element_type=F32)
        for kb in range(1, N_HID_STEPS):
            out += jnp.dot(hid_ref[kb], w2[kb * EXPERT_F_BLOCK:(kb + 1) * EXPERT_F_BLOCK],
                           preferred_element_type=F32)
        tile0 = (f - N_HID_STEPS) * (EXPERT_N_BLOCK // LANES)
        gate = gate_ref[0]
        for i in range(EXPERT_N_BLOCK // LANES):
            ye_ref[pl.ds(tile0 + i, cap, stride=TILES_PER_TOKEN), :] = out[:, i * LANES:(i + 1) * LANES] * gate

    @pl.when((e == N_EXPERTS - 1) & (f == EXPERT_STEPS - 1))
    def _():
        wait_rows(1 - slot)


def _experts(rows, h2t, gates, w1, w3, w2, cap):
    hid_blk = lambda e, f, idx: (e, 0, jnp.minimum(f, N_HID_STEPS - 1))
    n_out = D_MODEL // EXPERT_N_BLOCK

    def out_blk(e, f, idx):
        hold = (f == 0) & (e > 0)
        return (jnp.where(hold, e - 1, e), 0,
                jnp.where(hold, n_out - 1, jnp.maximum(f - N_HID_STEPS, 0)))

    grid_spec = pltpu.PrefetchScalarGridSpec(
        num_scalar_prefetch=1,
        grid=(N_EXPERTS, EXPERT_STEPS),
        in_specs=[pl.BlockSpec(memory_space=pl.ANY),
                  pl.BlockSpec((1, cap, LANES), lambda e, f, idx: (e, 0, 0)),
                  pl.BlockSpec((1, D_MODEL, EXPERT_F_BLOCK), hid_blk),
                  pl.BlockSpec((1, D_MODEL, EXPERT_F_BLOCK), hid_blk),
                  pl.BlockSpec((1, D_EXPERT, EXPERT_N_BLOCK), out_blk)],
        out_specs=pl.BlockSpec((cap * TILES_PER_TOKEN, LANES), lambda e, f, idx: (e, 0)),
        scratch_shapes=[pltpu.VMEM((2, cap * TILES_PER_TOKEN, LANES), F32),
                        pltpu.VMEM((cap, D_MODEL), BF16),
                        pltpu.VMEM((N_HID_STEPS, cap, EXPERT_F_BLOCK), BF16),
                        pltpu.SemaphoreType.DMA((2,))],
    )
    return pl.pallas_call(
        functools.partial(_expert_kernel, cap),
        grid_spec=grid_spec,
        out_shape=jax.ShapeDtypeStruct((N_EXPERTS * cap * TILES_PER_TOKEN, LANES), F32),
        compiler_params=_compiler_params(("arbitrary", "arbitrary")),
        name="expert_swiglu",
    )(rows, h2t, gates, w1, w3, w2)


COMBINE_BATCH = 8
ZERO_ROWS = 512


def _combine_kernel(cap, n_tok, mod_base, mod_per_seq, nb,
                    row_ref, ye_ref, x1_ref, mod_ref, fw_ref, y_ref,
                    acc_ref, xbuf_ref, ybuf_ref, sem_x, sem_y):
    e = pl.program_id(0)

    @pl.when(e == 0)
    def _():
        def zero(i, _):
            r = pl.multiple_of(i * ZERO_ROWS, ZERO_ROWS)
            acc_ref[pl.ds(r, ZERO_ROWS), :] = jnp.zeros((ZERO_ROWS, LANES), F32)
            return 0

        lax.fori_loop(0, n_tok * TILES_PER_TOKEN // ZERO_ROWS, zero, 0)

    def batch(jb, _):
        vals = []
        for u in range(COMBINE_BATCH):
            j = jb * COMBINE_BATCH + u
            t = pl.multiple_of(row_ref[e * cap + j], TILES_PER_TOKEN)
            src = pl.multiple_of(j * TILES_PER_TOKEN, TILES_PER_TOKEN)
            vals.append((t, acc_ref[pl.ds(t, TILES_PER_TOKEN), :] + ye_ref[pl.ds(src, TILES_PER_TOKEN), :]))
        for t, val in vals:
            acc_ref[pl.ds(t, TILES_PER_TOKEN), :] = val
        return 0

    lax.fori_loop(0, cap // COMBINE_BATCH, batch, 0)

    @pl.when(e == N_EXPERTS - 1)
    def _():
        n_blk = n_tok // TOKEN_BLOCK

        def x1_copy(b, slot):
            rows = pl.ds(pl.multiple_of(b * TOKEN_BLOCK, TOKEN_BLOCK), TOKEN_BLOCK)
            return pltpu.make_async_copy(x1_ref.at[rows, :], xbuf_ref.at[slot], sem_x.at[slot])

        def y_copy(b, slot):
            rows = pl.ds(pl.multiple_of(b * TOKEN_BLOCK, TOKEN_BLOCK), TOKEN_BLOCK)
            return pltpu.make_async_copy(ybuf_ref.at[slot], y_ref.at[rows, :], sem_y.at[slot])

        x1_copy(0, 0).start()

        def block(b, _):
            slot = b % 2

            @pl.when(b + 1 < n_blk)
            def _():
                x1_copy(b + 1, 1 - slot).start()

            x1_copy(b, slot).wait()

            @pl.when(b >= 2)
            def _():
                y_copy(b - 2, slot).wait()

            base = pl.multiple_of(b * (TOKEN_BLOCK * TILES_PER_TOKEN), TOKEN_BLOCK * TILES_PER_TOKEN)
            moe = jnp.concatenate(
                [acc_ref[pl.ds(base + s, TOKEN_BLOCK, stride=TILES_PER_TOKEN), :]
                 for s in range(TILES_PER_TOKEN)], axis=1)
            row = mod_base + ((b // nb) if mod_per_seq else 0)
            ybuf_ref[slot] = _rms(xbuf_ref[slot] + _mod_row(mod_ref, row, 5) * moe, fw_ref[...])
            y_copy(b, slot).start()
            return 0

        lax.fori_loop(0, n_blk, block, 0)
        y_copy(n_blk - 2, n_blk % 2).wait()
        y_copy(n_blk - 1, (n_blk - 1) % 2).wait()


def _combine_final(rows, ye, x1, mod, fw, cap, n_tok, *, nb, mod_base, mod_per_seq):
    full = lambda a: pl.BlockSpec(a.shape, lambda e, rows: (0,) * a.ndim)
    grid_spec = pltpu.PrefetchScalarGridSpec(
        num_scalar_prefetch=1,
        grid=(N_EXPERTS,),
        in_specs=[pl.BlockSpec((cap * TILES_PER_TOKEN, LANES), lambda e, rows: (e, 0)),
                  pl.BlockSpec(memory_space=pl.ANY), full(mod), full(fw)],
        out_specs=pl.BlockSpec(memory_space=pl.ANY),
        scratch_shapes=[pltpu.VMEM((n_tok * TILES_PER_TOKEN, LANES), F32),
                        pltpu.VMEM((2, TOKEN_BLOCK, D_MODEL), F32),
                        pltpu.VMEM((2, TOKEN_BLOCK, D_MODEL), F32),
                        pltpu.SemaphoreType.DMA((2,)),
                        pltpu.SemaphoreType.DMA((2,))],
    )
    return pl.pallas_call(
        functools.partial(_combine_kernel, cap, n_tok, mod_base, mod_per_seq, nb),
        grid_spec=grid_spec,
        out_shape=jax.ShapeDtypeStruct((n_tok, D_MODEL), F32),
        compiler_params=_compiler_params(("arbitrary",)),
        name="moe_combine_norm",
    )(rows, ye, x1, mod, fw)


def _trunk_and_norm(x, e_tab, mod, s0_f, s0_b, prm, *, nseq, seq_len, add_pos, mod_base, mod_per_seq):
    nb = seq_len // TOKEN_BLOCK
    n_tok = nseq * seq_len
    cap = EC_CAPACITY_FACTOR * n_tok // N_EXPERTS
    kw = dict(nseq=nseq, nb=nb, add_pos=add_pos, mod_base=mod_base, mod_per_seq=mod_per_seq)
    if nb == 1 and not add_pos:
        x1, h2t, probs, sfin_f, sfin_b = _mixer_fused(
            x, mod, prm, s0_f, s0_b, nseq=nseq, mod_base=mod_base, mod_per_seq=mod_per_seq)
    else:
        qkv, g, lab, s, of, sfin_f = _mixer_fwd(
            x, e_tab, mod, prm["n1"], prm["win"], prm["wa"], prm["ba"], prm["snw"], prm["sws"],
            prm["sbs"], s0_f, **kw)
        x1, h2t, probs, sfin_b = _mixer_bwd(
            x, e_tab, mod, qkv, g, lab, s, of, prm["gnw"], prm["wout"], prm["n2"], prm["rw_t"],
            s0_b, **kw)
    x1 = x1.reshape(n_tok, D_MODEL)
    h2t = h2t.reshape(n_tok * TILES_PER_TOKEN, LANES)
    probs = probs.reshape(n_tok // TOKEN_BLOCK, N_EXPERTS, TOKEN_BLOCK)
    rows, gates = _route(probs, n_tok, cap)
    rows = rows.reshape(-1)
    ye = _experts(rows, h2t, gates, prm["w1"], prm["w3"], prm["w2"], cap)
    y = _combine_final(rows, ye, x1, mod, prm["fw"], cap, n_tok,
                       nb=nb, mod_base=mod_base, mod_per_seq=mod_per_seq)
    return y.reshape(nseq, seq_len, D_MODEL), sfin_f, sfin_b


def kernel(x_prompt, x_sample, state_gla_fwd, state_gla_bwd, c, c_ctx, ada_w, ada_b, norm1_w, w_in, gla_wa2_f, gla_ba_f, gla_wa2_b, gla_ba_b, gla_norm_w, sgu_norm_w, sgu_ws, sgu_bs, w_out, norm2_w, router_w, exp_w1, exp_w3, exp_w2, final_norm_w):
    assert ada_w.shape[0] == 1, "single trunk layer"
    batch, seq, _ = x_prompt.shape
    dec_batch, dec_seq, _ = x_sample.shape

    assert w_in.shape[2] == D_IN
    win = _win_transposed_bf16(w_in[0].T)
    wa = jnp.zeros((2 * GLA_LOWRANK, 2 * QK_W), F32)
    wa = wa.at[0:GLA_LOWRANK, 0:QK_W].set(gla_wa2_f[0])
    wa = wa.at[GLA_LOWRANK:2 * GLA_LOWRANK, QK_W:].set(gla_wa2_b[0]).astype(BF16)
    prm = dict(
        n1=norm1_w, win=win, wa=wa,
        ba=jnp.concatenate([gla_ba_f[0], gla_ba_b[0]])[None, :],
        snw=sgu_norm_w, sws=sgu_ws[0].astype(BF16),
        sbs=jnp.broadcast_to(sgu_bs[0][:, :, None], (SGU_GROUPS, SGU_CHUNK, SGU_CH)),
        gnw=gla_norm_w, wout=w_out[0].astype(BF16), n2=norm2_w, rw_t=router_w[0].T,
        w1=exp_w1[0], w3=exp_w3[0], w2=exp_w2[0], fw=final_norm_w[None, :])

    cvec = jnp.concatenate([c_ctx[None, :], c, jnp.zeros((SUBLANES - 1 - dec_batch, D_MODEL), F32)])
    mod = _modulation(cvec, ada_w[0], ada_b)
    e_tab = _pos_table()

    zero_state = jnp.zeros((min(SEQ_GROUP, batch), GLA_HEADS, GLA_DK, GLA_DV), F32)
    y_prompt, sf, sb = _trunk_and_norm(
        x_prompt, e_tab, mod, zero_state, zero_state, prm,
        nseq=batch, seq_len=seq, add_pos=False, mod_base=0, mod_per_seq=False)
    y_sample, _, _ = _trunk_and_norm(
        x_sample, e_tab, mod, state_gla_fwd[:, 0], state_gla_bwd[:, 0], prm,
        nseq=dec_batch, seq_len=dec_seq, add_pos=True, mod_base=1, mod_per_seq=True)
    return (y_prompt, y_sample, sf[:, None], sb[:, None])
```

```python
import functools
import math

import jax
import jax.numpy as jnp
from jax import lax
from jax.experimental import pallas as pl
from jax.experimental.pallas import tpu as pltpu

F32 = jnp.float32
BF16 = jnp.bfloat16
I32 = jnp.int32

D_MODEL = 1024
GRID_W = 64
GLA_HEADS = 4
GLA_DK = 64
GLA_DV = 128
GLA_WIDTH = GLA_HEADS * GLA_DV
QK_W = GLA_HEADS * GLA_DK
GLA_LOWRANK = 16
GLA_GATE_NORM = 16.0
GLA_CHUNK = 64
SGU_WIDTH = 512
SGU_GROUPS = 4
SGU_CH = 128
SGU_CHUNK = 128
N_EXPERTS = 16
EC_CAPACITY_FACTOR = 2
D_EXPERT = 2048
EPS = 1e-6

SUBLANES = 8
LANES = 128
TILES_PER_TOKEN = D_MODEL // LANES

TOKEN_BLOCK = 256
SEQ_GROUP = 4
OFF_A = 2 * QK_W + 2 * GLA_WIDTH
OFF_U = OFF_A + 2 * GLA_LOWRANK
D_IN = OFF_U + 2 * SGU_WIDTH
GATHER_PITCH = 12
EXPERT_F_BLOCK = 512
EXPERT_N_BLOCK = 512
N_HID_STEPS = D_EXPERT // EXPERT_F_BLOCK
EXPERT_STEPS = N_HID_STEPS + D_MODEL // EXPERT_N_BLOCK
VMEM_LIMIT = 56 * 1024 * 1024
F32_ONE_BITS = 0x3F800000
THRESHOLD_STEPS = F32_ONE_BITS.bit_length() + 1


def _dot(a, b):
    return jnp.dot(a.astype(BF16), b.astype(BF16), preferred_element_type=F32)


def _dot_nt(a, b):
    return lax.dot_general(a.astype(BF16), b.astype(BF16), (((1,), (1,)), ((), ())),
                           preferred_element_type=F32)


def _dot_tn(a, b):
    return lax.dot_general(a.astype(BF16), b.astype(BF16), (((0,), (0,)), ((), ())),
                           preferred_element_type=F32)


def _dot_f32(a, b, dims=(((1,), (0,)), ((), ()))):
    return lax.dot_general(a, b, dims, precision=lax.Precision.HIGHEST, preferred_element_type=F32)


def _split_bf16(x, terms):
    parts = []
    for _ in range(terms - 1):
        part = x.astype(BF16)
        parts.append(part)
        x = x - part.astype(F32)
    parts.append(x.astype(BF16))
    return parts


def _select_dot(sel, x):
    s = sel.astype(BF16)
    hi, mid, lo = _split_bf16(x, 3)
    return (jnp.dot(s, lo, preferred_element_type=F32) + jnp.dot(s, mid, preferred_element_type=F32)
            + jnp.dot(s, hi, preferred_element_type=F32))


def _dot_nt_3pass(a, b):
    a_hi, a_lo = _split_bf16(a, 2)
    b_hi, b_lo = _split_bf16(b, 2)
    nt = lambda x, y: lax.dot_general(x, y, (((1,), (1,)), ((), ())), preferred_element_type=F32)
    return (nt(a_hi, b_lo) + nt(a_lo, b_hi)) + nt(a_hi, b_hi)


def _rms(x, w):
    return x * lax.rsqrt(jnp.mean(x * x, axis=-1, keepdims=True) + EPS) * w


def _compiler_params(sem):
    return pltpu.CompilerParams(dimension_semantics=sem, vmem_limit_bytes=VMEM_LIMIT)


def _mod_kernel(c_ref, w_ref, b_ref, o_ref):
    o_ref[...] = _dot(jax.nn.silu(c_ref[...]), w_ref[...]) + b_ref[...]


def _modulation(cvec, ada_w, ada_b):
    n = ada_w.shape[1]
    bn = 1536
    return pl.pallas_call(
        _mod_kernel,
        grid=(n // bn,),
        in_specs=[pl.BlockSpec((SUBLANES, D_MODEL), lambda j: (0, 0)),
                  pl.BlockSpec((D_MODEL, bn), lambda j: (0, j)),
                  pl.BlockSpec((1, bn), lambda j: (0, j))],
        out_specs=pl.BlockSpec((SUBLANES, bn), lambda j: (0, j)),
        out_shape=jax.ShapeDtypeStruct((SUBLANES, n), F32),
        compiler_params=_compiler_params(("arbitrary",)),
        name="adaln_mod",
    )(cvec, ada_w, ada_b)


def _cast_kernel(w_ref, o_ref):
    o_ref[...] = w_ref[...].astype(BF16)


def _win_transposed_bf16(w_t):
    rows = D_IN // 3
    return pl.pallas_call(
        _cast_kernel,
        grid=(D_IN // rows,),
        in_specs=[pl.BlockSpec((rows, D_MODEL), lambda i: (i, 0))],
        out_specs=pl.BlockSpec((rows, D_MODEL), lambda i: (i, 0)),
        out_shape=jax.ShapeDtypeStruct((D_IN, D_MODEL), BF16),
        compiler_params=_compiler_params(("arbitrary",)),
        name="win_cast",
    )(w_t)


def _pos_kernel(o_ref):
    nf = D_MODEL // 4
    p = lax.broadcasted_iota(I32, (GRID_W, nf), 0).astype(F32)
    i = lax.broadcasted_iota(I32, (GRID_W, nf), 1).astype(F32)
    omega = jnp.exp(i * (-math.log(10000.0) / nf))
    a = p * omega
    o_ref[:, 0:nf] = jnp.sin(a)
    o_ref[:, nf:2 * nf] = jnp.cos(a)


def _pos_table():
    return pl.pallas_call(
        _pos_kernel,
        out_shape=jax.ShapeDtypeStruct((GRID_W, D_MODEL // 2), F32),
        name="sincos_table",
    )()


def _add_pos(x, e_ref, blk, add_pos):
    if not add_pos:
        return x
    half = D_MODEL // 2
    e_all = e_ref[...]
    rows = []
    for j in range(TOKEN_BLOCK // GRID_W):
        xj = x[j * GRID_W:(j + 1) * GRID_W]
        e_row = e_ref[pl.ds(blk * (TOKEN_BLOCK // GRID_W) + j, 1), :]
        rows.append(jnp.concatenate([xj[:, 0:half] + e_row, xj[:, half:] + e_all], axis=1))
    return jnp.concatenate(rows, axis=0)


def _chunk_masks():
    r = lax.broadcasted_iota(I32, (TOKEN_BLOCK, TOKEN_BLOCK), 0)
    c = lax.broadcasted_iota(I32, (TOKEN_BLOCK, TOKEN_BLOCK), 1)
    same = (r // GLA_CHUNK) == (c // GLA_CHUNK)
    return same & (c <= r), same & (c >= r)


def _gla_direction(q, k, v, cum, fwd, att_mask, st_ref, ready, done):
    qe = q * jnp.exp(cum)
    ke = k * jnp.exp(-cum)
    yield
    lane = lax.broadcasted_iota(I32, (1, LANES), 1)
    o_intra = []
    for pair in range(2):
        qp = qe[:, pair * LANES:(pair + 1) * LANES]
        kp = ke[:, pair * LANES:(pair + 1) * LANES]
        for hh in range(2):
            qm = jnp.where((lane // GLA_DK) == hh, qp, 0.0)
            att = jnp.where(att_mask, _dot_nt(qm, kp), 0.0)
            head = 2 * pair + hh
            o_intra.append(_dot(att, v[:, head * GLA_DV:(head + 1) * GLA_DV]))
            yield
    o_intra = jnp.concatenate(o_intra, axis=1)
    while not ready():
        yield

    er = lax.broadcasted_iota(I32, (2 * GLA_DV, 2 * GLA_DK), 0)
    dc = lax.broadcasted_iota(I32, (2 * GLA_DV, 2 * GLA_DK), 1)
    same_head = (er // GLA_DV) == (dc // GLA_DK)
    n_chunks = TOKEN_BLOCK // GLA_CHUNK
    o_inter = [None] * n_chunks
    for c in (range(n_chunks) if fwd else reversed(range(n_chunks))):
        r0 = c * GLA_CHUNK
        rows = slice(r0, r0 + GLA_CHUNK)
        last = cum[r0 + GLA_CHUNK - 1:r0 + GLA_CHUNK] if fwd else cum[r0:r0 + 1]
        kd = k[rows] * jnp.exp(last - cum[rows])
        dec = jnp.exp(last)
        parts = []
        for pair in range(2):
            dl = slice(pair * LANES, (pair + 1) * LANES)
            st = st_ref[pair]
            parts.append(_dot_nt(qe[rows, dl], st))
            ds_t = _dot_tn(v[rows, pair * 2 * GLA_DV:(pair + 1) * 2 * GLA_DV], kd[:, dl])
            st_ref[pair] = dec[:, dl] * st + jnp.where(same_head, ds_t, 0.0)
        o_inter[c] = jnp.concatenate(parts, axis=1)
        yield
    done()
    return o_intra + jnp.concatenate(o_inter, axis=0)


def _interleave(chains):
    chains = list(chains)
    done = [False] * len(chains)
    tick = 0
    while not all(done):
        for i, ch in enumerate(chains):
            if tick >= i and not done[i]:
                try:
                    next(ch)
                except StopIteration:
                    done[i] = True
        tick += 1


def _load_state(s0_ref, u, st_ref):
    zero = jnp.zeros((GLA_DV, GLA_DK), F32)
    for pair in range(2):
        a = s0_ref[u, 2 * pair].T
        b = s0_ref[u, 2 * pair + 1].T
        st_ref[pair] = jnp.concatenate(
            [jnp.concatenate([a, zero], axis=1), jnp.concatenate([zero, b], axis=1)], axis=0)


def _store_state(st_ref, sfin_ref, u):
    for pair in range(2):
        st = st_ref[pair]
        sfin_ref[u, 2 * pair] = st[0:GLA_DV, 0:GLA_DK].T
        sfin_ref[u, 2 * pair + 1] = st[GLA_DV:2 * GLA_DV, GLA_DK:2 * GLA_DK].T


def _mod_row(mod_ref, row, part):
    return mod_ref[pl.ds(row, 1), part * D_MODEL:(part + 1) * D_MODEL]


def _front_stages(xin, row, mod_ref, n1_ref, win_ref, wa_ref, ba_ref, snw_ref, sws_ref, sbs_ref):
    h = _rms(xin, n1_ref[...]) * (1.0 + _mod_row(mod_ref, row, 1)) + _mod_row(mod_ref, row, 0)
    yield
    hb = h.astype(BF16)
    parts = []
    for r0, r1 in ((0, OFF_A), (OFF_U, D_IN), (OFF_A, OFF_U)):
        parts.append(lax.dot_general(hb, win_ref[r0:r1, :], (((1,), (1,)), ((), ())),
                                     preferred_element_type=F32))
        yield
    main, gate, low = parts
    q = main[:, 0:QK_W] * (GLA_DK ** -0.5)
    k = main[:, QK_W:2 * QK_W]
    v = main[:, 2 * QK_W:2 * QK_W + GLA_WIDTH]
    z = _dot(low, wa_ref[...]) + ba_ref[...]
    la = (jnp.minimum(z, 0.0) - jnp.log1p(jnp.exp(-jnp.abs(z)))) * (1.0 / GLA_GATE_NORM)
    yield

    ug = jax.nn.gelu(gate[:, 0:SGU_WIDTH])
    yield
    vg = jax.nn.gelu(gate[:, SGU_WIDTH:2 * SGU_WIDTH])
    yield
    s_cols = []
    for gi in range(SGU_GROUPS):
        cols = slice(gi * SGU_CH, (gi + 1) * SGU_CH)
        vn = _rms(vg[:, cols], snw_ref[:, cols])
        rhs = jnp.concatenate([vn[0:SGU_CHUNK], vn[SGU_CHUNK:2 * SGU_CHUNK]], axis=1)
        vm = _dot(sws_ref[gi], rhs) + jnp.concatenate([sbs_ref[gi], sbs_ref[gi]], axis=1)
        vm = jnp.concatenate([vm[:, 0:SGU_CH], vm[:, SGU_CH:2 * SGU_CH]], axis=0)
        s_cols.append(ug[:, cols] * vm)
    yield
    return q, k, v, main[:, 2 * QK_W + GLA_WIDTH:OFF_A], la, jnp.concatenate(s_cols, axis=1)


def _back_stages(o, g, s_val, xin, row, mod_ref, gnw_ref, wout_ref, n2_ref, rw_ref):
    cols = []
    for head in range(GLA_HEADS):
        hs = slice(head * GLA_DV, (head + 1) * GLA_DV)
        cols.append(_rms(o[:, hs], gnw_ref[...]) * jax.nn.silu(g[:, hs]))
    cols.append(s_val)
    yield
    y = _dot(jnp.concatenate(cols, axis=1), wout_ref[...])
    yield
    x1 = xin + _mod_row(mod_ref, row, 2) * y
    h2 = _rms(x1, n2_ref[...]) * (1.0 + _mod_row(mod_ref, row, 4)) + _mod_row(mod_ref, row, 3)
    yield
    logits = _dot_nt_3pass(rw_ref[...], h2)
    m = jnp.max(logits, axis=0, keepdims=True)
    ex = jnp.exp(logits - m)
    return x1, h2, ex / jnp.sum(ex, axis=0, keepdims=True)


def _store_token_tiles(h2t_ref, u, b, h2):
    for s in range(TILES_PER_TOKEN):
        h2t_ref[u, pl.ds(b * TOKEN_BLOCK * TILES_PER_TOKEN + s, TOKEN_BLOCK, stride=TILES_PER_TOKEN), :] = (
            h2[:, s * LANES:(s + 1) * LANES])


def _zip_stages(gen_a, gen_b):
    out = [None, None]
    live = [gen_a, gen_b]
    while any(g is not None for g in live):
        for i, g in enumerate(live):
            if g is not None:
                try:
                    next(g)
                except StopIteration as stop:
                    out[i] = stop.value
                    live[i] = None
        yield
    return out


def _mixer_fwd_kernel(group, add_pos, mod_base, mod_per_seq, nb,
                      x_ref, e_ref, mod_ref, n1_ref, win_ref, wa_ref, ba_ref,
                      snw_ref, sws_ref, sbs_ref, s0_ref,
                      qkv_ref, g_ref, lab_ref, s_ref, of_ref, sfin_ref,
                      st_ref):
    grp = pl.program_id(0)
    blk = pl.program_id(1)
    sub = x_ref.shape[1] // TOKEN_BLOCK
    lo_mask, _ = _chunk_masks()
    state_done = set()

    @pl.when(blk == 0)
    def _():
        for u in range(group):
            _load_state(s0_ref, u, st_ref.at[u])

    def chain(u, b):
        row = mod_base + ((grp * group + u) if mod_per_seq else 0)
        st_u = st_ref.at[u]
        rows = slice(b * TOKEN_BLOCK, (b + 1) * TOKEN_BLOCK)
        xin = _add_pos(x_ref[u, rows, :], e_ref, blk * sub + b, add_pos)
        q, k, v, g, la, s_val = yield from _front_stages(
            xin, row, mod_ref, n1_ref, win_ref, wa_ref, ba_ref, snw_ref, sws_ref, sbs_ref)
        qkv_ref[u, rows, :] = jnp.concatenate([q, k, v], axis=1)
        g_ref[u, rows, :] = g
        lab_ref[u, rows, :] = la[:, QK_W:2 * QK_W]
        s_ref[u, rows, :] = s_val

        cum = _select_dot(lo_mask, la[:, 0:QK_W])
        yield
        of_ref[u, rows, :] = yield from _gla_direction(
            q, k, v, cum, True, lo_mask, st_u,
            ready=lambda: b == 0 or (u, b - 1) in state_done, done=lambda: state_done.add((u, b)))

    _interleave(chain(u, b) for b in range(sub) for u in range(group))

    @pl.when(blk == nb - 1)
    def _():
        for u in range(group):
            _store_state(st_ref.at[u], sfin_ref, u)


def _mixer_fwd(x, e_tab, mod, n1, win, wa, ba, snw, sws, sbs, s0, *, nseq, nb, add_pos,
               mod_base, mod_per_seq):
    seq_len = nb * TOKEN_BLOCK
    group = min(SEQ_GROUP, nseq)
    sub = min(SEQ_GROUP // group, nb)
    nb //= sub
    tok = lambda w: pl.BlockSpec((group, sub * TOKEN_BLOCK, w), lambda s, i: (s, i, 0))
    full = lambda a: pl.BlockSpec(a.shape, lambda s, i: (0,) * a.ndim)
    st_spec = pl.BlockSpec((group, GLA_HEADS, GLA_DK, GLA_DV), lambda s, i: (s, 0, 0, 0))
    s0_spec = st_spec if s0.shape[0] == nseq else pl.BlockSpec(s0.shape, lambda s, i: (0, 0, 0, 0))
    act = lambda w: jax.ShapeDtypeStruct((nseq, seq_len, w), F32)
    kern = functools.partial(_mixer_fwd_kernel, group, add_pos, mod_base, mod_per_seq, nb)
    return pl.pallas_call(
        kern,
        grid=(nseq // group, nb),
        in_specs=[tok(D_MODEL), full(e_tab), full(mod), full(n1), full(win), full(wa), full(ba),
                  full(snw), full(sws), full(sbs), s0_spec],
        out_specs=[tok(1024), tok(GLA_WIDTH), tok(QK_W), tok(SGU_WIDTH), tok(GLA_WIDTH), st_spec],
        out_shape=[act(1024), act(GLA_WIDTH), act(QK_W), act(SGU_WIDTH), act(GLA_WIDTH),
                   jax.ShapeDtypeStruct((nseq, GLA_HEADS, GLA_DK, GLA_DV), F32)],
        scratch_shapes=[pltpu.VMEM((group, 2, 2 * GLA_DV, 2 * GLA_DK), F32)],
        compiler_params=_compiler_params(("arbitrary", "arbitrary")),
        name="mixer_fwd",
    )(x, e_tab, mod, n1, win, wa, ba, snw, sws, sbs, s0)


def _mixer_bwd_kernel(group, add_pos, mod_base, mod_per_seq, nb,
                      x_ref, e_ref, mod_ref, qkv_ref, g_ref, lab_ref, s_ref, of_ref,
                      gnw_ref, wout_ref, n2_ref, rw_ref, s0_ref,
                      x1_ref, h2t_ref, probs_ref, sfin_ref,
                      st_ref):
    grp = pl.program_id(0)
    step = pl.program_id(1)
    sub = x_ref.shape[1] // TOKEN_BLOCK
    blk = nb - 1 - step
    _, hi_mask = _chunk_masks()
    state_done = set()

    @pl.when(step == 0)
    def _():
        for u in range(group):
            _load_state(s0_ref, u, st_ref.at[u])

    def chain(u, b):
        row = mod_base + ((grp * group + u) if mod_per_seq else 0)
        st_u = st_ref.at[u]
        rows = slice(b * TOKEN_BLOCK, (b + 1) * TOKEN_BLOCK)
        qkv = qkv_ref[u, rows, :]
        q, k, v = qkv[:, 0:QK_W], qkv[:, QK_W:2 * QK_W], qkv[:, 2 * QK_W:]
        cum = _select_dot(hi_mask, lab_ref[u, rows, :])
        yield
        o_b = yield from _gla_direction(
            q, k, v, cum, False, hi_mask, st_u,
            ready=lambda: b == sub - 1 or (u, b + 1) in state_done,
            done=lambda: state_done.add((u, b)))
        xin = _add_pos(x_ref[u, rows, :], e_ref, blk * sub + b, add_pos)
        x1, h2, probs = yield from _back_stages(
            of_ref[u, rows, :] + o_b, g_ref[u, rows, :], s_ref[u, rows, :], xin, row,
            mod_ref, gnw_ref, wout_ref, n2_ref, rw_ref)
        x1_ref[u, rows, :] = x1
        _store_token_tiles(h2t_ref, u, b, h2)
        probs_ref[u, b] = probs

    _interleave(chain(u, b) for b in reversed(range(sub)) for u in range(group))

    @pl.when(step == nb - 1)
    def _():
        for u in range(group):
            _store_state(st_ref.at[u], sfin_ref, u)


def _mixer_bwd(x, e_tab, mod, qkv, g, lab, s, of, gnw, wout, n2, rw_t, s0, *, nseq, nb, add_pos,
               mod_base, mod_per_seq):
    seq_len = nb * TOKEN_BLOCK
    group = min(SEQ_GROUP, nseq)
    sub = min(SEQ_GROUP // group, nb)
    nb //= sub
    tok = lambda w: pl.BlockSpec((group, sub * TOKEN_BLOCK, w), lambda s_, i: (s_, nb - 1 - i, 0))
    full = lambda a: pl.BlockSpec(a.shape, lambda s_, i: (0,) * a.ndim)
    st_spec = pl.BlockSpec((group, GLA_HEADS, GLA_DK, GLA_DV), lambda s_, i: (s_, 0, 0, 0))
    s0_spec = st_spec if s0.shape[0] == nseq else pl.BlockSpec(s0.shape, lambda s_, i: (0, 0, 0, 0))
    kern = functools.partial(_mixer_bwd_kernel, group, add_pos, mod_base, mod_per_seq, nb)
    return pl.pallas_call(
        kern,
        grid=(nseq // group, nb),
        in_specs=[tok(D_MODEL), full(e_tab), full(mod), tok(1024), tok(GLA_WIDTH), tok(QK_W),
                  tok(SGU_WIDTH), tok(GLA_WIDTH), full(gnw), full(wout), full(n2), full(rw_t),
                  s0_spec],
        out_specs=[tok(D_MODEL),
                   pl.BlockSpec((group, sub * TOKEN_BLOCK * TILES_PER_TOKEN, LANES),
                                lambda s_, i: (s_, nb - 1 - i, 0)),
                   pl.BlockSpec((group, sub, N_EXPERTS, TOKEN_BLOCK),
                                lambda s_, i: (s_, nb - 1 - i, 0, 0)),
                   st_spec],
        out_shape=[jax.ShapeDtypeStruct((nseq, seq_len, D_MODEL), F32),
                   jax.ShapeDtypeStruct((nseq, seq_len * TILES_PER_TOKEN, LANES), F32),
                   jax.ShapeDtypeStruct((nseq, nb * sub, N_EXPERTS, TOKEN_BLOCK), F32),
                   jax.ShapeDtypeStruct((nseq, GLA_HEADS, GLA_DK, GLA_DV), F32)],
        scratch_shapes=[pltpu.VMEM((group, 2, 2 * GLA_DV, 2 * GLA_DK), F32)],
        compiler_params=_compiler_params(("arbitrary", "arbitrary")),
        name="mixer_bwd",
    )(x, e_tab, mod, qkv, g, lab, s, of, gnw, wout, n2, rw_t, s0)


def _mixer_fused_kernel(group, mod_base, mod_per_seq,
                        x_ref, mod_ref, n1_ref, win_ref, wa_ref, ba_ref, snw_ref, sws_ref, sbs_ref,
                        gnw_ref, wout_ref, n2_ref, rw_ref, s0f_ref, s0b_ref,
                        x1_ref, h2t_ref, probs_ref, sfin_f_ref, sfin_b_ref,
                        stf_ref, stb_ref):
    grp = pl.program_id(0)
    lo_mask, hi_mask = _chunk_masks()
    always = lambda: True
    nothing = lambda: None
    for u in range(group):
        _load_state(s0f_ref, u, stf_ref.at[u])
        _load_state(s0b_ref, u, stb_ref.at[u])

    def chain(u):
        row = mod_base + ((grp * group + u) if mod_per_seq else 0)
        xin = x_ref[u]
        q, k, v, g, la, s_val = yield from _front_stages(
            xin, row, mod_ref, n1_ref, win_ref, wa_ref, ba_ref, snw_ref, sws_ref, sbs_ref)
        cum_f = _select_dot(lo_mask, la[:, 0:QK_W])
        cum_b = _select_dot(hi_mask, la[:, QK_W:2 * QK_W])
        yield
        o_f, o_b = yield from _zip_stages(
            _gla_direction(q, k, v, cum_f, True, lo_mask, stf_ref.at[u], always, nothing),
            _gla_direction(q, k, v, cum_b, False, hi_mask, stb_ref.at[u], always, nothing))
        x1, h2, probs = yield from _back_stages(
            o_f + o_b, g, s_val, xin, row, mod_ref, gnw_ref, wout_ref, n2_ref, rw_ref)
        x1_ref[u] = x1
        _store_token_tiles(h2t_ref, u, 0, h2)
        probs_ref[u, 0] = probs

    _interleave(chain(u) for u in range(group))
    for u in range(group):
        _store_state(stf_ref.at[u], sfin_f_ref, u)
        _store_state(stb_ref.at[u], sfin_b_ref, u)


def _mixer_fused(x, mod, prm, s0_f, s0_b, *, nseq, mod_base, mod_per_seq):
    group = min(SEQ_GROUP, nseq)
    consts = [mod, prm["n1"], prm["win"], prm["wa"], prm["ba"], prm["snw"], prm["sws"], prm["sbs"],
              prm["gnw"], prm["wout"], prm["n2"], prm["rw_t"]]
    tok = lambda w: pl.BlockSpec((group, TOKEN_BLOCK, w), lambda s: (s, 0, 0))
    full = lambda a: pl.BlockSpec(a.shape, lambda s: (0,) * a.ndim)
    st_spec = pl.BlockSpec((group, GLA_HEADS, GLA_DK, GLA_DV), lambda s: (s, 0, 0, 0))
    s0_spec = lambda a: st_spec if a.shape[0] == nseq else pl.BlockSpec(a.shape, lambda s: (0, 0, 0, 0))
    state = jax.ShapeDtypeStruct((nseq, GLA_HEADS, GLA_DK, GLA_DV), F32)
    st_scratch = pltpu.VMEM((group, 2, 2 * GLA_DV, 2 * GLA_DK), F32)
    return pl.pallas_call(
        functools.partial(_mixer_fused_kernel, group, mod_base, mod_per_seq),
        grid=(nseq // group,),
        in_specs=[tok(D_MODEL)] + [full(a) for a in consts] + [s0_spec(s0_f), s0_spec(s0_b)],
        out_specs=[tok(D_MODEL),
                   pl.BlockSpec((group, TOKEN_BLOCK * TILES_PER_TOKEN, LANES), lambda s: (s, 0, 0)),
                   pl.BlockSpec((group, 1, N_EXPERTS, TOKEN_BLOCK), lambda s: (s, 0, 0, 0)),
                   st_spec, st_spec],
        out_shape=[jax.ShapeDtypeStruct((nseq, TOKEN_BLOCK, D_MODEL), F32),
                   jax.ShapeDtypeStruct((nseq, TOKEN_BLOCK * TILES_PER_TOKEN, LANES), F32),
                   jax.ShapeDtypeStruct((nseq, 1, N_EXPERTS, TOKEN_BLOCK), F32),
                   state, state],
        scratch_shapes=[st_scratch, st_scratch],
        compiler_params=_compiler_params(("arbitrary",)),
        name="mixer_fused",
    )(x, *consts, s0_f, s0_b)


def _route_kernel(n_tok, cap, probs_ref, row_ref, gate_ref, xs_ref, ps_ref):
    n_blk = n_tok // TOKEN_BLOCK
    n_chunk = n_tok // LANES
    probs = jnp.concatenate([probs_ref[b] for b in range(n_blk)], axis=1)
    capf = jnp.float32(cap)

    def count(mask):
        return jnp.sum(mask.astype(F32), axis=1, keepdims=True)

    def as_f32(bits):
        return lax.bitcast_convert_type(bits, F32)

    def thr_step(_, lohi):
        lo, hi = lohi
        mid = lo + ((hi - lo + 1) >> 1)
        ok = count(probs >= as_f32(mid)) >= capf
        return jnp.where(ok, mid, lo), jnp.where(ok, hi, mid - 1)

    lo0 = jnp.zeros((N_EXPERTS, 1), I32)
    hi0 = jnp.full((N_EXPERTS, 1), F32_ONE_BITS, I32)
    thr, _ = lax.fori_loop(0, THRESHOLD_STEPS, thr_step, (lo0, hi0))
    gt = probs >= as_f32(thr + 1)
    eq = (probs >= as_f32(thr)) & jnp.logical_not(gt)
    need = capf - count(gt)
    tok = lax.broadcasted_iota(I32, (N_EXPERTS, n_tok), 1)

    def tie_step(_, lohi):
        lo, hi = lohi
        mid = (lo + hi) >> 1
        ok = count(eq & (tok <= mid)) >= need
        return jnp.where(ok, lo, mid + 1), jnp.where(ok, mid, hi)

    n_bits = max(1, (n_tok - 1).bit_length())
    cut, _ = lax.fori_loop(0, n_bits, tie_step,
                           (jnp.zeros((N_EXPERTS, 1), I32), jnp.full((N_EXPERTS, 1), n_tok - 1, I32)))
    sel = (gt | (eq & (tok <= cut))).astype(F32)

    xs_ref[...] = jnp.concatenate([sel[:, c * LANES:(c + 1) * LANES] for c in range(n_chunk)], axis=0)
    ps_ref[...] = jnp.concatenate([probs[:, c * LANES:(c + 1) * LANES] for c in range(n_chunk)], axis=0)

    li = lax.broadcasted_iota(I32, (LANES, LANES), 0)
    lj = lax.broadcasted_iota(I32, (LANES, LANES), 1)
    upper = (li <= lj).astype(F32)
    ci = lax.broadcasted_iota(I32, (n_chunk, n_chunk), 0)
    cj = lax.broadcasted_iota(I32, (n_chunk, n_chunk), 1)
    lower = (cj <= ci).astype(F32)
    slot = lax.broadcasted_iota(I32, (1, cap), 1).astype(F32)
    chunk_id = lax.broadcasted_iota(I32, (n_chunk, cap), 0).astype(F32)
    lane_id = lax.broadcasted_iota(I32, (LANES, cap), 0).astype(F32)
    reps = cap // LANES

    def store_flat(ref, e, val):
        for i in range(reps):
            ref[pl.ds(e * reps + i, 1), :] = val[:, i * LANES:(i + 1) * LANES]

    def per_expert(e):
        x = xs_ref[pl.ds(e, n_chunk, stride=N_EXPERTS), :]
        pe = ps_ref[pl.ds(e, n_chunk, stride=N_EXPERTS), :]
        ploc = _dot(x, upper)
        tot = jnp.broadcast_to(ploc[:, LANES - 1:LANES], (n_chunk, LANES))
        cum = _dot(lower, tot)
        yield
        cum_w = jnp.concatenate([cum] * reps, axis=1)
        base_w = jnp.concatenate([cum - tot] * reps, axis=1)
        chunk_of = jnp.sum((cum_w <= slot).astype(F32), axis=0, keepdims=True)
        onehot = chunk_id == chunk_of
        local = slot - jnp.sum(jnp.where(onehot, base_w, 0.0), axis=0, keepdims=True)
        yield
        lhs = jnp.concatenate([ploc.astype(BF16)] + _split_bf16(pe, 3), axis=1)
        got = _dot_tn(lhs, onehot.astype(F32))
        yield
        pref = got[0:LANES]
        lane_of = jnp.sum((pref <= local).astype(F32), axis=0, keepdims=True)
        token = chunk_of * LANES + lane_of
        store_flat(row_ref, e, (token * TILES_PER_TOKEN).astype(I32))
        yield
        pg = (got[3 * LANES:4 * LANES] + got[2 * LANES:3 * LANES]) + got[LANES:2 * LANES]
        gate = jnp.sum(jnp.where(lane_id == lane_of, pg, 0.0), axis=0, keepdims=True)
        gate_ref[e] = jnp.transpose(jnp.broadcast_to(gate, (LANES, cap)))

    def expert_pair(i, _):
        _interleave(per_expert(2 * i + u) for u in range(2))
        return 0

    lax.fori_loop(0, N_EXPERTS // 2, expert_pair, 0)


def _route(probs, n_tok, cap):
    return pl.pallas_call(
        functools.partial(_route_kernel, n_tok, cap),
        out_shape=[jax.ShapeDtypeStruct((N_EXPERTS * cap // LANES, LANES), I32),
                   jax.ShapeDtypeStruct((N_EXPERTS, cap, LANES), F32)],
        scratch_shapes=[pltpu.VMEM((n_tok // LANES * N_EXPERTS, LANES), F32),
                        pltpu.VMEM((n_tok // LANES * N_EXPERTS, LANES), F32)],
        compiler_params=pltpu.CompilerParams(vmem_limit_bytes=VMEM_LIMIT),
        name="route_topk",
    )(probs)


def _expert_kernel(cap, row_ref, h2t_ref, gate_ref, w1_ref, w3_ref, w2_ref, ye_ref,
                   xe_ref, x2_ref, hid_ref, sem):
    e = pl.program_id(0)
    f = pl.program_id(1)
    slot = e % 2
    rows_per_step = cap // N_HID_STEPS

    def start_row(expert, buf, j):
        src = pl.multiple_of(row_ref[expert * cap + j], TILES_PER_TOKEN)
        dst = pl.multiple_of(j * GATHER_PITCH, GATHER_PITCH)
        pltpu.make_async_copy(h2t_ref.at[pl.ds(src, TILES_PER_TOKEN), :],
                              xe_ref.at[buf, pl.ds(dst, TILES_PER_TOKEN), :], sem.at[buf]).start()

    def wait_rows(buf):
        pltpu.make_async_copy(h2t_ref.at[pl.ds(0, cap * TILES_PER_TOKEN), :],
                              xe_ref.at[buf, pl.ds(0, cap * TILES_PER_TOKEN), :], sem.at[buf]).wait()

    def prefetch_next():
        nxt = jnp.minimum(e + 1, N_EXPERTS - 1)
        first = f * rows_per_step
        for j in range(rows_per_step):
            start_row(nxt, 1 - slot, first + j)

    @pl.when((e == 0) & (f == 0))
    def _():
        def issue(j, _):
            start_row(0, 0, j)
            return 0

        lax.fori_loop(0, cap, issue, 0, unroll=8)

    @pl.when(f == 0)
    def _():
        wait_rows(slot)
        for s in range(TILES_PER_TOKEN):
            x2_ref[:, s * LANES:(s + 1) * LANES] = (
                xe_ref[slot, pl.ds(s, cap, stride=GATHER_PITCH), :].astype(BF16))

    @pl.when(f < N_HID_STEPS)
    def _():
        prefetch_next()
        x2 = x2_ref[...]
        a = jnp.dot(x2, w1_ref[0].astype(BF16), preferred_element_type=F32)
        b = jnp.dot(x2, w3_ref[0].astype(BF16), preferred_element_type=F32)
        hid_ref[f] = (jax.nn.silu(a) * b).astype(BF16)

    @pl.when(f >= N_HID_STEPS)
    def _():
        w2 = w2_ref[0].astype(BF16)
        out = jnp.dot(hid_ref[0], w2[0:EXPERT_F_BLOCK], preferred_element_type=F32)
        for kb in range(1, N_HID_STEPS):
            out += jnp.dot(hid_ref[kb], w2[kb * EXPERT_F_BLOCK:(kb + 1) * EXPERT_F_BLOCK],
                           preferred_element_type=F32)
        tile0 = (f - N_HID_STEPS) * (EXPERT_N_BLOCK // LANES)
        gate = gate_ref[0]
        for i in range(EXPERT_N_BLOCK // LANES):
            ye_ref[pl.ds(tile0 + i, cap, stride=TILES_PER_TOKEN), :] = out[:, i * LANES:(i + 1) * LANES] * gate

    @pl.when((e == N_EXPERTS - 1) & (f == EXPERT_STEPS - 1))
    def _():
        wait_rows(1 - slot)


def _experts(rows, h2t, gates, w1, w3, w2, cap):
    hid_blk = lambda e, f, idx: (e, 0, jnp.minimum(f, N_HID_STEPS - 1))
    n_out = D_MODEL // EXPERT_N_BLOCK

    def out_blk(e, f, idx):
        hold = (f == 0) & (e > 0)
        return (jnp.where(hold, e - 1, e), 0,
                jnp.where(hold, n_out - 1, jnp.maximum(f - N_HID_STEPS, 0)))

    grid_spec = pltpu.PrefetchScalarGridSpec(
        num_scalar_prefetch=1,
        grid=(N_EXPERTS, EXPERT_STEPS),
        in_specs=[pl.BlockSpec(memory_space=pl.ANY),
                  pl.BlockSpec((1, cap, LANES), lambda e, f, idx: (e, 0, 0)),
                  pl.BlockSpec((1, D_MODEL, EXPERT_F_BLOCK), hid_blk),
                  pl.BlockSpec((1, D_MODEL, EXPERT_F_BLOCK), hid_blk),
                  pl.BlockSpec((1, D_EXPERT, EXPERT_N_BLOCK), out_blk)],
        out_specs=pl.BlockSpec((cap * TILES_PER_TOKEN, LANES), lambda e, f, idx: (e, 0)),
        scratch_shapes=[pltpu.VMEM((2, cap * GATHER_PITCH, LANES), F32),
                        pltpu.VMEM((cap, D_MODEL), BF16),
                        pltpu.VMEM((N_HID_STEPS, cap, EXPERT_F_BLOCK), BF16),
                        pltpu.SemaphoreType.DMA((2,))],
    )
    return pl.pallas_call(
        functools.partial(_expert_kernel, cap),
        grid_spec=grid_spec,
        out_shape=jax.ShapeDtypeStruct((N_EXPERTS * cap * TILES_PER_TOKEN, LANES), F32),
        compiler_params=_compiler_params(("arbitrary", "arbitrary")),
        name="expert_swiglu",
    )(rows, h2t, gates, w1, w3, w2)


COMBINE_BATCH = 8
ZERO_ROWS = 512


def _combine_kernel(cap, n_tok, mod_base, mod_per_seq, nb,
                    row_ref, ye_ref, x1_ref, mod_ref, fw_ref, y_ref,
                    acc_ref, xbuf_ref, ybuf_ref, sem_x, sem_y):
    e = pl.program_id(0)

    @pl.when(e == 0)
    def _():
        def zero(i, _):
            r = pl.multiple_of(i * ZERO_ROWS, ZERO_ROWS)
            acc_ref[pl.ds(r, ZERO_ROWS), :] = jnp.zeros((ZERO_ROWS, LANES), F32)
            return 0

        lax.fori_loop(0, n_tok * TILES_PER_TOKEN // ZERO_ROWS, zero, 0)

    def batch(jb, _):
        vals = []
        for u in range(COMBINE_BATCH):
            j = jb * COMBINE_BATCH + u
            t = pl.multiple_of(row_ref[e * cap + j], TILES_PER_TOKEN)
            src = pl.multiple_of(j * TILES_PER_TOKEN, TILES_PER_TOKEN)
            vals.append((t, acc_ref[pl.ds(t, TILES_PER_TOKEN), :] + ye_ref[pl.ds(src, TILES_PER_TOKEN), :]))
        for t, val in vals:
            acc_ref[pl.ds(t, TILES_PER_TOKEN), :] = val
        return 0

    lax.fori_loop(0, cap // COMBINE_BATCH, batch, 0)

    @pl.when(e == N_EXPERTS - 1)
    def _():
        n_blk = n_tok // TOKEN_BLOCK

        def x1_copy(b, slot):
            rows = pl.ds(pl.multiple_of(b * TOKEN_BLOCK, TOKEN_BLOCK), TOKEN_BLOCK)
            return pltpu.make_async_copy(x1_ref.at[rows, :], xbuf_ref.at[slot], sem_x.at[slot])

        def y_copy(b, slot):
            rows = pl.ds(pl.multiple_of(b * TOKEN_BLOCK, TOKEN_BLOCK), TOKEN_BLOCK)
            return pltpu.make_async_copy(ybuf_ref.at[slot], y_ref.at[rows, :], sem_y.at[slot])

        x1_copy(0, 0).start()

        def block(b, _):
            slot = b % 2

            @pl.when(b + 1 < n_blk)
            def _():
                x1_copy(b + 1, 1 - slot).start()

            x1_copy(b, slot).wait()

            @pl.when(b >= 2)
            def _():
                y_copy(b - 2, slot).wait()

            base = pl.multiple_of(b * (TOKEN_BLOCK * TILES_PER_TOKEN), TOKEN_BLOCK * TILES_PER_TOKEN)
            moe = jnp.concatenate(
                [acc_ref[pl.ds(base + s, TOKEN_BLOCK, stride=TILES_PER_TOKEN), :]
                 for s in range(TILES_PER_TOKEN)], axis=1)
            row = mod_base + ((b // nb) if mod_per_seq else 0)
            ybuf_ref[slot] = _rms(xbuf_ref[slot] + _mod_row(mod_ref, row, 5) * moe, fw_ref[...])
            y_copy(b, slot).start()
            return 0

        lax.fori_loop(0, n_blk, block, 0)
        y_copy(n_blk - 2, n_blk % 2).wait()
        y_copy(n_blk - 1, (n_blk - 1) % 2).wait()


def _combine_final(rows, ye, x1, mod, fw, cap, n_tok, *, nb, mod_base, mod_per_seq):
    full = lambda a: pl.BlockSpec(a.shape, lambda e, rows: (0,) * a.ndim)
    grid_spec = pltpu.PrefetchScalarGridSpec(
        num_scalar_prefetch=1,
        grid=(N_EXPERTS,),
        in_specs=[pl.BlockSpec((cap * TILES_PER_TOKEN, LANES), lambda e, rows: (e, 0)),
                  pl.BlockSpec(memory_space=pl.ANY), full(mod), full(fw)],
        out_specs=pl.BlockSpec(memory_space=pl.ANY),
        scratch_shapes=[pltpu.VMEM((n_tok * TILES_PER_TOKEN, LANES), F32),
                        pltpu.VMEM((2, TOKEN_BLOCK, D_MODEL), F32),
                        pltpu.VMEM((2, TOKEN_BLOCK, D_MODEL), F32),
                        pltpu.SemaphoreType.DMA((2,)),
                        pltpu.SemaphoreType.DMA((2,))],
    )
    return pl.pallas_call(
        functools.partial(_combine_kernel, cap, n_tok, mod_base, mod_per_seq, nb),
        grid_spec=grid_spec,
        out_shape=jax.ShapeDtypeStruct((n_tok, D_MODEL), F32),
        compiler_params=_compiler_params(("arbitrary",)),
        name="moe_combine_norm",
    )(rows, ye, x1, mod, fw)


def _trunk_and_norm(x, e_tab, mod, s0_f, s0_b, prm, *, nseq, seq_len, add_pos, mod_base, mod_per_seq):
    nb = seq_len // TOKEN_BLOCK
    n_tok = nseq * seq_len
    cap = EC_CAPACITY_FACTOR * n_tok // N_EXPERTS
    kw = dict(nseq=nseq, nb=nb, add_pos=add_pos, mod_base=mod_base, mod_per_seq=mod_per_seq)
    if nb == 1 and not add_pos:
        x1, h2t, probs, sfin_f, sfin_b = _mixer_fused(
            x, mod, prm, s0_f, s0_b, nseq=nseq, mod_base=mod_base, mod_per_seq=mod_per_seq)
    else:
        qkv, g, lab, s, of, sfin_f = _mixer_fwd(
            x, e_tab, mod, prm["n1"], prm["win"], prm["wa"], prm["ba"], prm["snw"], prm["sws"],
            prm["sbs"], s0_f, **kw)
        x1, h2t, probs, sfin_b = _mixer_bwd(
            x, e_tab, mod, qkv, g, lab, s, of, prm["gnw"], prm["wout"], prm["n2"], prm["rw_t"],
            s0_b, **kw)
    x1 = x1.reshape(n_tok, D_MODEL)
    h2t = h2t.reshape(n_tok * TILES_PER_TOKEN, LANES)
    probs = probs.reshape(n_tok // TOKEN_BLOCK, N_EXPERTS, TOKEN_BLOCK)
    rows, gates = _route(probs, n_tok, cap)
    rows = rows.reshape(-1)
    ye = _experts(rows, h2t, gates, prm["w1"], prm["w3"], prm["w2"], cap)
    y = _combine_final(rows, ye, x1, mod, prm["fw"], cap, n_tok,
                       nb=nb, mod_base=mod_base, mod_per_seq=mod_per_seq)
    return y.reshape(nseq, seq_len, D_MODEL), sfin_f, sfin_b


def kernel(x_prompt, x_sample, state_gla_fwd, state_gla_bwd, c, c_ctx, ada_w, ada_b, norm1_w, w_in, gla_wa2_f, gla_ba_f, gla_wa2_b, gla_ba_b, gla_norm_w, sgu_norm_w, sgu_ws, sgu_bs, w_out, norm2_w, router_w, exp_w1, exp_w3, exp_w2, final_norm_w):
    assert ada_w.shape[0] == 1, "single trunk layer"
    batch, seq, _ = x_prompt.shape
    dec_batch, dec_seq, _ = x_sample.shape

    assert w_in.shape[2] == D_IN
    win = _win_transposed_bf16(w_in[0].T)
    wa = jnp.zeros((2 * GLA_LOWRANK, 2 * QK_W), F32)
    wa = wa.at[0:GLA_LOWRANK, 0:QK_W].set(gla_wa2_f[0])
    wa = wa.at[GLA_LOWRANK:2 * GLA_LOWRANK, QK_W:].set(gla_wa2_b[0]).astype(BF16)
    prm = dict(
        n1=norm1_w, win=win, wa=wa,
        ba=jnp.concatenate([gla_ba_f[0], gla_ba_b[0]])[None, :],
        snw=sgu_norm_w, sws=sgu_ws[0].astype(BF16),
        sbs=jnp.broadcast_to(sgu_bs[0][:, :, None], (SGU_GROUPS, SGU_CHUNK, SGU_CH)),
        gnw=gla_norm_w, wout=w_out[0].astype(BF16), n2=norm2_w, rw_t=router_w[0].T,
        w1=exp_w1[0], w3=exp_w3[0], w2=exp_w2[0], fw=final_norm_w[None, :])

    cvec = jnp.concatenate([c_ctx[None, :], c, jnp.zeros((SUBLANES - 1 - dec_batch, D_MODEL), F32)])
    mod = _modulation(cvec, ada_w[0], ada_b)
    e_tab = _pos_table()

    zero_state = jnp.zeros((min(SEQ_GROUP, batch), GLA_HEADS, GLA_DK, GLA_DV), F32)
    y_prompt, sf, sb = _trunk_and_norm(
        x_prompt, e_tab, mod, zero_state, zero_state, prm,
        nseq=batch, seq_len=seq, add_pos=False, mod_base=0, mod_per_seq=False)
    y_sample, _, _ = _trunk_and_norm(
        x_sample, e_tab, mod, state_gla_fwd[:, 0], state_gla_bwd[:, 0], prm,
        nseq=dec_batch, seq_len=dec_seq, add_pos=True, mod_base=1, mod_per_seq=True)
    return (y_prompt, y_sample, sf[:, None], sb[:, None])
```

```python
import functools
import math

import jax
import jax.numpy as jnp
from jax import lax
from jax.experimental import pallas as pl
from jax.experimental.pallas import tpu as pltpu

F32 = jnp.float32
BF16 = jnp.bfloat16
I32 = jnp.int32

D_MODEL = 1024
GRID_W = 64
GLA_HEADS = 4
GLA_DK = 64
GLA_DV = 128
GLA_WIDTH = GLA_HEADS * GLA_DV
QK_W = GLA_HEADS * GLA_DK
GLA_LOWRANK = 16
GLA_GATE_NORM = 16.0
GLA_CHUNK = 64
SGU_WIDTH = 512
SGU_GROUPS = 4
SGU_CH = 128
SGU_CHUNK = 128
N_EXPERTS = 16
EC_CAPACITY_FACTOR = 2
D_EXPERT = 2048
EPS = 1e-6

SUBLANES = 8
LANES = 128
TILES_PER_TOKEN = D_MODEL // LANES

TOKEN_BLOCK = 256
SEQ_GROUP = 4
OFF_A = 2 * QK_W + 2 * GLA_WIDTH
OFF_U = OFF_A + 2 * GLA_LOWRANK
D_IN = OFF_U + 2 * SGU_WIDTH
EXPERT_F_BLOCK = 512
EXPERT_N_BLOCK = 512
N_HID_STEPS = D_EXPERT // EXPERT_F_BLOCK
EXPERT_STEPS = N_HID_STEPS + D_MODEL // EXPERT_N_BLOCK
VMEM_LIMIT = 56 * 1024 * 1024
F32_ONE_BITS = 0x3F800000
THRESHOLD_STEPS = F32_ONE_BITS.bit_length() + 1


def _dot(a, b):
    return jnp.dot(a.astype(BF16), b.astype(BF16), preferred_element_type=F32)


def _dot_nt(a, b):
    return lax.dot_general(a.astype(BF16), b.astype(BF16), (((1,), (1,)), ((), ())),
                           preferred_element_type=F32)


def _dot_tn(a, b):
    return lax.dot_general(a.astype(BF16), b.astype(BF16), (((0,), (0,)), ((), ())),
                           preferred_element_type=F32)


def _dot_f32(a, b, dims=(((1,), (0,)), ((), ()))):
    return lax.dot_general(a, b, dims, precision=lax.Precision.HIGHEST, preferred_element_type=F32)


def _split_bf16(x, terms):
    parts = []
    for _ in range(terms - 1):
        part = x.astype(BF16)
        parts.append(part)
        x = x - part.astype(F32)
    parts.append(x.astype(BF16))
    return parts


def _select_dot(sel, x):
    s = sel.astype(BF16)
    hi, mid, lo = _split_bf16(x, 3)
    return (jnp.dot(s, lo, preferred_element_type=F32) + jnp.dot(s, mid, preferred_element_type=F32)
            + jnp.dot(s, hi, preferred_element_type=F32))


def _dot_nt_3pass(a, b):
    a_hi, a_lo = _split_bf16(a, 2)
    b_hi, b_lo = _split_bf16(b, 2)
    nt = lambda x, y: lax.dot_general(x, y, (((1,), (1,)), ((), ())), preferred_element_type=F32)
    return (nt(a_hi, b_lo) + nt(a_lo, b_hi)) + nt(a_hi, b_hi)


def _rms(x, w):
    return x * lax.rsqrt(jnp.mean(x * x, axis=-1, keepdims=True) + EPS) * w


def _compiler_params(sem):
    return pltpu.CompilerParams(dimension_semantics=sem, vmem_limit_bytes=VMEM_LIMIT)


def _mod_kernel(c_ref, w_ref, b_ref, o_ref):
    o_ref[...] = _dot(jax.nn.silu(c_ref[...]), w_ref[...]) + b_ref[...]


def _modulation(cvec, ada_w, ada_b):
    n = ada_w.shape[1]
    bn = 1536
    return pl.pallas_call(
        _mod_kernel,
        grid=(n // bn,),
        in_specs=[pl.BlockSpec((SUBLANES, D_MODEL), lambda j: (0, 0)),
                  pl.BlockSpec((D_MODEL, bn), lambda j: (0, j)),
                  pl.BlockSpec((1, bn), lambda j: (0, j))],
        out_specs=pl.BlockSpec((SUBLANES, bn), lambda j: (0, j)),
        out_shape=jax.ShapeDtypeStruct((SUBLANES, n), F32),
        compiler_params=_compiler_params(("arbitrary",)),
        name="adaln_mod",
    )(cvec, ada_w, ada_b)


def _cast_kernel(w_ref, o_ref):
    o_ref[...] = w_ref[...].astype(BF16)


def _win_transposed_bf16(w_t):
    rows = D_IN // 3
    return pl.pallas_call(
        _cast_kernel,
        grid=(D_IN // rows,),
        in_specs=[pl.BlockSpec((rows, D_MODEL), lambda i: (i, 0))],
        out_specs=pl.BlockSpec((rows, D_MODEL), lambda i: (i, 0)),
        out_shape=jax.ShapeDtypeStruct((D_IN, D_MODEL), BF16),
        compiler_params=_compiler_params(("arbitrary",)),
        name="win_cast",
    )(w_t)


def _pos_kernel(o_ref):
    nf = D_MODEL // 4
    p = lax.broadcasted_iota(I32, (GRID_W, nf), 0).astype(F32)
    i = lax.broadcasted_iota(I32, (GRID_W, nf), 1).astype(F32)
    omega = jnp.exp(i * (-math.log(10000.0) / nf))
    a = p * omega
    o_ref[:, 0:nf] = jnp.sin(a)
    o_ref[:, nf:2 * nf] = jnp.cos(a)


def _pos_table():
    return pl.pallas_call(
        _pos_kernel,
        out_shape=jax.ShapeDtypeStruct((GRID_W, D_MODEL // 2), F32),
        name="sincos_table",
    )()


def _add_pos(x, e_ref, blk, add_pos):
    if not add_pos:
        return x
    half = D_MODEL // 2
    e_all = e_ref[...]
    rows = []
    for j in range(TOKEN_BLOCK // GRID_W):
        xj = x[j * GRID_W:(j + 1) * GRID_W]
        e_row = e_ref[pl.ds(blk * (TOKEN_BLOCK // GRID_W) + j, 1), :]
        rows.append(jnp.concatenate([xj[:, 0:half] + e_row, xj[:, half:] + e_all], axis=1))
    return jnp.concatenate(rows, axis=0)


def _chunk_masks():
    r = lax.broadcasted_iota(I32, (TOKEN_BLOCK, TOKEN_BLOCK), 0)
    c = lax.broadcasted_iota(I32, (TOKEN_BLOCK, TOKEN_BLOCK), 1)
    same = (r // GLA_CHUNK) == (c // GLA_CHUNK)
    return same & (c <= r), same & (c >= r)


def _gla_direction(q, k, v, cum, fwd, att_mask, st_ref, ready, done):
    qe = q * jnp.exp(cum)
    ke = k * jnp.exp(-cum)
    yield
    lane = lax.broadcasted_iota(I32, (1, LANES), 1)
    o_intra = []
    for pair in range(2):
        qp = qe[:, pair * LANES:(pair + 1) * LANES]
        kp = ke[:, pair * LANES:(pair + 1) * LANES]
        for hh in range(2):
            qm = jnp.where((lane // GLA_DK) == hh, qp, 0.0)
            att = jnp.where(att_mask, _dot_nt(qm, kp), 0.0)
            head = 2 * pair + hh
            o_intra.append(_dot(att, v[:, head * GLA_DV:(head + 1) * GLA_DV]))
            yield
    o_intra = jnp.concatenate(o_intra, axis=1)
    while not ready():
        yield

    er = lax.broadcasted_iota(I32, (2 * GLA_DV, 2 * GLA_DK), 0)
    dc = lax.broadcasted_iota(I32, (2 * GLA_DV, 2 * GLA_DK), 1)
    same_head = (er // GLA_DV) == (dc // GLA_DK)
    n_chunks = TOKEN_BLOCK // GLA_CHUNK
    o_inter = [None] * n_chunks
    for c in (range(n_chunks) if fwd else reversed(range(n_chunks))):
        r0 = c * GLA_CHUNK
        rows = slice(r0, r0 + GLA_CHUNK)
        last = cum[r0 + GLA_CHUNK - 1:r0 + GLA_CHUNK] if fwd else cum[r0:r0 + 1]
        kd = k[rows] * jnp.exp(last - cum[rows])
        dec = jnp.exp(last)
        parts = []
        for pair in range(2):
            dl = slice(pair * LANES, (pair + 1) * LANES)
            st = st_ref[pair]
            parts.append(_dot_nt(qe[rows, dl], st))
            ds_t = _dot_tn(v[rows, pair * 2 * GLA_DV:(pair + 1) * 2 * GLA_DV], kd[:, dl])
            st_ref[pair] = dec[:, dl] * st + jnp.where(same_head, ds_t, 0.0)
        o_inter[c] = jnp.concatenate(parts, axis=1)
        yield
    done()
    return o_intra + jnp.concatenate(o_inter, axis=0)


def _interleave(chains):
    chains = list(chains)
    done = [False] * len(chains)
    tick = 0
    while not all(done):
        for i, ch in enumerate(chains):
            if tick >= i and not done[i]:
                try:
                    next(ch)
                except StopIteration:
                    done[i] = True
        tick += 1


def _load_state(s0_ref, u, st_ref):
    zero = jnp.zeros((GLA_DV, GLA_DK), F32)
    for pair in range(2):
        a = s0_ref[u, 2 * pair].T
        b = s0_ref[u, 2 * pair + 1].T
        st_ref[pair] = jnp.concatenate(
            [jnp.concatenate([a, zero], axis=1), jnp.concatenate([zero, b], axis=1)], axis=0)


def _store_state(st_ref, sfin_ref, u):
    for pair in range(2):
        st = st_ref[pair]
        sfin_ref[u, 2 * pair] = st[0:GLA_DV, 0:GLA_DK].T
        sfin_ref[u, 2 * pair + 1] = st[GLA_DV:2 * GLA_DV, GLA_DK:2 * GLA_DK].T


def _mod_row(mod_ref, row, part):
    return mod_ref[pl.ds(row, 1), part * D_MODEL:(part + 1) * D_MODEL]


def _front_stages(xin, row, mod_ref, n1_ref, win_ref, wa_ref, ba_ref, snw_ref, sws_ref, sbs_ref):
    h = _rms(xin, n1_ref[...]) * (1.0 + _mod_row(mod_ref, row, 1)) + _mod_row(mod_ref, row, 0)
    yield
    hb = h.astype(BF16)
    parts = []
    for r0, r1 in ((0, OFF_A), (OFF_U, D_IN), (OFF_A, OFF_U)):
        parts.append(lax.dot_general(hb, win_ref[r0:r1, :], (((1,), (1,)), ((), ())),
                                     preferred_element_type=F32))
        yield
    main, gate, low = parts
    q = main[:, 0:QK_W] * (GLA_DK ** -0.5)
    k = main[:, QK_W:2 * QK_W]
    v = main[:, 2 * QK_W:2 * QK_W + GLA_WIDTH]
    z = _dot(low, wa_ref[...]) + ba_ref[...]
    la = (jnp.minimum(z, 0.0) - jnp.log1p(jnp.exp(-jnp.abs(z)))) * (1.0 / GLA_GATE_NORM)
    yield

    ug = jax.nn.gelu(gate[:, 0:SGU_WIDTH])
    yield
    vg = jax.nn.gelu(gate[:, SGU_WIDTH:2 * SGU_WIDTH])
    yield
    s_cols = []
    for gi in range(SGU_GROUPS):
        cols = slice(gi * SGU_CH, (gi + 1) * SGU_CH)
        vn = _rms(vg[:, cols], snw_ref[:, cols])
        rhs = jnp.concatenate([vn[0:SGU_CHUNK], vn[SGU_CHUNK:2 * SGU_CHUNK]], axis=1)
        vm = _dot(sws_ref[gi], rhs) + jnp.concatenate([sbs_ref[gi], sbs_ref[gi]], axis=1)
        vm = jnp.concatenate([vm[:, 0:SGU_CH], vm[:, SGU_CH:2 * SGU_CH]], axis=0)
        s_cols.append(ug[:, cols] * vm)
    yield
    return q, k, v, main[:, 2 * QK_W + GLA_WIDTH:OFF_A], la, jnp.concatenate(s_cols, axis=1)


def _back_stages(o, g, s_val, xin, row, mod_ref, gnw_ref, wout_ref, n2_ref, rw_ref):
    cols = []
    for head in range(GLA_HEADS):
        hs = slice(head * GLA_DV, (head + 1) * GLA_DV)
        cols.append(_rms(o[:, hs], gnw_ref[...]) * jax.nn.silu(g[:, hs]))
    cols.append(s_val)
    yield
    y = _dot(jnp.concatenate(cols, axis=1), wout_ref[...])
    yield
    x1 = xin + _mod_row(mod_ref, row, 2) * y
    h2 = _rms(x1, n2_ref[...]) * (1.0 + _mod_row(mod_ref, row, 4)) + _mod_row(mod_ref, row, 3)
    yield
    logits = _dot_nt_3pass(rw_ref[...], h2)
    m = jnp.max(logits, axis=0, keepdims=True)
    ex = jnp.exp(logits - m)
    return x1, h2, ex / jnp.sum(ex, axis=0, keepdims=True)


def _store_token_tiles(h2t_ref, u, b, h2):
    for s in range(TILES_PER_TOKEN):
        h2t_ref[u, pl.ds(b * TOKEN_BLOCK * TILES_PER_TOKEN + s, TOKEN_BLOCK, stride=TILES_PER_TOKEN), :] = (
            h2[:, s * LANES:(s + 1) * LANES])


def _zip_stages(gen_a, gen_b):
    out = [None, None]
    live = [gen_a, gen_b]
    while any(g is not None for g in live):
        for i, g in enumerate(live):
            if g is not None:
                try:
                    next(g)
                except StopIteration as stop:
                    out[i] = stop.value
                    live[i] = None
        yield
    return out


def _mixer_fwd_kernel(group, add_pos, mod_base, mod_per_seq, nb,
                      x_ref, e_ref, mod_ref, n1_ref, win_ref, wa_ref, ba_ref,
                      snw_ref, sws_ref, sbs_ref, s0_ref,
                      qkv_ref, g_ref, lab_ref, s_ref, of_ref, sfin_ref,
                      st_ref):
    grp = pl.program_id(0)
    blk = pl.program_id(1)
    sub = x_ref.shape[1] // TOKEN_BLOCK
    lo_mask, _ = _chunk_masks()
    state_done = set()

    @pl.when(blk == 0)
    def _():
        for u in range(group):
            _load_state(s0_ref, u, st_ref.at[u])

    def chain(u, b):
        row = mod_base + ((grp * group + u) if mod_per_seq else 0)
        st_u = st_ref.at[u]
        rows = slice(b * TOKEN_BLOCK, (b + 1) * TOKEN_BLOCK)
        xin = _add_pos(x_ref[u, rows, :], e_ref, blk * sub + b, add_pos)
        q, k, v, g, la, s_val = yield from _front_stages(
            xin, row, mod_ref, n1_ref, win_ref, wa_ref, ba_ref, snw_ref, sws_ref, sbs_ref)
        qkv_ref[u, rows, :] = jnp.concatenate([q, k, v], axis=1)
        g_ref[u, rows, :] = g
        lab_ref[u, rows, :] = la[:, QK_W:2 * QK_W]
        s_ref[u, rows, :] = s_val

        cum = _select_dot(lo_mask, la[:, 0:QK_W])
        yield
        of_ref[u, rows, :] = yield from _gla_direction(
            q, k, v, cum, True, lo_mask, st_u,
            ready=lambda: b == 0 or (u, b - 1) in state_done, done=lambda: state_done.add((u, b)))

    _interleave(chain(u, b) for b in range(sub) for u in range(group))

    @pl.when(blk == nb - 1)
    def _():
        for u in range(group):
            _store_state(st_ref.at[u], sfin_ref, u)


def _mixer_fwd(x, e_tab, mod, n1, win, wa, ba, snw, sws, sbs, s0, *, nseq, nb, add_pos,
               mod_base, mod_per_seq):
    seq_len = nb * TOKEN_BLOCK
    group = min(SEQ_GROUP, nseq)
    sub = min(SEQ_GROUP // group, nb)
    nb //= sub
    tok = lambda w: pl.BlockSpec((group, sub * TOKEN_BLOCK, w), lambda s, i: (s, i, 0))
    full = lambda a: pl.BlockSpec(a.shape, lambda s, i: (0,) * a.ndim)
    st_spec = pl.BlockSpec((group, GLA_HEADS, GLA_DK, GLA_DV), lambda s, i: (s, 0, 0, 0))
    s0_spec = st_spec if s0.shape[0] == nseq else pl.BlockSpec(s0.shape, lambda s, i: (0, 0, 0, 0))
    act = lambda w: jax.ShapeDtypeStruct((nseq, seq_len, w), F32)
    kern = functools.partial(_mixer_fwd_kernel, group, add_pos, mod_base, mod_per_seq, nb)
    return pl.pallas_call(
        kern,
        grid=(nseq // group, nb),
        in_specs=[tok(D_MODEL), full(e_tab), full(mod), full(n1), full(win), full(wa), full(ba),
                  full(snw), full(sws), full(sbs), s0_spec],
        out_specs=[tok(1024), tok(GLA_WIDTH), tok(QK_W), tok(SGU_WIDTH), tok(GLA_WIDTH), st_spec],
        out_shape=[act(1024), act(GLA_WIDTH), act(QK_W), act(SGU_WIDTH), act(GLA_WIDTH),
                   jax.ShapeDtypeStruct((nseq, GLA_HEADS, GLA_DK, GLA_DV), F32)],
        scratch_shapes=[pltpu.VMEM((group, 2, 2 * GLA_DV, 2 * GLA_DK), F32)],
        compiler_params=_compiler_params(("arbitrary", "arbitrary")),
        name="mixer_fwd",
    )(x, e_tab, mod, n1, win, wa, ba, snw, sws, sbs, s0)


def _mixer_bwd_kernel(group, add_pos, mod_base, mod_per_seq, nb,
                      x_ref, e_ref, mod_ref, qkv_ref, g_ref, lab_ref, s_ref, of_ref,
                      gnw_ref, wout_ref, n2_ref, rw_ref, s0_ref,
                      x1_ref, h2t_ref, probs_ref, sfin_ref,
                      st_ref):
    grp = pl.program_id(0)
    step = pl.program_id(1)
    sub = x_ref.shape[1] // TOKEN_BLOCK
    blk = nb - 1 - step
    _, hi_mask = _chunk_masks()
    state_done = set()

    @pl.when(step == 0)
    def _():
        for u in range(group):
            _load_state(s0_ref, u, st_ref.at[u])

    def chain(u, b):
        row = mod_base + ((grp * group + u) if mod_per_seq else 0)
        st_u = st_ref.at[u]
        rows = slice(b * TOKEN_BLOCK, (b + 1) * TOKEN_BLOCK)
        qkv = qkv_ref[u, rows, :]
        q, k, v = qkv[:, 0:QK_W], qkv[:, QK_W:2 * QK_W], qkv[:, 2 * QK_W:]
        cum = _select_dot(hi_mask, lab_ref[u, rows, :])
        yield
        o_b = yield from _gla_direction(
            q, k, v, cum, False, hi_mask, st_u,
            ready=lambda: b == sub - 1 or (u, b + 1) in state_done,
            done=lambda: state_done.add((u, b)))
        xin = _add_pos(x_ref[u, rows, :], e_ref, blk * sub + b, add_pos)
        x1, h2, probs = yield from _back_stages(
            of_ref[u, rows, :] + o_b, g_ref[u, rows, :], s_ref[u, rows, :], xin, row,
            mod_ref, gnw_ref, wout_ref, n2_ref, rw_ref)
        x1_ref[u, rows, :] = x1
        _store_token_tiles(h2t_ref, u, b, h2)
        probs_ref[u, b] = probs

    _interleave(chain(u, b) for b in reversed(range(sub)) for u in range(group))

    @pl.when(step == nb - 1)
    def _():
        for u in range(group):
            _store_state(st_ref.at[u], sfin_ref, u)


def _mixer_bwd(x, e_tab, mod, qkv, g, lab, s, of, gnw, wout, n2, rw_t, s0, *, nseq, nb, add_pos,
               mod_base, mod_per_seq):
    seq_len = nb * TOKEN_BLOCK
    group = min(SEQ_GROUP, nseq)
    sub = min(SEQ_GROUP // group, nb)
    nb //= sub
    tok = lambda w: pl.BlockSpec((group, sub * TOKEN_BLOCK, w), lambda s_, i: (s_, nb - 1 - i, 0))
    full = lambda a: pl.BlockSpec(a.shape, lambda s_, i: (0,) * a.ndim)
    st_spec = pl.BlockSpec((group, GLA_HEADS, GLA_DK, GLA_DV), lambda s_, i: (s_, 0, 0, 0))
    s0_spec = st_spec if s0.shape[0] == nseq else pl.BlockSpec(s0.shape, lambda s_, i: (0, 0, 0, 0))
    kern = functools.partial(_mixer_bwd_kernel, group, add_pos, mod_base, mod_per_seq, nb)
    return pl.pallas_call(
        kern,
        grid=(nseq // group, nb),
        in_specs=[tok(D_MODEL), full(e_tab), full(mod), tok(1024), tok(GLA_WIDTH), tok(QK_W),
                  tok(SGU_WIDTH), tok(GLA_WIDTH), full(gnw), full(wout), full(n2), full(rw_t),
                  s0_spec],
        out_specs=[tok(D_MODEL),
                   pl.BlockSpec((group, sub * TOKEN_BLOCK * TILES_PER_TOKEN, LANES),
                                lambda s_, i: (s_, nb - 1 - i, 0)),
                   pl.BlockSpec((group, sub, N_EXPERTS, TOKEN_BLOCK),
                                lambda s_, i: (s_, nb - 1 - i, 0, 0)),
                   st_spec],
        out_shape=[jax.ShapeDtypeStruct((nseq, seq_len, D_MODEL), F32),
                   jax.ShapeDtypeStruct((nseq, seq_len * TILES_PER_TOKEN, LANES), F32),
                   jax.ShapeDtypeStruct((nseq, nb * sub, N_EXPERTS, TOKEN_BLOCK), F32),
                   jax.ShapeDtypeStruct((nseq, GLA_HEADS, GLA_DK, GLA_DV), F32)],
        scratch_shapes=[pltpu.VMEM((group, 2, 2 * GLA_DV, 2 * GLA_DK), F32)],
        compiler_params=_compiler_params(("arbitrary", "arbitrary")),
        name="mixer_bwd",
    )(x, e_tab, mod, qkv, g, lab, s, of, gnw, wout, n2, rw_t, s0)


def _mixer_fused_kernel(group, mod_base, mod_per_seq,
                        x_ref, mod_ref, n1_ref, win_ref, wa_ref, ba_ref, snw_ref, sws_ref, sbs_ref,
                        gnw_ref, wout_ref, n2_ref, rw_ref, s0f_ref, s0b_ref,
                        x1_ref, h2t_ref, probs_ref, sfin_f_ref, sfin_b_ref,
                        stf_ref, stb_ref):
    grp = pl.program_id(0)
    lo_mask, hi_mask = _chunk_masks()
    always = lambda: True
    nothing = lambda: None
    for u in range(group):
        _load_state(s0f_ref, u, stf_ref.at[u])
        _load_state(s0b_ref, u, stb_ref.at[u])

    def chain(u):
        row = mod_base + ((grp * group + u) if mod_per_seq else 0)
        xin = x_ref[u]
        q, k, v, g, la, s_val = yield from _front_stages(
            xin, row, mod_ref, n1_ref, win_ref, wa_ref, ba_ref, snw_ref, sws_ref, sbs_ref)
        cum_f = _select_dot(lo_mask, la[:, 0:QK_W])
        cum_b = _select_dot(hi_mask, la[:, QK_W:2 * QK_W])
        yield
        o_f, o_b = yield from _zip_stages(
            _gla_direction(q, k, v, cum_f, True, lo_mask, stf_ref.at[u], always, nothing),
            _gla_direction(q, k, v, cum_b, False, hi_mask, stb_ref.at[u], always, nothing))
        x1, h2, probs = yield from _back_stages(
            o_f + o_b, g, s_val, xin, row, mod_ref, gnw_ref, wout_ref, n2_ref, rw_ref)
        x1_ref[u] = x1
        _store_token_tiles(h2t_ref, u, 0, h2)
        probs_ref[u, 0] = probs

    _interleave(chain(u) for u in range(group))
    for u in range(group):
        _store_state(stf_ref.at[u], sfin_f_ref, u)
        _store_state(stb_ref.at[u], sfin_b_ref, u)


def _mixer_fused(x, mod, prm, s0_f, s0_b, *, nseq, mod_base, mod_per_seq):
    group = min(SEQ_GROUP, nseq)
    consts = [mod, prm["n1"], prm["win"], prm["wa"], prm["ba"], prm["snw"], prm["sws"], prm["sbs"],
              prm["gnw"], prm["wout"], prm["n2"], prm["rw_t"]]
    tok = lambda w: pl.BlockSpec((group, TOKEN_BLOCK, w), lambda s: (s, 0, 0))
    full = lambda a: pl.BlockSpec(a.shape, lambda s: (0,) * a.ndim)
    st_spec = pl.BlockSpec((group, GLA_HEADS, GLA_DK, GLA_DV), lambda s: (s, 0, 0, 0))
    s0_spec = lambda a: st_spec if a.shape[0] == nseq else pl.BlockSpec(a.shape, lambda s: (0, 0, 0, 0))
    state = jax.ShapeDtypeStruct((nseq, GLA_HEADS, GLA_DK, GLA_DV), F32)
    st_scratch = pltpu.VMEM((group, 2, 2 * GLA_DV, 2 * GLA_DK), F32)
    return pl.pallas_call(
        functools.partial(_mixer_fused_kernel, group, mod_base, mod_per_seq),
        grid=(nseq // group,),
        in_specs=[tok(D_MODEL)] + [full(a) for a in consts] + [s0_spec(s0_f), s0_spec(s0_b)],
        out_specs=[tok(D_MODEL),
                   pl.BlockSpec((group, TOKEN_BLOCK * TILES_PER_TOKEN, LANES), lambda s: (s, 0, 0)),
                   pl.BlockSpec((group, 1, N_EXPERTS, TOKEN_BLOCK), lambda s: (s, 0, 0, 0)),
                   st_spec, st_spec],
        out_shape=[jax.ShapeDtypeStruct((nseq, TOKEN_BLOCK, D_MODEL), F32),
                   jax.ShapeDtypeStruct((nseq, TOKEN_BLOCK * TILES_PER_TOKEN, LANES), F32),
                   jax.ShapeDtypeStruct((nseq, 1, N_EXPERTS, TOKEN_BLOCK), F32),
                   state, state],
        scratch_shapes=[st_scratch, st_scratch],
        compiler_params=_compiler_params(("arbitrary",)),
        name="mixer_fused",
    )(x, *consts, s0_f, s0_b)


def _route_kernel(n_tok, cap, probs_ref, row_ref, gate_ref, xs_ref, ps_ref):
    n_blk = n_tok // TOKEN_BLOCK
    n_chunk = n_tok // LANES
    probs = jnp.concatenate([probs_ref[b] for b in range(n_blk)], axis=1)
    capf = jnp.float32(cap)

    def count(mask):
        return jnp.sum(mask.astype(F32), axis=1, keepdims=True)

    def as_f32(bits):
        return lax.bitcast_convert_type(bits, F32)

    def thr_step(_, lohi):
        lo, hi = lohi
        mid = lo + ((hi - lo + 1) >> 1)
        ok = count(probs >= as_f32(mid)) >= capf
        return jnp.where(ok, mid, lo), jnp.where(ok, hi, mid - 1)

    lo0 = jnp.zeros((N_EXPERTS, 1), I32)
    hi0 = jnp.full((N_EXPERTS, 1), F32_ONE_BITS, I32)
    thr, _ = lax.fori_loop(0, THRESHOLD_STEPS, thr_step, (lo0, hi0))
    gt = probs >= as_f32(thr + 1)
    eq = (probs >= as_f32(thr)) & jnp.logical_not(gt)
    need = capf - count(gt)
    tok = lax.broadcasted_iota(I32, (N_EXPERTS, n_tok), 1)

    def tie_step(_, lohi):
        lo, hi = lohi
        mid = (lo + hi) >> 1
        ok = count(eq & (tok <= mid)) >= need
        return jnp.where(ok, lo, mid + 1), jnp.where(ok, mid, hi)

    n_bits = max(1, (n_tok - 1).bit_length())
    cut, _ = lax.fori_loop(0, n_bits, tie_step,
                           (jnp.zeros((N_EXPERTS, 1), I32), jnp.full((N_EXPERTS, 1), n_tok - 1, I32)))
    sel = (gt | (eq & (tok <= cut))).astype(F32)

    xs_ref[...] = jnp.concatenate([sel[:, c * LANES:(c + 1) * LANES] for c in range(n_chunk)], axis=0)
    ps_ref[...] = jnp.concatenate([probs[:, c * LANES:(c + 1) * LANES] for c in range(n_chunk)], axis=0)

    li = lax.broadcasted_iota(I32, (LANES, LANES), 0)
    lj = lax.broadcasted_iota(I32, (LANES, LANES), 1)
    upper = (li <= lj).astype(F32)
    ci = lax.broadcasted_iota(I32, (n_chunk, n_chunk), 0)
    cj = lax.broadcasted_iota(I32, (n_chunk, n_chunk), 1)
    lower = (cj <= ci).astype(F32)
    slot = lax.broadcasted_iota(I32, (1, cap), 1).astype(F32)
    chunk_id = lax.broadcasted_iota(I32, (n_chunk, cap), 0).astype(F32)
    lane_id = lax.broadcasted_iota(I32, (LANES, cap), 0).astype(F32)
    reps = cap // LANES

    def store_flat(ref, e, val):
        for i in range(reps):
            ref[pl.ds(e * reps + i, 1), :] = val[:, i * LANES:(i + 1) * LANES]

    def per_expert(e):
        x = xs_ref[pl.ds(e, n_chunk, stride=N_EXPERTS), :]
        pe = ps_ref[pl.ds(e, n_chunk, stride=N_EXPERTS), :]
        ploc = _dot(x, upper)
        tot = jnp.broadcast_to(ploc[:, LANES - 1:LANES], (n_chunk, LANES))
        cum = _dot(lower, tot)
        yield
        cum_w = jnp.concatenate([cum] * reps, axis=1)
        base_w = jnp.concatenate([cum - tot] * reps, axis=1)
        chunk_of = jnp.sum((cum_w <= slot).astype(F32), axis=0, keepdims=True)
        onehot = chunk_id == chunk_of
        local = slot - jnp.sum(jnp.where(onehot, base_w, 0.0), axis=0, keepdims=True)
        yield
        lhs = jnp.concatenate([ploc.astype(BF16)] + _split_bf16(pe, 3), axis=1)
        got = _dot_tn(lhs, onehot.astype(F32))
        yield
        pref = got[0:LANES]
        lane_of = jnp.sum((pref <= local).astype(F32), axis=0, keepdims=True)
        token = chunk_of * LANES + lane_of
        store_flat(row_ref, e, (token * TILES_PER_TOKEN).astype(I32))
        yield
        pg = (got[3 * LANES:4 * LANES] + got[2 * LANES:3 * LANES]) + got[LANES:2 * LANES]
        gate = jnp.sum(jnp.where(lane_id == lane_of, pg, 0.0), axis=0, keepdims=True)
        gate_ref[e] = jnp.transpose(jnp.broadcast_to(gate, (LANES, cap)))

    def expert_pair(i, _):
        _interleave(per_expert(2 * i + u) for u in range(2))
        return 0

    lax.fori_loop(0, N_EXPERTS // 2, expert_pair, 0)


def _route(probs, n_tok, cap):
    return pl.pallas_call(
        functools.partial(_route_kernel, n_tok, cap),
        out_shape=[jax.ShapeDtypeStruct((N_EXPERTS * cap // LANES, LANES), I32),
                   jax.ShapeDtypeStruct((N_EXPERTS, cap, LANES), F32)],
        scratch_shapes=[pltpu.VMEM((n_tok // LANES * N_EXPERTS, LANES), F32),
                        pltpu.VMEM((n_tok // LANES * N_EXPERTS, LANES), F32)],
        compiler_params=pltpu.CompilerParams(vmem_limit_bytes=VMEM_LIMIT),
        name="route_topk",
    )(probs)


def _expert_kernel(cap, row_ref, h2t_ref, gate_ref, w1_ref, w3_ref, w2_ref, ye_ref,
                   xe_ref, x2_ref, hid_ref, sem):
    e = pl.program_id(0)
    f = pl.program_id(1)
    slot = e % 2
    rows_per_step = cap // N_HID_STEPS

    def start_row(expert, buf, j):
        src = pl.multiple_of(row_ref[expert * cap + j], TILES_PER_TOKEN)
        dst = pl.multiple_of(j * TILES_PER_TOKEN, TILES_PER_TOKEN)
        pltpu.make_async_copy(h2t_ref.at[pl.ds(src, TILES_PER_TOKEN), :],
                              xe_ref.at[buf, pl.ds(dst, TILES_PER_TOKEN), :], sem.at[buf]).start()

    def wait_rows(buf):
        pltpu.make_async_copy(h2t_ref.at[pl.ds(0, cap * TILES_PER_TOKEN), :], xe_ref.at[buf],
                              sem.at[buf]).wait()

    def prefetch_next():
        nxt = jnp.minimum(e + 1, N_EXPERTS - 1)
        first = f * rows_per_step
        for j in range(rows_per_step):
            start_row(nxt, 1 - slot, first + j)

    @pl.when((e == 0) & (f == 0))
    def _():
        def issue(j, _):
            start_row(0, 0, j)
            return 0

        lax.fori_loop(0, cap, issue, 0, unroll=8)

    @pl.when(f == 0)
    def _():
        wait_rows(slot)
        for s in range(TILES_PER_TOKEN):
            x2_ref[:, s * LANES:(s + 1) * LANES] = (
                xe_ref[slot, pl.ds(s, cap, stride=TILES_PER_TOKEN), :].astype(BF16))

    @pl.when(f < N_HID_STEPS)
    def _():
        prefetch_next()
        x2 = x2_ref[...]
        a = jnp.dot(x2, w1_ref[0].astype(BF16), preferred_element_type=F32)
        b = jnp.dot(x2, w3_ref[0].astype(BF16), preferred_element_type=F32)
        hid_ref[f] = (jax.nn.silu(a) * b).astype(BF16)

    @pl.when(f >= N_HID_STEPS)
    def _():
        w2 = w2_ref[0].astype(BF16)
        out = jnp.dot(hid_ref[0], w2[0:EXPERT_F_BLOCK], preferred_element_type=F32)
        for kb in range(1, N_HID_STEPS):
            out += jnp.dot(hid_ref[kb], w2[kb * EXPERT_F_BLOCK:(kb + 1) * EXPERT_F_BLOCK],
                           preferred_element_type=F32)
        tile0 = (f - N_HID_STEPS) * (EXPERT_N_BLOCK // LANES)
        gate = gate_ref[0]
        for i in range(EXPERT_N_BLOCK // LANES):
            ye_ref[pl.ds(tile0 + i, cap, stride=TILES_PER_TOKEN), :] = out[:, i * LANES:(i + 1) * LANES] * gate

    @pl.when((e == N_EXPERTS - 1) & (f == EXPERT_STEPS - 1))
    def _():
        wait_rows(1 - slot)


def _experts(rows, h2t, gates, w1, w3, w2, cap):
    hid_blk = lambda e, f, idx: (e, 0, jnp.minimum(f, N_HID_STEPS - 1))
    n_out = D_MODEL // EXPERT_N_BLOCK

    def out_blk(e, f, idx):
        hold = (f == 0) & (e > 0)
        return (jnp.where(hold, e - 1, e), 0,
                jnp.where(hold, n_out - 1, jnp.maximum(f - N_HID_STEPS, 0)))

    grid_spec = pltpu.PrefetchScalarGridSpec(
        num_scalar_prefetch=1,
        grid=(N_EXPERTS, EXPERT_STEPS),
        in_specs=[pl.BlockSpec(memory_space=pl.ANY),
                  pl.BlockSpec((1, cap, LANES), lambda e, f, idx: (e, 0, 0)),
                  pl.BlockSpec((1, D_MODEL, EXPERT_F_BLOCK), hid_blk),
                  pl.BlockSpec((1, D_MODEL, EXPERT_F_BLOCK), hid_blk),
                  pl.BlockSpec((1, D_EXPERT, EXPERT_N_BLOCK), out_blk)],
        out_specs=pl.BlockSpec((cap * TILES_PER_TOKEN, LANES), lambda e, f, idx: (e, 0)),
        scratch_shapes=[pltpu.VMEM((2, cap * TILES_PER_TOKEN, LANES), F32),
                        pltpu.VMEM((cap, D_MODEL), BF16),
                        pltpu.VMEM((N_HID_STEPS, cap, EXPERT_F_BLOCK), BF16),
                        pltpu.SemaphoreType.DMA((2,))],
    )
    return pl.pallas_call(
        functools.partial(_expert_kernel, cap),
        grid_spec=grid_spec,
        out_shape=jax.ShapeDtypeStruct((N_EXPERTS * cap * TILES_PER_TOKEN, LANES), F32),
        compiler_params=_compiler_params(("arbitrary", "arbitrary")),
        name="expert_swiglu",
    )(rows, h2t, gates, w1, w3, w2)


COMBINE_BATCH = 8
ZERO_ROWS = 512
X_SLOTS = 3


def _combine_kernel(cap, n_tok, mod_base, mod_per_seq, nb,
                    row_ref, ye_ref, x1_ref, mod_ref, fw_ref, y_ref,
                    acc_ref, xbuf_ref, ybuf_ref, sem_x, sem_y):
    e = pl.program_id(0)

    @pl.when(e == 0)
    def _():
        def zero(i, _):
            r = pl.multiple_of(i * ZERO_ROWS, ZERO_ROWS)
            acc_ref[pl.ds(r, ZERO_ROWS), :] = jnp.zeros((ZERO_ROWS, LANES), F32)
            return 0

        lax.fori_loop(0, n_tok * TILES_PER_TOKEN // ZERO_ROWS, zero, 0)

    def batch(jb, _):
        vals = []
        for u in range(COMBINE_BATCH):
            j = jb * COMBINE_BATCH + u
            t = pl.multiple_of(row_ref[e * cap + j], TILES_PER_TOKEN)
            src = pl.multiple_of(j * TILES_PER_TOKEN, TILES_PER_TOKEN)
            vals.append((t, acc_ref[pl.ds(t, TILES_PER_TOKEN), :] + ye_ref[pl.ds(src, TILES_PER_TOKEN), :]))
        for t, val in vals:
            acc_ref[pl.ds(t, TILES_PER_TOKEN), :] = val
        return 0

    lax.fori_loop(0, cap // COMBINE_BATCH, batch, 0)

    @pl.when(e == N_EXPERTS - 1)
    def _():
        n_blk = n_tok // TOKEN_BLOCK

        def x1_copy(b, slot):
            rows = pl.ds(pl.multiple_of(b * TOKEN_BLOCK, TOKEN_BLOCK), TOKEN_BLOCK)
            return pltpu.make_async_copy(x1_ref.at[rows, :], xbuf_ref.at[slot], sem_x.at[slot])

        def y_copy(b, slot):
            rows = pl.ds(pl.multiple_of(b * TOKEN_BLOCK, TOKEN_BLOCK), TOKEN_BLOCK)
            return pltpu.make_async_copy(ybuf_ref.at[slot], y_ref.at[rows, :], sem_y.at[slot])

        for b0 in range(min(X_SLOTS - 1, n_blk)):
            x1_copy(b0, b0).start()

        def block(b, _):
            xslot = b % X_SLOTS
            slot = b % 2

            @pl.when(b + (X_SLOTS - 1) < n_blk)
            def _():
                x1_copy(b + (X_SLOTS - 1), (b + (X_SLOTS - 1)) % X_SLOTS).start()

            x1_copy(b, xslot).wait()

            @pl.when(b >= 2)
            def _():
                y_copy(b - 2, slot).wait()

            base = pl.multiple_of(b * (TOKEN_BLOCK * TILES_PER_TOKEN), TOKEN_BLOCK * TILES_PER_TOKEN)
            moe = jnp.concatenate(
                [acc_ref[pl.ds(base + s, TOKEN_BLOCK, stride=TILES_PER_TOKEN), :]
                 for s in range(TILES_PER_TOKEN)], axis=1)
            row = mod_base + ((b // nb) if mod_per_seq else 0)
            ybuf_ref[slot] = _rms(xbuf_ref[xslot] + _mod_row(mod_ref, row, 5) * moe, fw_ref[...])
            y_copy(b, slot).start()
            return 0

        lax.fori_loop(0, n_blk, block, 0)
        y_copy(n_blk - 2, n_blk % 2).wait()
        y_copy(n_blk - 1, (n_blk - 1) % 2).wait()


def _combine_final(rows, ye, x1, mod, fw, cap, n_tok, *, nb, mod_base, mod_per_seq):
    full = lambda a: pl.BlockSpec(a.shape, lambda e, rows: (0,) * a.ndim)
    grid_spec = pltpu.PrefetchScalarGridSpec(
        num_scalar_prefetch=1,
        grid=(N_EXPERTS,),
        in_specs=[pl.BlockSpec((cap * TILES_PER_TOKEN, LANES), lambda e, rows: (e, 0)),
                  pl.BlockSpec(memory_space=pl.ANY), full(mod), full(fw)],
        out_specs=pl.BlockSpec(memory_space=pl.ANY),
        scratch_shapes=[pltpu.VMEM((n_tok * TILES_PER_TOKEN, LANES), F32),
                        pltpu.VMEM((X_SLOTS, TOKEN_BLOCK, D_MODEL), F32),
                        pltpu.VMEM((2, TOKEN_BLOCK, D_MODEL), F32),
                        pltpu.SemaphoreType.DMA((X_SLOTS,)),
                        pltpu.SemaphoreType.DMA((2,))],
    )
    return pl.pallas_call(
        functools.partial(_combine_kernel, cap, n_tok, mod_base, mod_per_seq, nb),
        grid_spec=grid_spec,
        out_shape=jax.ShapeDtypeStruct((n_tok, D_MODEL), F32),
        compiler_params=_compiler_params(("arbitrary",)),
        name="moe_combine_norm",
    )(rows, ye, x1, mod, fw)


def _trunk_and_norm(x, e_tab, mod, s0_f, s0_b, prm, *, nseq, seq_len, add_pos, mod_base, mod_per_seq):
    nb = seq_len // TOKEN_BLOCK
    n_tok = nseq * seq_len
    cap = EC_CAPACITY_FACTOR * n_tok // N_EXPERTS
    kw = dict(nseq=nseq, nb=nb, add_pos=add_pos, mod_base=mod_base, mod_per_seq=mod_per_seq)
    if nb == 1 and not add_pos:
        x1, h2t, probs, sfin_f, sfin_b = _mixer_fused(
            x, mod, prm, s0_f, s0_b, nseq=nseq, mod_base=mod_base, mod_per_seq=mod_per_seq)
    else:
        qkv, g, lab, s, of, sfin_f = _mixer_fwd(
            x, e_tab, mod, prm["n1"], prm["win"], prm["wa"], prm["ba"], prm["snw"], prm["sws"],
            prm["sbs"], s0_f, **kw)
        x1, h2t, probs, sfin_b = _mixer_bwd(
            x, e_tab, mod, qkv, g, lab, s, of, prm["gnw"], prm["wout"], prm["n2"], prm["rw_t"],
            s0_b, **kw)
    x1 = x1.reshape(n_tok, D_MODEL)
    h2t = h2t.reshape(n_tok * TILES_PER_TOKEN, LANES)
    probs = probs.reshape(n_tok // TOKEN_BLOCK, N_EXPERTS, TOKEN_BLOCK)
    rows, gates = _route(probs, n_tok, cap)
    rows = rows.reshape(-1)
    ye = _experts(rows, h2t, gates, prm["w1"], prm["w3"], prm["w2"], cap)
    y = _combine_final(rows, ye, x1, mod, prm["fw"], cap, n_tok,
                       nb=nb, mod_base=mod_base, mod_per_seq=mod_per_seq)
    return y.reshape(nseq, seq_len, D_MODEL), sfin_f, sfin_b


def kernel(x_prompt, x_sample, state_gla_fwd, state_gla_bwd, c, c_ctx, ada_w, ada_b, norm1_w, w_in, gla_wa2_f, gla_ba_f, gla_wa2_b, gla_ba_b, gla_norm_w, sgu_norm_w, sgu_ws, sgu_bs, w_out, norm2_w, router_w, exp_w1, exp_w3, exp_w2, final_norm_w):
    assert ada_w.shape[0] == 1, "single trunk layer"
    batch, seq, _ = x_prompt.shape
    dec_batch, dec_seq, _ = x_sample.shape

    assert w_in.shape[2] == D_IN
    win = _win_transposed_bf16(w_in[0].T)
    wa = jnp.zeros((2 * GLA_LOWRANK, 2 * QK_W), F32)
    wa = wa.at[0:GLA_LOWRANK, 0:QK_W].set(gla_wa2_f[0])
    wa = wa.at[GLA_LOWRANK:2 * GLA_LOWRANK, QK_W:].set(gla_wa2_b[0]).astype(BF16)
    prm = dict(
        n1=norm1_w, win=win, wa=wa,
        ba=jnp.concatenate([gla_ba_f[0], gla_ba_b[0]])[None, :],
        snw=sgu_norm_w, sws=sgu_ws[0].astype(BF16),
        sbs=jnp.broadcast_to(sgu_bs[0][:, :, None], (SGU_GROUPS, SGU_CHUNK, SGU_CH)),
        gnw=gla_norm_w, wout=w_out[0].astype(BF16), n2=norm2_w, rw_t=router_w[0].T,
        w1=exp_w1[0], w3=exp_w3[0], w2=exp_w2[0], fw=final_norm_w[None, :])

    cvec = jnp.concatenate([c_ctx[None, :], c, jnp.zeros((SUBLANES - 1 - dec_batch, D_MODEL), F32)])
    mod = _modulation(cvec, ada_w[0], ada_b)
    e_tab = _pos_table()

    zero_state = jnp.zeros((min(SEQ_GROUP, batch), GLA_HEADS, GLA_DK, GLA_DV), F32)
    y_prompt, sf, sb = _trunk_and_norm(
        x_prompt, e_tab, mod, zero_state, zero_state, prm,
        nseq=batch, seq_len=seq, add_pos=False, mod_base=0, mod_per_seq=False)
    y_sample, _, _ = _trunk_and_norm(
        x_sample, e_tab, mod, state_gla_fwd[:, 0], state_gla_bwd[:, 0], prm,
        nseq=dec_batch, seq_len=dec_seq, add_pos=True, mod_base=1, mod_per_seq=True)
    return (y_prompt, y_sample, sf[:, None], sb[:, None])
```
